```python
import jax, jax.numpy as jnp
from jax import lax
import numpy as np

D_MODEL = 1024
BATCH = 8
SEQ = 8192
DEPTH = 1

RET_HEADS = 4
RET_QK_DIM = 128
RET_V_DIM = 256
RET_CHUNK = 128
ATTN_Q_HEADS = 16
ATTN_KV_HEADS = 2
ATTN_HEAD_DIM = 64
WINDOW = 128
ATTN_BLOCK = 128
D_FF = -(-8 * D_MODEL // (3 * 256)) * 256
ROPE_THETA = 10000.0
EPS = 1e-6

RET_QK = RET_HEADS * RET_QK_DIM
RET_V = RET_HEADS * RET_V_DIM
ATTN_Q = ATTN_Q_HEADS * ATTN_HEAD_DIM
ATTN_KV = ATTN_KV_HEADS * ATTN_HEAD_DIM
SPLITS = [RET_QK, RET_QK, RET_V, RET_V, ATTN_Q, ATTN_KV, ATTN_KV, D_MODEL, D_MODEL]
D_IN = sum(SPLITS)
SPLIT_IDX = [int(v) for v in np.cumsum(SPLITS)[:-1]]

kernel_name = "hybrid_retention_swa_sink_gated_block"


def rms_norm(x, g):
    xf = x.astype(jnp.float32)
    y = xf * lax.rsqrt(jnp.mean(xf * xf, axis=-1, keepdims=True) + EPS)
    return (y * g.astype(jnp.float32)).astype(x.dtype)


def rotary(x, pos):
    d = x.shape[-1]
    half = d // 2
    inv_freq = ROPE_THETA ** (-jnp.arange(half, dtype=jnp.float32) / half)
    ang = pos.astype(jnp.float32)[:, None] * inv_freq[None, :]
    cos = jnp.cos(ang)[None, :, None, :]
    sin = jnp.sin(ang)[None, :, None, :]
    xf = x.astype(jnp.float32)
    x1, x2 = xf[..., :half], xf[..., half:]
    out = jnp.concatenate([x1 * cos - x2 * sin, x2 * cos + x1 * sin], axis=-1)
    return out.astype(x.dtype)


def retention_chunkwise(q, k, v):
    B, S, H, dk = q.shape
    dv = v.shape[-1]
    C = RET_CHUNK
    N = S // C
    log_gamma = jnp.log1p(-jnp.exp2(-5.0 - jnp.arange(H, dtype=jnp.float32)))
    idx = jnp.arange(C, dtype=jnp.float32)
    rel = idx[:, None] - idx[None, :]
    intra_decay = jnp.where(rel[None] >= 0,
                            jnp.exp(log_gamma[:, None, None] * jnp.maximum(rel, 0.0)[None]), 0.0)
    q_decay = jnp.exp(log_gamma[:, None] * (idx + 1.0))[None, :, :, None]
    k_decay = jnp.exp(log_gamma[:, None] * (C - 1.0 - idx))[None, :, :, None]
    chunk_decay = jnp.exp(log_gamma * C)[None, :, None, None]

    qf = q.astype(jnp.float32) * (dk ** -0.5)
    kf = k.astype(jnp.float32)
    vf = v.astype(jnp.float32)
    to_chunks = lambda t: t.reshape(B, N, C, H, t.shape[-1]).transpose(1, 0, 3, 2, 4)
    qc, kc, vc = to_chunks(qf), to_chunks(kf), to_chunks(vf)

    def step(state, inp):
        qn, kn, vn = inp
        scores = jnp.einsum('bhcd,bhsd->bhcs', qn, kn) * intra_decay
        inner = jnp.einsum('bhcs,bhse->bhce', scores, vn)
        cross = jnp.einsum('bhcd,bhde->bhce', qn, state) * q_decay
        new_state = state * chunk_decay + jnp.einsum('bhsd,bhse->bhde', kn * k_decay, vn)
        return new_state, inner + cross

    state0 = jnp.zeros((B, H, dk, dv), jnp.float32)
    _, out = lax.scan(step, state0, (qc, kc, vc))
    return out.transpose(1, 0, 3, 2, 4).reshape(B, S, H, dv)


def head_group_norm(y, g):
    B, S, H, dv = y.shape
    mu = jnp.mean(y, axis=-1, keepdims=True)
    yc = y - mu
    var = jnp.mean(yc * yc, axis=-1, keepdims=True)
    yn = (yc * lax.rsqrt(var + EPS)).reshape(B, S, H * dv)
    return yn * g.astype(jnp.float32)


def sliding_window_sink_attention(q, k, v, sinks):
    B, S, Hq, d = q.shape
    Hkv = k.shape[2]
    G = Hq // Hkv
    C = ATTN_BLOCK
    N = S // C
    qb = q.reshape(B, N, C, Hkv, G, d)
    pad = ((0, 0), (C, 0), (0, 0), (0, 0))
    kp = jnp.pad(k, pad).reshape(B, N + 1, C, Hkv, d)
    vp = jnp.pad(v, pad).reshape(B, N + 1, C, Hkv, d)
    kb = jnp.concatenate([kp[:, :-1], kp[:, 1:]], axis=2)
    vb = jnp.concatenate([vp[:, :-1], vp[:, 1:]], axis=2)
    scores = jnp.einsum('bnqhgd,bnkhd->bnhgqk', qb, kb).astype(jnp.float32) * (d ** -0.5)
    qi = jnp.arange(C)[:, None]
    kj = jnp.arange(2 * C)[None, :]
    rel = C + qi - kj
    key_pos = jnp.arange(N)[:, None, None] * C + kj[None] - C
    mask = (rel[None] >= 0) & (rel[None] < WINDOW) & (key_pos >= 0)
    scores = jnp.where(mask[None, :, None, None], scores, -1e30)
    sink = sinks.astype(jnp.float32).reshape(1, 1, Hkv, G, 1, 1)
    m = jnp.maximum(jnp.max(scores, axis=-1, keepdims=True), sink)
    e = jnp.exp(scores - m)
    probs = e / (jnp.sum(e, axis=-1, keepdims=True) + jnp.exp(sink - m))
    out = jnp.einsum('bnhgqk,bnkhd->bnqhgd', probs.astype(v.dtype), vb)
    return out.reshape(B, S, Hq, d)


def _fwd_setup_inputs(seed: int = 0) -> dict:
    key = jax.random.key(seed)
    ks = jax.random.split(key, 16)
    nrm = lambda k, shape, fan_in: jax.random.normal(k, shape, jnp.float32) * (fan_in ** -0.5)
    gain = lambda k, shape: 1.0 + 0.02 * jax.random.normal(k, shape, jnp.float32)
    return {
        "x": jax.random.normal(ks[0], (BATCH, SEQ, D_MODEL), jnp.float32),
        "ln1_g": gain(ks[1], (DEPTH, D_MODEL)),
        "w_in": nrm(ks[2], (DEPTH, D_MODEL, D_IN), D_MODEL),
        "b_in": 0.02 * jax.random.normal(ks[3], (DEPTH, D_IN), jnp.float32),
        "ret_norm_g": gain(ks[4], (DEPTH, RET_V)),
        "w_ret_out": nrm(ks[5], (DEPTH, RET_V, D_MODEL), RET_V),
        "attn_sinks": 0.5 * jax.random.normal(ks[6], (DEPTH, ATTN_Q_HEADS), jnp.float32),
        "w_attn_out": nrm(ks[7], (DEPTH, ATTN_Q, D_MODEL), ATTN_Q),
        "w_out": nrm(ks[8], (DEPTH, D_MODEL, D_MODEL), D_MODEL),
        "ln2_g": gain(ks[9], (DEPTH, D_MODEL)),
        "w_ffn_gate": nrm(ks[10], (DEPTH, D_MODEL, D_FF), D_MODEL),
        "w_ffn_up": nrm(ks[11], (DEPTH, D_MODEL, D_FF), D_MODEL),
        "w_ffn_down": nrm(ks[12], (DEPTH, D_FF, D_MODEL), D_FF),
        "lnf_g": gain(ks[13], (D_MODEL,)),
    }


def _fwd_reference(x, ln1_g, w_in, b_in, ret_norm_g, w_ret_out, attn_sinks, w_attn_out, w_out,
              ln2_g, w_ffn_gate, w_ffn_up, w_ffn_down, lnf_g):
    B, S, _ = x.shape
    pos = jnp.arange(S, dtype=jnp.int32)
    for l in range(DEPTH):
        h = rms_norm(x, ln1_g[l])
        proj = h @ w_in[l] + b_in[l]
        rq, rk, rv, rg, aq, ak, av, gate_a, gate_b = jnp.split(proj, SPLIT_IDX, axis=-1)

        rq = rotary(rq.reshape(B, S, RET_HEADS, RET_QK_DIM), pos)
        rk = rotary(rk.reshape(B, S, RET_HEADS, RET_QK_DIM), pos)
        ry = retention_chunkwise(rq, rk, rv.reshape(B, S, RET_HEADS, RET_V_DIM))
        ry = head_group_norm(ry, ret_norm_g[l]).astype(x.dtype)
        branch_a = (jax.nn.silu(rg) * ry) @ w_ret_out[l]

        aq = rotary(aq.reshape(B, S, ATTN_Q_HEADS, ATTN_HEAD_DIM), pos)
        ak = rotary(ak.reshape(B, S, ATTN_KV_HEADS, ATTN_HEAD_DIM), pos)
        ay = sliding_window_sink_attention(aq, ak, av.reshape(B, S, ATTN_KV_HEADS, ATTN_HEAD_DIM),
                                           attn_sinks[l])
        branch_b = ay.reshape(B, S, ATTN_Q) @ w_attn_out[l]

        merged = jax.nn.sigmoid(gate_a) * branch_a + jax.nn.sigmoid(gate_b) * branch_b
        x = x + merged @ w_out[l]

        h2 = rms_norm(x, ln2_g[l])
        x = x + (jax.nn.silu(h2 @ w_ffn_gate[l]) * (h2 @ w_ffn_up[l])) @ w_ffn_down[l]
    return rms_norm(x, lnf_g)


import jax as _jax
import jax.numpy as _jnp

TWIN_FORMAT = 'train_step'
FWD_PARAMS = ['x', 'ln1_g', 'w_in', 'b_in', 'ret_norm_g', 'w_ret_out', 'attn_sinks', 'w_attn_out', 'w_out', 'ln2_g', 'w_ffn_gate', 'w_ffn_up', 'w_ffn_down', 'lnf_g']
TWIN_WEIGHTS = ['ln1_g', 'w_in', 'b_in', 'ret_norm_g', 'w_ret_out', 'attn_sinks', 'w_attn_out', 'w_out', 'ln2_g', 'w_ffn_gate', 'w_ffn_up', 'w_ffn_down', 'lnf_g']
TWIN_DIFF_INPUT = 'x'
TWIN_INPUTS = ['x', 'ln1_g', 'w_in', 'b_in', 'ret_norm_g', 'w_ret_out', 'attn_sinks', 'w_attn_out', 'w_out', 'ln2_g', 'w_ffn_gate', 'w_ffn_up', 'w_ffn_down', 'lnf_g', 'loss_target', 'm_ln1_g', 'm_w_in', 'm_b_in', 'm_ret_norm_g', 'm_w_ret_out', 'm_attn_sinks', 'm_w_attn_out', 'm_w_out', 'm_ln2_g', 'm_w_ffn_gate', 'm_w_ffn_up', 'm_w_ffn_down', 'm_lnf_g', 'v_ln1_g', 'v_w_in', 'v_b_in', 'v_ret_norm_g', 'v_w_ret_out', 'v_attn_sinks', 'v_w_attn_out', 'v_w_out', 'v_ln2_g', 'v_w_ffn_gate', 'v_w_ffn_up', 'v_w_ffn_down', 'v_lnf_g']
TWIN_OUTPUTS = ['loss', 'grad_x', 'grad_ln1_g', 'grad_w_in', 'grad_b_in', 'grad_ret_norm_g', 'grad_w_ret_out', 'grad_attn_sinks', 'grad_w_attn_out', 'grad_w_out', 'grad_ln2_g', 'grad_w_ffn_gate', 'grad_w_ffn_up', 'grad_w_ffn_down', 'grad_lnf_g', 'delta_ln1_g', 'delta_w_in', 'delta_b_in', 'delta_ret_norm_g', 'delta_w_ret_out', 'delta_attn_sinks', 'delta_w_attn_out', 'delta_w_out', 'delta_ln2_g', 'delta_w_ffn_gate', 'delta_w_ffn_up', 'delta_w_ffn_down', 'delta_lnf_g', 'new_m_ln1_g', 'new_m_w_in', 'new_m_b_in', 'new_m_ret_norm_g', 'new_m_w_ret_out', 'new_m_attn_sinks', 'new_m_w_attn_out', 'new_m_w_out', 'new_m_ln2_g', 'new_m_w_ffn_gate', 'new_m_w_ffn_up', 'new_m_w_ffn_down', 'new_m_lnf_g', 'new_v_ln1_g', 'new_v_w_in', 'new_v_b_in', 'new_v_ret_norm_g', 'new_v_w_ret_out', 'new_v_attn_sinks', 'new_v_w_attn_out', 'new_v_w_out', 'new_v_ln2_g', 'new_v_w_ffn_gate', 'new_v_w_ffn_up', 'new_v_w_ffn_down', 'new_v_lnf_g']
TWIN_LEAF_KINDS = {'loss': 'loss', 'grad_x': 'grad_x', 'grad_ln1_g': 'grad_w', 'grad_w_in': 'grad_w', 'grad_b_in': 'grad_w', 'grad_ret_norm_g': 'grad_w', 'grad_w_ret_out': 'grad_w', 'grad_attn_sinks': 'grad_w', 'grad_w_attn_out': 'grad_w', 'grad_w_out': 'grad_w', 'grad_ln2_g': 'grad_w', 'grad_w_ffn_gate': 'grad_w', 'grad_w_ffn_up': 'grad_w', 'grad_w_ffn_down': 'grad_w', 'grad_lnf_g': 'grad_w', 'delta_ln1_g': 'delta_w', 'delta_w_in': 'delta_w', 'delta_b_in': 'delta_w', 'delta_ret_norm_g': 'delta_w', 'delta_w_ret_out': 'delta_w', 'delta_attn_sinks': 'delta_w', 'delta_w_attn_out': 'delta_w', 'delta_w_out': 'delta_w', 'delta_ln2_g': 'delta_w', 'delta_w_ffn_gate': 'delta_w', 'delta_w_ffn_up': 'delta_w', 'delta_w_ffn_down': 'delta_w', 'delta_lnf_g': 'delta_w', 'new_m_ln1_g': 'new_m', 'new_m_w_in': 'new_m', 'new_m_b_in': 'new_m', 'new_m_ret_norm_g': 'new_m', 'new_m_w_ret_out': 'new_m', 'new_m_attn_sinks': 'new_m', 'new_m_w_attn_out': 'new_m', 'new_m_w_out': 'new_m', 'new_m_ln2_g': 'new_m', 'new_m_w_ffn_gate': 'new_m', 'new_m_w_ffn_up': 'new_m', 'new_m_w_ffn_down': 'new_m', 'new_m_lnf_g': 'new_m', 'new_v_ln1_g': 'new_v', 'new_v_w_in': 'new_v', 'new_v_b_in': 'new_v', 'new_v_ret_norm_g': 'new_v', 'new_v_w_ret_out': 'new_v', 'new_v_attn_sinks': 'new_v', 'new_v_w_attn_out': 'new_v', 'new_v_w_out': 'new_v', 'new_v_ln2_g': 'new_v', 'new_v_w_ffn_gate': 'new_v', 'new_v_w_ffn_up': 'new_v', 'new_v_w_ffn_down': 'new_v', 'new_v_lnf_g': 'new_v'}


def _forward(args):
    return _fwd_reference(*[args[k] for k in FWD_PARAMS])


def _output_shape():
    def fwd():
        inp = _fwd_setup_inputs(0)
        return _fwd_reference(*[inp[k] for k in FWD_PARAMS])
    out = _jax.eval_shape(fwd)
    return out.shape, out.dtype

N_MICROBATCH = 1
ADAM_LR = 0.001
ADAM_B1 = 0.9
ADAM_B2 = 0.999
ADAM_EPS = 1e-08
ADAM_WD = 0.01
ADAM_STEP = 10
PER_EXAMPLE_BATCH_AXIS = {'x': 0, 'loss_target': 0}
SHARED_INPUTS = []
_WEIGHT_DTYPES = {'ln1_g': _jnp.float32, 'w_in': _jnp.float32, 'b_in': _jnp.float32, 'ret_norm_g': _jnp.float32, 'w_ret_out': _jnp.float32, 'attn_sinks': _jnp.float32, 'w_attn_out': _jnp.float32, 'w_out': _jnp.float32, 'ln2_g': _jnp.float32, 'w_ffn_gate': _jnp.float32, 'w_ffn_up': _jnp.float32, 'w_ffn_down': _jnp.float32, 'lnf_g': _jnp.float32}
MOMENT_SCALE = {'ln1_g': 1.958495e-01, 'w_in': 7.368138e-02, 'b_in': 9.119064e-02, 'ret_norm_g': 8.925343e-02, 'w_ret_out': 8.601460e-02, 'attn_sinks': 2.231383e-02, 'w_attn_out': 2.908379e-02, 'w_out': 9.031282e-02, 'ln2_g': 1.815520e-01, 'w_ffn_gate': 7.675091e-02, 'w_ffn_up': 7.440169e-02, 'w_ffn_down': 1.235107e-01, 'lnf_g': 6.401020e+01}


def _to_microbatches(a, axis):
    t = _jnp.moveaxis(a, axis, 0)
    t = t.reshape((N_MICROBATCH, t.shape[0] // N_MICROBATCH) + t.shape[1:])
    return _jnp.moveaxis(t, 1, axis + 1)


def setup_inputs(seed: int = 0) -> dict:
    inp = _fwd_setup_inputs(seed)
    key = _jax.random.fold_in(_jax.random.key(seed), 7919)
    shape, _ = _output_shape()
    out = dict(inp)
    out["loss_target"] = _jax.random.normal(_jax.random.fold_in(key, 0), shape, _jnp.float32)
    for i, name in enumerate(TWIN_WEIGHTS):
        w = inp[name].astype(_jnp.float32)
        if MOMENT_SCALE is None:
            s = _jnp.sqrt(_jnp.mean(_jnp.square(w)) + 1e-30)
        else:
            s = MOMENT_SCALE[name]
        km, kv = _jax.random.split(_jax.random.fold_in(key, i + 1))
        out[name] = w
        out["m_" + name] = s * _jax.random.normal(km, w.shape, _jnp.float32)
        out["v_" + name] = (s * s) * _jax.random.uniform(kv, w.shape, _jnp.float32, 0.5, 1.5)
    if N_MICROBATCH > 1:
        for name, axis in PER_EXAMPLE_BATCH_AXIS.items():
            out[name] = _to_microbatches(out[name], axis)
    return {'x': out['x'], 'ln1_g': out['ln1_g'], 'w_in': out['w_in'], 'b_in': out['b_in'], 'ret_norm_g': out['ret_norm_g'], 'w_ret_out': out['w_ret_out'], 'attn_sinks': out['attn_sinks'], 'w_attn_out': out['w_attn_out'], 'w_out': out['w_out'], 'ln2_g': out['ln2_g'], 'w_ffn_gate': out['w_ffn_gate'], 'w_ffn_up': out['w_ffn_up'], 'w_ffn_down': out['w_ffn_down'], 'lnf_g': out['lnf_g'], 'loss_target': out['loss_target'], 'm_ln1_g': out['m_ln1_g'], 'm_w_in': out['m_w_in'], 'm_b_in': out['m_b_in'], 'm_ret_norm_g': out['m_ret_norm_g'], 'm_w_ret_out': out['m_w_ret_out'], 'm_attn_sinks': out['m_attn_sinks'], 'm_w_attn_out': out['m_w_attn_out'], 'm_w_out': out['m_w_out'], 'm_ln2_g': out['m_ln2_g'], 'm_w_ffn_gate': out['m_w_ffn_gate'], 'm_w_ffn_up': out['m_w_ffn_up'], 'm_w_ffn_down': out['m_w_ffn_down'], 'm_lnf_g': out['m_lnf_g'], 'v_ln1_g': out['v_ln1_g'], 'v_w_in': out['v_w_in'], 'v_b_in': out['v_b_in'], 'v_ret_norm_g': out['v_ret_norm_g'], 'v_w_ret_out': out['v_w_ret_out'], 'v_attn_sinks': out['v_attn_sinks'], 'v_w_attn_out': out['v_w_attn_out'], 'v_w_out': out['v_w_out'], 'v_ln2_g': out['v_ln2_g'], 'v_w_ffn_gate': out['v_w_ffn_gate'], 'v_w_ffn_up': out['v_w_ffn_up'], 'v_w_ffn_down': out['v_w_ffn_down'], 'v_lnf_g': out['v_lnf_g']}


def _loss(weights, diff, rest, loss_target):
    with _jax.named_scope("forward"):
        args = {**rest, TWIN_DIFF_INPUT: diff, **{k: w.astype(_WEIGHT_DTYPES[k]) for k, w in weights.items()}}
        y = _forward(args)
    with _jax.named_scope("loss_head"):
        err = _jnp.square(y.astype(_jnp.float32) - loss_target)
        return 0.5 * _jnp.sum(_jnp.mean(err, axis=-1)) if err.ndim else 0.5 * err


def _adamw(w, g, m, v):
    m = ADAM_B1 * m + (1.0 - ADAM_B1) * g
    v = ADAM_B2 * v + (1.0 - ADAM_B2) * _jnp.square(g)
    m_hat = m / (1.0 - ADAM_B1 ** ADAM_STEP)
    v_hat = v / (1.0 - ADAM_B2 ** ADAM_STEP)
    delta = -ADAM_LR * (m_hat / (_jnp.sqrt(v_hat) + ADAM_EPS) + ADAM_WD * w)
    return delta, m, v


def reference(x, ln1_g, w_in, b_in, ret_norm_g, w_ret_out, attn_sinks, w_attn_out, w_out, ln2_g, w_ffn_gate, w_ffn_up, w_ffn_down, lnf_g, loss_target, m_ln1_g, m_w_in, m_b_in, m_ret_norm_g, m_w_ret_out, m_attn_sinks, m_w_attn_out, m_w_out, m_ln2_g, m_w_ffn_gate, m_w_ffn_up, m_w_ffn_down, m_lnf_g, v_ln1_g, v_w_in, v_b_in, v_ret_norm_g, v_w_ret_out, v_attn_sinks, v_w_attn_out, v_w_out, v_ln2_g, v_w_ffn_gate, v_w_ffn_up, v_w_ffn_down, v_lnf_g):
    given = dict(x=x, ln1_g=ln1_g, w_in=w_in, b_in=b_in, ret_norm_g=ret_norm_g, w_ret_out=w_ret_out, attn_sinks=attn_sinks, w_attn_out=w_attn_out, w_out=w_out, ln2_g=ln2_g, w_ffn_gate=w_ffn_gate, w_ffn_up=w_ffn_up, w_ffn_down=w_ffn_down, lnf_g=lnf_g, loss_target=loss_target, m_ln1_g=m_ln1_g, m_w_in=m_w_in, m_b_in=m_b_in, m_ret_norm_g=m_ret_norm_g, m_w_ret_out=m_w_ret_out, m_attn_sinks=m_attn_sinks, m_w_attn_out=m_w_attn_out, m_w_out=m_w_out, m_ln2_g=m_ln2_g, m_w_ffn_gate=m_w_ffn_gate, m_w_ffn_up=m_w_ffn_up, m_w_ffn_down=m_w_ffn_down, m_lnf_g=m_lnf_g, v_ln1_g=v_ln1_g, v_w_in=v_w_in, v_b_in=v_b_in, v_ret_norm_g=v_ret_norm_g, v_w_ret_out=v_w_ret_out, v_attn_sinks=v_attn_sinks, v_w_attn_out=v_w_attn_out, v_w_out=v_w_out, v_ln2_g=v_ln2_g, v_w_ffn_gate=v_w_ffn_gate, v_w_ffn_up=v_w_ffn_up, v_w_ffn_down=v_w_ffn_down, v_lnf_g=v_lnf_g)
    weights = {n: given[n] for n in TWIN_WEIGHTS}
    shared = {n: given[n] for n in SHARED_INPUTS}
    per_example = {n: given[n] for n in ['x']}
    grad_fn = _jax.value_and_grad(_loss, argnums=(0, 1))

    def one_microbatch(ex, loss_target):
        ex = dict(ex)
        diff = ex.pop(TWIN_DIFF_INPUT)
        return grad_fn(weights, diff, {**shared, **ex}, loss_target)

    if N_MICROBATCH == 1:
        loss, (grad_w, grad_x) = one_microbatch(per_example, given["loss_target"])
    else:
        def body(carry, xs):
            loss_sum, grad_sum = carry
            l_k, (gw_k, gx_k) = one_microbatch(xs[0], xs[1])
            with _jax.named_scope("update"):
                return (loss_sum + l_k, _jax.tree.map(_jnp.add, grad_sum, gw_k)), gx_k

        init = (_jnp.zeros((), _jnp.float32), _jax.tree.map(_jnp.zeros_like, weights))
        (loss, grad_w), grad_x = _jax.lax.scan(body, init, (per_example, given["loss_target"]))
    with _jax.named_scope("update"):
        delta_w, new_m, new_v = {}, {}, {}
        for n in TWIN_WEIGHTS:
            delta_w[n], new_m[n], new_v[n] = _adamw(weights[n], grad_w[n], given["m_" + n], given["v_" + n])
    return (loss, grad_x, *[grad_w[n] for n in TWIN_WEIGHTS], *[delta_w[n] for n in TWIN_WEIGHTS],
            *[new_m[n] for n in TWIN_WEIGHTS], *[new_v[n] for n in TWIN_WEIGHTS])
```

```python
import functools

import numpy as np
import jax
import jax.numpy as jnp
from jax import lax
from jax.experimental import pallas as pl
from jax.experimental.pallas import tpu as pltpu

F32 = jnp.float32
_MXU = jnp.bfloat16

N_DEV = 8
D = 1024
RET_HEADS, RET_DK, RET_DV = 4, 128, 256
BLK = 128
Q_HEADS, KV_HEADS, HEAD_DIM = 16, 2, 64
D_FF = 2816
N_RET, N_ATTN, N_GATE = 2048, 1280, 3072
ROPE_THETA = 10000.0
EPS = 1e-6
RET_SCALE = RET_DK ** -0.5
ATTN_SCALE = HEAD_DIM ** -0.5
LR, B1, B2, ADAM_EPS, WD, STEP = 0.001, 0.9, 0.999, 1e-08, 0.01, 10
VMEM_LIMIT_MB = 56
MESH = pl.DeviceIdType.MESH


def _dot(a, b):
    return jnp.dot(a.astype(_MXU), b.astype(_MXU), preferred_element_type=F32)


def _dot_nt(a, b):
    return lax.dot_general(a.astype(_MXU), b.astype(_MXU), (((1,), (1,)), ((), ())), preferred_element_type=F32)


def _dot_tn(a, b):
    return lax.dot_general(a.astype(_MXU), b.astype(_MXU), (((0,), (0,)), ((), ())), preferred_element_type=F32)


def _sigmoid(x):
    return 1.0 / (1.0 + jnp.exp(-x))


def _cparams(n_axes, big=False):
    kw = dict(dimension_semantics=("arbitrary",) * n_axes)
    if big:
        kw["vmem_limit_bytes"] = VMEM_LIMIT_MB * 2**20
    return pltpu.CompilerParams(**kw)


def _rows(tm, width, col=0):
    return pl.BlockSpec((tm, width), lambda i: (i, col))


def _const(shape):
    nd = len(shape)
    return pl.BlockSpec(shape, lambda *_: (0,) * nd, pipeline_mode=pl.Buffered(1))


def _acc(width):
    return pl.BlockSpec((1, width), lambda *_: (0, 0))


def _sds(shape, dtype):
    return jax.ShapeDtypeStruct(shape, dtype)


def _swap_halves(x, half):
    w = x.shape[-1]
    if 2 * half == w:
        return pltpu.roll(x, half, 1)
    lane = lax.broadcasted_iota(jnp.int32, x.shape, 1)
    return jnp.where(lane % (2 * half) < half, pltpu.roll(x, w - half, 1), pltpu.roll(x, half, 1))


def _rope_tables(seq, half):
    inv_freq = ROPE_THETA ** (-jnp.arange(half, dtype=F32) / half)
    ang = jnp.arange(seq, dtype=jnp.int32).astype(F32)[:, None] * inv_freq[None, :]
    cos, sin = jnp.cos(ang), jnp.sin(ang)
    reps = 128 // (2 * half)
    return jnp.tile(jnp.concatenate([cos, cos], 1), (1, reps)), jnp.tile(jnp.concatenate([-sin, sin], 1), (1, reps))


def _retention_decays():
    log_gamma = np.log1p(-np.exp2(-5.0 - np.arange(RET_HEADS, dtype=np.float32))).astype(np.float32)
    idx = np.arange(BLK, dtype=np.float32)
    rel = idx[:, None] - idx[None, :]
    intra = np.where(rel[None] >= 0, np.exp(log_gamma[:, None, None] * np.maximum(rel, 0.0)[None]), 0.0)
    q_decay = np.exp(log_gamma[:, None] * (idx + 1.0))[:, :, None]
    k_decay = np.exp(log_gamma[:, None] * (BLK - 1.0 - idx))[:, :, None]
    chunk_decay = [float(np.exp(np.float32(lg * BLK))) for lg in log_gamma]
    return (jnp.asarray(intra, F32), jnp.asarray(q_decay, F32), jnp.asarray(k_decay, F32), chunk_decay)


def _ln_call(x, g, tm):
    s = x.shape[0]

    def body(x_ref, g_ref, h_ref):
        xv = x_ref[...]
        r = lax.rsqrt(jnp.mean(xv * xv, axis=-1, keepdims=True) + EPS)
        h_ref[...] = ((xv * r) * g_ref[...]).astype(h_ref.dtype)

    return pl.pallas_call(
        body, name="ln1", grid=(s // tm,), in_specs=[_rows(tm, D), _acc(D)], out_specs=_rows(tm, D),
        out_shape=_sds((s, D), _MXU), compiler_params=_cparams(1))(x, g)


def _mm_bias(a, w, bias, name, tm, tn):
    s, k = a.shape
    n = w.shape[1]

    def body(a_ref, w_ref, b_ref, o_ref):
        o_ref[...] = _dot(a_ref[...], w_ref[...]) + b_ref[...]

    return pl.pallas_call(
        body, name=name, grid=(n // tn, s // tm),
        in_specs=[pl.BlockSpec((tm, k), lambda j, i: (i, 0)), pl.BlockSpec((k, tn), lambda j, i: (0, j)),
                  pl.BlockSpec((1, tn), lambda j, i: (0, j))],
        out_specs=pl.BlockSpec((tm, tn), lambda j, i: (i, j)), out_shape=_sds((s, n), F32),
        compiler_params=_cparams(2))(a, w, bias)


def _mm_tn(a, b, name, tm, tn, tk):
    s, m = a.shape
    n = b.shape[1]

    def body(a_ref, b_ref, o_ref):
        @pl.when(pl.program_id(2) == 0)
        def _():
            o_ref[...] = jnp.zeros_like(o_ref)

        o_ref[...] += _dot_tn(a_ref[...], b_ref[...])

    return pl.pallas_call(
        body, name=name, grid=(m // tm, n // tn, s // tk),
        in_specs=[pl.BlockSpec((tk, tm), lambda i, j, k: (k, i)), pl.BlockSpec((tk, tn), lambda i, j, k: (k, j))],
        out_specs=pl.BlockSpec((tm, tn), lambda i, j, k: (i, j)), out_shape=_sds((m, n), F32),
        compiler_params=_cparams(3, big=True))(a, b)


def _ret_fwd_call(proj_ret, cos, sin, decays):
    s = proj_ret.shape[0]
    nblk = s // BLK
    intra, q_decay, k_decay, chunk_decay = decays

    def body(rq_ref, rk_ref, rv_ref, cos_ref, sin_ref, intra_ref, qd_ref, kd_ref,
             ry_ref, qr_ref, kr_ref, st_ref, state):
        @pl.when(pl.program_id(0) == 0)
        def _():
            state[...] = jnp.zeros_like(state)

        cos_v, sin_v = cos_ref[...], sin_ref[...]
        for h in range(RET_HEADS):
            hk = slice(h * RET_DK, (h + 1) * RET_DK)
            hv = slice(h * RET_DV, (h + 1) * RET_DV)
            q, k = rq_ref[:, hk], rk_ref[:, hk]
            qr = (q * cos_v + _swap_halves(q, RET_DK // 2) * sin_v) * RET_SCALE
            kr = k * cos_v + _swap_halves(k, RET_DK // 2) * sin_v
            v = rv_ref[:, hv]
            s_h = state[h]
            st_ref[0, h] = s_h.astype(st_ref.dtype)
            scores = _dot_nt(qr, kr) * intra_ref[h]
            ry_ref[:, hv] = _dot(scores, v) + _dot(qr, s_h) * qd_ref[h]
            state[h] = s_h * chunk_decay[h] + _dot_tn(kr * kd_ref[h], v)
            qr_ref[:, hk] = qr.astype(qr_ref.dtype)
            kr_ref[:, hk] = kr.astype(kr_ref.dtype)

    blk = lambda w, c: pl.BlockSpec((BLK, w), lambda n: (n, c))
    return pl.pallas_call(
        body, name="ret_fwd", grid=(nblk,),
        in_specs=[blk(512, 0), blk(512, 1), blk(1024, 1), blk(128, 0), blk(128, 0),
                  _const(intra.shape), _const(q_decay.shape), _const(k_decay.shape)],
        out_specs=[blk(1024, 0), blk(512, 0), blk(512, 0),
                   pl.BlockSpec((1, RET_HEADS, RET_DK, RET_DV), lambda n: (n, 0, 0, 0))],
        out_shape=[_sds((s, 1024), F32), _sds((s, 512), _MXU), _sds((s, 512), _MXU),
                   _sds((nblk, RET_HEADS, RET_DK, RET_DV), _MXU)],
        scratch_shapes=[pltpu.VMEM((RET_HEADS, RET_DK, RET_DV), F32)],
        compiler_params=_cparams(1))(proj_ret, proj_ret, proj_ret, cos, sin, intra, q_decay, k_decay)


def _ret_bwd_call(qr, kr, proj_ret, states, dry, cos, sin, decays):
    s = qr.shape[0]
    nblk = s // BLK
    intra, q_decay, k_decay, chunk_decay = decays

    def body(qr_ref, kr_ref, rv_ref, st_ref, dry_ref, cos_ref, sin_ref, intra_ref, qd_ref, kd_ref,
             dp_ref, db_ref, dstate):
        @pl.when(pl.program_id(0) == 0)
        def _():
            dstate[...] = jnp.zeros_like(dstate)
            db_ref[...] = jnp.zeros_like(db_ref)

        cos_v, sin_v = cos_ref[...], sin_ref[...]
        for h in range(RET_HEADS):
            hk = slice(h * RET_DK, (h + 1) * RET_DK)
            hv = slice(h * RET_DV, (h + 1) * RET_DV)
            q, k, v, d_out = qr_ref[:, hk], kr_ref[:, hk], rv_ref[:, hv], dry_ref[:, hv]
            d_next = dstate[h]
            scores = _dot_nt(q, k) * intra_ref[h]
            d_scores = _dot_nt(d_out, v) * intra_ref[h]
            d_cross = d_out * qd_ref[h]
            dq = _dot(d_scores, k) + _dot_nt(d_cross, st_ref[0, h])
            dk = _dot_tn(d_scores, q) + _dot_nt(v, d_next) * kd_ref[h]
            dv = _dot_tn(scores, d_out) + _dot(k.astype(F32) * kd_ref[h], d_next)
            dstate[h] = d_next * chunk_decay[h] + _dot_tn(q, d_cross)
            dq = (dq * cos_v - _swap_halves(dq, RET_DK // 2) * sin_v) * RET_SCALE
            dk = dk * cos_v - _swap_halves(dk, RET_DK // 2) * sin_v
            dp_ref[:, hk] = dq.astype(dp_ref.dtype)
            dp_ref[:, slice(512 + h * RET_DK, 512 + (h + 1) * RET_DK)] = dk.astype(dp_ref.dtype)
            dp_ref[:, slice(1024 + h * RET_DV, 1024 + (h + 1) * RET_DV)] = dv.astype(dp_ref.dtype)
            db_ref[:, hk] += jnp.sum(dq, axis=0, keepdims=True)
            db_ref[:, slice(512 + h * RET_DK, 512 + (h + 1) * RET_DK)] += jnp.sum(dk, axis=0, keepdims=True)
            db_ref[:, slice(1024 + h * RET_DV, 1024 + (h + 1) * RET_DV)] += jnp.sum(dv, axis=0, keepdims=True)

    rblk = lambda w, c: pl.BlockSpec((BLK, w), lambda n: (nblk - 1 - n, c))
    return pl.pallas_call(
        body, name="ret_bwd", grid=(nblk,),
        in_specs=[rblk(512, 0), rblk(512, 0), rblk(1024, 1),
                  pl.BlockSpec((1, RET_HEADS, RET_DK, RET_DV), lambda n: (nblk - 1 - n, 0, 0, 0)),
                  rblk(1024, 0), rblk(128, 0), rblk(128, 0),
                  _const(intra.shape), _const(q_decay.shape), _const(k_decay.shape)],
        out_specs=[rblk(N_RET, 0), _acc(N_RET)],
        out_shape=[_sds((s, N_RET), _MXU), _sds((1, N_RET), F32)],
        scratch_shapes=[pltpu.VMEM((RET_HEADS, RET_DK, RET_DV), F32)],
        compiler_params=_cparams(1))(qr, kr, proj_ret, states, dry, cos, sin, intra, q_decay, k_decay)


def _both_halves(x, g):
    lane = lax.broadcasted_iota(jnp.int32, x.shape, 1)
    keep = lane < HEAD_DIM if g == 0 else lane >= HEAD_DIM
    return jnp.where(keep, x, pltpu.roll(x, HEAD_DIM, 1))


def _window_mask(first_block):
    qi = lax.broadcasted_iota(jnp.int32, (BLK, 2 * BLK), 0)
    kj = lax.broadcasted_iota(jnp.int32, (BLK, 2 * BLK), 1)
    first_key = jnp.where(first_block, BLK, 0)
    return (kj > qi) & (kj <= qi + BLK) & (kj >= first_key)


def _sink_softmax(scores, mask, sink):
    scores = jnp.where(mask, scores, -1e30)
    m = jnp.maximum(jnp.max(scores, axis=-1, keepdims=True), sink)
    e = jnp.exp(scores - m)
    e_sink = jnp.exp(sink - m)
    inv = 1.0 / (jnp.sum(e, axis=-1, keepdims=True) + e_sink)
    return e * inv, e_sink * inv


def _attn_fwd_call(proj_attn, sinks, cos, sin):
    s = proj_attn.shape[0]
    nblk = s // BLK

    def body(sink_ref, q_ref, k_ref, v_ref, cos_ref, sin_ref, ay_ref, qr_ref, kr_ref, vb_ref, kwin, vwin):
        n = pl.program_id(0)

        @pl.when(n == 0)
        def _():
            kwin[...] = jnp.zeros_like(kwin)
            vwin[...] = jnp.zeros_like(vwin)

        @pl.when(n > 0)
        def _():
            kwin[0:BLK] = kwin[BLK:2 * BLK]
            vwin[0:BLK] = vwin[BLK:2 * BLK]

        cos_v, sin_v = cos_ref[...], sin_ref[...]
        k = k_ref[...]
        kr = (k * cos_v + _swap_halves(k, HEAD_DIM // 2) * sin_v).astype(kwin.dtype)
        kwin[BLK:2 * BLK] = kr
        vwin[BLK:2 * BLK] = v_ref[...].astype(vwin.dtype)
        kr_ref[...] = kr
        vb_ref[...] = vwin[BLK:2 * BLK]
        mask = _window_mask(n == 0)
        lane = lax.broadcasted_iota(jnp.int32, (BLK, 128), 1)
        low, high = lane < HEAD_DIM, lane >= HEAD_DIM
        for g in range(KV_HEADS):
            kg = _both_halves(kwin[...], g)
            vg = _both_halves(vwin[...], g)
            for j in range(g * 4, g * 4 + 4):
                cols = slice(j * 128, (j + 1) * 128)
                q = q_ref[:, cols]
                qr = ((q * cos_v + _swap_halves(q, HEAD_DIM // 2) * sin_v) * ATTN_SCALE).astype(qr_ref.dtype)
                qr_ref[:, cols] = qr
                outs = []
                for t in range(2):
                    q_t = jnp.where(low if t == 0 else high, qr, jnp.zeros_like(qr))
                    p, _ = _sink_softmax(_dot_nt(q_t, kg), mask, sink_ref[0, 2 * j + t])
                    outs.append(_dot(p, vg))
                ay_ref[:, cols] = jnp.where(low, outs[0], outs[1]).astype(ay_ref.dtype)

    blk = lambda w, c: pl.BlockSpec((BLK, w), lambda n: (n, c))
    return pl.pallas_call(
        body, name="attn_fwd", grid=(nblk,),
        in_specs=[pl.BlockSpec(memory_space=pltpu.SMEM), blk(1024, 0), blk(128, 8), blk(128, 9), blk(128, 0), blk(128, 0)],
        out_specs=[blk(1024, 0), blk(1024, 0), blk(128, 0), blk(128, 0)],
        out_shape=[_sds((s, 1024), _MXU), _sds((s, 1024), _MXU), _sds((s, 128), _MXU), _sds((s, 128), _MXU)],
        scratch_shapes=[pltpu.VMEM((2 * BLK, 128), _MXU), pltpu.VMEM((2 * BLK, 128), _MXU)],
        compiler_params=_cparams(1))(sinks, proj_attn, proj_attn, proj_attn, cos, sin)


def _attn_bwd_call(qr, kr, vb, day, sinks, cos, sin):
    s = qr.shape[0]
    nblk = s // BLK

    def body(sink_ref, q_ref, kc_ref, kp_ref, vc_ref, vp_ref, do_ref, cos_ref, sin_ref,
             dq_ref, cur_ref, prev_ref, dsink_ref, db_ref):
        n = pl.program_id(0)

        @pl.when(n == 0)
        def _():
            dsink_ref[...] = jnp.zeros_like(dsink_ref)
            db_ref[...] = jnp.zeros_like(db_ref)

        cos_v, sin_v = cos_ref[...], sin_ref[...]
        mask = _window_mask(n == 0)
        lane = lax.broadcasted_iota(jnp.int32, (BLK, 128), 1)
        low, high = lane < HEAD_DIM, lane >= HEAD_DIM
        lane1 = lax.broadcasted_iota(jnp.int32, (1, 128), 1)
        kwin = jnp.concatenate([kp_ref[...], kc_ref[...]], axis=0)
        vwin = jnp.concatenate([vp_ref[...], vc_ref[...]], axis=0)
        dk_heads, dv_heads = [], []
        dsink = jnp.zeros((1, 128), F32)
        for g in range(KV_HEADS):
            kg = _both_halves(kwin, g)
            vg = _both_halves(vwin, g)
            dk_acc = jnp.zeros((2 * BLK, 128), F32)
            dv_acc = jnp.zeros((2 * BLK, 128), F32)
            for j in range(g * 4, g * 4 + 4):
                cols = slice(j * 128, (j + 1) * 128)
                q, d_out = q_ref[:, cols], do_ref[:, cols]
                dqs = []
                for t in range(2):
                    sel = low if t == 0 else high
                    q_t = jnp.where(sel, q, jnp.zeros_like(q))
                    do_t = jnp.where(sel, d_out, jnp.zeros_like(d_out))
                    p, p_sink = _sink_softmax(_dot_nt(q_t, kg), mask, sink_ref[0, 2 * j + t])
                    dp = _dot_nt(do_t, vg)
                    delta = jnp.sum(p * dp, axis=-1, keepdims=True)
                    ds = p * (dp - delta)
                    dsink = dsink + jnp.where(lane1 == 2 * j + t, -jnp.sum(p_sink * delta), 0.0)
                    dv_acc = dv_acc + _dot_tn(p, do_t)
                    dk_acc = dk_acc + _dot_tn(ds, q_t)
                    dqs.append(_dot(ds, kg))
                dq = jnp.where(low, dqs[0], dqs[1])
                dq = (dq * cos_v - _swap_halves(dq, HEAD_DIM // 2) * sin_v) * ATTN_SCALE
                dq_ref[:, cols] = dq.astype(dq_ref.dtype)
                db_ref[:, cols] += jnp.sum(dq, axis=0, keepdims=True)
            dk_heads.append(dk_acc + pltpu.roll(dk_acc, HEAD_DIM, 1))
            dv_heads.append(dv_acc + pltpu.roll(dv_acc, HEAD_DIM, 1))
        lane2 = lax.broadcasted_iota(jnp.int32, (2 * BLK, 128), 1)
        dk_all = jnp.where(lane2 < HEAD_DIM, dk_heads[0], dk_heads[1])
        dv_all = jnp.where(lane2 < HEAD_DIM, dv_heads[0], dv_heads[1])
        prev_ref[:, 0:128] = dk_all[0:BLK]
        prev_ref[:, 128:256] = dv_all[0:BLK]
        cur_ref[:, 0:128] = dk_all[BLK:2 * BLK]
        cur_ref[:, 128:256] = dv_all[BLK:2 * BLK]
        dsink_ref[...] += dsink

    blk = lambda w, c: pl.BlockSpec((BLK, w), lambda n: (n, c))
    pblk = lambda w: pl.BlockSpec((BLK, w), lambda n: (jnp.maximum(n - 1, 0), 0))
    return pl.pallas_call(
        body, name="attn_bwd", grid=(nblk,),
        in_specs=[pl.BlockSpec(memory_space=pltpu.SMEM), blk(1024, 0), blk(128, 0), pblk(128), blk(128, 0), pblk(128),
                  blk(1024, 0), blk(128, 0), blk(128, 0)],
        out_specs=[blk(1024, 0), blk(256, 0), blk(256, 0), _acc(128), _acc(1024)],
        out_shape=[_sds((s, 1024), _MXU), _sds((s, 256), F32), _sds((s, 256), F32), _sds((1, 128), F32),
                   _sds((1, 1024), F32)],
        compiler_params=_cparams(1))(sinks, qr, kr, kr, vb, vb, day, cos, sin)


def _attn_combine_call(dq, dkv_cur, dkv_prev, cos, sin):
    s = dq.shape[0]
    nblk = s // BLK

    def body(dq_ref, cur_ref, nxt_ref, cos_ref, sin_ref, dp_ref, db_ref):
        n = pl.program_id(0)

        @pl.when(n == 0)
        def _():
            db_ref[...] = jnp.zeros_like(db_ref)

        dkv = cur_ref[...] + nxt_ref[...] * (n < nblk - 1).astype(F32)
        dk = dkv[:, 0:128]
        dk = dk * cos_ref[...] - _swap_halves(dk, HEAD_DIM // 2) * sin_ref[...]
        dv = dkv[:, 128:256]
        dp_ref[:, 0:1024] = dq_ref[...]
        dp_ref[:, 1024:1152] = dk.astype(dp_ref.dtype)
        dp_ref[:, 1152:1280] = dv.astype(dp_ref.dtype)
        db_ref[:, 0:128] += jnp.sum(dk, axis=0, keepdims=True)
        db_ref[:, 128:256] += jnp.sum(dv, axis=0, keepdims=True)

    blk = lambda w: pl.BlockSpec((BLK, w), lambda n: (n, 0))
    return pl.pallas_call(
        body, name="attn_combine", grid=(nblk,),
        in_specs=[blk(1024), blk(256), pl.BlockSpec((BLK, 256), lambda n: (jnp.minimum(n + 1, nblk - 1), 0)),
                  blk(128), blk(128)],
        out_specs=[blk(N_ATTN), _acc(256)],
        out_shape=[_sds((s, N_ATTN), _MXU), _sds((1, 256), F32)],
        compiler_params=_cparams(1))(dq, dkv_cur, dkv_prev, cos, sin)


def _group_norm(y):
    mu = jnp.mean(y, axis=-1, keepdims=True)
    yc = y - mu
    rs = lax.rsqrt(jnp.mean(yc * yc, axis=-1, keepdims=True) + EPS)
    return yc * rs, rs


def _merge_fwd_call(x, ry, proj_gate, ay, gn_g, w_ro, w_ao, w_o, tm):
    s = x.shape[0]

    def body(x_ref, ry_ref, rg_ref, ga_ref, gb_ref, ay_ref, gn_ref, wro_ref, wao_ref, wo_ref,
             ain_ref, a_ref, b_ref, mg_ref, x1_ref):
        for h in range(RET_HEADS):
            hv = slice(h * RET_DV, (h + 1) * RET_DV)
            yhat, _ = _group_norm(ry_ref[:, hv])
            rg = rg_ref[:, hv]
            ain_ref[:, hv] = ((rg * _sigmoid(rg)) * (yhat * gn_ref[:, hv])).astype(ain_ref.dtype)
        a = _dot(ain_ref[...], wro_ref[...])
        b = _dot(ay_ref[...], wao_ref[...])
        a_ref[...] = a
        b_ref[...] = b
        merged = (_sigmoid(ga_ref[...]) * a + _sigmoid(gb_ref[...]) * b).astype(mg_ref.dtype)
        mg_ref[...] = merged
        x1_ref[...] = x_ref[...] + _dot(merged, wo_ref[...])

    return pl.pallas_call(
        body, name="merge_fwd", grid=(s // tm,),
        in_specs=[_rows(tm, D), _rows(tm, 1024), _rows(tm, 1024, 0), _rows(tm, 1024, 1), _rows(tm, 1024, 2),
                  _rows(tm, 1024), _acc(1024), _const((D, D)), _const((D, D)), _const((D, D))],
        out_specs=[_rows(tm, D)] * 5,
        out_shape=[_sds((s, D), _MXU), _sds((s, D), F32), _sds((s, D), F32), _sds((s, D), _MXU), _sds((s, D), F32)],
        compiler_params=_cparams(1, big=True))(x, ry, proj_gate, proj_gate, proj_gate, ay, gn_g, w_ro, w_ao, w_o)


def _merge_bwd_call(dx1, a, b, ry, proj_gate, gn_g, w_ro, w_ao, w_o, tm):
    s = dx1.shape[0]

    def body(dx1_ref, a_ref, b_ref, ry_ref, rg_ref, ga_ref, gb_ref, gn_ref, wro_ref, wao_ref, wo_ref,
             da_ref, dbr_ref, dp_ref, day_ref, dry_ref, dbias_ref, dgn_ref):
        @pl.when(pl.program_id(0) == 0)
        def _():
            dbias_ref[...] = jnp.zeros_like(dbias_ref)
            dgn_ref[...] = jnp.zeros_like(dgn_ref)

        d_merged = _dot_nt(dx1_ref[...], wo_ref[...])
        sa, sb = _sigmoid(ga_ref[...]), _sigmoid(gb_ref[...])
        d_a = d_merged * sa
        d_b = d_merged * sb
        da_ref[...] = d_a.astype(da_ref.dtype)
        dbr_ref[...] = d_b.astype(dbr_ref.dtype)
        d_ga = d_merged * a_ref[...] * (sa * (1.0 - sa))
        d_gb = d_merged * b_ref[...] * (sb * (1.0 - sb))
        dp_ref[:, 1024:2048] = d_ga.astype(dp_ref.dtype)
        dp_ref[:, 2048:3072] = d_gb.astype(dp_ref.dtype)
        dbias_ref[:, 1024:2048] += jnp.sum(d_ga, axis=0, keepdims=True)
        dbias_ref[:, 2048:3072] += jnp.sum(d_gb, axis=0, keepdims=True)
        day_ref[...] = _dot_nt(d_b, wao_ref[...]).astype(day_ref.dtype)
        d_ain = _dot_nt(d_a, wro_ref[...])
        for h in range(RET_HEADS):
            hv = slice(h * RET_DV, (h + 1) * RET_DV)
            yhat, rs = _group_norm(ry_ref[:, hv])
            rg = rg_ref[:, hv]
            sg = _sigmoid(rg)
            gn = gn_ref[:, hv]
            d_h = d_ain[:, hv]
            d_rg = d_h * (yhat * gn) * (sg * (1.0 + rg * (1.0 - sg)))
            d_ryn = d_h * (rg * sg)
            dgn_ref[:, hv] += jnp.sum(d_ryn * yhat, axis=0, keepdims=True)
            d_yhat = d_ryn * gn
            dry_ref[:, hv] = rs * (d_yhat - jnp.mean(d_yhat, axis=-1, keepdims=True)
                                   - yhat * jnp.mean(d_yhat * yhat, axis=-1, keepdims=True))
            dp_ref[:, hv] = d_rg.astype(dp_ref.dtype)
            dbias_ref[:, hv] += jnp.sum(d_rg, axis=0, keepdims=True)

    return pl.pallas_call(
        body, name="merge_bwd", grid=(s // tm,),
        in_specs=[_rows(tm, D), _rows(tm, D), _rows(tm, D), _rows(tm, 1024), _rows(tm, 1024, 0), _rows(tm, 1024, 1),
                  _rows(tm, 1024, 2), _acc(1024), _const((D, D)), _const((D, D)), _const((D, D))],
        out_specs=[_rows(tm, D), _rows(tm, D), _rows(tm, N_GATE), _rows(tm, D), _rows(tm, D), _acc(N_GATE), _acc(1024)],
        out_shape=[_sds((s, D), _MXU), _sds((s, D), _MXU), _sds((s, N_GATE), _MXU), _sds((s, D), _MXU),
                   _sds((s, D), F32), _sds((1, N_GATE), F32), _sds((1, 1024), F32)],
        compiler_params=_cparams(1, big=True))(dx1, a, b, ry, proj_gate, proj_gate, proj_gate, gn_g, w_ro, w_ao, w_o)


def _ffn_fwd_call(x1, target, ln2_g, lnf_g, w_g, w_u, w_d, tm):
    s = x1.shape[0]

    def body(x1_ref, t_ref, g2_ref, gf_ref, wg_ref, wu_ref, wd_ref,
             h2_ref, g_ref, u_ref, f_ref, dx2_ref, loss_ref, dgf_ref):
        @pl.when(pl.program_id(0) == 0)
        def _():
            loss_ref[...] = jnp.zeros_like(loss_ref)
            dgf_ref[...] = jnp.zeros_like(dgf_ref)

        x1v = x1_ref[...]
        r1 = lax.rsqrt(jnp.mean(x1v * x1v, axis=-1, keepdims=True) + EPS)
        h2 = ((x1v * r1) * g2_ref[...]).astype(h2_ref.dtype)
        h2_ref[...] = h2
        g = _dot(h2, wg_ref[...])
        u = _dot(h2, wu_ref[...])
        g_ref[...] = g
        u_ref[...] = u
        f = ((g * _sigmoid(g)) * u).astype(f_ref.dtype)
        f_ref[...] = f
        x2 = x1v + _dot(f, wd_ref[...])
        r2 = lax.rsqrt(jnp.mean(x2 * x2, axis=-1, keepdims=True) + EPS)
        xhat = x2 * r2
        err = xhat * gf_ref[...] - t_ref[...]
        loss_ref[...] += 0.5 * jnp.sum(jnp.mean(err * err, axis=-1, keepdims=True))
        dy = err * (1.0 / D)
        dgf_ref[...] += jnp.sum(dy * xhat, axis=0, keepdims=True)
        dxh = dy * gf_ref[...]
        dx2_ref[...] = r2 * (dxh - xhat * jnp.mean(dxh * xhat, axis=-1, keepdims=True))

    return pl.pallas_call(
        body, name="ffn_fwd", grid=(s // tm,),
        in_specs=[_rows(tm, D), _rows(tm, D), _acc(D), _acc(D), _const((D, D_FF)), _const((D, D_FF)), _const((D_FF, D))],
        out_specs=[_rows(tm, D), _rows(tm, D_FF), _rows(tm, D_FF), _rows(tm, D_FF), _rows(tm, D), _acc(128), _acc(D)],
        out_shape=[_sds((s, D), _MXU), _sds((s, D_FF), F32), _sds((s, D_FF), F32), _sds((s, D_FF), _MXU),
                   _sds((s, D), F32), _sds((1, 128), F32), _sds((1, D), F32)],
        compiler_params=_cparams(1, big=True))(x1, target, ln2_g, lnf_g, w_g, w_u, w_d)


def _ffn_bwd_call(dx2, x1, g, u, ln2_g, w_g, w_u, w_d, tm):
    s = dx2.shape[0]

    def body(dx2_ref, x1_ref, g_ref, u_ref, g2_ref, wg_ref, wu_ref, wd_ref, dx1_ref, dg_ref, du_ref, dg2_ref):
        @pl.when(pl.program_id(0) == 0)
        def _():
            dg2_ref[...] = jnp.zeros_like(dg2_ref)

        dx2v = dx2_ref[...]
        df = _dot_nt(dx2v, wd_ref[...])
        gv, uv = g_ref[...], u_ref[...]
        sg = _sigmoid(gv)
        du = (df * (gv * sg)).astype(du_ref.dtype)
        dg = (df * uv * (sg * (1.0 + gv * (1.0 - sg)))).astype(dg_ref.dtype)
        du_ref[...] = du
        dg_ref[...] = dg
        dh2 = _dot_nt(dg, wg_ref[...]) + _dot_nt(du, wu_ref[...])
        x1v = x1_ref[...]
        r1 = lax.rsqrt(jnp.mean(x1v * x1v, axis=-1, keepdims=True) + EPS)
        xhat = x1v * r1
        dg2_ref[...] += jnp.sum(dh2 * xhat, axis=0, keepdims=True)
        dxh = dh2 * g2_ref[...]
        dx1_ref[...] = dx2v + r1 * (dxh - xhat * jnp.mean(dxh * xhat, axis=-1, keepdims=True))

    return pl.pallas_call(
        body, name="ffn_bwd", grid=(s // tm,),
        in_specs=[_rows(tm, D), _rows(tm, D), _rows(tm, D_FF), _rows(tm, D_FF), _acc(D),
                  _const((D, D_FF)), _const((D, D_FF)), _const((D_FF, D))],
        out_specs=[_rows(tm, D), _rows(tm, D_FF), _rows(tm, D_FF), _acc(D)],
        out_shape=[_sds((s, D), F32), _sds((s, D_FF), _MXU), _sds((s, D_FF), _MXU), _sds((1, D), F32)],
        compiler_params=_cparams(1, big=True))(dx2, x1, g, u, ln2_g, w_g, w_u, w_d)


def _dx_call(x, dx1, dp_ret, dp_attn, dp_gate, ln1_g, w_ret, w_attn, w_gate, tm):
    s = x.shape[0]

    def body(x_ref, dx1_ref, dr_ref, da_ref, dg_ref, g1_ref, wr_ref, wa_ref, wg_ref, dx_ref, dg1_ref):
        @pl.when(pl.program_id(0) == 0)
        def _():
            dg1_ref[...] = jnp.zeros_like(dg1_ref)

        dh = (_dot_nt(dr_ref[...], wr_ref[...]) + _dot_nt(da_ref[...], wa_ref[...])
              + _dot_nt(dg_ref[...], wg_ref[...]))
        xv = x_ref[...]
        r = lax.rsqrt(jnp.mean(xv * xv, axis=-1, keepdims=True) + EPS)
        xhat = xv * r
        dg1_ref[...] += jnp.sum(dh * xhat, axis=0, keepdims=True)
        dxh = dh * g1_ref[...]
        dx_ref[...] = dx1_ref[...] + r * (dxh - xhat * jnp.mean(dxh * xhat, axis=-1, keepdims=True))

    return pl.pallas_call(
        body, name="dx", grid=(s // tm,),
        in_specs=[_rows(tm, D), _rows(tm, D), _rows(tm, N_RET), _rows(tm, N_ATTN), _rows(tm, N_GATE), _acc(D),
                  _const((D, N_RET)), _const((D, N_ATTN)), _const((D, N_GATE))],
        out_specs=[_rows(tm, D), _acc(D)],
        out_shape=[_sds((s, D), F32), _sds((1, D), F32)],
        compiler_params=_cparams(1, big=True))(x, dx1, dp_ret, dp_attn, dp_gate, ln1_g, w_ret, w_attn, w_gate)


def _position():
    return lax.axis_index("x"), lax.axis_index("y"), lax.axis_index("c")


def _slot(px, py, pc):
    return 4 * px + 2 * py + pc


def _all_gather_call(blocks, name):
    nb = len(blocks)

    def body(*refs):
        ins, outs = refs[:nb], refs[nb:2 * nb]
        send_sems, recv_sems, local_sems = refs[2 * nb:]
        x, y, c = _position()
        me, sibling = (x, y, c), (x, y, 1 - c)
        chips = [(1 - x, y), (x, 1 - y), (1 - x, 1 - y)]

        def copy(b, k, block, to, src=None):
            dst = outs[b].at[_slot(*block)]
            return pltpu.make_async_remote_copy(
                src_ref=dst if src is None else src, dst_ref=dst, send_sem=send_sems.at[b, k],
                recv_sem=recv_sems.at[b, k], device_id=to, device_id_type=MESH)

        mine, first, passed = [], [], []
        for b in range(nb):
            cp = pltpu.make_async_copy(ins[b], outs[b].at[_slot(*me)], local_sems.at[b])
            cp.start()
            mine.append(cp)
            first.append(copy(b, 0, me, sibling, src=ins[b]))
            first += [copy(b, 1 + j, me, (*chip, c), src=ins[b]) for j, chip in enumerate(chips)]
        for cp in first:
            cp.start()
        for b in range(nb):
            for j, chip in enumerate(chips):
                copy(b, 1 + j, (*chip, c), me).wait_recv()
                cp = copy(b, 4 + j, (*chip, c), sibling)
                cp.start()
                passed.append(cp)
        for b in range(nb):
            copy(b, 0, sibling, me).wait_recv()
            for j, chip in enumerate(chips):
                copy(b, 4 + j, (*chip, 1 - c), me).wait_recv()
        for cp in first + passed:
            cp.wait_send()
        for cp in mine:
            cp.wait()

    any_spec = pl.BlockSpec(memory_space=pl.ANY)
    return pl.pallas_call(
        body, name=name, in_specs=[any_spec] * nb, out_specs=[any_spec] * nb,
        out_shape=[_sds((N_DEV,) + b.shape, b.dtype) for b in blocks],
        scratch_shapes=[pltpu.SemaphoreType.DMA((nb, 7)), pltpu.SemaphoreType.DMA((nb, 7)),
                        pltpu.SemaphoreType.DMA((nb,))])(*blocks)


def _all_to_all_call(blocks, name):
    nb = len(blocks)

    def body(*refs):
        ins, outs = refs[:nb], refs[nb:2 * nb]
        send_sems, recv_sems, local_sems = refs[2 * nb:]
        x, y, c = _position()
        flip = lambda v, bit: 1 - v if bit else v
        peers = [(flip(x, k >> 2 & 1), flip(y, k >> 1 & 1), flip(c, k & 1)) for k in range(1, N_DEV)]
        copies, mine = [], []
        for b in range(nb):
            cp = pltpu.make_async_copy(ins[b].at[_slot(x, y, c)], outs[b].at[_slot(x, y, c)], local_sems.at[b])
            cp.start()
            mine.append(cp)
            for k, peer in enumerate(peers):
                cp = pltpu.make_async_remote_copy(
                    src_ref=ins[b].at[_slot(*peer)], dst_ref=outs[b].at[_slot(x, y, c)], send_sem=send_sems.at[b, k],
                    recv_sem=recv_sems.at[b, k], device_id=peer, device_id_type=MESH)
                cp.start()
                copies.append((b, k, peer, cp))
        for b, k, peer, cp in copies:
            pltpu.make_async_remote_copy(
                src_ref=ins[b].at[_slot(*peer)], dst_ref=outs[b].at[_slot(*peer)], send_sem=send_sems.at[b, k],
                recv_sem=recv_sems.at[b, k], device_id=peer, device_id_type=MESH).wait_recv()
        for b, k, peer, cp in copies:
            cp.wait_send()
        for cp in mine:
            cp.wait()

    any_spec = pl.BlockSpec(memory_space=pl.ANY)
    return pl.pallas_call(
        body, name=name, in_specs=[any_spec] * nb, out_specs=[any_spec] * nb,
        out_shape=[_sds(b.shape, b.dtype) for b in blocks],
        scratch_shapes=[pltpu.SemaphoreType.DMA((nb, 7)), pltpu.SemaphoreType.DMA((nb, 7)),
                        pltpu.SemaphoreType.DMA((nb,))])(*blocks)


def _adamw_call(parts, w, m, v, name, tr, row0=0, lead=None):
    rows, cols = w.shape

    def body(p_ref, w_ref, m_ref, v_ref, g_ref, dw_ref, nm_ref, nv_ref):
        g = p_ref[0].astype(F32)
        for k in range(1, N_DEV):
            g = g + p_ref[k].astype(F32)
        g_ref[...] = g
        m_new = B1 * m_ref[...] + (1.0 - B1) * g
        v_new = B2 * v_ref[...] + (1.0 - B2) * (g * g)
        m_hat = m_new / (1.0 - B1 ** STEP)
        v_hat = v_new / (1.0 - B2 ** STEP)
        dw_ref[...] = -LR * (m_hat / (jnp.sqrt(v_hat) + ADAM_EPS) + WD * w_ref[...])
        nm_ref[...] = m_new
        nv_ref[...] = v_new

    if lead is None:
        p_spec = pl.BlockSpec((N_DEV, tr, cols), lambda i: (0, row0 + i, 0))
    else:
        p_spec = pl.BlockSpec((N_DEV, None, tr, cols), lambda i: (0, lead, row0 + i, 0))
    spec = pl.BlockSpec((tr, cols), lambda i: (i, 0))
    return pl.pallas_call(
        body, name=name, grid=(rows // tr,), in_specs=[p_spec, spec, spec, spec], out_specs=[spec] * 4,
        out_shape=[_sds((rows, cols), F32)] * 4, compiler_params=_cparams(1))(parts, w, m, v)


SMALL_ROWS = 16


def _pack_small(ln1, b_in, gn, sinks, ln2, lnf, loss=None):
    row = lambda a: jnp.pad(a.reshape(1, -1), ((0, 0), (0, 1024 - a.size)))
    rows = [row(ln1), jnp.pad(b_in.reshape(-1), (0, 7 * 1024 - b_in.size)).reshape(7, 1024), row(gn), row(sinks),
            row(ln2), row(lnf), row(jnp.zeros((1,), F32) if loss is None else loss.reshape(1))]
    packed = jnp.concatenate(rows, axis=0)
    return jnp.pad(packed, ((0, SMALL_ROWS - packed.shape[0]), (0, 0)))


def _unpack_small(p):
    return (p[0:1], p[1:8].reshape(1, 7168)[:, :6400], p[8:9], p[9:10, :16], p[10:11], p[11], p[12, 0])


def _local_step(x, target, ln1_g, b_in, gn_g, sinks, ln2_g, lnf_g, w_in, w_ro, w_ao, w_o, w_g, w_u, w_d):
    s = x.shape[0]
    tm = min(256, s)
    w_ret, w_attn = w_in[:, 0:2048], w_in[:, 3072:4352]
    w_gate = jnp.concatenate([w_in[:, 2048:3072], w_in[:, 4352:6400]], axis=1)
    b_ret, b_attn = b_in[:, 0:2048], b_in[:, 3072:4352]
    b_gate = jnp.concatenate([b_in[:, 2048:3072], b_in[:, 4352:6400]], axis=1)
    cos_r, sin_r = _rope_tables(s, RET_DK // 2)
    cos_a, sin_a = _rope_tables(s, HEAD_DIM // 2)
    decays = _retention_decays()

    h = _ln_call(x, ln1_g, tm)
    proj_ret = _mm_bias(h, w_ret, b_ret, "proj_ret", min(512, s), 1024)
    proj_attn = _mm_bias(h, w_attn, b_attn, "proj_attn", min(512, s), 1280)
    proj_gate = _mm_bias(h, w_gate, b_gate, "proj_gate", min(512, s), 1024)
    ry, qr, kr, states = _ret_fwd_call(proj_ret, cos_r, sin_r, decays)
    ay, aqr, akr, avb = _attn_fwd_call(proj_attn, sinks, cos_a, sin_a)
    a_in, br_a, br_b, merged, x1 = _merge_fwd_call(x, ry, proj_gate, ay, gn_g, w_ro, w_ao, w_o, tm)
    h2, g, u, f, dx2, loss, d_lnf = _ffn_fwd_call(x1, target, ln2_g, lnf_g, w_g, w_u, w_d, tm)

    dx1, dg, du, d_ln2 = _ffn_bwd_call(dx2, x1, g, u, ln2_g, w_g, w_u, w_d, tm)
    tk = min(512, s)
    dw_d = _mm_tn(f, dx2, "dw_ffn_down", 1408, 1024, tk)
    dw_g = _mm_tn(h2, dg, "dw_ffn_gate", 1024, 1408, tk)
    dw_u = _mm_tn(h2, du, "dw_ffn_up", 1024, 1408, tk)
    d_a, d_b, dp_gate, day, dry, db_gate, d_gn = _merge_bwd_call(dx1, br_a, br_b, ry, proj_gate, gn_g, w_ro, w_ao, w_o, tm)
    dw_o = _mm_tn(merged, dx1, "dw_out", 1024, 1024, tk)
    dw_ro = _mm_tn(a_in, d_a, "dw_ret_out", 1024, 1024, tk)
    dw_ao = _mm_tn(ay, d_b, "dw_attn_out", 1024, 1024, tk)
    dq, dkv_cur, dkv_prev, d_sinks, db_aq = _attn_bwd_call(aqr, akr, avb, day, sinks, cos_a, sin_a)
    dp_attn, db_akv = _attn_combine_call(dq, dkv_cur, dkv_prev, cos_a, sin_a)
    dp_ret, db_ret = _ret_bwd_call(qr, kr, proj_ret, states, dry, cos_r, sin_r, decays)
    dx, d_ln1 = _dx_call(x, dx1, dp_ret, dp_attn, dp_gate, ln1_g, w_ret, w_attn, w_gate, tm)
    dw_ret = _mm_tn(h, dp_ret, "dw_in_ret", 1024, 1024, tk)
    dw_attn = _mm_tn(h, dp_attn, "dw_in_attn", 1024, 1280, tk)
    dw_gate = _mm_tn(h, dp_gate, "dw_in_gate", 1024, 1024, tk)

    dw_in = jnp.concatenate([dw_ret, dw_gate[:, 0:1024], dw_attn, dw_gate[:, 1024:3072]], axis=1)
    db_in = jnp.concatenate([db_ret, db_gate[:, 0:1024], db_aq, db_akv, db_gate[:, 1024:3072]], axis=1)
    return (loss[0, 0], dx, dw_in, dw_ro, dw_ao, dw_o, dw_g, dw_u, dw_d,
            (d_ln1, db_in, d_gn, d_sinks[:, 0:16], d_ln2, d_lnf))


def kernel(x, ln1_g, w_in, b_in, ret_norm_g, w_ret_out, attn_sinks, w_attn_out, w_out, ln2_g, w_ffn_gate, w_ffn_up, w_ffn_down, lnf_g, loss_target, m_ln1_g, m_w_in, m_b_in, m_ret_norm_g, m_w_ret_out, m_attn_sinks, m_w_attn_out, m_w_out, m_ln2_g, m_w_ffn_gate, m_w_ffn_up, m_w_ffn_down, m_lnf_g, v_ln1_g, v_w_in, v_b_in, v_ret_norm_g, v_w_ret_out, v_attn_sinks, v_w_attn_out, v_w_out, v_ln2_g, v_w_ffn_gate, v_w_ffn_up, v_w_ffn_down, v_lnf_g):
    cast = lambda a: a.astype(_MXU)
    r_sq = w_ret_out.shape[1]
    r_dn = w_ffn_down.shape[1]
    c_in = w_in.shape[2]
    c_ff = w_ffn_gate.shape[2]

    blk_in = cast(w_in[0])
    blk_ff = cast(jnp.stack([w_ffn_gate[0], w_ffn_up[0]]))
    blk_rows = cast(jnp.concatenate([w_ret_out[0], w_attn_out[0], w_out[0], w_ffn_down[0]], axis=0))
    all_in, all_ff, all_rows = _all_gather_call([blk_in, blk_ff, blk_rows], "gather_weights")
    full_in = all_in.transpose(1, 0, 2).reshape(D, N_DEV * c_in)
    full_g = all_ff[:, 0].transpose(1, 0, 2).reshape(D, N_DEV * c_ff)
    full_u = all_ff[:, 1].transpose(1, 0, 2).reshape(D, N_DEV * c_ff)
    full_ro = all_rows[:, 0:r_sq].reshape(N_DEV * r_sq, D)
    full_ao = all_rows[:, r_sq:2 * r_sq].reshape(N_DEV * r_sq, D)
    full_o = all_rows[:, 2 * r_sq:3 * r_sq].reshape(N_DEV * r_sq, D)
    full_d = all_rows[:, 3 * r_sq:].reshape(N_DEV * r_dn, D)

    loss, dx, dw_in, dw_ro, dw_ao, dw_o, dw_g, dw_u, dw_d, small = _local_step(
        x[0], loss_target[0], ln1_g, b_in, ret_norm_g, attn_sinks, ln2_g, lnf_g.reshape(1, D),
        full_in, full_ro, full_ao, full_o, full_g, full_u, full_d)

    send_in = cast(dw_in.reshape(D, N_DEV, c_in).transpose(1, 0, 2))
    send_ff = cast(jnp.stack([dw_g.reshape(D, N_DEV, c_ff), dw_u.reshape(D, N_DEV, c_ff)]).transpose(2, 0, 1, 3))
    send_rows = cast(jnp.concatenate(
        [dw_ro.reshape(N_DEV, r_sq, D), dw_ao.reshape(N_DEV, r_sq, D), dw_o.reshape(N_DEV, r_sq, D),
         dw_d.reshape(N_DEV, r_dn, D)], axis=1))
    got_in, got_ff, got_rows = _all_to_all_call([send_in, send_ff, send_rows], "exchange_grads")
    (got_small,) = _all_gather_call([_pack_small(*small, loss=loss)], "gather_small")

    tr = 32
    res = {}
    res["w_in"] = _adamw_call(got_in, w_in[0], m_w_in[0], v_w_in[0], "adamw_w_in", 128)
    res["w_ffn_gate"] = _adamw_call(got_ff, w_ffn_gate[0], m_w_ffn_gate[0], v_w_ffn_gate[0], "adamw_ffn_gate", 128, lead=0)
    res["w_ffn_up"] = _adamw_call(got_ff, w_ffn_up[0], m_w_ffn_up[0], v_w_ffn_up[0], "adamw_ffn_up", 128, lead=1)
    res["w_ret_out"] = _adamw_call(got_rows, w_ret_out[0], m_w_ret_out[0], v_w_ret_out[0], "adamw_ret_out", tr, row0=0)
    res["w_attn_out"] = _adamw_call(got_rows, w_attn_out[0], m_w_attn_out[0], v_w_attn_out[0], "adamw_attn_out", tr,
                                    row0=r_sq // tr)
    res["w_out"] = _adamw_call(got_rows, w_out[0], m_w_out[0], v_w_out[0], "adamw_out", tr, row0=2 * r_sq // tr)
    res["w_ffn_down"] = _adamw_call(got_rows, w_ffn_down[0], m_w_ffn_down[0], v_w_ffn_down[0], "adamw_ffn_down", tr,
                                    row0=3 * r_sq // tr)
    pk = lambda ln1, b, gn, sk, ln2, lnf: _pack_small(ln1, b, gn, sk, ln2, lnf)
    small_res = _adamw_call(got_small, pk(ln1_g, b_in, ret_norm_g, attn_sinks, ln2_g, lnf_g),
                            pk(m_ln1_g, m_b_in, m_ret_norm_g, m_attn_sinks, m_ln2_g, m_lnf_g),
                            pk(v_ln1_g, v_b_in, v_ret_norm_g, v_attn_sinks, v_ln2_g, v_lnf_g), "adamw_small", SMALL_ROWS)
    small_names = ["ln1_g", "b_in", "ret_norm_g", "attn_sinks", "ln2_g", "lnf_g"]
    loss_total = None
    for kind, packed in enumerate(small_res):
        parts = _unpack_small(packed)
        if kind == 0:
            loss_total = parts[6]
        for nm, val in zip(small_names, parts[:6]):
            res.setdefault(nm, [None] * 4)[kind] = val

    order = ["ln1_g", "w_in", "b_in", "ret_norm_g", "w_ret_out", "attn_sinks", "w_attn_out", "w_out", "ln2_g",
             "w_ffn_gate", "w_ffn_up", "w_ffn_down", "lnf_g"]
    outs = [loss_total, dx[None]]
    for kind in range(4):
        for nm in order:
            val = res[nm][kind]
            outs.append(val[None] if nm.startswith("w_") else val)
    return tuple(outs)
```

```python
import math

import numpy as np
import jax
import jax.numpy as jnp
from jax import lax
from jax.experimental import pallas as pl
from jax.experimental.pallas import tpu as pltpu

F32 = jnp.float32
_MXU = jnp.bfloat16

N_DEV = 8
D = 1024
RET_HEADS, RET_DK, RET_DV = 4, 128, 256
BLK = 128
Q_HEADS, KV_HEADS, HEAD_DIM = 16, 2, 64
GROUP = Q_HEADS // KV_HEADS
D_FF = 2816
N_RET, N_GATE, N_ATTN = 2048, 3072, 1280
D_IN = N_RET + N_GATE + N_ATTN
ROPE_THETA = 10000.0
EPS = 1e-6
RET_SCALE = RET_DK ** -0.5
ATTN_SCALE = HEAD_DIM ** -0.5
LR, B1, B2, ADAM_EPS, WD, STEP = 0.001, 0.9, 0.999, 1e-08, 0.01, 10
VMEM_LIMIT_MB = 56
MESH = pl.DeviceIdType.MESH


def _dot(a, b):
    return jnp.dot(a.astype(_MXU), b.astype(_MXU), preferred_element_type=F32)


def _dot_nt(a, b):
    return lax.dot_general(a.astype(_MXU), b.astype(_MXU), (((1,), (1,)), ((), ())), preferred_element_type=F32)


def _dot_tn(a, b):
    return lax.dot_general(a.astype(_MXU), b.astype(_MXU), (((0,), (0,)), ((), ())), preferred_element_type=F32)


def _sigmoid(x):
    return 1.0 / (1.0 + jnp.exp(-x))


def _cparams(n_axes, big=False):
    kw = dict(dimension_semantics=("arbitrary",) * n_axes)
    if big:
        kw["vmem_limit_bytes"] = VMEM_LIMIT_MB * 2**20
    return pltpu.CompilerParams(**kw)


def _rows(tm, width, col=0):
    return pl.BlockSpec((tm, width), lambda i: (i, col))


def _const(shape):
    nd = len(shape)
    return pl.BlockSpec(shape, lambda *_: (0,) * nd, pipeline_mode=pl.Buffered(1))


def _acc(width):
    return pl.BlockSpec((1, width), lambda *_: (0, 0))


def _sds(shape, dtype):
    return jax.ShapeDtypeStruct(shape, dtype)


def _swap_halves(x, half):
    w = x.shape[-1]
    if 2 * half == w:
        return pltpu.roll(x, half, 1)
    lane = lax.broadcasted_iota(jnp.int32, x.shape, 1)
    return jnp.where(lane % (2 * half) < half, pltpu.roll(x, w - half, 1), pltpu.roll(x, half, 1))


def _rope_tables(seq, half):
    inv_freq = ROPE_THETA ** (-jnp.arange(half, dtype=F32) / half)
    ang = jnp.arange(seq, dtype=jnp.int32).astype(F32)[:, None] * inv_freq[None, :]
    cos, sin = jnp.cos(ang), jnp.sin(ang)
    reps = 128 // (2 * half)
    return jnp.tile(jnp.concatenate([cos, cos], 1), (1, reps)), jnp.tile(jnp.concatenate([-sin, sin], 1), (1, reps))


def _retention_decays():
    log_gamma = np.log1p(-np.exp2(-5.0 - np.arange(RET_HEADS, dtype=np.float32))).astype(np.float32)
    idx = np.arange(BLK, dtype=np.float32)
    rel = idx[:, None] - idx[None, :]
    intra = np.where(rel[None] >= 0, np.exp(log_gamma[:, None, None] * np.maximum(rel, 0.0)[None]), 0.0)
    q_decay = np.exp(log_gamma[:, None] * (idx + 1.0))[:, :, None]
    k_decay = np.exp(log_gamma[:, None] * (BLK - 1.0 - idx))[:, :, None]
    chunk_decay = [float(np.exp(np.float32(lg * BLK))) for lg in log_gamma]
    return (jnp.asarray(intra, F32), jnp.asarray(q_decay, F32), jnp.asarray(k_decay, F32), chunk_decay)


def _position():
    return lax.axis_index("x"), lax.axis_index("y"), lax.axis_index("c")


def _slot(px, py, pc):
    return 4 * px + 2 * py + pc


class _AllGather:
    def __init__(self, blocks):
        self.blocks = list(blocks)
        nb = len(self.blocks)
        self.out_shape = [_sds((N_DEV,) + b.shape, b.dtype) for b in self.blocks]
        self.scratch = [pltpu.SemaphoreType.DMA((nb, 7)), pltpu.SemaphoreType.DMA((nb, 7)),
                        pltpu.SemaphoreType.DMA((nb,))]

    def phases(self, ins, outs, send_sems, recv_sems, local_sems):
        nb = len(ins)
        x, y, c = _position()
        me, sibling = (x, y, c), (x, y, 1 - c)
        chips = [(1 - x, y), (x, 1 - y), (1 - x, 1 - y)]

        def copy(b, k, block, to, src=None):
            dst = outs[b].at[_slot(*block)]
            return pltpu.make_async_remote_copy(
                src_ref=dst if src is None else src, dst_ref=dst, send_sem=send_sems.at[b, k],
                recv_sem=recv_sems.at[b, k], device_id=to, device_id_type=MESH)

        def own(b):
            return pltpu.make_async_copy(ins[b], outs[b].at[_slot(*me)], local_sems.at[b])

        def first(b):
            return [copy(b, 0, me, sibling, src=ins[b])] + [
                copy(b, 1 + j, me, (*chip, c), src=ins[b]) for j, chip in enumerate(chips)]

        def start():
            for b in range(nb):
                own(b).start()
                for cp in first(b):
                    cp.start()

        def forward():
            for b in range(nb):
                for j, chip in enumerate(chips):
                    copy(b, 1 + j, (*chip, c), me).wait_recv()
                    copy(b, 4 + j, (*chip, c), sibling).start()

        def finish():
            for b in range(nb):
                copy(b, 0, sibling, me).wait_recv()
                for j, chip in enumerate(chips):
                    copy(b, 4 + j, (*chip, 1 - c), me).wait_recv()
            for b in range(nb):
                for cp in first(b):
                    cp.wait_send()
                for j, chip in enumerate(chips):
                    copy(b, 4 + j, (*chip, c), sibling).wait_send()
                own(b).wait()

        return start, forward, finish


class _AllToAll:
    def __init__(self, blocks):
        self.blocks = list(blocks)
        nb = len(self.blocks)
        self.out_shape = [_sds(b.shape, b.dtype) for b in self.blocks]
        self.scratch = [pltpu.SemaphoreType.DMA((nb, 7)), pltpu.SemaphoreType.DMA((nb, 7)),
                        pltpu.SemaphoreType.DMA((nb,))]

    def phases(self, ins, outs, send_sems, recv_sems, local_sems):
        nb = len(ins)
        x, y, c = _position()
        flip = lambda v, bit: 1 - v if bit else v
        peers = [(flip(x, k >> 2 & 1), flip(y, k >> 1 & 1), flip(c, k & 1)) for k in range(1, N_DEV)]

        def copy(b, k, peer, landed=False):
            return pltpu.make_async_remote_copy(
                src_ref=ins[b].at[_slot(*peer)], dst_ref=outs[b].at[_slot(*peer) if landed else _slot(x, y, c)],
                send_sem=send_sems.at[b, k], recv_sem=recv_sems.at[b, k], device_id=peer, device_id_type=MESH)

        def own(b):
            return pltpu.make_async_copy(ins[b].at[_slot(x, y, c)], outs[b].at[_slot(x, y, c)], local_sems.at[b])

        def start():
            for b in range(nb):
                own(b).start()
                for k, peer in enumerate(peers):
                    copy(b, k, peer).start()

        def forward():
            pass

        def finish():
            for b in range(nb):
                for k, peer in enumerate(peers):
                    copy(b, k, peer, landed=True).wait_recv()
            for b in range(nb):
                for k, peer in enumerate(peers):
                    copy(b, k, peer).wait_send()
                own(b).wait()

        return start, forward, finish


def _call(body, *, name, grid, in_specs, out_specs, out_shape, args, scratch_shapes=(), big=False, exchange=None):
    params = _cparams(len(grid), big)
    if exchange is None:
        return pl.pallas_call(body, name=name, grid=grid, in_specs=in_specs, out_specs=out_specs, out_shape=out_shape,
                              scratch_shapes=list(scratch_shapes), compiler_params=params)(*args)
    n_in, n_out, n_scr, nb = len(in_specs), len(out_specs), len(scratch_shapes), len(exchange.blocks)
    steps = math.prod(grid)

    def carried(*refs):
        pos = 0
        parts = []
        for n in (n_in, nb, n_out, nb, n_scr, len(exchange.scratch)):
            parts.append(refs[pos:pos + n])
            pos += n
        ins, x_ins, outs, x_outs, scr, sems = parts
        step = pl.program_id(0)
        for axis in range(1, len(grid)):
            step = step * grid[axis] + pl.program_id(axis)
        start, forward, finish = exchange.phases(x_ins, x_outs, *sems)
        pl.when(step == 0)(start)
        body(*ins, *outs, *scr)
        pl.when(step == steps // 2)(forward)
        pl.when(step == steps - 1)(finish)

    any_spec = pl.BlockSpec(memory_space=pl.ANY)
    res = pl.pallas_call(
        carried, name=name, grid=grid, in_specs=list(in_specs) + [any_spec] * nb,
        out_specs=list(out_specs) + [any_spec] * nb, out_shape=list(out_shape) + exchange.out_shape,
        scratch_shapes=list(scratch_shapes) + exchange.scratch, compiler_params=params)(*args, *exchange.blocks)
    return res[:n_out], res[n_out:]


def _exchange_call(exchange, name):
    nb = len(exchange.blocks)

    def body(*refs):
        start, forward, finish = exchange.phases(refs[:nb], refs[nb:2 * nb], *refs[2 * nb:])
        start()
        forward()
        finish()

    any_spec = pl.BlockSpec(memory_space=pl.ANY)
    return pl.pallas_call(body, name=name, in_specs=[any_spec] * nb, out_specs=[any_spec] * nb,
                          out_shape=exchange.out_shape, scratch_shapes=exchange.scratch)(*exchange.blocks)


def _ln_call(x, g, tm, exchange):
    s = x.shape[0]

    def body(x_ref, g_ref, h_ref):
        xv = x_ref[...]
        r = lax.rsqrt(jnp.mean(xv * xv, axis=-1, keepdims=True) + EPS)
        h_ref[...] = ((xv * r) * g_ref[...]).astype(h_ref.dtype)

    return _call(body, name="ln1", grid=(s // tm,), in_specs=[_rows(tm, D), _acc(D)], out_specs=[_rows(tm, D)],
                 out_shape=[_sds((s, D), _MXU)], args=(x, g), exchange=exchange)


def _proj_call(a, w, bias, tm, tn, exchange):
    s, k = a.shape
    n = w.shape[1]

    def body(a_ref, w_ref, b_ref, o_ref):
        o_ref[...] = _dot(a_ref[...], w_ref[...]) + b_ref[...]

    return _call(body, name="proj", grid=(n // tn, s // tm),
                 in_specs=[pl.BlockSpec((tm, k), lambda j, i: (i, 0)), pl.BlockSpec((k, tn), lambda j, i: (0, j)),
                           pl.BlockSpec((1, tn), lambda j, i: (0, j))],
                 out_specs=[pl.BlockSpec((tm, tn), lambda j, i: (i, j))], out_shape=[_sds((s, n), F32)],
                 args=(a, w, bias), exchange=exchange)


def _mm_tn(a, b, name, tm, tn, tk):
    s, m = a.shape
    n = b.shape[1]

    def body(a_ref, b_ref, o_ref):
        @pl.when(pl.program_id(2) == 0)
        def _():
            o_ref[...] = jnp.zeros_like(o_ref)

        o_ref[...] += _dot_tn(a_ref[...], b_ref[...])

    return _call(body, name=name, grid=(m // tm, n // tn, s // tk),
                 in_specs=[pl.BlockSpec((tk, tm), lambda i, j, k: (k, i)), pl.BlockSpec((tk, tn), lambda i, j, k: (k, j))],
                 out_specs=pl.BlockSpec((tm, tn), lambda i, j, k: (i, j)), out_shape=_sds((m, n), F32),
                 args=(a, b), big=True)


def _ret_fwd_call(proj, cos, sin, decays):
    s = proj.shape[0]
    nblk = s // BLK
    intra, q_decay, k_decay, chunk_decay = decays

    def body(rq_ref, rk_ref, rv_ref, cos_ref, sin_ref, intra_ref, qd_ref, kd_ref,
             ry_ref, qr_ref, kr_ref, st_ref, state):
        @pl.when(pl.program_id(0) == 0)
        def _():
            state[...] = jnp.zeros_like(state)

        cos_v, sin_v = cos_ref[...], sin_ref[...]
        for h in range(RET_HEADS):
            hk = slice(h * RET_DK, (h + 1) * RET_DK)
            hv = slice(h * RET_DV, (h + 1) * RET_DV)
            q, k = rq_ref[:, hk], rk_ref[:, hk]
            qr = (q * cos_v + _swap_halves(q, RET_DK // 2) * sin_v) * RET_SCALE
            kr = k * cos_v + _swap_halves(k, RET_DK // 2) * sin_v
            v = rv_ref[:, hv]
            s_h = state[h]
            st_ref[0, h] = s_h.astype(st_ref.dtype)
            scores = _dot_nt(qr, kr) * intra_ref[h]
            ry_ref[:, hv] = _dot(scores, v) + _dot(qr, s_h) * qd_ref[h]
            state[h] = s_h * chunk_decay[h] + _dot_tn(kr * kd_ref[h], v)
            qr_ref[:, hk] = qr.astype(qr_ref.dtype)
            kr_ref[:, hk] = kr.astype(kr_ref.dtype)

    blk = lambda w, c: pl.BlockSpec((BLK, w), lambda n: (n, c))
    return _call(body, name="ret_fwd", grid=(nblk,),
                 in_specs=[blk(512, 0), blk(512, 1), blk(1024, 1), blk(128, 0), blk(128, 0),
                           _const(intra.shape), _const(q_decay.shape), _const(k_decay.shape)],
                 out_specs=[blk(1024, 0), blk(512, 0), blk(512, 0),
                            pl.BlockSpec((1, RET_HEADS, RET_DK, RET_DV), lambda n: (n, 0, 0, 0))],
                 out_shape=[_sds((s, 1024), F32), _sds((s, 512), _MXU), _sds((s, 512), _MXU),
                            _sds((nblk, RET_HEADS, RET_DK, RET_DV), _MXU)],
                 scratch_shapes=[pltpu.VMEM((RET_HEADS, RET_DK, RET_DV), F32)],
                 args=(proj, proj, proj, cos, sin, intra, q_decay, k_decay))


def _ret_bwd_call(qr, kr, proj, states, dry, cos, sin, decays):
    s = qr.shape[0]
    nblk = s // BLK
    intra, q_decay, k_decay, chunk_decay = decays

    def body(qr_ref, kr_ref, rv_ref, st_ref, dry_ref, cos_ref, sin_ref, intra_ref, qd_ref, kd_ref,
             dp_ref, db_ref, dstate):
        @pl.when(pl.program_id(0) == 0)
        def _():
            dstate[...] = jnp.zeros_like(dstate)
            db_ref[...] = jnp.zeros_like(db_ref)

        cos_v, sin_v = cos_ref[...], sin_ref[...]
        for h in range(RET_HEADS):
            hk = slice(h * RET_DK, (h + 1) * RET_DK)
            hv = slice(h * RET_DV, (h + 1) * RET_DV)
            q, k, v, d_out = qr_ref[:, hk], kr_ref[:, hk], rv_ref[:, hv], dry_ref[:, hv]
            d_next = dstate[h]
            scores = _dot_nt(q, k) * intra_ref[h]
            d_scores = _dot_nt(d_out, v) * intra_ref[h]
            d_cross = d_out * qd_ref[h]
            dq = _dot(d_scores, k) + _dot_nt(d_cross, st_ref[0, h])
            dk = _dot_tn(d_scores, q) + _dot_nt(v, d_next) * kd_ref[h]
            dv = _dot_tn(scores, d_out) + _dot(k.astype(F32) * kd_ref[h], d_next)
            dstate[h] = d_next * chunk_decay[h] + _dot_tn(q, d_cross)
            dq = (dq * cos_v - _swap_halves(dq, RET_DK // 2) * sin_v) * RET_SCALE
            dk = dk * cos_v - _swap_halves(dk, RET_DK // 2) * sin_v
            dp_ref[:, hk] = dq.astype(dp_ref.dtype)
            dp_ref[:, slice(512 + h * RET_DK, 512 + (h + 1) * RET_DK)] = dk.astype(dp_ref.dtype)
            dp_ref[:, slice(1024 + h * RET_DV, 1024 + (h + 1) * RET_DV)] = dv.astype(dp_ref.dtype)
            db_ref[:, hk] += jnp.sum(dq, axis=0, keepdims=True)
            db_ref[:, slice(512 + h * RET_DK, 512 + (h + 1) * RET_DK)] += jnp.sum(dk, axis=0, keepdims=True)
            db_ref[:, slice(1024 + h * RET_DV, 1024 + (h + 1) * RET_DV)] += jnp.sum(dv, axis=0, keepdims=True)

    rblk = lambda w, c: pl.BlockSpec((BLK, w), lambda n: (nblk - 1 - n, c))
    return _call(body, name="ret_bwd", grid=(nblk,),
                 in_specs=[rblk(512, 0), rblk(512, 0), rblk(1024, 1),
                           pl.BlockSpec((1, RET_HEADS, RET_DK, RET_DV), lambda n: (nblk - 1 - n, 0, 0, 0)),
                           rblk(1024, 0), rblk(128, 0), rblk(128, 0),
                           _const(intra.shape), _const(q_decay.shape), _const(k_decay.shape)],
                 out_specs=[rblk(N_RET, 0), _acc(N_RET)],
                 out_shape=[_sds((s, N_RET), _MXU), _sds((1, N_RET), F32)],
                 scratch_shapes=[pltpu.VMEM((RET_HEADS, RET_DK, RET_DV), F32)],
                 args=(qr, kr, proj, states, dry, cos, sin, intra, q_decay, k_decay))


def _both_halves(x, g):
    lane = lax.broadcasted_iota(jnp.int32, x.shape, 1)
    keep = lane < HEAD_DIM if g == 0 else lane >= HEAD_DIM
    return jnp.where(keep, x, pltpu.roll(x, HEAD_DIM, 1))


def _stack_heads(ref, g):
    lane = lax.broadcasted_iota(jnp.int32, (BLK, 128), 1)
    pieces = []
    for j in range(g * 4, g * 4 + 4):
        chunk = ref[:, j * 128:(j + 1) * 128]
        pieces += [jnp.where(lane < HEAD_DIM, chunk, jnp.zeros_like(chunk)),
                   jnp.where(lane >= HEAD_DIM, chunk, jnp.zeros_like(chunk))]
    return jnp.concatenate(pieces, axis=0)


def _unstack_heads(stacked, jj):
    lane = lax.broadcasted_iota(jnp.int32, (BLK, 128), 1)
    return jnp.where(lane < HEAD_DIM, stacked[2 * jj * BLK:(2 * jj + 1) * BLK], stacked[(2 * jj + 1) * BLK:(2 * jj + 2) * BLK])


def _window_mask(first_block):
    qi = lax.broadcasted_iota(jnp.int32, (1, BLK, 2 * BLK), 1)
    kj = lax.broadcasted_iota(jnp.int32, (1, BLK, 2 * BLK), 2)
    first_key = jnp.where(first_block, BLK, 0)
    return (kj > qi) & (kj <= qi + BLK) & (kj >= first_key)


def _sink_softmax(scores, mask, sink):
    scores = jnp.where(mask, scores, -1e30)
    m = jnp.maximum(jnp.max(scores, axis=-1, keepdims=True), sink)
    e = jnp.exp(scores - m)
    e_sink = jnp.exp(sink - m)
    inv = 1.0 / (jnp.sum(e, axis=-1, keepdims=True) + e_sink)
    return e * inv, e_sink * inv


def _attn_fwd_call(proj, sinks, cos, sin):
    s = proj.shape[0]
    nblk = s // BLK

    def body(sink_ref, q_ref, k_ref, v_ref, cos_ref, sin_ref, ay_ref, qr_ref, kr_ref, vb_ref, kwin, vwin):
        n = pl.program_id(0)

        @pl.when(n == 0)
        def _():
            kwin[...] = jnp.zeros_like(kwin)
            vwin[...] = jnp.zeros_like(vwin)

        @pl.when(n > 0)
        def _():
            kwin[0:BLK] = kwin[BLK:2 * BLK]
            vwin[0:BLK] = vwin[BLK:2 * BLK]

        cos_v, sin_v = cos_ref[...], sin_ref[...]
        k = k_ref[...]
        kr = (k * cos_v + _swap_halves(k, HEAD_DIM // 2) * sin_v).astype(kwin.dtype)
        kwin[BLK:2 * BLK] = kr
        vwin[BLK:2 * BLK] = v_ref[...].astype(vwin.dtype)
        kr_ref[...] = kr
        vb_ref[...] = vwin[BLK:2 * BLK]
        for j in range(Q_HEADS // 2):
            cols = slice(j * 128, (j + 1) * 128)
            q = q_ref[:, cols]
            qr_ref[:, cols] = ((q * cos_v + _swap_halves(q, HEAD_DIM // 2) * sin_v) * ATTN_SCALE).astype(qr_ref.dtype)
        mask = _window_mask(n == 0)
        for g in range(KV_HEADS):
            kg = _both_halves(kwin[...], g)
            vg = _both_halves(vwin[...], g)
            scores = _dot_nt(_stack_heads(qr_ref, g), kg).reshape(GROUP, BLK, 2 * BLK)
            p, _ = _sink_softmax(scores, mask, sink_ref[g])
            out = _dot(p.reshape(GROUP * BLK, 2 * BLK), vg)
            for jj in range(4):
                j = g * 4 + jj
                ay_ref[:, j * 128:(j + 1) * 128] = _unstack_heads(out, jj).astype(ay_ref.dtype)

    blk = lambda w, c: pl.BlockSpec((BLK, w), lambda n: (n, c))
    off = (N_RET + N_GATE) // 128
    return _call(body, name="attn_fwd", grid=(nblk,),
                 in_specs=[_const((KV_HEADS, GROUP, 1, 1)), blk(1024, off // 8), blk(128, off + 8), blk(128, off + 9),
                           blk(128, 0), blk(128, 0)],
                 out_specs=[blk(1024, 0), blk(1024, 0), blk(128, 0), blk(128, 0)],
                 out_shape=[_sds((s, 1024), _MXU), _sds((s, 1024), _MXU), _sds((s, 128), _MXU), _sds((s, 128), _MXU)],
                 scratch_shapes=[pltpu.VMEM((2 * BLK, 128), _MXU), pltpu.VMEM((2 * BLK, 128), _MXU)],
                 args=(sinks, proj, proj, proj, cos, sin))


def _attn_bwd_call(qr, kr, vb, day, sinks, cos, sin, exchange):
    s = qr.shape[0]
    nblk = s // BLK

    def body(sink_ref, q_ref, kc_ref, kp_ref, vc_ref, vp_ref, do_ref, cos_ref, sin_ref,
             dq_ref, cur_ref, prev_ref, dsink_ref, db_ref):
        n = pl.program_id(0)

        @pl.when(n == 0)
        def _():
            dsink_ref[...] = jnp.zeros_like(dsink_ref)
            db_ref[...] = jnp.zeros_like(db_ref)

        cos_v, sin_v = cos_ref[...], sin_ref[...]
        mask = _window_mask(n == 0)
        kwin = jnp.concatenate([kp_ref[...], kc_ref[...]], axis=0)
        vwin = jnp.concatenate([vp_ref[...], vc_ref[...]], axis=0)
        dk_t, dv_t = [], []
        for g in range(KV_HEADS):
            kg = _both_halves(kwin, g)
            vg = _both_halves(vwin, g)
            q_all = _stack_heads(q_ref, g)
            do_all = _stack_heads(do_ref, g)
            scores = _dot_nt(q_all, kg).reshape(GROUP, BLK, 2 * BLK)
            p, p_sink = _sink_softmax(scores, mask, sink_ref[g])
            dp = _dot_nt(do_all, vg).reshape(GROUP, BLK, 2 * BLK)
            delta = jnp.sum(p * dp, axis=-1, keepdims=True)
            ds = (p * (dp - delta)).reshape(GROUP * BLK, 2 * BLK)
            dsink_ref[g] += -jnp.sum(p_sink * delta, axis=1, keepdims=True)
            dv_both = _dot_tn(do_all, p.reshape(GROUP * BLK, 2 * BLK))
            dk_both = _dot_tn(q_all, ds)
            dv_t.append(dv_both[0:HEAD_DIM] + dv_both[HEAD_DIM:128])
            dk_t.append(dk_both[0:HEAD_DIM] + dk_both[HEAD_DIM:128])
            dq_all = _dot(ds, kg)
            for jj in range(4):
                cols = slice((g * 4 + jj) * 128, (g * 4 + jj + 1) * 128)
                dq = _unstack_heads(dq_all, jj)
                dq = (dq * cos_v - _swap_halves(dq, HEAD_DIM // 2) * sin_v) * ATTN_SCALE
                dq_ref[:, cols] = dq.astype(dq_ref.dtype)
                db_ref[:, cols] += jnp.sum(dq, axis=0, keepdims=True)
        dk_all = jnp.concatenate(dk_t, axis=0).T
        dv_all = jnp.concatenate(dv_t, axis=0).T
        prev_ref[:, 0:128] = dk_all[0:BLK]
        prev_ref[:, 128:256] = dv_all[0:BLK]
        cur_ref[:, 0:128] = dk_all[BLK:2 * BLK]
        cur_ref[:, 128:256] = dv_all[BLK:2 * BLK]

    blk = lambda w, c: pl.BlockSpec((BLK, w), lambda n: (n, c))
    pblk = lambda w: pl.BlockSpec((BLK, w), lambda n: (jnp.maximum(n - 1, 0), 0))
    return _call(body, name="attn_bwd", grid=(nblk,),
                 in_specs=[_const((KV_HEADS, GROUP, 1, 1)), blk(1024, 0), blk(128, 0), pblk(128), blk(128, 0), pblk(128),
                           blk(1024, 0), blk(128, 0), blk(128, 0)],
                 out_specs=[blk(1024, 0), blk(256, 0), blk(256, 0),
                            pl.BlockSpec((KV_HEADS, GROUP, 1, 1), lambda n: (0, 0, 0, 0)), _acc(1024)],
                 out_shape=[_sds((s, 1024), _MXU), _sds((s, 256), F32), _sds((s, 256), F32),
                            _sds((KV_HEADS, GROUP, 1, 1), F32), _sds((1, 1024), F32)],
                 args=(sinks, qr, kr, kr, vb, vb, day, cos, sin), exchange=exchange)


def _attn_combine_call(dq, dkv_cur, dkv_prev, cos, sin):
    s = dq.shape[0]
    nblk = s // BLK

    def body(dq_ref, cur_ref, nxt_ref, cos_ref, sin_ref, dp_ref, db_ref):
        n = pl.program_id(0)

        @pl.when(n == 0)
        def _():
            db_ref[...] = jnp.zeros_like(db_ref)

        dkv = cur_ref[...] + nxt_ref[...] * (n < nblk - 1).astype(F32)
        dk = dkv[:, 0:128]
        dk = dk * cos_ref[...] - _swap_halves(dk, HEAD_DIM // 2) * sin_ref[...]
        dv = dkv[:, 128:256]
        dp_ref[:, 0:1024] = dq_ref[...]
        dp_ref[:, 1024:1152] = dk.astype(dp_ref.dtype)
        dp_ref[:, 1152:1280] = dv.astype(dp_ref.dtype)
        db_ref[:, 0:128] += jnp.sum(dk, axis=0, keepdims=True)
        db_ref[:, 128:256] += jnp.sum(dv, axis=0, keepdims=True)

    blk = lambda w: pl.BlockSpec((BLK, w), lambda n: (n, 0))
    return _call(body, name="attn_combine", grid=(nblk,),
                 in_specs=[blk(1024), blk(256), pl.BlockSpec((BLK, 256), lambda n: (jnp.minimum(n + 1, nblk - 1), 0)),
                           blk(128), blk(128)],
                 out_specs=[blk(N_ATTN), _acc(256)],
                 out_shape=[_sds((s, N_ATTN), _MXU), _sds((1, 256), F32)],
                 args=(dq, dkv_cur, dkv_prev, cos, sin))


def _group_norm(y):
    mu = jnp.mean(y, axis=-1, keepdims=True)
    yc = y - mu
    rs = lax.rsqrt(jnp.mean(yc * yc, axis=-1, keepdims=True) + EPS)
    return yc * rs, rs


GATE_COL = N_RET // 1024


def _merge_fwd_call(x, ry, proj, ay, gn_g, w_ro, w_ao, w_o, tm):
    s = x.shape[0]

    def body(x_ref, ry_ref, rg_ref, ga_ref, gb_ref, ay_ref, gn_ref, wro_ref, wao_ref, wo_ref,
             ain_ref, a_ref, b_ref, mg_ref, x1_ref):
        for h in range(RET_HEADS):
            hv = slice(h * RET_DV, (h + 1) * RET_DV)
            yhat, _ = _group_norm(ry_ref[:, hv])
            rg = rg_ref[:, hv]
            ain_ref[:, hv] = ((rg * _sigmoid(rg)) * (yhat * gn_ref[:, hv])).astype(ain_ref.dtype)
        a = _dot(ain_ref[...], wro_ref[...])
        b = _dot(ay_ref[...], wao_ref[...])
        a_ref[...] = a
        b_ref[...] = b
        merged = (_sigmoid(ga_ref[...]) * a + _sigmoid(gb_ref[...]) * b).astype(mg_ref.dtype)
        mg_ref[...] = merged
        x1_ref[...] = x_ref[...] + _dot(merged, wo_ref[...])

    return _call(body, name="merge_fwd", grid=(s // tm,),
                 in_specs=[_rows(tm, D), _rows(tm, 1024), _rows(tm, 1024, GATE_COL), _rows(tm, 1024, GATE_COL + 1),
                           _rows(tm, 1024, GATE_COL + 2), _rows(tm, 1024), _acc(1024),
                           _const((D, D)), _const((D, D)), _const((D, D))],
                 out_specs=[_rows(tm, D)] * 5,
                 out_shape=[_sds((s, D), _MXU), _sds((s, D), F32), _sds((s, D), F32), _sds((s, D), _MXU), _sds((s, D), F32)],
                 args=(x, ry, proj, proj, proj, ay, gn_g, w_ro, w_ao, w_o), big=True)


def _merge_bwd_call(dx1, a, b, ry, proj, gn_g, w_ro, w_ao, w_o, tm, exchange):
    s = dx1.shape[0]

    def body(dx1_ref, a_ref, b_ref, ry_ref, rg_ref, ga_ref, gb_ref, gn_ref, wro_ref, wao_ref, wo_ref,
             da_ref, dbr_ref, dp_ref, day_ref, dry_ref, dbias_ref, dgn_ref):
        @pl.when(pl.program_id(0) == 0)
        def _():
            dbias_ref[...] = jnp.zeros_like(dbias_ref)
            dgn_ref[...] = jnp.zeros_like(dgn_ref)

        d_merged = _dot_nt(dx1_ref[...], wo_ref[...])
        sa, sb = _sigmoid(ga_ref[...]), _sigmoid(gb_ref[...])
        d_a = d_merged * sa
        d_b = d_merged * sb
        da_ref[...] = d_a.astype(da_ref.dtype)
        dbr_ref[...] = d_b.astype(dbr_ref.dtype)
        d_ga = d_merged * a_ref[...] * (sa * (1.0 - sa))
        d_gb = d_merged * b_ref[...] * (sb * (1.0 - sb))
        dp_ref[:, 1024:2048] = d_ga.astype(dp_ref.dtype)
        dp_ref[:, 2048:3072] = d_gb.astype(dp_ref.dtype)
        dbias_ref[:, 1024:2048] += jnp.sum(d_ga, axis=0, keepdims=True)
        dbias_ref[:, 2048:3072] += jnp.sum(d_gb, axis=0, keepdims=True)
        day_ref[...] = _dot_nt(d_b, wao_ref[...]).astype(day_ref.dtype)
        d_ain = _dot_nt(d_a, wro_ref[...])
        for h in range(RET_HEADS):
            hv = slice(h * RET_DV, (h + 1) * RET_DV)
            yhat, rs = _group_norm(ry_ref[:, hv])
            rg = rg_ref[:, hv]
            sg = _sigmoid(rg)
            gn = gn_ref[:, hv]
            d_h = d_ain[:, hv]
            d_rg = d_h * (yhat * gn) * (sg * (1.0 + rg * (1.0 - sg)))
            d_ryn = d_h * (rg * sg)
            dgn_ref[:, hv] += jnp.sum(d_ryn * yhat, axis=0, keepdims=True)
            d_yhat = d_ryn * gn
            dry_ref[:, hv] = rs * (d_yhat - jnp.mean(d_yhat, axis=-1, keepdims=True)
                                   - yhat * jnp.mean(d_yhat * yhat, axis=-1, keepdims=True))
            dp_ref[:, hv] = d_rg.astype(dp_ref.dtype)
            dbias_ref[:, hv] += jnp.sum(d_rg, axis=0, keepdims=True)

    return _call(body, name="merge_bwd", grid=(s // tm,),
                 in_specs=[_rows(tm, D), _rows(tm, D), _rows(tm, D), _rows(tm, 1024), _rows(tm, 1024, GATE_COL),
                           _rows(tm, 1024, GATE_COL + 1), _rows(tm, 1024, GATE_COL + 2), _acc(1024),
                           _const((D, D)), _const((D, D)), _const((D, D))],
                 out_specs=[_rows(tm, D), _rows(tm, D), _rows(tm, N_GATE), _rows(tm, D), _rows(tm, D), _acc(N_GATE),
                            _acc(1024)],
                 out_shape=[_sds((s, D), _MXU), _sds((s, D), _MXU), _sds((s, N_GATE), _MXU), _sds((s, D), _MXU),
                            _sds((s, D), F32), _sds((1, N_GATE), F32), _sds((1, 1024), F32)],
                 args=(dx1, a, b, ry, proj, proj, proj, gn_g, w_ro, w_ao, w_o), big=True, exchange=exchange)


def _ffn_fwd_call(x1, target, ln2_g, lnf_g, w_g, w_u, w_d, tm):
    s = x1.shape[0]

    def body(x1_ref, t_ref, g2_ref, gf_ref, wg_ref, wu_ref, wd_ref,
             h2_ref, g_ref, u_ref, f_ref, dx2_ref, loss_ref, dgf_ref):
        @pl.when(pl.program_id(0) == 0)
        def _():
            loss_ref[...] = jnp.zeros_like(loss_ref)
            dgf_ref[...] = jnp.zeros_like(dgf_ref)

        x1v = x1_ref[...]
        r1 = lax.rsqrt(jnp.mean(x1v * x1v, axis=-1, keepdims=True) + EPS)
        h2 = ((x1v * r1) * g2_ref[...]).astype(h2_ref.dtype)
        h2_ref[...] = h2
        g = _dot(h2, wg_ref[...])
        u = _dot(h2, wu_ref[...])
        g_ref[...] = g
        u_ref[...] = u
        f = ((g * _sigmoid(g)) * u).astype(f_ref.dtype)
        f_ref[...] = f
        x2 = x1v + _dot(f, wd_ref[...])
        r2 = lax.rsqrt(jnp.mean(x2 * x2, axis=-1, keepdims=True) + EPS)
        xhat = x2 * r2
        err = xhat * gf_ref[...] - t_ref[...]
        loss_ref[...] += 0.5 * jnp.sum(jnp.mean(err * err, axis=-1, keepdims=True))
        dy = err * (1.0 / D)
        dgf_ref[...] += jnp.sum(dy * xhat, axis=0, keepdims=True)
        dxh = dy * gf_ref[...]
        dx2_ref[...] = r2 * (dxh - xhat * jnp.mean(dxh * xhat, axis=-1, keepdims=True))

    return _call(body, name="ffn_fwd", grid=(s // tm,),
                 in_specs=[_rows(tm, D), _rows(tm, D), _acc(D), _acc(D), _const((D, D_FF)), _const((D, D_FF)),
                           _const((D_FF, D))],
                 out_specs=[_rows(tm, D), _rows(tm, D_FF), _rows(tm, D_FF), _rows(tm, D_FF), _rows(tm, D), _acc(128),
                            _acc(D)],
                 out_shape=[_sds((s, D), _MXU), _sds((s, D_FF), F32), _sds((s, D_FF), F32), _sds((s, D_FF), _MXU),
                            _sds((s, D), F32), _sds((1, 128), F32), _sds((1, D), F32)],
                 args=(x1, target, ln2_g, lnf_g, w_g, w_u, w_d), big=True)


def _ffn_bwd_call(dx2, x1, g, u, ln2_g, w_g, w_u, w_d, tm):
    s = dx2.shape[0]

    def body(dx2_ref, x1_ref, g_ref, u_ref, g2_ref, wg_ref, wu_ref, wd_ref, dx1_ref, dg_ref, du_ref, dg2_ref):
        @pl.when(pl.program_id(0) == 0)
        def _():
            dg2_ref[...] = jnp.zeros_like(dg2_ref)

        dx2v = dx2_ref[...]
        df = _dot_nt(dx2v, wd_ref[...])
        gv, uv = g_ref[...], u_ref[...]
        sg = _sigmoid(gv)
        du = (df * (gv * sg)).astype(du_ref.dtype)
        dg = (df * uv * (sg * (1.0 + gv * (1.0 - sg)))).astype(dg_ref.dtype)
        du_ref[...] = du
        dg_ref[...] = dg
        dh2 = _dot_nt(dg, wg_ref[...]) + _dot_nt(du, wu_ref[...])
        x1v = x1_ref[...]
        r1 = lax.rsqrt(jnp.mean(x1v * x1v, axis=-1, keepdims=True) + EPS)
        xhat = x1v * r1
        dg2_ref[...] += jnp.sum(dh2 * xhat, axis=0, keepdims=True)
        dxh = dh2 * g2_ref[...]
        dx1_ref[...] = dx2v + r1 * (dxh - xhat * jnp.mean(dxh * xhat, axis=-1, keepdims=True))

    return _call(body, name="ffn_bwd", grid=(s // tm,),
                 in_specs=[_rows(tm, D), _rows(tm, D), _rows(tm, D_FF), _rows(tm, D_FF), _acc(D),
                           _const((D, D_FF)), _const((D, D_FF)), _const((D_FF, D))],
                 out_specs=[_rows(tm, D), _rows(tm, D_FF), _rows(tm, D_FF), _acc(D)],
                 out_shape=[_sds((s, D), F32), _sds((s, D_FF), _MXU), _sds((s, D_FF), _MXU), _sds((1, D), F32)],
                 args=(dx2, x1, g, u, ln2_g, w_g, w_u, w_d), big=True)


def _dx_call(x, dx1, dp_ret, dp_gate, dp_attn, ln1_g, w_in, tm, exchange):
    s = x.shape[0]

    def body(x_ref, dx1_ref, dr_ref, dg_ref, da_ref, g1_ref, w_ref, dx_ref, dg1_ref):
        @pl.when(pl.program_id(0) == 0)
        def _():
            dg1_ref[...] = jnp.zeros_like(dg1_ref)

        dh = (_dot_nt(dr_ref[...], w_ref[:, 0:N_RET]) + _dot_nt(dg_ref[...], w_ref[:, N_RET:N_RET + N_GATE])
              + _dot_nt(da_ref[...], w_ref[:, N_RET + N_GATE:D_IN]))
        xv = x_ref[...]
        r = lax.rsqrt(jnp.mean(xv * xv, axis=-1, keepdims=True) + EPS)
        xhat = xv * r
        dg1_ref[...] += jnp.sum(dh * xhat, axis=0, keepdims=True)
        dxh = dh * g1_ref[...]
        dx_ref[...] = dx1_ref[...] + r * (dxh - xhat * jnp.mean(dxh * xhat, axis=-1, keepdims=True))

    return _call(body, name="dx", grid=(s // tm,),
                 in_specs=[_rows(tm, D), _rows(tm, D), _rows(tm, N_RET), _rows(tm, N_GATE), _rows(tm, N_ATTN), _acc(D),
                           _const((D, D_IN))],
                 out_specs=[_rows(tm, D), _acc(D)],
                 out_shape=[_sds((s, D), F32), _sds((1, D), F32)],
                 args=(x, dx1, dp_ret, dp_gate, dp_attn, ln1_g, w_in), big=True, exchange=exchange)


def _adamw(g, w, m, v):
    m_new = B1 * m + (1.0 - B1) * g
    v_new = B2 * v + (1.0 - B2) * (g * g)
    m_hat = m_new / (1.0 - B1 ** STEP)
    v_hat = v_new / (1.0 - B2 ** STEP)
    return -LR * (m_hat / (jnp.sqrt(v_hat) + ADAM_EPS) + WD * w), m_new, v_new


def _slot_sum(p_ref):
    g = p_ref[0].astype(F32)
    for k in range(1, N_DEV):
        g = g + p_ref[k].astype(F32)
    return g


def _adamw_call(parts, w, m, v, name, tr, row0=0, lead=None):
    rows, cols = w.shape

    def body(p_ref, w_ref, m_ref, v_ref, g_ref, dw_ref, nm_ref, nv_ref):
        g = _slot_sum(p_ref)
        g_ref[...] = g
        dw_ref[...], nm_ref[...], nv_ref[...] = _adamw(g, w_ref[...], m_ref[...], v_ref[...])

    if lead is None:
        p_spec = pl.BlockSpec((N_DEV, tr, cols), lambda i: (0, row0 + i, 0))
    else:
        p_spec = pl.BlockSpec((N_DEV, None, tr, cols), lambda i: (0, lead, row0 + i, 0))
    spec = pl.BlockSpec((tr, cols), lambda i: (i, 0))
    return _call(body, name=name, grid=(rows // tr,), in_specs=[p_spec, spec, spec, spec], out_specs=[spec] * 4,
                 out_shape=[_sds((rows, cols), F32)] * 4, args=(parts, w, m, v))


SMALL_WIDTHS = [1024, 6400, 1024, 16, 1024, 1024]
SMALL_OFFSETS = [0, 1024, 7424, 8448, 8576, 9600]
LOSS_OFFSET = 10624
SMALL_LEN = 10752


def _pack_small(grads, loss):
    pieces = []
    for gr, width in zip(grads, SMALL_WIDTHS):
        pieces.append(jnp.pad(gr.reshape(1, width), ((0, 0), (0, -width % 128))))
    pieces.append(jnp.pad(loss.reshape(1, 1), ((0, 0), (0, 127))))
    return jnp.concatenate(pieces, axis=1)


def _adamw_small_call(parts, ws, ms, vs):
    n = len(ws)

    def body(*refs):
        p_ref, w_refs, m_refs, v_refs = refs[0], refs[1:1 + n], refs[1 + n:1 + 2 * n], refs[1 + 2 * n:1 + 3 * n]
        outs = refs[1 + 3 * n:]
        g_all = _slot_sum(p_ref)
        for i, (off, width) in enumerate(zip(SMALL_OFFSETS, SMALL_WIDTHS)):
            g = g_all[:, off:off + width]
            outs[i][...] = g
            outs[n + i][...], outs[2 * n + i][...], outs[3 * n + i][...] = _adamw(
                g, w_refs[i][...], m_refs[i][...], v_refs[i][...])
        outs[4 * n][...] = g_all[:, LOSS_OFFSET:LOSS_OFFSET + 128]

    whole = lambda shape: pl.BlockSpec(shape, lambda i: (0,) * len(shape))
    small = [whole((1, w)) for w in SMALL_WIDTHS]
    res = _call(body, name="adamw_small", grid=(1,), in_specs=[whole((N_DEV, 1, SMALL_LEN))] + small * 3,
                out_specs=small * 4 + [whole((1, 128))],
                out_shape=[_sds((1, w), F32) for w in SMALL_WIDTHS] * 4 + [_sds((1, 128), F32)],
                args=(parts, *ws, *ms, *vs))
    return [res[k * n:(k + 1) * n] for k in range(4)], res[4 * n]


def kernel(x, ln1_g, w_in, b_in, ret_norm_g, w_ret_out, attn_sinks, w_attn_out, w_out, ln2_g, w_ffn_gate, w_ffn_up, w_ffn_down, lnf_g, loss_target, m_ln1_g, m_w_in, m_b_in, m_ret_norm_g, m_w_ret_out, m_attn_sinks, m_w_attn_out, m_w_out, m_ln2_g, m_w_ffn_gate, m_w_ffn_up, m_w_ffn_down, m_lnf_g, v_ln1_g, v_w_in, v_b_in, v_ret_norm_g, v_w_ret_out, v_attn_sinks, v_w_attn_out, v_w_out, v_ln2_g, v_w_ffn_gate, v_w_ffn_up, v_w_ffn_down, v_lnf_g):
    cast = lambda a: a.astype(_MXU)
    xs, target = x[0], loss_target[0]
    s = xs.shape[0]
    r_sq = w_ret_out.shape[1]
    r_dn = w_ffn_down.shape[1]
    c_in = w_in.shape[2]
    c_ff = w_ffn_gate.shape[2]
    tm, tk = min(256, s), min(512, s)
    lnf_row = lnf_g.reshape(1, D)
    sinks = attn_sinks.reshape(KV_HEADS, GROUP, 1, 1)
    cos_r, sin_r = _rope_tables(s, RET_DK // 2)
    cos_a, sin_a = _rope_tables(s, HEAD_DIM // 2)
    decays = _retention_decays()
    regroup = lambda a: jnp.concatenate([a[:, 0:2048], a[:, 2048:3072], a[:, 4352:6400], a[:, 3072:4352]], axis=1)

    (h,), (all_in,) = _ln_call(xs, ln1_g, tm, _AllGather([cast(w_in[0])]))
    w_perm = regroup(all_in.transpose(1, 0, 2).reshape(D, N_DEV * c_in))
    blk_ff = cast(jnp.stack([w_ffn_gate[0], w_ffn_up[0]]))
    blk_rows = cast(jnp.concatenate([w_ret_out[0], w_attn_out[0], w_out[0], w_ffn_down[0]], axis=0))
    (proj,), (all_ff, all_rows) = _proj_call(h, w_perm, regroup(b_in), min(512, s), 1280, _AllGather([blk_ff, blk_rows]))
    full_g = all_ff[:, 0].transpose(1, 0, 2).reshape(D, N_DEV * c_ff)
    full_u = all_ff[:, 1].transpose(1, 0, 2).reshape(D, N_DEV * c_ff)
    full_ro = all_rows[:, 0:r_sq].reshape(N_DEV * r_sq, D)
    full_ao = all_rows[:, r_sq:2 * r_sq].reshape(N_DEV * r_sq, D)
    full_o = all_rows[:, 2 * r_sq:3 * r_sq].reshape(N_DEV * r_sq, D)
    full_d = all_rows[:, 3 * r_sq:].reshape(N_DEV * r_dn, D)
    ry, qr, kr, states = _ret_fwd_call(proj, cos_r, sin_r, decays)
    ay, aqr, akr, avb = _attn_fwd_call(proj, sinks, cos_a, sin_a)
    a_in, br_a, br_b, merged, x1 = _merge_fwd_call(xs, ry, proj, ay, ret_norm_g, full_ro, full_ao, full_o, tm)
    h2, g, u, f, dx2, loss, d_lnf = _ffn_fwd_call(x1, target, ln2_g, lnf_row, full_g, full_u, full_d, tm)

    dx1, dg, du, d_ln2 = _ffn_bwd_call(dx2, x1, g, u, ln2_g, full_g, full_u, full_d, tm)
    dw_d = _mm_tn(f, dx2, "dw_ffn_down", 1408, 1024, tk)
    dw_g = _mm_tn(h2, dg, "dw_ffn_gate", 1024, 1408, tk)
    dw_u = _mm_tn(h2, du, "dw_ffn_up", 1024, 1408, tk)
    send_ff = cast(jnp.stack([dw_g.reshape(D, N_DEV, c_ff), dw_u.reshape(D, N_DEV, c_ff)]).transpose(2, 0, 1, 3))
    (d_a, d_b, dp_gate, day, dry, db_gate, d_gn), (got_ff,) = _merge_bwd_call(
        dx1, br_a, br_b, ry, proj, ret_norm_g, full_ro, full_ao, full_o, tm, _AllToAll([send_ff]))
    dw_o = _mm_tn(merged, dx1, "dw_out", 1024, 1024, tk)
    dw_ro = _mm_tn(a_in, d_a, "dw_ret_out", 1024, 1024, tk)
    dw_ao = _mm_tn(ay, d_b, "dw_attn_out", 1024, 1024, tk)
    send_rows = cast(jnp.concatenate(
        [dw_ro.reshape(N_DEV, r_sq, D), dw_ao.reshape(N_DEV, r_sq, D), dw_o.reshape(N_DEV, r_sq, D),
         dw_d.reshape(N_DEV, r_dn, D)], axis=1))
    (dq, dkv_cur, dkv_prev, d_sinks, db_aq), (got_rows,) = _attn_bwd_call(
        aqr, akr, avb, day, sinks, cos_a, sin_a, _AllToAll([send_rows]))
    dp_attn, db_akv = _attn_combine_call(dq, dkv_cur, dkv_prev, cos_a, sin_a)
    dp_ret, db_ret = _ret_bwd_call(qr, kr, proj, states, dry, cos_r, sin_r, decays)
    dw_ret = _mm_tn(h, dp_ret, "dw_in_ret", 1024, 1024, tk)
    dw_gate = _mm_tn(h, dp_gate, "dw_in_gate", 1024, 1024, tk)
    dw_attn = _mm_tn(h, dp_attn, "dw_in_attn", 1024, 1280, tk)
    dw_in = jnp.concatenate([dw_ret, dw_gate[:, 0:1024], dw_attn, dw_gate[:, 1024:3072]], axis=1)
    send_in = cast(dw_in.reshape(D, N_DEV, c_in).transpose(1, 0, 2))
    (dx, d_ln1), (got_in,) = _dx_call(xs, dx1, dp_ret, dp_gate, dp_attn, ln1_g, w_perm, tm, _AllToAll([send_in]))
    db_in = jnp.concatenate([db_ret, db_gate[:, 0:1024], db_aq, db_akv, db_gate[:, 1024:3072]], axis=1)
    small = [d_ln1, db_in, d_gn, d_sinks.reshape(1, Q_HEADS), d_ln2, d_lnf]
    (got_small,) = _exchange_call(_AllGather([_pack_small(small, loss[0, 0])]), "gather_small")

    tr = 32
    res = {}
    res["w_in"] = _adamw_call(got_in, w_in[0], m_w_in[0], v_w_in[0], "adamw_w_in", 128)
    res["w_ffn_gate"] = _adamw_call(got_ff, w_ffn_gate[0], m_w_ffn_gate[0], v_w_ffn_gate[0], "adamw_ffn_gate", 128, lead=0)
    res["w_ffn_up"] = _adamw_call(got_ff, w_ffn_up[0], m_w_ffn_up[0], v_w_ffn_up[0], "adamw_ffn_up", 128, lead=1)
    res["w_ret_out"] = _adamw_call(got_rows, w_ret_out[0], m_w_ret_out[0], v_w_ret_out[0], "adamw_ret_out", tr, row0=0)
    res["w_attn_out"] = _adamw_call(got_rows, w_attn_out[0], m_w_attn_out[0], v_w_attn_out[0], "adamw_attn_out", tr,
                                    row0=r_sq // tr)
    res["w_out"] = _adamw_call(got_rows, w_out[0], m_w_out[0], v_w_out[0], "adamw_out", tr, row0=2 * r_sq // tr)
    res["w_ffn_down"] = _adamw_call(got_rows, w_ffn_down[0], m_w_ffn_down[0], v_w_ffn_down[0], "adamw_ffn_down", tr,
                                    row0=3 * r_sq // tr)
    small_names = ["ln1_g", "b_in", "ret_norm_g", "attn_sinks", "ln2_g", "lnf_g"]
    small_res, loss_row = _adamw_small_call(
        got_small, [ln1_g, b_in, ret_norm_g, attn_sinks, ln2_g, lnf_row],
        [m_ln1_g, m_b_in, m_ret_norm_g, m_attn_sinks, m_ln2_g, m_lnf_g.reshape(1, D)],
        [v_ln1_g, v_b_in, v_ret_norm_g, v_attn_sinks, v_ln2_g, v_lnf_g.reshape(1, D)])
    for i, nm in enumerate(small_names):
        res[nm] = [small_res[kind][i] for kind in range(4)]

    order = ["ln1_g", "w_in", "b_in", "ret_norm_g", "w_ret_out", "attn_sinks", "w_attn_out", "w_out", "ln2_g",
             "w_ffn_gate", "w_ffn_up", "w_ffn_down", "lnf_g"]
    outs = [loss_row[0, 0], dx[None]]
    for kind in range(4):
        for nm in order:
            val = res[nm][kind]
            outs.append(val[None] if nm.startswith("w_") else val.reshape(D) if nm == "lnf_g" else val)
    return tuple(outs)
```

```python
import math

import numpy as np
import jax
import jax.numpy as jnp
from jax import lax
from jax.experimental import pallas as pl
from jax.experimental.pallas import tpu as pltpu

F32 = jnp.float32
_MXU = jnp.bfloat16

N_DEV = 8
D = 1024
RET_HEADS, RET_DK, RET_DV = 4, 128, 256
BLK = 128
Q_HEADS, KV_HEADS, HEAD_DIM = 16, 2, 64
GROUP = Q_HEADS // KV_HEADS
D_FF = 2816
N_RET, N_GATE, N_ATTN = 2048, 3072, 1280
D_IN = N_RET + N_GATE + N_ATTN
ROPE_THETA = 10000.0
EPS = 1e-6
RET_SCALE = RET_DK ** -0.5
ATTN_SCALE = HEAD_DIM ** -0.5
LR, B1, B2, ADAM_EPS, WD, STEP = 0.001, 0.9, 0.999, 1e-08, 0.01, 10
VMEM_LIMIT_MB = 56
MESH = pl.DeviceIdType.MESH


def _dot(a, b):
    return jnp.dot(a.astype(_MXU), b.astype(_MXU), preferred_element_type=F32)


def _dot_nt(a, b):
    return lax.dot_general(a.astype(_MXU), b.astype(_MXU), (((1,), (1,)), ((), ())), preferred_element_type=F32)


def _dot_tn(a, b):
    return lax.dot_general(a.astype(_MXU), b.astype(_MXU), (((0,), (0,)), ((), ())), preferred_element_type=F32)


def _sigmoid(x):
    return 1.0 / (1.0 + jnp.exp(-x))


def _cparams(n_axes, big=False):
    kw = dict(dimension_semantics=("arbitrary",) * n_axes)
    if big:
        kw["vmem_limit_bytes"] = VMEM_LIMIT_MB * 2**20
    return pltpu.CompilerParams(**kw)


def _rows(tm, width, col=0):
    return pl.BlockSpec((tm, width), lambda i: (i, col))


def _const(shape):
    nd = len(shape)
    return pl.BlockSpec(shape, lambda *_: (0,) * nd, pipeline_mode=pl.Buffered(1))


def _acc(width):
    return pl.BlockSpec((1, width), lambda *_: (0, 0))


def _sds(shape, dtype):
    return jax.ShapeDtypeStruct(shape, dtype)


def _swap_halves(x, half):
    w = x.shape[-1]
    if 2 * half == w:
        return pltpu.roll(x, half, 1)
    lane = lax.broadcasted_iota(jnp.int32, x.shape, 1)
    return jnp.where(lane % (2 * half) < half, pltpu.roll(x, w - half, 1), pltpu.roll(x, half, 1))


def _rope_tables(seq, half):
    lane = jnp.arange(128, dtype=jnp.int32)
    inv_freq = ROPE_THETA ** (-(lane % half).astype(F32) / half)
    sign = jnp.where(lane % (2 * half) < half, -1.0, 1.0).astype(F32)
    ang = jnp.arange(seq, dtype=jnp.int32).astype(F32)[:, None] * inv_freq[None, :]
    return jnp.cos(ang), jnp.sin(ang) * sign[None, :]


def _retention_decays():
    log_gamma = np.log1p(-np.exp2(-5.0 - np.arange(RET_HEADS, dtype=np.float32))).astype(np.float32)
    idx = np.arange(BLK, dtype=np.float32)
    rel = idx[:, None] - idx[None, :]
    intra = np.where(rel[None] >= 0, np.exp(log_gamma[:, None, None] * np.maximum(rel, 0.0)[None]), 0.0)
    q_decay = np.exp(log_gamma[:, None] * (idx + 1.0))[:, :, None]
    k_decay = np.exp(log_gamma[:, None] * (BLK - 1.0 - idx))[:, :, None]
    chunk_decay = [float(np.exp(np.float32(lg * BLK))) for lg in log_gamma]
    return (jnp.asarray(intra, F32), jnp.asarray(q_decay, F32), jnp.asarray(k_decay, F32), chunk_decay)


def _position():
    return lax.axis_index("x"), lax.axis_index("y"), lax.axis_index("c")


def _slot(px, py, pc):
    return 4 * px + 2 * py + pc


class _AllGather:
    def __init__(self, blocks):
        self.blocks = list(blocks)
        nb = len(self.blocks)
        self.out_shape = [_sds((N_DEV,) + b.shape, b.dtype) for b in self.blocks]
        self.scratch = [pltpu.SemaphoreType.DMA((nb, 7)), pltpu.SemaphoreType.DMA((nb, 7)),
                        pltpu.SemaphoreType.DMA((nb,))]

    def phases(self, ins, outs, send_sems, recv_sems, local_sems):
        nb = len(ins)
        x, y, c = _position()
        me, sibling = (x, y, c), (x, y, 1 - c)
        chips = [(1 - x, y), (x, 1 - y), (1 - x, 1 - y)]

        def copy(b, k, block, to, src=None):
            dst = outs[b].at[_slot(*block)]
            return pltpu.make_async_remote_copy(
                src_ref=dst if src is None else src, dst_ref=dst, send_sem=send_sems.at[b, k],
                recv_sem=recv_sems.at[b, k], device_id=to, device_id_type=MESH)

        def own(b):
            return pltpu.make_async_copy(ins[b], outs[b].at[_slot(*me)], local_sems.at[b])

        def first(b):
            return [copy(b, 0, me, sibling, src=ins[b])] + [
                copy(b, 1 + j, me, (*chip, c), src=ins[b]) for j, chip in enumerate(chips)]

        def start():
            for b in range(nb):
                own(b).start()
                for cp in first(b):
                    cp.start()

        def forward():
            for b in range(nb):
                for j, chip in enumerate(chips):
                    copy(b, 1 + j, (*chip, c), me).wait_recv()
                    copy(b, 4 + j, (*chip, c), sibling).start()

        def finish():
            for b in range(nb):
                copy(b, 0, sibling, me).wait_recv()
                for j, chip in enumerate(chips):
                    copy(b, 4 + j, (*chip, 1 - c), me).wait_recv()
            for b in range(nb):
                for cp in first(b):
                    cp.wait_send()
                for j, chip in enumerate(chips):
                    copy(b, 4 + j, (*chip, c), sibling).wait_send()
                own(b).wait()

        return start, forward, finish


class _AllToAll:
    def __init__(self, blocks):
        self.blocks = list(blocks)
        nb = len(self.blocks)
        self.out_shape = [_sds(b.shape, b.dtype) for b in self.blocks]
        self.scratch = [pltpu.SemaphoreType.DMA((nb, 7)), pltpu.SemaphoreType.DMA((nb, 7)),
                        pltpu.SemaphoreType.DMA((nb,))]

    def phases(self, ins, outs, send_sems, recv_sems, local_sems):
        nb = len(ins)
        x, y, c = _position()
        flip = lambda v, bit: 1 - v if bit else v
        peers = [(flip(x, k >> 2 & 1), flip(y, k >> 1 & 1), flip(c, k & 1)) for k in range(1, N_DEV)]

        def copy(b, k, peer, landed=False):
            return pltpu.make_async_remote_copy(
                src_ref=ins[b].at[_slot(*peer)], dst_ref=outs[b].at[_slot(*peer) if landed else _slot(x, y, c)],
                send_sem=send_sems.at[b, k], recv_sem=recv_sems.at[b, k], device_id=peer, device_id_type=MESH)

        def own(b):
            return pltpu.make_async_copy(ins[b].at[_slot(x, y, c)], outs[b].at[_slot(x, y, c)], local_sems.at[b])

        def start():
            for b in range(nb):
                own(b).start()
                for k, peer in enumerate(peers):
                    copy(b, k, peer).start()

        def forward():
            pass

        def finish():
            for b in range(nb):
                for k, peer in enumerate(peers):
                    copy(b, k, peer, landed=True).wait_recv()
            for b in range(nb):
                for k, peer in enumerate(peers):
                    copy(b, k, peer).wait_send()
                own(b).wait()

        return start, forward, finish


def _call(body, *, name, grid, in_specs, out_specs, out_shape, args, scratch_shapes=(), big=False, exchange=None):
    params = _cparams(len(grid), big)
    if exchange is None:
        return pl.pallas_call(body, name=name, grid=grid, in_specs=in_specs, out_specs=out_specs, out_shape=out_shape,
                              scratch_shapes=list(scratch_shapes), compiler_params=params)(*args)
    n_in, n_out, n_scr, nb = len(in_specs), len(out_specs), len(scratch_shapes), len(exchange.blocks)
    steps = math.prod(grid)

    def carried(*refs):
        pos = 0
        parts = []
        for n in (n_in, nb, n_out, nb, n_scr, len(exchange.scratch)):
            parts.append(refs[pos:pos + n])
            pos += n
        ins, x_ins, outs, x_outs, scr, sems = parts
        step = pl.program_id(0)
        for axis in range(1, len(grid)):
            step = step * grid[axis] + pl.program_id(axis)
        start, forward, finish = exchange.phases(x_ins, x_outs, *sems)
        pl.when(step == 0)(start)
        body(*ins, *outs, *scr)
        pl.when(step == steps // 2)(forward)
        pl.when(step == steps - 1)(finish)

    any_spec = pl.BlockSpec(memory_space=pl.ANY)
    res = pl.pallas_call(
        carried, name=name, grid=grid, in_specs=list(in_specs) + [any_spec] * nb,
        out_specs=list(out_specs) + [any_spec] * nb, out_shape=list(out_shape) + exchange.out_shape,
        scratch_shapes=list(scratch_shapes) + exchange.scratch, compiler_params=params)(*args, *exchange.blocks)
    return res[:n_out], res[n_out:]


def _exchange_call(exchange, name):
    nb = len(exchange.blocks)

    def body(*refs):
        start, forward, finish = exchange.phases(refs[:nb], refs[nb:2 * nb], *refs[2 * nb:])
        start()
        forward()
        finish()

    any_spec = pl.BlockSpec(memory_space=pl.ANY)
    return pl.pallas_call(body, name=name, in_specs=[any_spec] * nb, out_specs=[any_spec] * nb,
                          out_shape=exchange.out_shape, scratch_shapes=exchange.scratch)(*exchange.blocks)


def _ln_call(x, g, tm, exchange):
    s = x.shape[0]

    def body(x_ref, g_ref, h_ref):
        xv = x_ref[...]
        r = lax.rsqrt(jnp.mean(xv * xv, axis=-1, keepdims=True) + EPS)
        h_ref[...] = ((xv * r) * g_ref[...]).astype(h_ref.dtype)

    return _call(body, name="ln1", grid=(s // tm,), in_specs=[_rows(tm, D), _acc(D)], out_specs=[_rows(tm, D)],
                 out_shape=[_sds((s, D), _MXU)], args=(x, g), exchange=exchange)


def _proj_call(a, wt, bias, tm, tn, exchange):
    s, k = a.shape
    n = wt.shape[0]

    def body(a_ref, w_ref, b_ref, o_ref):
        o_ref[...] = _dot_nt(a_ref[...], w_ref[...]) + b_ref[...]

    return _call(body, name="proj", grid=(n // tn, s // tm),
                 in_specs=[pl.BlockSpec((tm, k), lambda j, i: (i, 0)), pl.BlockSpec((tn, k), lambda j, i: (j, 0)),
                           pl.BlockSpec((1, tn), lambda j, i: (0, j))],
                 out_specs=[pl.BlockSpec((tm, tn), lambda j, i: (i, j))], out_shape=[_sds((s, n), F32)],
                 args=(a, wt, bias), exchange=exchange)


def _mm_tn(a, b, name, tm, tn, tk):
    s, m = a.shape
    n = b.shape[1]

    def body(a_ref, b_ref, o_ref):
        @pl.when(pl.program_id(2) == 0)
        def _():
            o_ref[...] = jnp.zeros_like(o_ref)

        o_ref[...] += _dot_tn(a_ref[...], b_ref[...])

    return _call(body, name=name, grid=(m // tm, n // tn, s // tk),
                 in_specs=[pl.BlockSpec((tk, tm), lambda i, j, k: (k, i)), pl.BlockSpec((tk, tn), lambda i, j, k: (k, j))],
                 out_specs=pl.BlockSpec((tm, tn), lambda i, j, k: (i, j)), out_shape=_sds((m, n), F32),
                 args=(a, b), big=True)


def _ret_fwd_call(proj, cos, sin, decays):
    s = proj.shape[0]
    nblk = s // BLK
    intra, q_decay, k_decay, chunk_decay = decays

    def body(rq_ref, rk_ref, rv_ref, cos_ref, sin_ref, intra_ref, qd_ref, kd_ref,
             ry_ref, qr_ref, kr_ref, st_ref, state):
        @pl.when(pl.program_id(0) == 0)
        def _():
            state[...] = jnp.zeros_like(state)

        cos_v, sin_v = cos_ref[...], sin_ref[...]
        for h in range(RET_HEADS):
            hk = slice(h * RET_DK, (h + 1) * RET_DK)
            hv = slice(h * RET_DV, (h + 1) * RET_DV)
            q, k = rq_ref[:, hk], rk_ref[:, hk]
            qr = (q * cos_v + _swap_halves(q, RET_DK // 2) * sin_v) * RET_SCALE
            kr = k * cos_v + _swap_halves(k, RET_DK // 2) * sin_v
            v = rv_ref[:, hv]
            s_h = state[h]
            st_ref[0, h] = s_h.astype(st_ref.dtype)
            scores = _dot_nt(qr, kr) * intra_ref[h]
            ry_ref[:, hv] = _dot(scores, v) + _dot(qr, s_h) * qd_ref[h]
            state[h] = s_h * chunk_decay[h] + _dot_tn(kr * kd_ref[h], v)
            qr_ref[:, hk] = qr.astype(qr_ref.dtype)
            kr_ref[:, hk] = kr.astype(kr_ref.dtype)

    blk = lambda w, c: pl.BlockSpec((BLK, w), lambda n: (n, c))
    return _call(body, name="ret_fwd", grid=(nblk,),
                 in_specs=[blk(512, 0), blk(512, 1), blk(1024, 1), blk(128, 0), blk(128, 0),
                           _const(intra.shape), _const(q_decay.shape), _const(k_decay.shape)],
                 out_specs=[blk(1024, 0), blk(512, 0), blk(512, 0),
                            pl.BlockSpec((1, RET_HEADS, RET_DK, RET_DV), lambda n: (n, 0, 0, 0))],
                 out_shape=[_sds((s, 1024), F32), _sds((s, 512), _MXU), _sds((s, 512), _MXU),
                            _sds((nblk, RET_HEADS, RET_DK, RET_DV), _MXU)],
                 scratch_shapes=[pltpu.VMEM((RET_HEADS, RET_DK, RET_DV), F32)],
                 args=(proj, proj, proj, cos, sin, intra, q_decay, k_decay))


def _ret_bwd_call(qr, kr, proj, states, dry, cos, sin, decays):
    s = qr.shape[0]
    nblk = s // BLK
    intra, q_decay, k_decay, chunk_decay = decays

    def body(qr_ref, kr_ref, rv_ref, st_ref, dry_ref, cos_ref, sin_ref, intra_ref, qd_ref, kd_ref,
             dp_ref, db_ref, dstate):
        @pl.when(pl.program_id(0) == 0)
        def _():
            dstate[...] = jnp.zeros_like(dstate)
            db_ref[...] = jnp.zeros_like(db_ref)

        cos_v, sin_v = cos_ref[...], sin_ref[...]
        for h in range(RET_HEADS):
            hk = slice(h * RET_DK, (h + 1) * RET_DK)
            hv = slice(h * RET_DV, (h + 1) * RET_DV)
            q, k, v, d_out = qr_ref[:, hk], kr_ref[:, hk], rv_ref[:, hv], dry_ref[:, hv]
            d_next = dstate[h]
            scores = _dot_nt(q, k) * intra_ref[h]
            d_scores = _dot_nt(d_out, v) * intra_ref[h]
            d_cross = d_out * qd_ref[h]
            dq = _dot(d_scores, k) + _dot_nt(d_cross, st_ref[0, h])
            dk = _dot_tn(d_scores, q) + _dot_nt(v, d_next) * kd_ref[h]
            dv = _dot_tn(scores, d_out) + _dot(k.astype(F32) * kd_ref[h], d_next)
            dstate[h] = d_next * chunk_decay[h] + _dot_tn(q, d_cross)
            dq = (dq * cos_v - _swap_halves(dq, RET_DK // 2) * sin_v) * RET_SCALE
            dk = dk * cos_v - _swap_halves(dk, RET_DK // 2) * sin_v
            dp_ref[:, hk] = dq.astype(dp_ref.dtype)
            dp_ref[:, slice(512 + h * RET_DK, 512 + (h + 1) * RET_DK)] = dk.astype(dp_ref.dtype)
            dp_ref[:, slice(1024 + h * RET_DV, 1024 + (h + 1) * RET_DV)] = dv.astype(dp_ref.dtype)
            db_ref[:, hk] += jnp.sum(dq, axis=0, keepdims=True)
            db_ref[:, slice(512 + h * RET_DK, 512 + (h + 1) * RET_DK)] += jnp.sum(dk, axis=0, keepdims=True)
            db_ref[:, slice(1024 + h * RET_DV, 1024 + (h + 1) * RET_DV)] += jnp.sum(dv, axis=0, keepdims=True)

    rblk = lambda w, c: pl.BlockSpec((BLK, w), lambda n: (nblk - 1 - n, c))
    return _call(body, name="ret_bwd", grid=(nblk,),
                 in_specs=[rblk(512, 0), rblk(512, 0), rblk(1024, 1),
                           pl.BlockSpec((1, RET_HEADS, RET_DK, RET_DV), lambda n: (nblk - 1 - n, 0, 0, 0)),
                           rblk(1024, 0), rblk(128, 0), rblk(128, 0),
                           _const(intra.shape), _const(q_decay.shape), _const(k_decay.shape)],
                 out_specs=[rblk(N_RET, 0), _acc(N_RET)],
                 out_shape=[_sds((s, N_RET), _MXU), _sds((1, N_RET), F32)],
                 scratch_shapes=[pltpu.VMEM((RET_HEADS, RET_DK, RET_DV), F32)],
                 args=(qr, kr, proj, states, dry, cos, sin, intra, q_decay, k_decay))


def _both_halves(x, g):
    lane = lax.broadcasted_iota(jnp.int32, x.shape, 1)
    keep = lane < HEAD_DIM if g == 0 else lane >= HEAD_DIM
    return jnp.where(keep, x, pltpu.roll(x, HEAD_DIM, 1))


def _stack_heads(ref, g):
    lane = lax.broadcasted_iota(jnp.int32, (BLK, 128), 1)
    pieces = []
    for j in range(g * 4, g * 4 + 4):
        chunk = ref[:, j * 128:(j + 1) * 128]
        pieces += [jnp.where(lane < HEAD_DIM, chunk, jnp.zeros_like(chunk)),
                   jnp.where(lane >= HEAD_DIM, chunk, jnp.zeros_like(chunk))]
    return jnp.concatenate(pieces, axis=0)


def _unstack_heads(stacked, jj):
    lane = lax.broadcasted_iota(jnp.int32, (BLK, 128), 1)
    return jnp.where(lane < HEAD_DIM, stacked[2 * jj * BLK:(2 * jj + 1) * BLK], stacked[(2 * jj + 1) * BLK:(2 * jj + 2) * BLK])


ROWS = 8


def _window_bias(first_block):
    qi = lax.broadcasted_iota(jnp.int32, (BLK, 2 * BLK), 0)
    kj = lax.broadcasted_iota(jnp.int32, (BLK, 2 * BLK), 1)
    first_key = jnp.where(first_block, BLK, 0)
    seen = (kj > qi) & (kj <= qi + BLK) & (kj >= first_key)
    return jnp.where(seen, 0.0, -1e30)


def _sink_softmax(scores, sink):
    m = jnp.maximum(jnp.max(scores, axis=-1, keepdims=True), sink)
    e = jnp.exp(scores - m)
    e_sink = jnp.exp(sink - m)
    return e, e_sink, 1.0 / (jnp.sum(e, axis=-1, keepdims=True) + e_sink)


def _attn_fwd_call(proj, sinks, cos, sin):
    s = proj.shape[0]
    nblk = s // BLK

    def body(sink_ref, q_ref, k_ref, v_ref, cos_ref, sin_ref, ay_ref, qr_ref, kr_ref, vb_ref,
             kwin, vwin, bias, s_scr, p_scr):
        n = pl.program_id(0)

        @pl.when(n == 0)
        def _():
            kwin[...] = jnp.zeros_like(kwin)
            vwin[...] = jnp.zeros_like(vwin)

        @pl.when(n > 0)
        def _():
            kwin[0:BLK] = kwin[BLK:2 * BLK]
            vwin[0:BLK] = vwin[BLK:2 * BLK]

        cos_v, sin_v = cos_ref[...], sin_ref[...]
        k = k_ref[...]
        kr = (k * cos_v + _swap_halves(k, HEAD_DIM // 2) * sin_v).astype(kwin.dtype)
        kwin[BLK:2 * BLK] = kr
        vwin[BLK:2 * BLK] = v_ref[...].astype(vwin.dtype)
        kr_ref[...] = kr
        vb_ref[...] = vwin[BLK:2 * BLK]
        for j in range(Q_HEADS // 2):
            cols = slice(j * 128, (j + 1) * 128)
            q = q_ref[:, cols]
            qr_ref[:, cols] = ((q * cos_v + _swap_halves(q, HEAD_DIM // 2) * sin_v) * ATTN_SCALE).astype(qr_ref.dtype)
        bias[...] = _window_bias(n == 0)
        for g in range(KV_HEADS):
            kg = _both_halves(kwin[...], g)
            vg = _both_halves(vwin[...], g)
            s_scr[...] = _dot_nt(_stack_heads(qr_ref, g), kg)
            for r in range(0, GROUP * BLK, ROWS):
                rows = slice(r, r + ROWS)
                e, _, inv = _sink_softmax(s_scr[rows, :] + bias[r % BLK:r % BLK + ROWS, :],
                                          sink_ref[0, g * GROUP + r // BLK])
                p_scr[rows, :] = (e * inv).astype(p_scr.dtype)
            out = _dot(p_scr[...], vg)
            for jj in range(4):
                j = g * 4 + jj
                ay_ref[:, j * 128:(j + 1) * 128] = _unstack_heads(out, jj).astype(ay_ref.dtype)

    blk = lambda w, c: pl.BlockSpec((BLK, w), lambda n: (n, c))
    off = (N_RET + N_GATE) // 128
    return _call(body, name="attn_fwd", grid=(nblk,),
                 in_specs=[pl.BlockSpec(memory_space=pltpu.SMEM), blk(1024, off // 8), blk(128, off + 8), blk(128, off + 9),
                           blk(128, 0), blk(128, 0)],
                 out_specs=[blk(1024, 0), blk(1024, 0), blk(128, 0), blk(128, 0)],
                 out_shape=[_sds((s, 1024), _MXU), _sds((s, 1024), _MXU), _sds((s, 128), _MXU), _sds((s, 128), _MXU)],
                 scratch_shapes=[pltpu.VMEM((2 * BLK, 128), _MXU), pltpu.VMEM((2 * BLK, 128), _MXU),
                                 pltpu.VMEM((BLK, 2 * BLK), F32), pltpu.VMEM((GROUP * BLK, 2 * BLK), F32),
                                 pltpu.VMEM((GROUP * BLK, 2 * BLK), _MXU)],
                 args=(sinks, proj, proj, proj, cos, sin))


def _attn_bwd_call(qr, kr, vb, day, sinks, cos, sin, exchange):
    s = qr.shape[0]
    nblk = s // BLK

    def body(sink_ref, q_ref, kc_ref, kp_ref, vc_ref, vp_ref, do_ref, cos_ref, sin_ref,
             dq_ref, cur_ref, prev_ref, dsink_ref, db_ref, bias, s_scr, dp_scr, p_scr, ds_scr):
        n = pl.program_id(0)

        @pl.when(n == 0)
        def _():
            dsink_ref[...] = jnp.zeros_like(dsink_ref)
            db_ref[...] = jnp.zeros_like(db_ref)

        cos_v, sin_v = cos_ref[...], sin_ref[...]
        bias[...] = _window_bias(n == 0)
        lane1 = lax.broadcasted_iota(jnp.int32, (1, 128), 1)
        kwin = jnp.concatenate([kp_ref[...], kc_ref[...]], axis=0)
        vwin = jnp.concatenate([vp_ref[...], vc_ref[...]], axis=0)
        dk_t, dv_t = [], []
        dsink = jnp.zeros((1, 128), F32)
        for g in range(KV_HEADS):
            kg = _both_halves(kwin, g)
            vg = _both_halves(vwin, g)
            q_all = _stack_heads(q_ref, g)
            do_all = _stack_heads(do_ref, g)
            s_scr[...] = _dot_nt(q_all, kg)
            dp_scr[...] = _dot_nt(do_all, vg)
            for head in range(g * GROUP, (g + 1) * GROUP):
                sink_part = jnp.zeros((ROWS, 1), F32)
                for i0 in range(0, BLK, ROWS):
                    rows = slice((head - g * GROUP) * BLK + i0, (head - g * GROUP) * BLK + i0 + ROWS)
                    e, e_sink, inv = _sink_softmax(s_scr[rows, :] + bias[i0:i0 + ROWS, :], sink_ref[0, head])
                    p = e * inv
                    dp = dp_scr[rows, :]
                    delta = jnp.sum(p * dp, axis=-1, keepdims=True)
                    p_scr[rows, :] = p.astype(p_scr.dtype)
                    ds_scr[rows, :] = (p * (dp - delta)).astype(ds_scr.dtype)
                    sink_part = sink_part + e_sink * inv * delta
                dsink = dsink + jnp.where(lane1 == head, -jnp.sum(sink_part, axis=0, keepdims=True), 0.0)
            dv_both = _dot_tn(do_all, p_scr[...])
            dk_both = _dot_tn(q_all, ds_scr[...])
            dv_t.append(dv_both[0:HEAD_DIM] + dv_both[HEAD_DIM:128])
            dk_t.append(dk_both[0:HEAD_DIM] + dk_both[HEAD_DIM:128])
            dq_all = _dot(ds_scr[...], kg)
            for jj in range(4):
                cols = slice((g * 4 + jj) * 128, (g * 4 + jj + 1) * 128)
                dq = _unstack_heads(dq_all, jj)
                dq = (dq * cos_v - _swap_halves(dq, HEAD_DIM // 2) * sin_v) * ATTN_SCALE
                dq_ref[:, cols] = dq.astype(dq_ref.dtype)
                db_ref[:, cols] += jnp.sum(dq, axis=0, keepdims=True)
        dsink_ref[...] += dsink
        dk_all = jnp.concatenate(dk_t, axis=0).T
        dv_all = jnp.concatenate(dv_t, axis=0).T
        prev_ref[:, 0:128] = dk_all[0:BLK]
        prev_ref[:, 128:256] = dv_all[0:BLK]
        cur_ref[:, 0:128] = dk_all[BLK:2 * BLK]
        cur_ref[:, 128:256] = dv_all[BLK:2 * BLK]

    blk = lambda w, c: pl.BlockSpec((BLK, w), lambda n: (n, c))
    pblk = lambda w: pl.BlockSpec((BLK, w), lambda n: (jnp.maximum(n - 1, 0), 0))
    big = (GROUP * BLK, 2 * BLK)
    return _call(body, name="attn_bwd", grid=(nblk,),
                 in_specs=[pl.BlockSpec(memory_space=pltpu.SMEM), blk(1024, 0), blk(128, 0), pblk(128), blk(128, 0),
                           pblk(128), blk(1024, 0), blk(128, 0), blk(128, 0)],
                 out_specs=[blk(1024, 0), blk(256, 0), blk(256, 0), _acc(128), _acc(1024)],
                 out_shape=[_sds((s, 1024), _MXU), _sds((s, 256), F32), _sds((s, 256), F32), _sds((1, 128), F32),
                            _sds((1, 1024), F32)],
                 scratch_shapes=[pltpu.VMEM((BLK, 2 * BLK), F32), pltpu.VMEM(big, F32), pltpu.VMEM(big, F32),
                                 pltpu.VMEM(big, _MXU), pltpu.VMEM(big, _MXU)],
                 args=(sinks, qr, kr, kr, vb, vb, day, cos, sin), exchange=exchange)


def _attn_combine_call(dq, dkv_cur, dkv_prev, cos, sin):
    s = dq.shape[0]
    nblk = s // BLK

    def body(dq_ref, cur_ref, nxt_ref, cos_ref, sin_ref, dp_ref, db_ref):
        n = pl.program_id(0)

        @pl.when(n == 0)
        def _():
            db_ref[...] = jnp.zeros_like(db_ref)

        dkv = cur_ref[...] + nxt_ref[...] * (n < nblk - 1).astype(F32)
        dk = dkv[:, 0:128]
        dk = dk * cos_ref[...] - _swap_halves(dk, HEAD_DIM // 2) * sin_ref[...]
        dv = dkv[:, 128:256]
        dp_ref[:, 0:1024] = dq_ref[...]
        dp_ref[:, 1024:1152] = dk.astype(dp_ref.dtype)
        dp_ref[:, 1152:1280] = dv.astype(dp_ref.dtype)
        db_ref[:, 0:128] += jnp.sum(dk, axis=0, keepdims=True)
        db_ref[:, 128:256] += jnp.sum(dv, axis=0, keepdims=True)

    blk = lambda w: pl.BlockSpec((BLK, w), lambda n: (n, 0))
    return _call(body, name="attn_combine", grid=(nblk,),
                 in_specs=[blk(1024), blk(256), pl.BlockSpec((BLK, 256), lambda n: (jnp.minimum(n + 1, nblk - 1), 0)),
                           blk(128), blk(128)],
                 out_specs=[blk(N_ATTN), _acc(256)],
                 out_shape=[_sds((s, N_ATTN), _MXU), _sds((1, 256), F32)],
                 args=(dq, dkv_cur, dkv_prev, cos, sin))


def _group_norm(y):
    mu = jnp.mean(y, axis=-1, keepdims=True)
    yc = y - mu
    rs = lax.rsqrt(jnp.mean(yc * yc, axis=-1, keepdims=True) + EPS)
    return yc * rs, rs


GATE_COL = N_RET // 1024


def _merge_fwd_call(x, ry, proj, ay, gn_g, w_ro, w_ao, w_o, tm):
    s = x.shape[0]

    def body(x_ref, ry_ref, rg_ref, ga_ref, gb_ref, ay_ref, gn_ref, wro_ref, wao_ref, wo_ref,
             ain_ref, a_ref, b_ref, mg_ref, x1_ref):
        for h in range(RET_HEADS):
            hv = slice(h * RET_DV, (h + 1) * RET_DV)
            yhat, _ = _group_norm(ry_ref[:, hv])
            rg = rg_ref[:, hv]
            ain_ref[:, hv] = ((rg * _sigmoid(rg)) * (yhat * gn_ref[:, hv])).astype(ain_ref.dtype)
        a = _dot(ain_ref[...], wro_ref[...])
        b = _dot(ay_ref[...], wao_ref[...])
        a_ref[...] = a
        b_ref[...] = b
        merged = (_sigmoid(ga_ref[...]) * a + _sigmoid(gb_ref[...]) * b).astype(mg_ref.dtype)
        mg_ref[...] = merged
        x1_ref[...] = x_ref[...] + _dot(merged, wo_ref[...])

    return _call(body, name="merge_fwd", grid=(s // tm,),
                 in_specs=[_rows(tm, D), _rows(tm, 1024), _rows(tm, 1024, GATE_COL), _rows(tm, 1024, GATE_COL + 1),
                           _rows(tm, 1024, GATE_COL + 2), _rows(tm, 1024), _acc(1024),
                           _const((D, D)), _const((D, D)), _const((D, D))],
                 out_specs=[_rows(tm, D)] * 5,
                 out_shape=[_sds((s, D), _MXU), _sds((s, D), F32), _sds((s, D), F32), _sds((s, D), _MXU), _sds((s, D), F32)],
                 args=(x, ry, proj, proj, proj, ay, gn_g, w_ro, w_ao, w_o), big=True)


def _merge_bwd_call(dx1, a, b, ry, proj, gn_g, w_ro, w_ao, w_o, tm, exchange):
    s = dx1.shape[0]

    def body(dx1_ref, a_ref, b_ref, ry_ref, rg_ref, ga_ref, gb_ref, gn_ref, wro_ref, wao_ref, wo_ref,
             da_ref, dbr_ref, dp_ref, day_ref, dry_ref, dbias_ref, dgn_ref):
        @pl.when(pl.program_id(0) == 0)
        def _():
            dbias_ref[...] = jnp.zeros_like(dbias_ref)
            dgn_ref[...] = jnp.zeros_like(dgn_ref)

        d_merged = _dot_nt(dx1_ref[...], wo_ref[...])
        sa, sb = _sigmoid(ga_ref[...]), _sigmoid(gb_ref[...])
        d_a = d_merged * sa
        d_b = d_merged * sb
        da_ref[...] = d_a.astype(da_ref.dtype)
        dbr_ref[...] = d_b.astype(dbr_ref.dtype)
        d_ga = d_merged * a_ref[...] * (sa * (1.0 - sa))
        d_gb = d_merged * b_ref[...] * (sb * (1.0 - sb))
        dp_ref[:, 1024:2048] = d_ga.astype(dp_ref.dtype)
        dp_ref[:, 2048:3072] = d_gb.astype(dp_ref.dtype)
        dbias_ref[:, 1024:2048] += jnp.sum(d_ga, axis=0, keepdims=True)
        dbias_ref[:, 2048:3072] += jnp.sum(d_gb, axis=0, keepdims=True)
        day_ref[...] = _dot_nt(d_b, wao_ref[...]).astype(day_ref.dtype)
        d_ain = _dot_nt(d_a, wro_ref[...])
        for h in range(RET_HEADS):
            hv = slice(h * RET_DV, (h + 1) * RET_DV)
            yhat, rs = _group_norm(ry_ref[:, hv])
            rg = rg_ref[:, hv]
            sg = _sigmoid(rg)
            gn = gn_ref[:, hv]
            d_h = d_ain[:, hv]
            d_rg = d_h * (yhat * gn) * (sg * (1.0 + rg * (1.0 - sg)))
            d_ryn = d_h * (rg * sg)
            dgn_ref[:, hv] += jnp.sum(d_ryn * yhat, axis=0, keepdims=True)
            d_yhat = d_ryn * gn
            dry_ref[:, hv] = rs * (d_yhat - jnp.mean(d_yhat, axis=-1, keepdims=True)
                                   - yhat * jnp.mean(d_yhat * yhat, axis=-1, keepdims=True))
            dp_ref[:, hv] = d_rg.astype(dp_ref.dtype)
            dbias_ref[:, hv] += jnp.sum(d_rg, axis=0, keepdims=True)

    return _call(body, name="merge_bwd", grid=(s // tm,),
                 in_specs=[_rows(tm, D), _rows(tm, D), _rows(tm, D), _rows(tm, 1024), _rows(tm, 1024, GATE_COL),
                           _rows(tm, 1024, GATE_COL + 1), _rows(tm, 1024, GATE_COL + 2), _acc(1024),
                           _const((D, D)), _const((D, D)), _const((D, D))],
                 out_specs=[_rows(tm, D), _rows(tm, D), _rows(tm, N_GATE), _rows(tm, D), _rows(tm, D), _acc(N_GATE),
                            _acc(1024)],
                 out_shape=[_sds((s, D), _MXU), _sds((s, D), _MXU), _sds((s, N_GATE), _MXU), _sds((s, D), _MXU),
                            _sds((s, D), F32), _sds((1, N_GATE), F32), _sds((1, 1024), F32)],
                 args=(dx1, a, b, ry, proj, proj, proj, gn_g, w_ro, w_ao, w_o), big=True, exchange=exchange)


def _ffn_fwd_call(x1, target, ln2_g, lnf_g, w_g, w_u, w_d, tm):
    s = x1.shape[0]

    def body(x1_ref, t_ref, g2_ref, gf_ref, wg_ref, wu_ref, wd_ref,
             h2_ref, g_ref, u_ref, f_ref, dx2_ref, loss_ref, dgf_ref):
        @pl.when(pl.program_id(0) == 0)
        def _():
            loss_ref[...] = jnp.zeros_like(loss_ref)
            dgf_ref[...] = jnp.zeros_like(dgf_ref)

        x1v = x1_ref[...]
        r1 = lax.rsqrt(jnp.mean(x1v * x1v, axis=-1, keepdims=True) + EPS)
        h2 = ((x1v * r1) * g2_ref[...]).astype(h2_ref.dtype)
        h2_ref[...] = h2
        g = _dot_nt(h2, wg_ref[...])
        u = _dot_nt(h2, wu_ref[...])
        g_ref[...] = g
        u_ref[...] = u
        f = ((g * _sigmoid(g)) * u).astype(f_ref.dtype)
        f_ref[...] = f
        x2 = x1v + _dot(f, wd_ref[...])
        r2 = lax.rsqrt(jnp.mean(x2 * x2, axis=-1, keepdims=True) + EPS)
        xhat = x2 * r2
        err = xhat * gf_ref[...] - t_ref[...]
        loss_ref[...] += 0.5 * jnp.sum(jnp.mean(err * err, axis=-1, keepdims=True))
        dy = err * (1.0 / D)
        dgf_ref[...] += jnp.sum(dy * xhat, axis=0, keepdims=True)
        dxh = dy * gf_ref[...]
        dx2_ref[...] = r2 * (dxh - xhat * jnp.mean(dxh * xhat, axis=-1, keepdims=True))

    return _call(body, name="ffn_fwd", grid=(s // tm,),
                 in_specs=[_rows(tm, D), _rows(tm, D), _acc(D), _acc(D), _const((D_FF, D)), _const((D_FF, D)),
                           _const((D_FF, D))],
                 out_specs=[_rows(tm, D), _rows(tm, D_FF), _rows(tm, D_FF), _rows(tm, D_FF), _rows(tm, D), _acc(128),
                            _acc(D)],
                 out_shape=[_sds((s, D), _MXU), _sds((s, D_FF), F32), _sds((s, D_FF), F32), _sds((s, D_FF), _MXU),
                            _sds((s, D), F32), _sds((1, 128), F32), _sds((1, D), F32)],
                 args=(x1, target, ln2_g, lnf_g, w_g, w_u, w_d), big=True)


def _ffn_bwd_call(dx2, x1, g, u, ln2_g, w_g, w_u, w_d, tm):
    s = dx2.shape[0]

    def body(dx2_ref, x1_ref, g_ref, u_ref, g2_ref, wg_ref, wu_ref, wd_ref, dx1_ref, dg_ref, du_ref, dg2_ref):
        @pl.when(pl.program_id(0) == 0)
        def _():
            dg2_ref[...] = jnp.zeros_like(dg2_ref)

        dx2v = dx2_ref[...]
        df = _dot_nt(dx2v, wd_ref[...])
        gv, uv = g_ref[...], u_ref[...]
        sg = _sigmoid(gv)
        du = (df * (gv * sg)).astype(du_ref.dtype)
        dg = (df * uv * (sg * (1.0 + gv * (1.0 - sg)))).astype(dg_ref.dtype)
        du_ref[...] = du
        dg_ref[...] = dg
        dh2 = _dot(dg, wg_ref[...]) + _dot(du, wu_ref[...])
        x1v = x1_ref[...]
        r1 = lax.rsqrt(jnp.mean(x1v * x1v, axis=-1, keepdims=True) + EPS)
        xhat = x1v * r1
        dg2_ref[...] += jnp.sum(dh2 * xhat, axis=0, keepdims=True)
        dxh = dh2 * g2_ref[...]
        dx1_ref[...] = dx2v + r1 * (dxh - xhat * jnp.mean(dxh * xhat, axis=-1, keepdims=True))

    return _call(body, name="ffn_bwd", grid=(s // tm,),
                 in_specs=[_rows(tm, D), _rows(tm, D), _rows(tm, D_FF), _rows(tm, D_FF), _acc(D),
                           _const((D_FF, D)), _const((D_FF, D)), _const((D_FF, D))],
                 out_specs=[_rows(tm, D), _rows(tm, D_FF), _rows(tm, D_FF), _acc(D)],
                 out_shape=[_sds((s, D), F32), _sds((s, D_FF), _MXU), _sds((s, D_FF), _MXU), _sds((1, D), F32)],
                 args=(dx2, x1, g, u, ln2_g, w_g, w_u, w_d), big=True)


def _dx_call(x, dx1, dp_ret, dp_gate, dp_attn, ln1_g, w_in, tm, exchange):
    s = x.shape[0]

    def body(x_ref, dx1_ref, dr_ref, dg_ref, da_ref, g1_ref, w_ref, dx_ref, dg1_ref):
        @pl.when(pl.program_id(0) == 0)
        def _():
            dg1_ref[...] = jnp.zeros_like(dg1_ref)

        dh = (_dot(dr_ref[...], w_ref[0:N_RET, :]) + _dot(dg_ref[...], w_ref[N_RET:N_RET + N_GATE, :])
              + _dot(da_ref[...], w_ref[N_RET + N_GATE:D_IN, :]))
        xv = x_ref[...]
        r = lax.rsqrt(jnp.mean(xv * xv, axis=-1, keepdims=True) + EPS)
        xhat = xv * r
        dg1_ref[...] += jnp.sum(dh * xhat, axis=0, keepdims=True)
        dxh = dh * g1_ref[...]
        dx_ref[...] = dx1_ref[...] + r * (dxh - xhat * jnp.mean(dxh * xhat, axis=-1, keepdims=True))

    return _call(body, name="dx", grid=(s // tm,),
                 in_specs=[_rows(tm, D), _rows(tm, D), _rows(tm, N_RET), _rows(tm, N_GATE), _rows(tm, N_ATTN), _acc(D),
                           _const((D_IN, D))],
                 out_specs=[_rows(tm, D), _acc(D)],
                 out_shape=[_sds((s, D), F32), _sds((1, D), F32)],
                 args=(x, dx1, dp_ret, dp_gate, dp_attn, ln1_g, w_in), big=True, exchange=exchange)


def _adamw(g, w, m, v):
    m_new = B1 * m + (1.0 - B1) * g
    v_new = B2 * v + (1.0 - B2) * (g * g)
    m_hat = m_new / (1.0 - B1 ** STEP)
    v_hat = v_new / (1.0 - B2 ** STEP)
    return -LR * (m_hat / (jnp.sqrt(v_hat) + ADAM_EPS) + WD * w), m_new, v_new


def _slot_sum(p_ref):
    g = p_ref[0].astype(F32)
    for k in range(1, N_DEV):
        g = g + p_ref[k].astype(F32)
    return g


def _adamw_call(parts, w, m, v, name, tr, row0=0):
    rows, cols = w.shape

    def body(p_ref, w_ref, m_ref, v_ref, g_ref, dw_ref, nm_ref, nv_ref):
        g = _slot_sum(p_ref)
        g_ref[...] = g
        dw_ref[...], nm_ref[...], nv_ref[...] = _adamw(g, w_ref[...], m_ref[...], v_ref[...])

    p_spec = pl.BlockSpec((N_DEV, tr, cols), lambda i: (0, row0 + i, 0))
    spec = pl.BlockSpec((tr, cols), lambda i: (i, 0))
    return _call(body, name=name, grid=(rows // tr,), in_specs=[p_spec, spec, spec, spec], out_specs=[spec] * 4,
                 out_shape=[_sds((rows, cols), F32)] * 4, args=(parts, w, m, v))


SMALL_WIDTHS = [1024, 6400, 1024, 16, 1024, 1024]
SMALL_OFFSETS = [0, 1024, 7424, 8448, 8576, 9600]
LOSS_OFFSET = 10624
SMALL_LEN = 10752


def _pack_small(grads, loss):
    pieces = []
    for gr, width in zip(grads, SMALL_WIDTHS):
        pieces.append(jnp.pad(gr.reshape(1, width), ((0, 0), (0, -width % 128))))
    pieces.append(jnp.pad(loss.reshape(1, 1), ((0, 0), (0, 127))))
    return jnp.concatenate(pieces, axis=1)


def _adamw_small_call(parts, ws, ms, vs):
    n = len(ws)

    def body(*refs):
        p_ref, w_refs, m_refs, v_refs = refs[0], refs[1:1 + n], refs[1 + n:1 + 2 * n], refs[1 + 2 * n:1 + 3 * n]
        outs = refs[1 + 3 * n:]
        g_all = _slot_sum(p_ref)
        for i, (off, width) in enumerate(zip(SMALL_OFFSETS, SMALL_WIDTHS)):
            g = g_all[:, off:off + width]
            outs[i][...] = g
            outs[n + i][...], outs[2 * n + i][...], outs[3 * n + i][...] = _adamw(
                g, w_refs[i][...], m_refs[i][...], v_refs[i][...])
        outs[4 * n][...] = g_all[:, LOSS_OFFSET:LOSS_OFFSET + 128]

    whole = lambda shape: pl.BlockSpec(shape, lambda i: (0,) * len(shape))
    small = [whole((1, w)) for w in SMALL_WIDTHS]
    res = _call(body, name="adamw_small", grid=(1,), in_specs=[whole((N_DEV, 1, SMALL_LEN))] + small * 3,
                out_specs=small * 4 + [whole((1, 128))],
                out_shape=[_sds((1, w), F32) for w in SMALL_WIDTHS] * 4 + [_sds((1, 128), F32)],
                args=(parts, *ws, *ms, *vs))
    return [res[k * n:(k + 1) * n] for k in range(4)], res[4 * n]


def kernel(x, ln1_g, w_in, b_in, ret_norm_g, w_ret_out, attn_sinks, w_attn_out, w_out, ln2_g, w_ffn_gate, w_ffn_up, w_ffn_down, lnf_g, loss_target, m_ln1_g, m_w_in, m_b_in, m_ret_norm_g, m_w_ret_out, m_attn_sinks, m_w_attn_out, m_w_out, m_ln2_g, m_w_ffn_gate, m_w_ffn_up, m_w_ffn_down, m_lnf_g, v_ln1_g, v_w_in, v_b_in, v_ret_norm_g, v_w_ret_out, v_attn_sinks, v_w_attn_out, v_w_out, v_ln2_g, v_w_ffn_gate, v_w_ffn_up, v_w_ffn_down, v_lnf_g):
    cast = lambda a: a.astype(_MXU)
    xs, target = x[0], loss_target[0]
    s = xs.shape[0]
    r_sq = w_ret_out.shape[1]
    r_dn = w_ffn_down.shape[1]
    c_in = w_in.shape[2]
    c_ff = w_ffn_gate.shape[2]
    tm, tk = min(256, s), min(2048, s)
    lnf_row = lnf_g.reshape(1, D)
    cos_r, sin_r = _rope_tables(s, RET_DK // 2)
    cos_a, sin_a = _rope_tables(s, HEAD_DIM // 2)
    decays = _retention_decays()
    tr_shard = lambda a: a[0].T
    per_dev = lambda a, n: a.reshape(N_DEV, n, D)
    cut = lax.slice_in_dim
    regroup = lambda a, ax: jnp.concatenate(
        [cut(a, 0, 2048, axis=ax), cut(a, 2048, 3072, axis=ax), cut(a, 4352, 6400, axis=ax), cut(a, 3072, 4352, axis=ax)], axis=ax)

    (h,), (all_in,) = _ln_call(xs, ln1_g, tm, _AllGather([cast(tr_shard(w_in))]))
    wt_in = regroup(all_in.reshape(N_DEV * c_in, D), 0)
    blk_rest = cast(jnp.concatenate([tr_shard(w_ffn_gate), tr_shard(w_ffn_up), w_ret_out[0], w_attn_out[0], w_out[0],
                                     w_ffn_down[0]], axis=0))
    (proj,), (all_rest,) = _proj_call(h, wt_in, regroup(b_in, 1), min(512, s), 1280, _AllGather([blk_rest]))
    rows_of = lambda lo, n: all_rest[:, lo:lo + n].reshape(N_DEV * n, D)
    wt_g, wt_u = rows_of(0, c_ff), rows_of(c_ff, c_ff)
    full_ro, full_ao, full_o = (rows_of(2 * c_ff + i * r_sq, r_sq) for i in range(3))
    full_d = rows_of(2 * c_ff + 3 * r_sq, r_dn)
    ry, qr, kr, states = _ret_fwd_call(proj, cos_r, sin_r, decays)
    ay, aqr, akr, avb = _attn_fwd_call(proj, attn_sinks, cos_a, sin_a)
    a_in, br_a, br_b, merged, x1 = _merge_fwd_call(xs, ry, proj, ay, ret_norm_g, full_ro, full_ao, full_o, tm)
    h2, g, u, f, dx2, loss, d_lnf = _ffn_fwd_call(x1, target, ln2_g, lnf_row, wt_g, wt_u, full_d, tm)

    dx1, dg, du, d_ln2 = _ffn_bwd_call(dx2, x1, g, u, ln2_g, wt_g, wt_u, full_d, tm)
    dw_d = _mm_tn(f, dx2, "dw_ffn_down", 1408, 1024, tk)
    dwt_g = _mm_tn(dg, h2, "dw_ffn_gate", 1408, 1024, tk)
    dwt_u = _mm_tn(du, h2, "dw_ffn_up", 1408, 1024, tk)
    send_ff = cast(jnp.concatenate([per_dev(dwt_g, c_ff), per_dev(dwt_u, c_ff)], axis=1))
    (d_a, d_b, dp_gate, day, dry, db_gate, d_gn), (got_ff,) = _merge_bwd_call(
        dx1, br_a, br_b, ry, proj, ret_norm_g, full_ro, full_ao, full_o, tm, _AllToAll([send_ff]))
    dw_o = _mm_tn(merged, dx1, "dw_out", 1024, 1024, tk)
    dw_ro = _mm_tn(a_in, d_a, "dw_ret_out", 1024, 1024, tk)
    dw_ao = _mm_tn(ay, d_b, "dw_attn_out", 1024, 1024, tk)
    send_rows = cast(jnp.concatenate(
        [per_dev(dw_ro, r_sq), per_dev(dw_ao, r_sq), per_dev(dw_o, r_sq), per_dev(dw_d, r_dn)], axis=1))
    (dq, dkv_cur, dkv_prev, d_sinks, db_aq), (got_rows,) = _attn_bwd_call(
        aqr, akr, avb, day, attn_sinks, cos_a, sin_a, _AllToAll([send_rows]))
    dp_attn, db_akv = _attn_combine_call(dq, dkv_cur, dkv_prev, cos_a, sin_a)
    dp_ret, db_ret = _ret_bwd_call(qr, kr, proj, states, dry, cos_r, sin_r, decays)
    dwt_ret = _mm_tn(dp_ret, h, "dw_in_ret", 1024, 1024, tk)
    dwt_gate = _mm_tn(dp_gate, h, "dw_in_gate", 1024, 1024, tk)
    dwt_attn = _mm_tn(dp_attn, h, "dw_in_attn", 1280, 1024, tk)
    dwt_in = jnp.concatenate([dwt_ret, dwt_gate[0:1024], dwt_attn, dwt_gate[1024:3072]], axis=0)
    (dx, d_ln1), (got_in,) = _dx_call(xs, dx1, dp_ret, dp_gate, dp_attn, ln1_g, wt_in, tm,
                                      _AllToAll([cast(per_dev(dwt_in, c_in))]))
    db_in = jnp.concatenate([db_ret, db_gate[:, 0:1024], db_aq, db_akv, db_gate[:, 1024:3072]], axis=1)
    small = [d_ln1, db_in, d_gn, d_sinks[:, 0:Q_HEADS], d_ln2, d_lnf]
    (got_small,) = _exchange_call(_AllGather([_pack_small(small, loss[0, 0])]), "gather_small")

    transposed = ("w_in", "w_ffn_gate", "w_ffn_up")
    res = {}
    res["w_in"] = _adamw_call(got_in, tr_shard(w_in), tr_shard(m_w_in), tr_shard(v_w_in), "adamw_w_in", 160)
    res["w_ffn_gate"] = _adamw_call(got_ff, tr_shard(w_ffn_gate), tr_shard(m_w_ffn_gate), tr_shard(v_w_ffn_gate),
                                    "adamw_ffn_gate", 176)
    res["w_ffn_up"] = _adamw_call(got_ff, tr_shard(w_ffn_up), tr_shard(m_w_ffn_up), tr_shard(v_w_ffn_up),
                                  "adamw_ffn_up", 176, row0=c_ff // 176)
    tr = 32
    res["w_ret_out"] = _adamw_call(got_rows, w_ret_out[0], m_w_ret_out[0], v_w_ret_out[0], "adamw_ret_out", tr, row0=0)
    res["w_attn_out"] = _adamw_call(got_rows, w_attn_out[0], m_w_attn_out[0], v_w_attn_out[0], "adamw_attn_out", tr,
                                    row0=r_sq // tr)
    res["w_out"] = _adamw_call(got_rows, w_out[0], m_w_out[0], v_w_out[0], "adamw_out", tr, row0=2 * r_sq // tr)
    res["w_ffn_down"] = _adamw_call(got_rows, w_ffn_down[0], m_w_ffn_down[0], v_w_ffn_down[0], "adamw_ffn_down", tr,
                                    row0=3 * r_sq // tr)
    small_names = ["ln1_g", "b_in", "ret_norm_g", "attn_sinks", "ln2_g", "lnf_g"]
    small_res, loss_row = _adamw_small_call(
        got_small, [ln1_g, b_in, ret_norm_g, attn_sinks, ln2_g, lnf_row],
        [m_ln1_g, m_b_in, m_ret_norm_g, m_attn_sinks, m_ln2_g, m_lnf_g.reshape(1, D)],
        [v_ln1_g, v_b_in, v_ret_norm_g, v_attn_sinks, v_ln2_g, v_lnf_g.reshape(1, D)])
    for i, nm in enumerate(small_names):
        res[nm] = [small_res[kind][i] for kind in range(4)]

    order = ["ln1_g", "w_in", "b_in", "ret_norm_g", "w_ret_out", "attn_sinks", "w_attn_out", "w_out", "ln2_g",
             "w_ffn_gate", "w_ffn_up", "w_ffn_down", "lnf_g"]
    outs = [loss_row[0, 0], dx[None]]
    for kind in range(4):
        for nm in order:
            val = res[nm][kind]
            if nm in transposed:
                val = val.T
            outs.append(val[None] if nm.startswith("w_") else val.reshape(D) if nm == "lnf_g" else val)
    return tuple(outs)
```

```python
import math

import numpy as np
import jax
import jax.numpy as jnp
from jax import lax
from jax.experimental import pallas as pl
from jax.experimental.pallas import tpu as pltpu

F32 = jnp.float32
_MXU = jnp.bfloat16

N_DEV = 8
D = 1024
RET_HEADS, RET_DK, RET_DV = 4, 128, 256
BLK = 128
Q_HEADS, KV_HEADS, HEAD_DIM = 16, 2, 64
GROUP = Q_HEADS // KV_HEADS
D_FF = 2816
N_RET, N_GATE, N_ATTN = 2048, 3072, 1280
D_IN = N_RET + N_GATE + N_ATTN
ROPE_THETA = 10000.0
EPS = 1e-6
RET_SCALE = RET_DK ** -0.5
ATTN_SCALE = HEAD_DIM ** -0.5
LR, B1, B2, ADAM_EPS, WD, STEP = 0.001, 0.9, 0.999, 1e-08, 0.01, 10
VMEM_LIMIT_MB = 56
MESH = pl.DeviceIdType.MESH


def _dot(a, b):
    return jnp.dot(a.astype(_MXU), b.astype(_MXU), preferred_element_type=F32)


def _dot_nt(a, b):
    return lax.dot_general(a.astype(_MXU), b.astype(_MXU), (((1,), (1,)), ((), ())), preferred_element_type=F32)


def _dot_tn(a, b):
    return lax.dot_general(a.astype(_MXU), b.astype(_MXU), (((0,), (0,)), ((), ())), preferred_element_type=F32)


def _sigmoid(x):
    return 1.0 / (1.0 + jnp.exp(-x))


def _cparams(n_axes, big=False):
    kw = dict(dimension_semantics=("arbitrary",) * n_axes)
    if big:
        kw["vmem_limit_bytes"] = VMEM_LIMIT_MB * 2**20
    return pltpu.CompilerParams(**kw)


def _rows(tm, width, col=0):
    return pl.BlockSpec((tm, width), lambda i: (i, col))


def _const(shape):
    nd = len(shape)
    return pl.BlockSpec(shape, lambda *_: (0,) * nd, pipeline_mode=pl.Buffered(1))


def _acc(width):
    return pl.BlockSpec((1, width), lambda *_: (0, 0))


def _sds(shape, dtype):
    return jax.ShapeDtypeStruct(shape, dtype)


def _swap_halves(x, half):
    w = x.shape[-1]
    if 2 * half == w:
        return pltpu.roll(x, half, 1)
    lane = lax.broadcasted_iota(jnp.int32, x.shape, 1)
    return jnp.where(lane % (2 * half) < half, pltpu.roll(x, w - half, 1), pltpu.roll(x, half, 1))


def _rope_tables(seq, half):
    lane = jnp.arange(128, dtype=jnp.int32)
    inv_freq = ROPE_THETA ** (-(lane % half).astype(F32) / half)
    sign = jnp.where(lane % (2 * half) < half, -1.0, 1.0).astype(F32)
    ang = jnp.arange(seq, dtype=jnp.int32).astype(F32)[:, None] * inv_freq[None, :]
    return jnp.cos(ang), jnp.sin(ang) * sign[None, :]


def _retention_decays():
    log_gamma = np.log1p(-np.exp2(-5.0 - np.arange(RET_HEADS, dtype=np.float32))).astype(np.float32)
    idx = np.arange(BLK, dtype=np.float32)
    rel = idx[:, None] - idx[None, :]
    intra = np.where(rel[None] >= 0, np.exp(log_gamma[:, None, None] * np.maximum(rel, 0.0)[None]), 0.0)
    q_decay = np.exp(log_gamma[:, None] * (idx + 1.0))[:, :, None]
    k_decay = np.exp(log_gamma[:, None] * (BLK - 1.0 - idx))[:, :, None]
    chunk_decay = [float(np.exp(np.float32(lg * BLK))) for lg in log_gamma]
    return (jnp.asarray(intra, F32), jnp.asarray(q_decay, F32), jnp.asarray(k_decay, F32), chunk_decay)


def _position():
    return lax.axis_index("x"), lax.axis_index("y"), lax.axis_index("c")


def _slot(px, py, pc):
    return 4 * px + 2 * py + pc


class _AllGather:
    def __init__(self, blocks):
        self.blocks = list(blocks)
        nb = len(self.blocks)
        self.out_shape = [_sds((N_DEV,) + b.shape, b.dtype) for b in self.blocks]
        self.scratch = [pltpu.SemaphoreType.DMA((nb, 7)), pltpu.SemaphoreType.DMA((nb, 7)),
                        pltpu.SemaphoreType.DMA((nb,))]

    def phases(self, ins, outs, send_sems, recv_sems, local_sems):
        nb = len(ins)
        x, y, c = _position()
        me, sibling = (x, y, c), (x, y, 1 - c)
        chips = [(1 - x, y), (x, 1 - y), (1 - x, 1 - y)]

        def copy(b, k, block, to, src=None):
            dst = outs[b].at[_slot(*block)]
            return pltpu.make_async_remote_copy(
                src_ref=dst if src is None else src, dst_ref=dst, send_sem=send_sems.at[b, k],
                recv_sem=recv_sems.at[b, k], device_id=to, device_id_type=MESH)

        def own(b):
            return pltpu.make_async_copy(ins[b], outs[b].at[_slot(*me)], local_sems.at[b])

        def first(b):
            return [copy(b, 0, me, sibling, src=ins[b])] + [
                copy(b, 1 + j, me, (*chip, c), src=ins[b]) for j, chip in enumerate(chips)]

        def start():
            for b in range(nb):
                own(b).start()
                for cp in first(b):
                    cp.start()

        def forward():
            for b in range(nb):
                for j, chip in enumerate(chips):
                    copy(b, 1 + j, (*chip, c), me).wait_recv()
                    copy(b, 4 + j, (*chip, c), sibling).start()

        def finish():
            for b in range(nb):
                copy(b, 0, sibling, me).wait_recv()
                for j, chip in enumerate(chips):
                    copy(b, 4 + j, (*chip, 1 - c), me).wait_recv()
            for b in range(nb):
                for cp in first(b):
                    cp.wait_send()
                for j, chip in enumerate(chips):
                    copy(b, 4 + j, (*chip, c), sibling).wait_send()
                own(b).wait()

        return start, forward, finish


class _AllToAll:
    def __init__(self, blocks):
        self.blocks = list(blocks)
        nb = len(self.blocks)
        self.out_shape = [_sds(b.shape, b.dtype) for b in self.blocks]
        self.scratch = [pltpu.SemaphoreType.DMA((nb, 7)), pltpu.SemaphoreType.DMA((nb, 7)),
                        pltpu.SemaphoreType.DMA((nb,))]

    def phases(self, ins, outs, send_sems, recv_sems, local_sems):
        nb = len(ins)
        x, y, c = _position()
        flip = lambda v, bit: 1 - v if bit else v
        peers = [(flip(x, k >> 2 & 1), flip(y, k >> 1 & 1), flip(c, k & 1)) for k in range(1, N_DEV)]

        def copy(b, k, peer, landed=False):
            return pltpu.make_async_remote_copy(
                src_ref=ins[b].at[_slot(*peer)], dst_ref=outs[b].at[_slot(*peer) if landed else _slot(x, y, c)],
                send_sem=send_sems.at[b, k], recv_sem=recv_sems.at[b, k], device_id=peer, device_id_type=MESH)

        def own(b):
            return pltpu.make_async_copy(ins[b].at[_slot(x, y, c)], outs[b].at[_slot(x, y, c)], local_sems.at[b])

        def start():
            for b in range(nb):
                own(b).start()
                for k, peer in enumerate(peers):
                    copy(b, k, peer).start()

        def forward():
            pass

        def finish():
            for b in range(nb):
                for k, peer in enumerate(peers):
                    copy(b, k, peer, landed=True).wait_recv()
            for b in range(nb):
                for k, peer in enumerate(peers):
                    copy(b, k, peer).wait_send()
                own(b).wait()

        return start, forward, finish


def _call(body, *, name, grid, in_specs, out_specs, out_shape, args, scratch_shapes=(), big=False, exchange=None):
    params = _cparams(len(grid), big)
    if exchange is None:
        return pl.pallas_call(body, name=name, grid=grid, in_specs=in_specs, out_specs=out_specs, out_shape=out_shape,
                              scratch_shapes=list(scratch_shapes), compiler_params=params)(*args)
    n_in, n_out, n_scr, nb = len(in_specs), len(out_specs), len(scratch_shapes), len(exchange.blocks)
    steps = math.prod(grid)

    def carried(*refs):
        pos = 0
        parts = []
        for n in (n_in, nb, n_out, nb, n_scr, len(exchange.scratch)):
            parts.append(refs[pos:pos + n])
            pos += n
        ins, x_ins, outs, x_outs, scr, sems = parts
        step = pl.program_id(0)
        for axis in range(1, len(grid)):
            step = step * grid[axis] + pl.program_id(axis)
        start, forward, finish = exchange.phases(x_ins, x_outs, *sems)
        pl.when(step == 0)(start)
        body(*ins, *outs, *scr)

        @pl.when(step == steps - 1)
        def _():
            forward()
            finish()

    any_spec = pl.BlockSpec(memory_space=pl.ANY)
    res = pl.pallas_call(
        carried, name=name, grid=grid, in_specs=list(in_specs) + [any_spec] * nb,
        out_specs=list(out_specs) + [any_spec] * nb, out_shape=list(out_shape) + exchange.out_shape,
        scratch_shapes=list(scratch_shapes) + exchange.scratch, compiler_params=params)(*args, *exchange.blocks)
    return res[:n_out], res[n_out:]


def _exchange_call(exchange, name):
    nb = len(exchange.blocks)

    def body(*refs):
        start, forward, finish = exchange.phases(refs[:nb], refs[nb:2 * nb], *refs[2 * nb:])
        start()
        forward()
        finish()

    any_spec = pl.BlockSpec(memory_space=pl.ANY)
    return pl.pallas_call(body, name=name, in_specs=[any_spec] * nb, out_specs=[any_spec] * nb,
                          out_shape=exchange.out_shape, scratch_shapes=exchange.scratch)(*exchange.blocks)


def _ln_call(x, g, tm, exchange):
    s = x.shape[0]

    def body(x_ref, g_ref, h_ref):
        xv = x_ref[...]
        r = lax.rsqrt(jnp.mean(xv * xv, axis=-1, keepdims=True) + EPS)
        h_ref[...] = ((xv * r) * g_ref[...]).astype(h_ref.dtype)

    return _call(body, name="ln1", grid=(s // tm,), in_specs=[_rows(tm, D), _acc(D)], out_specs=[_rows(tm, D)],
                 out_shape=[_sds((s, D), _MXU)], args=(x, g), exchange=exchange)


PROJ_TILE = 256


def _proj_source_tile(j):
    gate_end, attn_end, end = 3072 // PROJ_TILE, 4352 // PROJ_TILE, 6400 // PROJ_TILE
    n_gates = end - attn_end
    return jnp.where(j < gate_end, j, jnp.where(j < gate_end + n_gates, j + (attn_end - gate_end), j - n_gates))


def _proj_call(a, wt, bias, exchange):
    s, k = a.shape
    n = wt.shape[0]
    rows = min(1024, s)

    def body(a_ref, w_ref, b_ref, o_ref):
        for r in range(0, s, rows):
            o_ref[r:r + rows, :] = _dot_nt(a_ref[r:r + rows, :], w_ref[...]) + b_ref[...]

    return _call(body, name="proj", grid=(n // PROJ_TILE,),
                 in_specs=[_const((s, k)), pl.BlockSpec((PROJ_TILE, k), lambda j: (_proj_source_tile(j), 0)),
                           pl.BlockSpec((1, PROJ_TILE), lambda j: (0, _proj_source_tile(j)))],
                 out_specs=[pl.BlockSpec((s, PROJ_TILE), lambda j: (0, j))], out_shape=[_sds((s, n), F32)],
                 args=(a, wt, bias), big=True, exchange=exchange)


def _mm_tn(a, b, name, tm, tn, tk, exchange=None):
    s, m = a.shape
    n = b.shape[1]
    last = s // tk - 1

    def body(a_ref, b_ref, o_ref, acc):
        k = pl.program_id(2)
        part = _dot_tn(a_ref[...], b_ref[...])

        @pl.when(k == 0)
        def _():
            acc[...] = part

        @pl.when(k > 0)
        def _():
            acc[...] += part

        @pl.when(k == last)
        def _():
            o_ref[...] = acc[...].astype(o_ref.dtype)

    res = _call(body, name=name, grid=(m // tm, n // tn, s // tk),
                in_specs=[pl.BlockSpec((tk, tm), lambda i, j, k: (k, i)), pl.BlockSpec((tk, tn), lambda i, j, k: (k, j))],
                out_specs=[pl.BlockSpec((tm, tn), lambda i, j, k: (i, j))], out_shape=[_sds((m, n), _MXU)],
                scratch_shapes=[pltpu.VMEM((tm, tn), F32)], args=(a, b), big=True, exchange=exchange)
    return res[0] if exchange is None else (res[0][0], res[1])


def _ret_fwd_call(proj, cos, sin, decays):
    s = proj.shape[0]
    nblk = s // BLK
    intra, q_decay, k_decay, chunk_decay = decays

    def body(rq_ref, rk_ref, rv_ref, cos_ref, sin_ref, intra_ref, qd_ref, kd_ref,
             ry_ref, qr_ref, kr_ref, st_ref, state):
        @pl.when(pl.program_id(0) == 0)
        def _():
            state[...] = jnp.zeros_like(state)

        cos_v, sin_v = cos_ref[...], sin_ref[...]
        for h in range(RET_HEADS):
            hk = slice(h * RET_DK, (h + 1) * RET_DK)
            hv = slice(h * RET_DV, (h + 1) * RET_DV)
            q, k = rq_ref[:, hk], rk_ref[:, hk]
            qr = (q * cos_v + _swap_halves(q, RET_DK // 2) * sin_v) * RET_SCALE
            kr = k * cos_v + _swap_halves(k, RET_DK // 2) * sin_v
            v = rv_ref[:, hv]
            s_h = state[h]
            st_ref[0, h] = s_h.astype(st_ref.dtype)
            scores = _dot_nt(qr, kr) * intra_ref[h]
            ry_ref[:, hv] = _dot(scores, v) + _dot(qr, s_h) * qd_ref[h]
            state[h] = s_h * chunk_decay[h] + _dot_tn(kr * kd_ref[h], v)
            qr_ref[:, hk] = qr.astype(qr_ref.dtype)
            kr_ref[:, hk] = kr.astype(kr_ref.dtype)

    blk = lambda w, c: pl.BlockSpec((BLK, w), lambda n: (n, c))
    return _call(body, name="ret_fwd", grid=(nblk,),
                 in_specs=[blk(512, 0), blk(512, 1), blk(1024, 1), blk(128, 0), blk(128, 0),
                           _const(intra.shape), _const(q_decay.shape), _const(k_decay.shape)],
                 out_specs=[blk(1024, 0), blk(512, 0), blk(512, 0),
                            pl.BlockSpec((1, RET_HEADS, RET_DK, RET_DV), lambda n: (n, 0, 0, 0))],
                 out_shape=[_sds((s, 1024), F32), _sds((s, 512), _MXU), _sds((s, 512), _MXU),
                            _sds((nblk, RET_HEADS, RET_DK, RET_DV), _MXU)],
                 scratch_shapes=[pltpu.VMEM((RET_HEADS, RET_DK, RET_DV), F32)],
                 args=(proj, proj, proj, cos, sin, intra, q_decay, k_decay))


def _ret_bwd_call(qr, kr, proj, states, dry, cos, sin, decays):
    s = qr.shape[0]
    nblk = s // BLK
    intra, q_decay, k_decay, chunk_decay = decays

    def body(qr_ref, kr_ref, rv_ref, st_ref, dry_ref, cos_ref, sin_ref, intra_ref, qd_ref, kd_ref,
             dp_ref, db_ref, dstate):
        @pl.when(pl.program_id(0) == 0)
        def _():
            dstate[...] = jnp.zeros_like(dstate)
            db_ref[...] = jnp.zeros_like(db_ref)

        cos_v, sin_v = cos_ref[...], sin_ref[...]
        for h in range(RET_HEADS):
            hk = slice(h * RET_DK, (h + 1) * RET_DK)
            hv = slice(h * RET_DV, (h + 1) * RET_DV)
            q, k, v, d_out = qr_ref[:, hk], kr_ref[:, hk], rv_ref[:, hv], dry_ref[:, hv]
            d_next = dstate[h]
            scores = _dot_nt(q, k) * intra_ref[h]
            d_scores = _dot_nt(d_out, v) * intra_ref[h]
            d_cross = d_out * qd_ref[h]
            dq = _dot(d_scores, k) + _dot_nt(d_cross, st_ref[0, h])
            dk = _dot_tn(d_scores, q) + _dot_nt(v, d_next) * kd_ref[h]
            dv = _dot_tn(scores, d_out) + _dot(k.astype(F32) * kd_ref[h], d_next)
            dstate[h] = d_next * chunk_decay[h] + _dot_tn(q, d_cross)
            dq = (dq * cos_v - _swap_halves(dq, RET_DK // 2) * sin_v) * RET_SCALE
            dk = dk * cos_v - _swap_halves(dk, RET_DK // 2) * sin_v
            dp_ref[:, hk] = dq.astype(dp_ref.dtype)
            dp_ref[:, slice(512 + h * RET_DK, 512 + (h + 1) * RET_DK)] = dk.astype(dp_ref.dtype)
            dp_ref[:, slice(1024 + h * RET_DV, 1024 + (h + 1) * RET_DV)] = dv.astype(dp_ref.dtype)
            db_ref[:, hk] += jnp.sum(dq, axis=0, keepdims=True)
            db_ref[:, slice(512 + h * RET_DK, 512 + (h + 1) * RET_DK)] += jnp.sum(dk, axis=0, keepdims=True)
            db_ref[:, slice(1024 + h * RET_DV, 1024 + (h + 1) * RET_DV)] += jnp.sum(dv, axis=0, keepdims=True)

    rblk = lambda w, c: pl.BlockSpec((BLK, w), lambda n: (nblk - 1 - n, c))
    return _call(body, name="ret_bwd", grid=(nblk,),
                 in_specs=[rblk(512, 0), rblk(512, 0), rblk(1024, 1),
                           pl.BlockSpec((1, RET_HEADS, RET_DK, RET_DV), lambda n: (nblk - 1 - n, 0, 0, 0)),
                           rblk(1024, 0), rblk(128, 0), rblk(128, 0),
                           _const(intra.shape), _const(q_decay.shape), _const(k_decay.shape)],
                 out_specs=[rblk(N_RET, 0), _acc(N_RET)],
                 out_shape=[_sds((s, N_RET), _MXU), _sds((1, N_RET), F32)],
                 scratch_shapes=[pltpu.VMEM((RET_HEADS, RET_DK, RET_DV), F32)],
                 args=(qr, kr, proj, states, dry, cos, sin, intra, q_decay, k_decay))


def _both_halves(x, g):
    lane = lax.broadcasted_iota(jnp.int32, x.shape, 1)
    keep = lane < HEAD_DIM if g == 0 else lane >= HEAD_DIM
    return jnp.where(keep, x, pltpu.roll(x, HEAD_DIM, 1))


def _stack_heads(ref, g):
    lane = lax.broadcasted_iota(jnp.int32, (BLK, 128), 1)
    pieces = []
    for j in range(g * 4, g * 4 + 4):
        chunk = ref[:, j * 128:(j + 1) * 128]
        pieces += [jnp.where(lane < HEAD_DIM, chunk, jnp.zeros_like(chunk)),
                   jnp.where(lane >= HEAD_DIM, chunk, jnp.zeros_like(chunk))]
    return jnp.concatenate(pieces, axis=0)


def _unstack_heads(stacked, jj):
    lane = lax.broadcasted_iota(jnp.int32, (BLK, 128), 1)
    return jnp.where(lane < HEAD_DIM, stacked[2 * jj * BLK:(2 * jj + 1) * BLK], stacked[(2 * jj + 1) * BLK:(2 * jj + 2) * BLK])


ROWS = 8


def _window_bias(first_block):
    qi = lax.broadcasted_iota(jnp.int32, (BLK, 2 * BLK), 0)
    kj = lax.broadcasted_iota(jnp.int32, (BLK, 2 * BLK), 1)
    first_key = jnp.where(first_block, BLK, 0)
    seen = (kj > qi) & (kj <= qi + BLK) & (kj >= first_key)
    return jnp.where(seen, 0.0, -1e30)


def _sink_softmax(scores, sink):
    m = jnp.maximum(jnp.max(scores, axis=-1, keepdims=True), sink)
    e = jnp.exp(scores - m)
    e_sink = jnp.exp(sink - m)
    return e, e_sink, 1.0 / (jnp.sum(e, axis=-1, keepdims=True) + e_sink)


def _attn_fwd_call(proj, sinks, cos, sin):
    s = proj.shape[0]
    nblk = s // BLK

    def body(sink_ref, q_ref, k_ref, v_ref, cos_ref, sin_ref, ay_ref, qr_ref, kr_ref, vb_ref,
             kwin, vwin, bias, s_scr, p_scr):
        n = pl.program_id(0)

        @pl.when(n == 0)
        def _():
            kwin[...] = jnp.zeros_like(kwin)
            vwin[...] = jnp.zeros_like(vwin)

        @pl.when(n > 0)
        def _():
            kwin[0:BLK] = kwin[BLK:2 * BLK]
            vwin[0:BLK] = vwin[BLK:2 * BLK]

        cos_v, sin_v = cos_ref[...], sin_ref[...]
        k = k_ref[...]
        kr = (k * cos_v + _swap_halves(k, HEAD_DIM // 2) * sin_v).astype(kwin.dtype)
        kwin[BLK:2 * BLK] = kr
        vwin[BLK:2 * BLK] = v_ref[...].astype(vwin.dtype)
        kr_ref[...] = kr
        vb_ref[...] = vwin[BLK:2 * BLK]
        for j in range(Q_HEADS // 2):
            cols = slice(j * 128, (j + 1) * 128)
            q = q_ref[:, cols]
            qr_ref[:, cols] = ((q * cos_v + _swap_halves(q, HEAD_DIM // 2) * sin_v) * ATTN_SCALE).astype(qr_ref.dtype)
        bias[...] = _window_bias(n == 0)
        for g in range(KV_HEADS):
            kg = _both_halves(kwin[...], g)
            vg = _both_halves(vwin[...], g)
            s_scr[...] = _dot_nt(_stack_heads(qr_ref, g), kg)
            for r in range(0, GROUP * BLK, ROWS):
                rows = slice(r, r + ROWS)
                e, _, inv = _sink_softmax(s_scr[rows, :] + bias[r % BLK:r % BLK + ROWS, :],
                                          sink_ref[0, g * GROUP + r // BLK])
                p_scr[rows, :] = (e * inv).astype(p_scr.dtype)
            out = _dot(p_scr[...], vg)
            for jj in range(4):
                j = g * 4 + jj
                ay_ref[:, j * 128:(j + 1) * 128] = _unstack_heads(out, jj).astype(ay_ref.dtype)

    blk = lambda w, c: pl.BlockSpec((BLK, w), lambda n: (n, c))
    off = (N_RET + N_GATE) // 128
    return _call(body, name="attn_fwd", grid=(nblk,),
                 in_specs=[pl.BlockSpec(memory_space=pltpu.SMEM), blk(1024, off // 8), blk(128, off + 8), blk(128, off + 9),
                           blk(128, 0), blk(128, 0)],
                 out_specs=[blk(1024, 0), blk(1024, 0), blk(128, 0), blk(128, 0)],
                 out_shape=[_sds((s, 1024), _MXU), _sds((s, 1024), _MXU), _sds((s, 128), _MXU), _sds((s, 128), _MXU)],
                 scratch_shapes=[pltpu.VMEM((2 * BLK, 128), _MXU), pltpu.VMEM((2 * BLK, 128), _MXU),
                                 pltpu.VMEM((BLK, 2 * BLK), F32), pltpu.VMEM((GROUP * BLK, 2 * BLK), F32),
                                 pltpu.VMEM((GROUP * BLK, 2 * BLK), _MXU)],
                 args=(sinks, proj, proj, proj, cos, sin))


def _attn_bwd_call(qr, kr, vb, day, sinks, cos, sin):
    s = qr.shape[0]
    nblk = s // BLK

    def body(sink_ref, q_ref, kc_ref, kp_ref, vc_ref, vp_ref, do_ref, cos_ref, sin_ref,
             dq_ref, cur_ref, prev_ref, dsink_ref, db_ref, bias, s_scr, dp_scr, p_scr, ds_scr):
        n = pl.program_id(0)

        @pl.when(n == 0)
        def _():
            dsink_ref[...] = jnp.zeros_like(dsink_ref)
            db_ref[...] = jnp.zeros_like(db_ref)

        cos_v, sin_v = cos_ref[...], sin_ref[...]
        bias[...] = _window_bias(n == 0)
        lane1 = lax.broadcasted_iota(jnp.int32, (1, 128), 1)
        kwin = jnp.concatenate([kp_ref[...], kc_ref[...]], axis=0)
        vwin = jnp.concatenate([vp_ref[...], vc_ref[...]], axis=0)
        dk_t, dv_t = [], []
        dsink = jnp.zeros((1, 128), F32)
        for g in range(KV_HEADS):
            kg = _both_halves(kwin, g)
            vg = _both_halves(vwin, g)
            q_all = _stack_heads(q_ref, g)
            do_all = _stack_heads(do_ref, g)
            s_scr[...] = _dot_nt(q_all, kg)
            dp_scr[...] = _dot_nt(do_all, vg)
            for head in range(g * GROUP, (g + 1) * GROUP):
                sink_part = jnp.zeros((ROWS, 1), F32)
                for i0 in range(0, BLK, ROWS):
                    rows = slice((head - g * GROUP) * BLK + i0, (head - g * GROUP) * BLK + i0 + ROWS)
                    e, e_sink, inv = _sink_softmax(s_scr[rows, :] + bias[i0:i0 + ROWS, :], sink_ref[0, head])
                    p = e * inv
                    dp = dp_scr[rows, :]
                    delta = jnp.sum(p * dp, axis=-1, keepdims=True)
                    p_scr[rows, :] = p.astype(p_scr.dtype)
                    ds_scr[rows, :] = (p * (dp - delta)).astype(ds_scr.dtype)
                    sink_part = sink_part + e_sink * inv * delta
                dsink = dsink + jnp.where(lane1 == head, -jnp.sum(sink_part, axis=0, keepdims=True), 0.0)
            dv_both = _dot_tn(do_all, p_scr[...])
            dk_both = _dot_tn(q_all, ds_scr[...])
            dv_t.append(dv_both[0:HEAD_DIM] + dv_both[HEAD_DIM:128])
            dk_t.append(dk_both[0:HEAD_DIM] + dk_both[HEAD_DIM:128])
            dq_all = _dot(ds_scr[...], kg)
            for jj in range(4):
                cols = slice((g * 4 + jj) * 128, (g * 4 + jj + 1) * 128)
                dq = _unstack_heads(dq_all, jj)
                dq = (dq * cos_v - _swap_halves(dq, HEAD_DIM // 2) * sin_v) * ATTN_SCALE
                dq_ref[:, cols] = dq.astype(dq_ref.dtype)
                db_ref[:, cols] += jnp.sum(dq, axis=0, keepdims=True)
        dsink_ref[...] += dsink
        dk_all = jnp.concatenate(dk_t, axis=0).T
        dv_all = jnp.concatenate(dv_t, axis=0).T
        prev_ref[:, 0:128] = dk_all[0:BLK]
        prev_ref[:, 128:256] = dv_all[0:BLK]
        cur_ref[:, 0:128] = dk_all[BLK:2 * BLK]
        cur_ref[:, 128:256] = dv_all[BLK:2 * BLK]

    blk = lambda w, c: pl.BlockSpec((BLK, w), lambda n: (n, c))
    pblk = lambda w: pl.BlockSpec((BLK, w), lambda n: (jnp.maximum(n - 1, 0), 0))
    big = (GROUP * BLK, 2 * BLK)
    return _call(body, name="attn_bwd", grid=(nblk,),
                 in_specs=[pl.BlockSpec(memory_space=pltpu.SMEM), blk(1024, 0), blk(128, 0), pblk(128), blk(128, 0),
                           pblk(128), blk(1024, 0), blk(128, 0), blk(128, 0)],
                 out_specs=[blk(1024, 0), blk(256, 0), blk(256, 0), _acc(128), _acc(1024)],
                 out_shape=[_sds((s, 1024), _MXU), _sds((s, 256), F32), _sds((s, 256), F32), _sds((1, 128), F32),
                            _sds((1, 1024), F32)],
                 scratch_shapes=[pltpu.VMEM((BLK, 2 * BLK), F32), pltpu.VMEM(big, F32), pltpu.VMEM(big, F32),
                                 pltpu.VMEM(big, _MXU), pltpu.VMEM(big, _MXU)],
                 args=(sinks, qr, kr, kr, vb, vb, day, cos, sin))


def _attn_combine_call(dq, dkv_cur, dkv_prev, cos, sin):
    s = dq.shape[0]
    nblk = s // BLK

    def body(dq_ref, cur_ref, nxt_ref, cos_ref, sin_ref, dp_ref, db_ref):
        n = pl.program_id(0)

        @pl.when(n == 0)
        def _():
            db_ref[...] = jnp.zeros_like(db_ref)

        dkv = cur_ref[...] + nxt_ref[...] * (n < nblk - 1).astype(F32)
        dk = dkv[:, 0:128]
        dk = dk * cos_ref[...] - _swap_halves(dk, HEAD_DIM // 2) * sin_ref[...]
        dv = dkv[:, 128:256]
        dp_ref[:, 0:1024] = dq_ref[...]
        dp_ref[:, 1024:1152] = dk.astype(dp_ref.dtype)
        dp_ref[:, 1152:1280] = dv.astype(dp_ref.dtype)
        db_ref[:, 0:128] += jnp.sum(dk, axis=0, keepdims=True)
        db_ref[:, 128:256] += jnp.sum(dv, axis=0, keepdims=True)

    blk = lambda w: pl.BlockSpec((BLK, w), lambda n: (n, 0))
    return _call(body, name="attn_combine", grid=(nblk,),
                 in_specs=[blk(1024), blk(256), pl.BlockSpec((BLK, 256), lambda n: (jnp.minimum(n + 1, nblk - 1), 0)),
                           blk(128), blk(128)],
                 out_specs=[blk(N_ATTN), _acc(256)],
                 out_shape=[_sds((s, N_ATTN), _MXU), _sds((1, 256), F32)],
                 args=(dq, dkv_cur, dkv_prev, cos, sin))


def _group_norm(y):
    mu = jnp.mean(y, axis=-1, keepdims=True)
    yc = y - mu
    rs = lax.rsqrt(jnp.mean(yc * yc, axis=-1, keepdims=True) + EPS)
    return yc * rs, rs


GATE_COL = N_RET // 1024


def _merge_fwd_call(x, ry, proj, ay, gn_g, w_ro, w_ao, w_o, tm):
    s = x.shape[0]

    def body(x_ref, ry_ref, rg_ref, ga_ref, gb_ref, ay_ref, gn_ref, wro_ref, wao_ref, wo_ref,
             ain_ref, a_ref, b_ref, mg_ref, x1_ref):
        for h in range(RET_HEADS):
            hv = slice(h * RET_DV, (h + 1) * RET_DV)
            yhat, _ = _group_norm(ry_ref[:, hv])
            rg = rg_ref[:, hv]
            ain_ref[:, hv] = ((rg * _sigmoid(rg)) * (yhat * gn_ref[:, hv])).astype(ain_ref.dtype)
        a = _dot(ain_ref[...], wro_ref[...])
        b = _dot(ay_ref[...], wao_ref[...])
        a_ref[...] = a
        b_ref[...] = b
        merged = (_sigmoid(ga_ref[...]) * a + _sigmoid(gb_ref[...]) * b).astype(mg_ref.dtype)
        mg_ref[...] = merged
        x1_ref[...] = x_ref[...] + _dot(merged, wo_ref[...])

    return _call(body, name="merge_fwd", grid=(s // tm,),
                 in_specs=[_rows(tm, D), _rows(tm, 1024), _rows(tm, 1024, GATE_COL), _rows(tm, 1024, GATE_COL + 1),
                           _rows(tm, 1024, GATE_COL + 2), _rows(tm, 1024), _acc(1024),
                           _const((D, D)), _const((D, D)), _const((D, D))],
                 out_specs=[_rows(tm, D)] * 5,
                 out_shape=[_sds((s, D), _MXU), _sds((s, D), F32), _sds((s, D), F32), _sds((s, D), _MXU), _sds((s, D), F32)],
                 args=(x, ry, proj, proj, proj, ay, gn_g, w_ro, w_ao, w_o), big=True)


def _merge_bwd_call(dx1, a, b, ry, proj, gn_g, w_ro, w_ao, w_o, tm, exchange):
    s = dx1.shape[0]

    def body(dx1_ref, a_ref, b_ref, ry_ref, rg_ref, ga_ref, gb_ref, gn_ref, wro_ref, wao_ref, wo_ref,
             da_ref, dbr_ref, dp_ref, day_ref, dry_ref, dbias_ref, dgn_ref):
        @pl.when(pl.program_id(0) == 0)
        def _():
            dbias_ref[...] = jnp.zeros_like(dbias_ref)
            dgn_ref[...] = jnp.zeros_like(dgn_ref)

        d_merged = _dot_nt(dx1_ref[...], wo_ref[...])
        sa, sb = _sigmoid(ga_ref[...]), _sigmoid(gb_ref[...])
        d_a = d_merged * sa
        d_b = d_merged * sb
        da_ref[...] = d_a.astype(da_ref.dtype)
        dbr_ref[...] = d_b.astype(dbr_ref.dtype)
        d_ga = d_merged * a_ref[...] * (sa * (1.0 - sa))
        d_gb = d_merged * b_ref[...] * (sb * (1.0 - sb))
        dp_ref[:, 1024:2048] = d_ga.astype(dp_ref.dtype)
        dp_ref[:, 2048:3072] = d_gb.astype(dp_ref.dtype)
        dbias_ref[:, 1024:2048] += jnp.sum(d_ga, axis=0, keepdims=True)
        dbias_ref[:, 2048:3072] += jnp.sum(d_gb, axis=0, keepdims=True)
        day_ref[...] = _dot_nt(d_b, wao_ref[...]).astype(day_ref.dtype)
        d_ain = _dot_nt(d_a, wro_ref[...])
        for h in range(RET_HEADS):
            hv = slice(h * RET_DV, (h + 1) * RET_DV)
            yhat, rs = _group_norm(ry_ref[:, hv])
            rg = rg_ref[:, hv]
            sg = _sigmoid(rg)
            gn = gn_ref[:, hv]
            d_h = d_ain[:, hv]
            d_rg = d_h * (yhat * gn) * (sg * (1.0 + rg * (1.0 - sg)))
            d_ryn = d_h * (rg * sg)
            dgn_ref[:, hv] += jnp.sum(d_ryn * yhat, axis=0, keepdims=True)
            d_yhat = d_ryn * gn
            dry_ref[:, hv] = rs * (d_yhat - jnp.mean(d_yhat, axis=-1, keepdims=True)
                                   - yhat * jnp.mean(d_yhat * yhat, axis=-1, keepdims=True))
            dp_ref[:, hv] = d_rg.astype(dp_ref.dtype)
            dbias_ref[:, hv] += jnp.sum(d_rg, axis=0, keepdims=True)

    return _call(body, name="merge_bwd", grid=(s // tm,),
                 in_specs=[_rows(tm, D), _rows(tm, D), _rows(tm, D), _rows(tm, 1024), _rows(tm, 1024, GATE_COL),
                           _rows(tm, 1024, GATE_COL + 1), _rows(tm, 1024, GATE_COL + 2), _acc(1024),
                           _const((D, D)), _const((D, D)), _const((D, D))],
                 out_specs=[_rows(tm, D), _rows(tm, D), _rows(tm, N_GATE), _rows(tm, D), _rows(tm, D), _acc(N_GATE),
                            _acc(1024)],
                 out_shape=[_sds((s, D), _MXU), _sds((s, D), _MXU), _sds((s, N_GATE), _MXU), _sds((s, D), _MXU),
                            _sds((s, D), F32), _sds((1, N_GATE), F32), _sds((1, 1024), F32)],
                 args=(dx1, a, b, ry, proj, proj, proj, gn_g, w_ro, w_ao, w_o), big=True, exchange=exchange)


def _ffn_fwd_call(x1, target, ln2_g, lnf_g, w_g, w_u, w_d, tm):
    s = x1.shape[0]

    def body(x1_ref, t_ref, g2_ref, gf_ref, wg_ref, wu_ref, wd_ref,
             h2_ref, g_ref, u_ref, f_ref, dx2_ref, loss_ref, dgf_ref):
        @pl.when(pl.program_id(0) == 0)
        def _():
            loss_ref[...] = jnp.zeros_like(loss_ref)
            dgf_ref[...] = jnp.zeros_like(dgf_ref)

        x1v = x1_ref[...]
        r1 = lax.rsqrt(jnp.mean(x1v * x1v, axis=-1, keepdims=True) + EPS)
        h2 = ((x1v * r1) * g2_ref[...]).astype(h2_ref.dtype)
        h2_ref[...] = h2
        g = _dot_nt(h2, wg_ref[...])
        u = _dot_nt(h2, wu_ref[...])
        g_ref[...] = g
        u_ref[...] = u
        f = ((g * _sigmoid(g)) * u).astype(f_ref.dtype)
        f_ref[...] = f
        x2 = x1v + _dot(f, wd_ref[...])
        r2 = lax.rsqrt(jnp.mean(x2 * x2, axis=-1, keepdims=True) + EPS)
        xhat = x2 * r2
        err = xhat * gf_ref[...] - t_ref[...]
        loss_ref[...] += 0.5 * jnp.sum(jnp.mean(err * err, axis=-1, keepdims=True))
        dy = err * (1.0 / D)
        dgf_ref[...] += jnp.sum(dy * xhat, axis=0, keepdims=True)
        dxh = dy * gf_ref[...]
        dx2_ref[...] = r2 * (dxh - xhat * jnp.mean(dxh * xhat, axis=-1, keepdims=True))

    return _call(body, name="ffn_fwd", grid=(s // tm,),
                 in_specs=[_rows(tm, D), _rows(tm, D), _acc(D), _acc(D), _const((D_FF, D)), _const((D_FF, D)),
                           _const((D_FF, D))],
                 out_specs=[_rows(tm, D), _rows(tm, D_FF), _rows(tm, D_FF), _rows(tm, D_FF), _rows(tm, D), _acc(128),
                            _acc(D)],
                 out_shape=[_sds((s, D), _MXU), _sds((s, D_FF), F32), _sds((s, D_FF), F32), _sds((s, D_FF), _MXU),
                            _sds((s, D), F32), _sds((1, 128), F32), _sds((1, D), F32)],
                 args=(x1, target, ln2_g, lnf_g, w_g, w_u, w_d), big=True)


def _ffn_bwd_call(dx2, x1, g, u, ln2_g, w_g, w_u, w_d, tm):
    s = dx2.shape[0]

    def body(dx2_ref, x1_ref, g_ref, u_ref, g2_ref, wg_ref, wu_ref, wd_ref, dx1_ref, dg_ref, du_ref, dg2_ref):
        @pl.when(pl.program_id(0) == 0)
        def _():
            dg2_ref[...] = jnp.zeros_like(dg2_ref)

        dx2v = dx2_ref[...]
        df = _dot_nt(dx2v, wd_ref[...])
        gv, uv = g_ref[...], u_ref[...]
        sg = _sigmoid(gv)
        du = (df * (gv * sg)).astype(du_ref.dtype)
        dg = (df * uv * (sg * (1.0 + gv * (1.0 - sg)))).astype(dg_ref.dtype)
        du_ref[...] = du
        dg_ref[...] = dg
        dh2 = _dot(dg, wg_ref[...]) + _dot(du, wu_ref[...])
        x1v = x1_ref[...]
        r1 = lax.rsqrt(jnp.mean(x1v * x1v, axis=-1, keepdims=True) + EPS)
        xhat = x1v * r1
        dg2_ref[...] += jnp.sum(dh2 * xhat, axis=0, keepdims=True)
        dxh = dh2 * g2_ref[...]
        dx1_ref[...] = dx2v + r1 * (dxh - xhat * jnp.mean(dxh * xhat, axis=-1, keepdims=True))

    return _call(body, name="ffn_bwd", grid=(s // tm,),
                 in_specs=[_rows(tm, D), _rows(tm, D), _rows(tm, D_FF), _rows(tm, D_FF), _acc(D),
                           _const((D_FF, D)), _const((D_FF, D)), _const((D_FF, D))],
                 out_specs=[_rows(tm, D), _rows(tm, D_FF), _rows(tm, D_FF), _acc(D)],
                 out_shape=[_sds((s, D), F32), _sds((s, D_FF), _MXU), _sds((s, D_FF), _MXU), _sds((1, D), F32)],
                 args=(dx2, x1, g, u, ln2_g, w_g, w_u, w_d), big=True)


def _dx_call(x, dx1, dp_ret, dp_gate, dp_attn, ln1_g, w_in, tm, exchange):
    s = x.shape[0]

    def body(x_ref, dx1_ref, dr_ref, dg_ref, da_ref, g1_ref, w_ref, dx_ref, dg1_ref):
        @pl.when(pl.program_id(0) == 0)
        def _():
            dg1_ref[...] = jnp.zeros_like(dg1_ref)

        dh = (_dot(dr_ref[...], w_ref[0:2048, :]) + _dot(dg_ref[:, 0:1024], w_ref[2048:3072, :])
              + _dot(da_ref[...], w_ref[3072:4352, :]) + _dot(dg_ref[:, 1024:3072], w_ref[4352:6400, :]))
        xv = x_ref[...]
        r = lax.rsqrt(jnp.mean(xv * xv, axis=-1, keepdims=True) + EPS)
        xhat = xv * r
        dg1_ref[...] += jnp.sum(dh * xhat, axis=0, keepdims=True)
        dxh = dh * g1_ref[...]
        dx_ref[...] = dx1_ref[...] + r * (dxh - xhat * jnp.mean(dxh * xhat, axis=-1, keepdims=True))

    return _call(body, name="dx", grid=(s // tm,),
                 in_specs=[_rows(tm, D), _rows(tm, D), _rows(tm, N_RET), _rows(tm, N_GATE), _rows(tm, N_ATTN), _acc(D),
                           _const((D_IN, D))],
                 out_specs=[_rows(tm, D), _acc(D)],
                 out_shape=[_sds((s, D), F32), _sds((1, D), F32)],
                 args=(x, dx1, dp_ret, dp_gate, dp_attn, ln1_g, w_in), big=True, exchange=exchange)


def _adamw(g, w, m, v):
    m_new = B1 * m + (1.0 - B1) * g
    v_new = B2 * v + (1.0 - B2) * (g * g)
    m_hat = m_new / (1.0 - B1 ** STEP)
    v_hat = v_new / (1.0 - B2 ** STEP)
    return -LR * (m_hat / (jnp.sqrt(v_hat) + ADAM_EPS) + WD * w), m_new, v_new


def _slot_sum(p_ref):
    g = p_ref[0].astype(F32)
    for k in range(1, N_DEV):
        g = g + p_ref[k].astype(F32)
    return g


def _adamw_call(parts, w, m, v, name, tr):
    rows, cols = w.shape

    def body(p_ref, w_ref, m_ref, v_ref, g_ref, dw_ref, nm_ref, nv_ref):
        g = _slot_sum(p_ref)
        g_ref[...] = g
        dw_ref[...], nm_ref[...], nv_ref[...] = _adamw(g, w_ref[...], m_ref[...], v_ref[...])

    p_spec = pl.BlockSpec((N_DEV, tr, cols), lambda i: (0, i, 0))
    spec = pl.BlockSpec((tr, cols), lambda i: (i, 0))
    return _call(body, name=name, grid=(rows // tr,), in_specs=[p_spec, spec, spec, spec], out_specs=[spec] * 4,
                 out_shape=[_sds((rows, cols), F32)] * 4, args=(parts, w, m, v))


SMALL_WIDTHS = [1024, 6400, 1024, 16, 1024, 1024]
SMALL_OFFSETS = [0, 1024, 7424, 8448, 8576, 9600]
LOSS_OFFSET = 10624
SMALL_LEN = 10752


def _pack_small(grads, loss):
    pieces = []
    for gr, width in zip(grads, SMALL_WIDTHS):
        pieces.append(jnp.pad(gr.reshape(1, width), ((0, 0), (0, -width % 128))))
    pieces.append(jnp.pad(loss.reshape(1, 1), ((0, 0), (0, 127))))
    return jnp.concatenate(pieces, axis=1)


def _adamw_small_call(parts, ws, ms, vs):
    n = len(ws)

    def body(*refs):
        p_ref, w_refs, m_refs, v_refs = refs[0], refs[1:1 + n], refs[1 + n:1 + 2 * n], refs[1 + 2 * n:1 + 3 * n]
        outs = refs[1 + 3 * n:]
        g_all = _slot_sum(p_ref)
        for i, (off, width) in enumerate(zip(SMALL_OFFSETS, SMALL_WIDTHS)):
            g = g_all[:, off:off + width]
            outs[i][...] = g
            outs[n + i][...], outs[2 * n + i][...], outs[3 * n + i][...] = _adamw(
                g, w_refs[i][...], m_refs[i][...], v_refs[i][...])
        outs[4 * n][...] = g_all[:, LOSS_OFFSET:LOSS_OFFSET + 128]

    whole = lambda shape: pl.BlockSpec(shape, lambda i: (0,) * len(shape))
    small = [whole((1, w)) for w in SMALL_WIDTHS]
    res = _call(body, name="adamw_small", grid=(1,), in_specs=[whole((N_DEV, 1, SMALL_LEN))] + small * 3,
                out_specs=small * 4 + [whole((1, 128))],
                out_shape=[_sds((1, w), F32) for w in SMALL_WIDTHS] * 4 + [_sds((1, 128), F32)],
                args=(parts, *ws, *ms, *vs))
    return [res[k * n:(k + 1) * n] for k in range(4)], res[4 * n]


def kernel(x, ln1_g, w_in, b_in, ret_norm_g, w_ret_out, attn_sinks, w_attn_out, w_out, ln2_g, w_ffn_gate, w_ffn_up, w_ffn_down, lnf_g, loss_target, m_ln1_g, m_w_in, m_b_in, m_ret_norm_g, m_w_ret_out, m_attn_sinks, m_w_attn_out, m_w_out, m_ln2_g, m_w_ffn_gate, m_w_ffn_up, m_w_ffn_down, m_lnf_g, v_ln1_g, v_w_in, v_b_in, v_ret_norm_g, v_w_ret_out, v_attn_sinks, v_w_attn_out, v_w_out, v_ln2_g, v_w_ffn_gate, v_w_ffn_up, v_w_ffn_down, v_lnf_g):
    cast = lambda a: a.astype(_MXU)
    xs, target = x[0], loss_target[0]
    s = xs.shape[0]
    r_sq = w_ret_out.shape[1]
    r_dn = w_ffn_down.shape[1]
    c_in = w_in.shape[2]
    c_ff = w_ffn_gate.shape[2]
    tm, tk = min(256, s), min(2048, s)
    lnf_row = lnf_g.reshape(1, D)
    cos_r, sin_r = _rope_tables(s, RET_DK // 2)
    cos_a, sin_a = _rope_tables(s, HEAD_DIM // 2)
    decays = _retention_decays()
    tr_shard = lambda a: a[0].T
    per_dev = lambda a, n: a.reshape(N_DEV, n, D)

    (h,), (all_in,) = _ln_call(xs, ln1_g, tm, _AllGather([cast(tr_shard(w_in))]))
    wt_in = all_in.reshape(N_DEV * c_in, D)
    rest = [tr_shard(w_ffn_gate), tr_shard(w_ffn_up), w_ret_out[0], w_attn_out[0], w_out[0], w_ffn_down[0]]
    (proj,), gathered = _proj_call(h, wt_in, b_in, _AllGather([cast(a) for a in rest]))
    wt_g, wt_u, full_ro, full_ao, full_o, full_d = (a.reshape(N_DEV * a.shape[1], D) for a in gathered)
    ry, qr, kr, states = _ret_fwd_call(proj, cos_r, sin_r, decays)
    ay, aqr, akr, avb = _attn_fwd_call(proj, attn_sinks, cos_a, sin_a)
    a_in, br_a, br_b, merged, x1 = _merge_fwd_call(xs, ry, proj, ay, ret_norm_g, full_ro, full_ao, full_o, tm)
    h2, g, u, f, dx2, loss, d_lnf = _ffn_fwd_call(x1, target, ln2_g, lnf_row, wt_g, wt_u, full_d, tm)

    dx1, dg, du, d_ln2 = _ffn_bwd_call(dx2, x1, g, u, ln2_g, wt_g, wt_u, full_d, tm)
    dw_d = _mm_tn(f, dx2, "dw_ffn_down", 1408, 1024, tk)
    dwt_g, (got_d,) = _mm_tn(dg, h2, "dw_ffn_gate", 1408, 1024, tk, _AllToAll([per_dev(dw_d, r_dn)]))
    dwt_u = _mm_tn(du, h2, "dw_ffn_up", 1408, 1024, tk)
    (d_a, d_b, dp_gate, day, dry, db_gate, d_gn), (got_g, got_u) = _merge_bwd_call(
        dx1, br_a, br_b, ry, proj, ret_norm_g, full_ro, full_ao, full_o, tm,
        _AllToAll([per_dev(dwt_g, c_ff), per_dev(dwt_u, c_ff)]))
    dw_o = _mm_tn(merged, dx1, "dw_out", 1024, 1024, tk)
    dw_ro = _mm_tn(a_in, d_a, "dw_ret_out", 1024, 1024, tk)
    dw_ao = _mm_tn(ay, d_b, "dw_attn_out", 1024, 1024, tk)
    dq, dkv_cur, dkv_prev, d_sinks, db_aq = _attn_bwd_call(aqr, akr, avb, day, attn_sinks, cos_a, sin_a)
    dp_attn, db_akv = _attn_combine_call(dq, dkv_cur, dkv_prev, cos_a, sin_a)
    dp_ret, db_ret = _ret_bwd_call(qr, kr, proj, states, dry, cos_r, sin_r, decays)
    dwt_ret = _mm_tn(dp_ret, h, "dw_in_ret", 1024, 1024, tk)
    dwt_gate, (got_ro, got_ao, got_o) = _mm_tn(
        dp_gate, h, "dw_in_gate", 1024, 1024, tk,
        _AllToAll([per_dev(dw_ro, r_sq), per_dev(dw_ao, r_sq), per_dev(dw_o, r_sq)]))
    dwt_attn = _mm_tn(dp_attn, h, "dw_in_attn", 1280, 1024, tk)
    dwt_in = jnp.concatenate([dwt_ret, dwt_gate[0:1024], dwt_attn, dwt_gate[1024:3072]], axis=0)
    (dx, d_ln1), (got_in,) = _dx_call(xs, dx1, dp_ret, dp_gate, dp_attn, ln1_g, wt_in, tm,
                                      _AllToAll([per_dev(dwt_in, c_in)]))
    db_in = jnp.concatenate([db_ret, db_gate[:, 0:1024], db_aq, db_akv, db_gate[:, 1024:3072]], axis=1)
    small = [d_ln1, db_in, d_gn, d_sinks[:, 0:Q_HEADS], d_ln2, d_lnf]
    (got_small,) = _exchange_call(_AllGather([_pack_small(small, loss[0, 0])]), "gather_small")

    transposed = ("w_in", "w_ffn_gate", "w_ffn_up")
    res = {}
    res["w_in"] = _adamw_call(got_in, tr_shard(w_in), tr_shard(m_w_in), tr_shard(v_w_in), "adamw_w_in", 160)
    res["w_ffn_gate"] = _adamw_call(got_g, tr_shard(w_ffn_gate), tr_shard(m_w_ffn_gate), tr_shard(v_w_ffn_gate),
                                    "adamw_ffn_gate", 176)
    res["w_ffn_up"] = _adamw_call(got_u, tr_shard(w_ffn_up), tr_shard(m_w_ffn_up), tr_shard(v_w_ffn_up),
                                  "adamw_ffn_up", 176)
    res["w_ret_out"] = _adamw_call(got_ro, w_ret_out[0], m_w_ret_out[0], v_w_ret_out[0], "adamw_ret_out", r_sq)
    res["w_attn_out"] = _adamw_call(got_ao, w_attn_out[0], m_w_attn_out[0], v_w_attn_out[0], "adamw_attn_out", r_sq)
    res["w_out"] = _adamw_call(got_o, w_out[0], m_w_out[0], v_w_out[0], "adamw_out", r_sq)
    res["w_ffn_down"] = _adamw_call(got_d, w_ffn_down[0], m_w_ffn_down[0], v_w_ffn_down[0], "adamw_ffn_down", 176)
    small_names = ["ln1_g", "b_in", "ret_norm_g", "attn_sinks", "ln2_g", "lnf_g"]
    small_res, loss_row = _adamw_small_call(
        got_small, [ln1_g, b_in, ret_norm_g, attn_sinks, ln2_g, lnf_row],
        [m_ln1_g, m_b_in, m_ret_norm_g, m_attn_sinks, m_ln2_g, m_lnf_g.reshape(1, D)],
        [v_ln1_g, v_b_in, v_ret_norm_g, v_attn_sinks, v_ln2_g, v_lnf_g.reshape(1, D)])
    for i, nm in enumerate(small_names):
        res[nm] = [small_res[kind][i] for kind in range(4)]

    order = ["ln1_g", "w_in", "b_in", "ret_norm_g", "w_ret_out", "attn_sinks", "w_attn_out", "w_out", "ln2_g",
             "w_ffn_gate", "w_ffn_up", "w_ffn_down", "lnf_g"]
    outs = [loss_row[0, 0], dx[None]]
    for kind in range(4):
        for nm in order:
            val = res[nm][kind]
            if nm in transposed:
                val = val.T
            outs.append(val[None] if nm.startswith("w_") else val.reshape(D) if nm == "lnf_g" else val)
    return tuple(outs)
```

```python
import math

import numpy as np
import jax
import jax.numpy as jnp
from jax import lax
from jax.experimental import pallas as pl
from jax.experimental.pallas import tpu as pltpu

F32 = jnp.float32
_MXU = jnp.bfloat16

N_DEV = 8
D = 1024
RET_HEADS, RET_DK, RET_DV = 4, 128, 256
BLK = 128
Q_HEADS, KV_HEADS, HEAD_DIM = 16, 2, 64
GROUP = Q_HEADS // KV_HEADS
D_FF = 2816
N_RET, N_GATE, N_ATTN = 2048, 3072, 1280
D_IN = N_RET + N_GATE + N_ATTN
ROPE_THETA = 10000.0
EPS = 1e-6
RET_SCALE = RET_DK ** -0.5
ATTN_SCALE = HEAD_DIM ** -0.5
LR, B1, B2, ADAM_EPS, WD, STEP = 0.001, 0.9, 0.999, 1e-08, 0.01, 10
VMEM_LIMIT_MB = 56
MESH = pl.DeviceIdType.MESH


def _dot(a, b):
    return jnp.dot(a.astype(_MXU), b.astype(_MXU), preferred_element_type=F32)


def _dot_nt(a, b):
    return lax.dot_general(a.astype(_MXU), b.astype(_MXU), (((1,), (1,)), ((), ())), preferred_element_type=F32)


def _dot_tn(a, b):
    return lax.dot_general(a.astype(_MXU), b.astype(_MXU), (((0,), (0,)), ((), ())), preferred_element_type=F32)


def _sigmoid(x):
    return 1.0 / (1.0 + jnp.exp(-x))


def _cparams(n_axes, big=False):
    kw = dict(dimension_semantics=("arbitrary",) * n_axes)
    if big:
        kw["vmem_limit_bytes"] = VMEM_LIMIT_MB * 2**20
    return pltpu.CompilerParams(**kw)


def _rows(tm, width, col=0):
    return pl.BlockSpec((tm, width), lambda i: (i, col))


def _const(shape):
    nd = len(shape)
    return pl.BlockSpec(shape, lambda *_: (0,) * nd, pipeline_mode=pl.Buffered(1))


def _acc(width):
    return pl.BlockSpec((1, width), lambda *_: (0, 0))


def _sds(shape, dtype):
    return jax.ShapeDtypeStruct(shape, dtype)


def _swap_halves(x, half):
    w = x.shape[-1]
    if 2 * half == w:
        return pltpu.roll(x, half, 1)
    lane = lax.broadcasted_iota(jnp.int32, x.shape, 1)
    return jnp.where(lane % (2 * half) < half, pltpu.roll(x, w - half, 1), pltpu.roll(x, half, 1))


def _rope_tables(seq, half):
    lane = jnp.arange(128, dtype=jnp.int32)
    inv_freq = ROPE_THETA ** (-(lane % half).astype(F32) / half)
    sign = jnp.where(lane % (2 * half) < half, -1.0, 1.0).astype(F32)
    ang = jnp.arange(seq, dtype=jnp.int32).astype(F32)[:, None] * inv_freq[None, :]
    return jnp.cos(ang), jnp.sin(ang) * sign[None, :]


def _retention_decays():
    log_gamma = np.log1p(-np.exp2(-5.0 - np.arange(RET_HEADS, dtype=np.float32))).astype(np.float32)
    idx = np.arange(BLK, dtype=np.float32)
    rel = idx[:, None] - idx[None, :]
    intra = np.where(rel[None] >= 0, np.exp(log_gamma[:, None, None] * np.maximum(rel, 0.0)[None]), 0.0)
    q_decay = np.exp(log_gamma[:, None] * (idx + 1.0))[:, :, None]
    k_decay = np.exp(log_gamma[:, None] * (BLK - 1.0 - idx))[:, :, None]
    chunk_decay = [float(np.exp(np.float32(lg * BLK))) for lg in log_gamma]
    return (jnp.asarray(intra, F32), jnp.asarray(q_decay, F32), jnp.asarray(k_decay, F32), chunk_decay)


def _position():
    return lax.axis_index("x"), lax.axis_index("y"), lax.axis_index("c")


def _slot(px, py, pc):
    return 4 * px + 2 * py + pc


class _AllGather:
    def __init__(self, blocks):
        self.blocks = list(blocks)
        nb = len(self.blocks)
        self.out_shape = [_sds((N_DEV,) + b.shape, b.dtype) for b in self.blocks]
        self.scratch = [pltpu.SemaphoreType.DMA((nb, 7)), pltpu.SemaphoreType.DMA((nb, 7)),
                        pltpu.SemaphoreType.DMA((nb,))]

    def phases(self, ins, outs, send_sems, recv_sems, local_sems):
        nb = len(ins)
        x, y, c = _position()
        me, sibling = (x, y, c), (x, y, 1 - c)
        chips = [(1 - x, y), (x, 1 - y), (1 - x, 1 - y)]

        def copy(b, k, block, to, src=None):
            dst = outs[b].at[_slot(*block)]
            return pltpu.make_async_remote_copy(
                src_ref=dst if src is None else src, dst_ref=dst, send_sem=send_sems.at[b, k],
                recv_sem=recv_sems.at[b, k], device_id=to, device_id_type=MESH)

        def own(b):
            return pltpu.make_async_copy(ins[b], outs[b].at[_slot(*me)], local_sems.at[b])

        def first(b):
            return [copy(b, 0, me, sibling, src=ins[b])] + [
                copy(b, 1 + j, me, (*chip, c), src=ins[b]) for j, chip in enumerate(chips)]

        def start():
            for b in range(nb):
                own(b).start()
                for cp in first(b):
                    cp.start()

        def forward():
            for b in range(nb):
                for j, chip in enumerate(chips):
                    copy(b, 1 + j, (*chip, c), me).wait_recv()
                    copy(b, 4 + j, (*chip, c), sibling).start()

        def finish():
            for b in range(nb):
                copy(b, 0, sibling, me).wait_recv()
                for j, chip in enumerate(chips):
                    copy(b, 4 + j, (*chip, 1 - c), me).wait_recv()
            for b in range(nb):
                for cp in first(b):
                    cp.wait_send()
                for j, chip in enumerate(chips):
                    copy(b, 4 + j, (*chip, c), sibling).wait_send()
                own(b).wait()

        return start, forward, finish


class _AllToAll:
    def __init__(self, blocks):
        self.blocks = list(blocks)
        nb = len(self.blocks)
        self.out_shape = [_sds(b.shape, b.dtype) for b in self.blocks]
        self.scratch = [pltpu.SemaphoreType.DMA((nb, 7)), pltpu.SemaphoreType.DMA((nb, 7)),
                        pltpu.SemaphoreType.DMA((nb,))]

    def phases(self, ins, outs, send_sems, recv_sems, local_sems):
        nb = len(ins)
        x, y, c = _position()
        flip = lambda v, bit: 1 - v if bit else v
        peers = [(flip(x, k >> 2 & 1), flip(y, k >> 1 & 1), flip(c, k & 1)) for k in range(1, N_DEV)]

        def copy(b, k, peer, landed=False):
            return pltpu.make_async_remote_copy(
                src_ref=ins[b].at[_slot(*peer)], dst_ref=outs[b].at[_slot(*peer) if landed else _slot(x, y, c)],
                send_sem=send_sems.at[b, k], recv_sem=recv_sems.at[b, k], device_id=peer, device_id_type=MESH)

        def own(b):
            return pltpu.make_async_copy(ins[b].at[_slot(x, y, c)], outs[b].at[_slot(x, y, c)], local_sems.at[b])

        def start():
            for b in range(nb):
                own(b).start()
                for k, peer in enumerate(peers):
                    copy(b, k, peer).start()

        def forward():
            pass

        def finish():
            for b in range(nb):
                for k, peer in enumerate(peers):
                    copy(b, k, peer, landed=True).wait_recv()
            for b in range(nb):
                for k, peer in enumerate(peers):
                    copy(b, k, peer).wait_send()
                own(b).wait()

        return start, forward, finish


def _call(body, *, name, grid, in_specs, out_specs, out_shape, args, scratch_shapes=(), big=False, exchange=None):
    params = _cparams(len(grid), big)
    if exchange is None:
        return pl.pallas_call(body, name=name, grid=grid, in_specs=in_specs, out_specs=out_specs, out_shape=out_shape,
                              scratch_shapes=list(scratch_shapes), compiler_params=params)(*args)
    n_in, n_out, n_scr, nb = len(in_specs), len(out_specs), len(scratch_shapes), len(exchange.blocks)
    steps = math.prod(grid)

    def carried(*refs):
        pos = 0
        parts = []
        for n in (n_in, nb, n_out, nb, n_scr, len(exchange.scratch)):
            parts.append(refs[pos:pos + n])
            pos += n
        ins, x_ins, outs, x_outs, scr, sems = parts
        step = pl.program_id(0)
        for axis in range(1, len(grid)):
            step = step * grid[axis] + pl.program_id(axis)
        start, forward, finish = exchange.phases(x_ins, x_outs, *sems)
        pl.when(step == 0)(start)
        body(*ins, *outs, *scr)

        @pl.when(step == steps - 1)
        def _():
            forward()
            finish()

    any_spec = pl.BlockSpec(memory_space=pl.ANY)
    res = pl.pallas_call(
        carried, name=name, grid=grid, in_specs=list(in_specs) + [any_spec] * nb,
        out_specs=list(out_specs) + [any_spec] * nb, out_shape=list(out_shape) + exchange.out_shape,
        scratch_shapes=list(scratch_shapes) + exchange.scratch, compiler_params=params)(*args, *exchange.blocks)
    return res[:n_out], res[n_out:]


def _exchange_call(exchange, name):
    nb = len(exchange.blocks)

    def body(*refs):
        start, forward, finish = exchange.phases(refs[:nb], refs[nb:2 * nb], *refs[2 * nb:])
        start()
        forward()
        finish()

    any_spec = pl.BlockSpec(memory_space=pl.ANY)
    return pl.pallas_call(body, name=name, in_specs=[any_spec] * nb, out_specs=[any_spec] * nb,
                          out_shape=exchange.out_shape, scratch_shapes=exchange.scratch)(*exchange.blocks)


def _ln_call(x, g, tm, exchange):
    s = x.shape[0]

    def body(x_ref, g_ref, h_ref):
        xv = x_ref[...]
        r = lax.rsqrt(jnp.mean(xv * xv, axis=-1, keepdims=True) + EPS)
        h_ref[...] = ((xv * r) * g_ref[...]).astype(h_ref.dtype)

    return _call(body, name="ln1", grid=(s // tm,), in_specs=[_rows(tm, D), _acc(D)], out_specs=[_rows(tm, D)],
                 out_shape=[_sds((s, D), _MXU)], args=(x, g), exchange=exchange)


PROJ_TILE = 256


def _proj_source_tile(j):
    gate_end, attn_end, end = 3072 // PROJ_TILE, 4352 // PROJ_TILE, 6400 // PROJ_TILE
    n_gates = end - attn_end
    return jnp.where(j < gate_end, j, jnp.where(j < gate_end + n_gates, j + (attn_end - gate_end), j - n_gates))


def _proj_call(a, wt, bias, exchange):
    s, k = a.shape
    n = wt.shape[0]
    rows = min(1024, s)

    def body(a_ref, w_ref, b_ref, o_ref):
        for r in range(0, s, rows):
            o_ref[r:r + rows, :] = _dot_nt(a_ref[r:r + rows, :], w_ref[...]) + b_ref[...]

    return _call(body, name="proj", grid=(n // PROJ_TILE,),
                 in_specs=[_const((s, k)), pl.BlockSpec((PROJ_TILE, k), lambda j: (_proj_source_tile(j), 0)),
                           pl.BlockSpec((1, PROJ_TILE), lambda j: (0, _proj_source_tile(j)))],
                 out_specs=[pl.BlockSpec((s, PROJ_TILE), lambda j: (0, j))], out_shape=[_sds((s, n), F32)],
                 args=(a, wt, bias), big=True, exchange=exchange)


def _mm_tn(a, b, name, tm, tn, tk, exchange=None):
    s, m = a.shape
    n = b.shape[1]
    last = s // tk - 1

    def body(a_ref, b_ref, o_ref, acc):
        k = pl.program_id(2)
        part = _dot_tn(a_ref[...], b_ref[...])

        @pl.when(k == 0)
        def _():
            acc[...] = part

        @pl.when(k > 0)
        def _():
            acc[...] += part

        @pl.when(k == last)
        def _():
            o_ref[...] = acc[...].astype(o_ref.dtype)

    res = _call(body, name=name, grid=(m // tm, n // tn, s // tk),
                in_specs=[pl.BlockSpec((tk, tm), lambda i, j, k: (k, i)), pl.BlockSpec((tk, tn), lambda i, j, k: (k, j))],
                out_specs=[pl.BlockSpec((tm, tn), lambda i, j, k: (i, j))], out_shape=[_sds((m, n), _MXU)],
                scratch_shapes=[pltpu.VMEM((tm, tn), F32)], args=(a, b), big=True, exchange=exchange)
    return res[0] if exchange is None else (res[0][0], res[1])


def _ret_fwd_call(proj, cos, sin, decays):
    s = proj.shape[0]
    nblk = s // BLK
    intra, q_decay, k_decay, chunk_decay = decays

    def body(rq_ref, rk_ref, rv_ref, cos_ref, sin_ref, intra_ref, qd_ref, kd_ref,
             ry_ref, qr_ref, kr_ref, st_ref, state):
        @pl.when(pl.program_id(0) == 0)
        def _():
            state[...] = jnp.zeros_like(state)

        cos_v, sin_v = cos_ref[...], sin_ref[...]
        for h in range(RET_HEADS):
            hk = slice(h * RET_DK, (h + 1) * RET_DK)
            hv = slice(h * RET_DV, (h + 1) * RET_DV)
            q, k = rq_ref[:, hk], rk_ref[:, hk]
            qr = (q * cos_v + _swap_halves(q, RET_DK // 2) * sin_v) * RET_SCALE
            kr = k * cos_v + _swap_halves(k, RET_DK // 2) * sin_v
            v = rv_ref[:, hv]
            s_h = state[h]
            st_ref[0, h] = s_h.astype(st_ref.dtype)
            scores = _dot_nt(qr, kr) * intra_ref[h]
            ry_ref[:, hv] = _dot(scores, v) + _dot(qr, s_h) * qd_ref[h]
            state[h] = s_h * chunk_decay[h] + _dot_tn(kr * kd_ref[h], v)
            qr_ref[:, hk] = qr.astype(qr_ref.dtype)
            kr_ref[:, hk] = kr.astype(kr_ref.dtype)

    blk = lambda w, c: pl.BlockSpec((BLK, w), lambda n: (n, c))
    return _call(body, name="ret_fwd", grid=(nblk,),
                 in_specs=[blk(512, 0), blk(512, 1), blk(1024, 1), blk(128, 0), blk(128, 0),
                           _const(intra.shape), _const(q_decay.shape), _const(k_decay.shape)],
                 out_specs=[blk(1024, 0), blk(512, 0), blk(512, 0),
                            pl.BlockSpec((1, RET_HEADS, RET_DK, RET_DV), lambda n: (n, 0, 0, 0))],
                 out_shape=[_sds((s, 1024), F32), _sds((s, 512), _MXU), _sds((s, 512), _MXU),
                            _sds((nblk, RET_HEADS, RET_DK, RET_DV), _MXU)],
                 scratch_shapes=[pltpu.VMEM((RET_HEADS, RET_DK, RET_DV), F32)],
                 args=(proj, proj, proj, cos, sin, intra, q_decay, k_decay))


def _ret_bwd_call(qr, kr, proj, states, dry, cos, sin, decays):
    s = qr.shape[0]
    nblk = s // BLK
    intra, q_decay, k_decay, chunk_decay = decays

    def body(qr_ref, kr_ref, rv_ref, st_ref, dry_ref, cos_ref, sin_ref, intra_ref, qd_ref, kd_ref,
             dp_ref, db_ref, dstate):
        @pl.when(pl.program_id(0) == 0)
        def _():
            dstate[...] = jnp.zeros_like(dstate)
            db_ref[...] = jnp.zeros_like(db_ref)

        cos_v, sin_v = cos_ref[...], sin_ref[...]
        for h in range(RET_HEADS):
            hk = slice(h * RET_DK, (h + 1) * RET_DK)
            hv = slice(h * RET_DV, (h + 1) * RET_DV)
            q, k, v, d_out = qr_ref[:, hk], kr_ref[:, hk], rv_ref[:, hv], dry_ref[:, hv]
            d_next = dstate[h]
            scores = _dot_nt(q, k) * intra_ref[h]
            d_scores = _dot_nt(d_out, v) * intra_ref[h]
            d_cross = d_out * qd_ref[h]
            dq = _dot(d_scores, k) + _dot_nt(d_cross, st_ref[0, h])
            dk = _dot_tn(d_scores, q) + _dot_nt(v, d_next) * kd_ref[h]
            dv = _dot_tn(scores, d_out) + _dot(k.astype(F32) * kd_ref[h], d_next)
            dstate[h] = d_next * chunk_decay[h] + _dot_tn(q, d_cross)
            dq = (dq * cos_v - _swap_halves(dq, RET_DK // 2) * sin_v) * RET_SCALE
            dk = dk * cos_v - _swap_halves(dk, RET_DK // 2) * sin_v
            dp_ref[:, hk] = dq.astype(dp_ref.dtype)
            dp_ref[:, slice(512 + h * RET_DK, 512 + (h + 1) * RET_DK)] = dk.astype(dp_ref.dtype)
            dp_ref[:, slice(1024 + h * RET_DV, 1024 + (h + 1) * RET_DV)] = dv.astype(dp_ref.dtype)
            db_ref[:, hk] += jnp.sum(dq, axis=0, keepdims=True)
            db_ref[:, slice(512 + h * RET_DK, 512 + (h + 1) * RET_DK)] += jnp.sum(dk, axis=0, keepdims=True)
            db_ref[:, slice(1024 + h * RET_DV, 1024 + (h + 1) * RET_DV)] += jnp.sum(dv, axis=0, keepdims=True)

    rblk = lambda w, c: pl.BlockSpec((BLK, w), lambda n: (nblk - 1 - n, c))
    return _call(body, name="ret_bwd", grid=(nblk,),
                 in_specs=[rblk(512, 0), rblk(512, 0), rblk(1024, 1),
                           pl.BlockSpec((1, RET_HEADS, RET_DK, RET_DV), lambda n: (nblk - 1 - n, 0, 0, 0)),
                           rblk(1024, 0), rblk(128, 0), rblk(128, 0),
                           _const(intra.shape), _const(q_decay.shape), _const(k_decay.shape)],
                 out_specs=[rblk(N_RET, 0), _acc(N_RET)],
                 out_shape=[_sds((s, N_RET), _MXU), _sds((1, N_RET), F32)],
                 scratch_shapes=[pltpu.VMEM((RET_HEADS, RET_DK, RET_DV), F32)],
                 args=(qr, kr, proj, states, dry, cos, sin, intra, q_decay, k_decay))


def _both_halves(x, g):
    lane = lax.broadcasted_iota(jnp.int32, x.shape, 1)
    keep = lane < HEAD_DIM if g == 0 else lane >= HEAD_DIM
    return jnp.where(keep, x, pltpu.roll(x, HEAD_DIM, 1))


def _stack_heads(ref, g):
    lane = lax.broadcasted_iota(jnp.int32, (BLK, 128), 1)
    pieces = []
    for j in range(g * 4, g * 4 + 4):
        chunk = ref[:, j * 128:(j + 1) * 128]
        pieces += [jnp.where(lane < HEAD_DIM, chunk, jnp.zeros_like(chunk)),
                   jnp.where(lane >= HEAD_DIM, chunk, jnp.zeros_like(chunk))]
    return jnp.concatenate(pieces, axis=0)


def _window_bias(first_block):
    kj = lax.broadcasted_iota(jnp.int32, (2 * BLK, BLK), 0)
    qi = lax.broadcasted_iota(jnp.int32, (2 * BLK, BLK), 1)
    first_key = jnp.where(first_block, BLK, 0)
    seen = (kj > qi) & (kj <= qi + BLK) & (kj >= first_key)
    return jnp.where(seen, 0.0, -1e30)


def _sink_softmax(scores, sink):
    m = jnp.maximum(jnp.max(scores, axis=0, keepdims=True), sink)
    e = jnp.exp(scores - m)
    e_sink = jnp.exp(sink - m)
    return e, e_sink, 1.0 / (jnp.sum(e, axis=0, keepdims=True) + e_sink)


def _head_pair(stacked_t, jj):
    even = stacked_t[0:HEAD_DIM, 2 * jj * BLK:(2 * jj + 1) * BLK]
    odd = stacked_t[HEAD_DIM:128, (2 * jj + 1) * BLK:(2 * jj + 2) * BLK]
    return jnp.concatenate([even, odd], axis=0).T


def _attn_fwd_call(proj, sinks, cos, sin):
    s = proj.shape[0]
    nblk = s // BLK

    def body(sink_ref, q_ref, k_ref, v_ref, cos_ref, sin_ref, ay_ref, qr_ref, kr_ref, vb_ref,
             kwin, vwin, bias, s_scr, p_scr):
        n = pl.program_id(0)

        @pl.when(n == 0)
        def _():
            kwin[...] = jnp.zeros_like(kwin)
            vwin[...] = jnp.zeros_like(vwin)

        @pl.when(n > 0)
        def _():
            kwin[0:BLK] = kwin[BLK:2 * BLK]
            vwin[0:BLK] = vwin[BLK:2 * BLK]

        cos_v, sin_v = cos_ref[...], sin_ref[...]
        k = k_ref[...]
        kr = (k * cos_v + _swap_halves(k, HEAD_DIM // 2) * sin_v).astype(kwin.dtype)
        kwin[BLK:2 * BLK] = kr
        vwin[BLK:2 * BLK] = v_ref[...].astype(vwin.dtype)
        kr_ref[...] = kr
        vb_ref[...] = vwin[BLK:2 * BLK]
        for j in range(Q_HEADS // 2):
            cols = slice(j * 128, (j + 1) * 128)
            q = q_ref[:, cols]
            qr_ref[:, cols] = ((q * cos_v + _swap_halves(q, HEAD_DIM // 2) * sin_v) * ATTN_SCALE).astype(qr_ref.dtype)
        bias[...] = _window_bias(n == 0)
        for g in range(KV_HEADS):
            kg = _both_halves(kwin[...], g)
            vg_t = _both_halves(vwin[...], g).astype(F32).T
            s_scr[...] = _dot_nt(kg, _stack_heads(qr_ref, g))
            for i in range(GROUP):
                cols = slice(i * BLK, (i + 1) * BLK)
                e, _, inv = _sink_softmax(s_scr[:, cols] + bias[...], sink_ref[0, g * GROUP + i])
                p_scr[:, cols] = (e * inv).astype(p_scr.dtype)
            out_t = _dot(vg_t, p_scr[...])
            for jj in range(4):
                j = g * 4 + jj
                ay_ref[:, j * 128:(j + 1) * 128] = _head_pair(out_t, jj).astype(ay_ref.dtype)

    blk = lambda w, c: pl.BlockSpec((BLK, w), lambda n: (n, c))
    off = (N_RET + N_GATE) // 128
    wide = (2 * BLK, GROUP * BLK)
    return _call(body, name="attn_fwd", grid=(nblk,),
                 in_specs=[pl.BlockSpec(memory_space=pltpu.SMEM), blk(1024, off // 8), blk(128, off + 8), blk(128, off + 9),
                           blk(128, 0), blk(128, 0)],
                 out_specs=[blk(1024, 0), blk(1024, 0), blk(128, 0), blk(128, 0)],
                 out_shape=[_sds((s, 1024), _MXU), _sds((s, 1024), _MXU), _sds((s, 128), _MXU), _sds((s, 128), _MXU)],
                 scratch_shapes=[pltpu.VMEM((2 * BLK, 128), _MXU), pltpu.VMEM((2 * BLK, 128), _MXU),
                                 pltpu.VMEM((2 * BLK, BLK), F32), pltpu.VMEM(wide, F32), pltpu.VMEM(wide, _MXU)],
                 args=(sinks, proj, proj, proj, cos, sin))


def _attn_bwd_call(qr, kr, vb, day, sinks, cos, sin):
    s = qr.shape[0]
    nblk = s // BLK

    def body(sink_ref, q_ref, kc_ref, kp_ref, vc_ref, vp_ref, do_ref, cos_ref, sin_ref,
             dq_ref, cur_ref, prev_ref, dsink_ref, db_ref, bias, s_scr, dp_scr, p_scr, ds_scr):
        n = pl.program_id(0)

        @pl.when(n == 0)
        def _():
            dsink_ref[...] = jnp.zeros_like(dsink_ref)
            db_ref[...] = jnp.zeros_like(db_ref)

        cos_v, sin_v = cos_ref[...], sin_ref[...]
        bias[...] = _window_bias(n == 0)
        lane1 = lax.broadcasted_iota(jnp.int32, (1, 128), 1)
        kwin = jnp.concatenate([kp_ref[...], kc_ref[...]], axis=0)
        vwin = jnp.concatenate([vp_ref[...], vc_ref[...]], axis=0)
        dk_heads, dv_heads = [], []
        dsink = jnp.zeros((1, 128), F32)
        for g in range(KV_HEADS):
            kg = _both_halves(kwin, g)
            vg = _both_halves(vwin, g)
            q_all = _stack_heads(q_ref, g)
            do_all = _stack_heads(do_ref, g)
            s_scr[...] = _dot_nt(kg, q_all)
            dp_scr[...] = _dot_nt(vg, do_all)
            for i in range(GROUP):
                head = g * GROUP + i
                cols = slice(i * BLK, (i + 1) * BLK)
                e, e_sink, inv = _sink_softmax(s_scr[:, cols] + bias[...], sink_ref[0, head])
                p = e * inv
                dp = dp_scr[:, cols]
                delta = jnp.sum(p * dp, axis=0, keepdims=True)
                p_scr[:, cols] = p.astype(p_scr.dtype)
                ds_scr[:, cols] = (p * (dp - delta)).astype(ds_scr.dtype)
                dsink = dsink + jnp.where(lane1 == head, -jnp.sum(e_sink * inv * delta, axis=1, keepdims=True), 0.0)
            dv_both = _dot(p_scr[...], do_all)
            dk_both = _dot(ds_scr[...], q_all)
            dv_heads.append(dv_both + pltpu.roll(dv_both, HEAD_DIM, 1))
            dk_heads.append(dk_both + pltpu.roll(dk_both, HEAD_DIM, 1))
            dq_t = _dot(kg.astype(F32).T, ds_scr[...])
            for jj in range(4):
                cols = slice((g * 4 + jj) * 128, (g * 4 + jj + 1) * 128)
                dq = _head_pair(dq_t, jj)
                dq = (dq * cos_v - _swap_halves(dq, HEAD_DIM // 2) * sin_v) * ATTN_SCALE
                dq_ref[:, cols] = dq.astype(dq_ref.dtype)
                db_ref[:, cols] += jnp.sum(dq, axis=0, keepdims=True)
        dsink_ref[...] += dsink
        lane2 = lax.broadcasted_iota(jnp.int32, (2 * BLK, 128), 1)
        dk_all = jnp.where(lane2 < HEAD_DIM, dk_heads[0], dk_heads[1])
        dv_all = jnp.where(lane2 < HEAD_DIM, dv_heads[0], dv_heads[1])
        prev_ref[:, 0:128] = dk_all[0:BLK]
        prev_ref[:, 128:256] = dv_all[0:BLK]
        cur_ref[:, 0:128] = dk_all[BLK:2 * BLK]
        cur_ref[:, 128:256] = dv_all[BLK:2 * BLK]

    blk = lambda w, c: pl.BlockSpec((BLK, w), lambda n: (n, c))
    pblk = lambda w: pl.BlockSpec((BLK, w), lambda n: (jnp.maximum(n - 1, 0), 0))
    wide = (2 * BLK, GROUP * BLK)
    return _call(body, name="attn_bwd", grid=(nblk,),
                 in_specs=[pl.BlockSpec(memory_space=pltpu.SMEM), blk(1024, 0), blk(128, 0), pblk(128), blk(128, 0),
                           pblk(128), blk(1024, 0), blk(128, 0), blk(128, 0)],
                 out_specs=[blk(1024, 0), blk(256, 0), blk(256, 0), _acc(128), _acc(1024)],
                 out_shape=[_sds((s, 1024), _MXU), _sds((s, 256), F32), _sds((s, 256), F32), _sds((1, 128), F32),
                            _sds((1, 1024), F32)],
                 scratch_shapes=[pltpu.VMEM((2 * BLK, BLK), F32), pltpu.VMEM(wide, F32), pltpu.VMEM(wide, F32),
                                 pltpu.VMEM(wide, _MXU), pltpu.VMEM(wide, _MXU)],
                 args=(sinks, qr, kr, kr, vb, vb, day, cos, sin))


def _attn_combine_call(dq, dkv_cur, dkv_prev, cos, sin):
    s = dq.shape[0]
    nblk = s // BLK

    def body(dq_ref, cur_ref, nxt_ref, cos_ref, sin_ref, dp_ref, db_ref):
        n = pl.program_id(0)

        @pl.when(n == 0)
        def _():
            db_ref[...] = jnp.zeros_like(db_ref)

        dkv = cur_ref[...] + nxt_ref[...] * (n < nblk - 1).astype(F32)
        dk = dkv[:, 0:128]
        dk = dk * cos_ref[...] - _swap_halves(dk, HEAD_DIM // 2) * sin_ref[...]
        dv = dkv[:, 128:256]
        dp_ref[:, 0:1024] = dq_ref[...]
        dp_ref[:, 1024:1152] = dk.astype(dp_ref.dtype)
        dp_ref[:, 1152:1280] = dv.astype(dp_ref.dtype)
        db_ref[:, 0:128] += jnp.sum(dk, axis=0, keepdims=True)
        db_ref[:, 128:256] += jnp.sum(dv, axis=0, keepdims=True)

    blk = lambda w: pl.BlockSpec((BLK, w), lambda n: (n, 0))
    return _call(body, name="attn_combine", grid=(nblk,),
                 in_specs=[blk(1024), blk(256), pl.BlockSpec((BLK, 256), lambda n: (jnp.minimum(n + 1, nblk - 1), 0)),
                           blk(128), blk(128)],
                 out_specs=[blk(N_ATTN), _acc(256)],
                 out_shape=[_sds((s, N_ATTN), _MXU), _sds((1, 256), F32)],
                 args=(dq, dkv_cur, dkv_prev, cos, sin))


def _group_norm(y):
    mu = jnp.mean(y, axis=-1, keepdims=True)
    yc = y - mu
    rs = lax.rsqrt(jnp.mean(yc * yc, axis=-1, keepdims=True) + EPS)
    return yc * rs, rs


GATE_COL = N_RET // 1024


def _merge_fwd_call(x, ry, proj, ay, gn_g, w_ro, w_ao, w_o, tm):
    s = x.shape[0]

    def body(x_ref, ry_ref, rg_ref, ga_ref, gb_ref, ay_ref, gn_ref, wro_ref, wao_ref, wo_ref,
             ain_ref, a_ref, b_ref, mg_ref, x1_ref):
        for h in range(RET_HEADS):
            hv = slice(h * RET_DV, (h + 1) * RET_DV)
            yhat, _ = _group_norm(ry_ref[:, hv])
            rg = rg_ref[:, hv]
            ain_ref[:, hv] = ((rg * _sigmoid(rg)) * (yhat * gn_ref[:, hv])).astype(ain_ref.dtype)
        a = _dot(ain_ref[...], wro_ref[...])
        b = _dot(ay_ref[...], wao_ref[...])
        a_ref[...] = a
        b_ref[...] = b
        merged = (_sigmoid(ga_ref[...]) * a + _sigmoid(gb_ref[...]) * b).astype(mg_ref.dtype)
        mg_ref[...] = merged
        x1_ref[...] = x_ref[...] + _dot(merged, wo_ref[...])

    return _call(body, name="merge_fwd", grid=(s // tm,),
                 in_specs=[_rows(tm, D), _rows(tm, 1024), _rows(tm, 1024, GATE_COL), _rows(tm, 1024, GATE_COL + 1),
                           _rows(tm, 1024, GATE_COL + 2), _rows(tm, 1024), _acc(1024),
                           _const((D, D)), _const((D, D)), _const((D, D))],
                 out_specs=[_rows(tm, D)] * 5,
                 out_shape=[_sds((s, D), _MXU), _sds((s, D), F32), _sds((s, D), F32), _sds((s, D), _MXU), _sds((s, D), F32)],
                 args=(x, ry, proj, proj, proj, ay, gn_g, w_ro, w_ao, w_o), big=True)


def _merge_bwd_call(dx1, a, b, ry, proj, gn_g, w_ro, w_ao, w_o, tm, exchange):
    s = dx1.shape[0]

    def body(dx1_ref, a_ref, b_ref, ry_ref, rg_ref, ga_ref, gb_ref, gn_ref, wro_ref, wao_ref, wo_ref,
             da_ref, dbr_ref, dp_ref, day_ref, dry_ref, dbias_ref, dgn_ref):
        @pl.when(pl.program_id(0) == 0)
        def _():
            dbias_ref[...] = jnp.zeros_like(dbias_ref)
            dgn_ref[...] = jnp.zeros_like(dgn_ref)

        d_merged = _dot_nt(dx1_ref[...], wo_ref[...])
        sa, sb = _sigmoid(ga_ref[...]), _sigmoid(gb_ref[...])
        d_a = d_merged * sa
        d_b = d_merged * sb
        da_ref[...] = d_a.astype(da_ref.dtype)
        dbr_ref[...] = d_b.astype(dbr_ref.dtype)
        d_ga = d_merged * a_ref[...] * (sa * (1.0 - sa))
        d_gb = d_merged * b_ref[...] * (sb * (1.0 - sb))
        dp_ref[:, 1024:2048] = d_ga.astype(dp_ref.dtype)
        dp_ref[:, 2048:3072] = d_gb.astype(dp_ref.dtype)
        dbias_ref[:, 1024:2048] += jnp.sum(d_ga, axis=0, keepdims=True)
        dbias_ref[:, 2048:3072] += jnp.sum(d_gb, axis=0, keepdims=True)
        day_ref[...] = _dot_nt(d_b, wao_ref[...]).astype(day_ref.dtype)
        d_ain = _dot_nt(d_a, wro_ref[...])
        for h in range(RET_HEADS):
            hv = slice(h * RET_DV, (h + 1) * RET_DV)
            yhat, rs = _group_norm(ry_ref[:, hv])
            rg = rg_ref[:, hv]
            sg = _sigmoid(rg)
            gn = gn_ref[:, hv]
            d_h = d_ain[:, hv]
            d_rg = d_h * (yhat * gn) * (sg * (1.0 + rg * (1.0 - sg)))
            d_ryn = d_h * (rg * sg)
            dgn_ref[:, hv] += jnp.sum(d_ryn * yhat, axis=0, keepdims=True)
            d_yhat = d_ryn * gn
            dry_ref[:, hv] = rs * (d_yhat - jnp.mean(d_yhat, axis=-1, keepdims=True)
                                   - yhat * jnp.mean(d_yhat * yhat, axis=-1, keepdims=True))
            dp_ref[:, hv] = d_rg.astype(dp_ref.dtype)
            dbias_ref[:, hv] += jnp.sum(d_rg, axis=0, keepdims=True)

    return _call(body, name="merge_bwd", grid=(s // tm,),
                 in_specs=[_rows(tm, D), _rows(tm, D), _rows(tm, D), _rows(tm, 1024), _rows(tm, 1024, GATE_COL),
                           _rows(tm, 1024, GATE_COL + 1), _rows(tm, 1024, GATE_COL + 2), _acc(1024),
                           _const((D, D)), _const((D, D)), _const((D, D))],
                 out_specs=[_rows(tm, D), _rows(tm, D), _rows(tm, N_GATE), _rows(tm, D), _rows(tm, D), _acc(N_GATE),
                            _acc(1024)],
                 out_shape=[_sds((s, D), _MXU), _sds((s, D), _MXU), _sds((s, N_GATE), _MXU), _sds((s, D), _MXU),
                            _sds((s, D), F32), _sds((1, N_GATE), F32), _sds((1, 1024), F32)],
                 args=(dx1, a, b, ry, proj, proj, proj, gn_g, w_ro, w_ao, w_o), big=True, exchange=exchange)


def _ffn_fwd_call(x1, target, ln2_g, lnf_g, w_g, w_u, w_d, tm):
    s = x1.shape[0]

    def body(x1_ref, t_ref, g2_ref, gf_ref, wg_ref, wu_ref, wd_ref,
             h2_ref, g_ref, u_ref, f_ref, dx2_ref, loss_ref, dgf_ref):
        @pl.when(pl.program_id(0) == 0)
        def _():
            loss_ref[...] = jnp.zeros_like(loss_ref)
            dgf_ref[...] = jnp.zeros_like(dgf_ref)

        x1v = x1_ref[...]
        r1 = lax.rsqrt(jnp.mean(x1v * x1v, axis=-1, keepdims=True) + EPS)
        h2 = ((x1v * r1) * g2_ref[...]).astype(h2_ref.dtype)
        h2_ref[...] = h2
        g = _dot_nt(h2, wg_ref[...])
        u = _dot_nt(h2, wu_ref[...])
        g_ref[...] = g
        u_ref[...] = u
        f = ((g * _sigmoid(g)) * u).astype(f_ref.dtype)
        f_ref[...] = f
        x2 = x1v + _dot(f, wd_ref[...])
        r2 = lax.rsqrt(jnp.mean(x2 * x2, axis=-1, keepdims=True) + EPS)
        xhat = x2 * r2
        err = xhat * gf_ref[...] - t_ref[...]
        loss_ref[...] += 0.5 * jnp.sum(jnp.mean(err * err, axis=-1, keepdims=True))
        dy = err * (1.0 / D)
        dgf_ref[...] += jnp.sum(dy * xhat, axis=0, keepdims=True)
        dxh = dy * gf_ref[...]
        dx2_ref[...] = r2 * (dxh - xhat * jnp.mean(dxh * xhat, axis=-1, keepdims=True))

    return _call(body, name="ffn_fwd", grid=(s // tm,),
                 in_specs=[_rows(tm, D), _rows(tm, D), _acc(D), _acc(D), _const((D_FF, D)), _const((D_FF, D)),
                           _const((D_FF, D))],
                 out_specs=[_rows(tm, D), _rows(tm, D_FF), _rows(tm, D_FF), _rows(tm, D_FF), _rows(tm, D), _acc(128),
                            _acc(D)],
                 out_shape=[_sds((s, D), _MXU), _sds((s, D_FF), F32), _sds((s, D_FF), F32), _sds((s, D_FF), _MXU),
                            _sds((s, D), F32), _sds((1, 128), F32), _sds((1, D), F32)],
                 args=(x1, target, ln2_g, lnf_g, w_g, w_u, w_d), big=True)


def _ffn_bwd_call(dx2, x1, g, u, ln2_g, w_g, w_u, w_d, tm):
    s = dx2.shape[0]

    def body(dx2_ref, x1_ref, g_ref, u_ref, g2_ref, wg_ref, wu_ref, wd_ref, dx1_ref, dg_ref, du_ref, dg2_ref):
        @pl.when(pl.program_id(0) == 0)
        def _():
            dg2_ref[...] = jnp.zeros_like(dg2_ref)

        dx2v = dx2_ref[...]
        df = _dot_nt(dx2v, wd_ref[...])
        gv, uv = g_ref[...], u_ref[...]
        sg = _sigmoid(gv)
        du = (df * (gv * sg)).astype(du_ref.dtype)
        dg = (df * uv * (sg * (1.0 + gv * (1.0 - sg)))).astype(dg_ref.dtype)
        du_ref[...] = du
        dg_ref[...] = dg
        dh2 = _dot(dg, wg_ref[...]) + _dot(du, wu_ref[...])
        x1v = x1_ref[...]
        r1 = lax.rsqrt(jnp.mean(x1v * x1v, axis=-1, keepdims=True) + EPS)
        xhat = x1v * r1
        dg2_ref[...] += jnp.sum(dh2 * xhat, axis=0, keepdims=True)
        dxh = dh2 * g2_ref[...]
        dx1_ref[...] = dx2v + r1 * (dxh - xhat * jnp.mean(dxh * xhat, axis=-1, keepdims=True))

    return _call(body, name="ffn_bwd", grid=(s // tm,),
                 in_specs=[_rows(tm, D), _rows(tm, D), _rows(tm, D_FF), _rows(tm, D_FF), _acc(D),
                           _const((D_FF, D)), _const((D_FF, D)), _const((D_FF, D))],
                 out_specs=[_rows(tm, D), _rows(tm, D_FF), _rows(tm, D_FF), _acc(D)],
                 out_shape=[_sds((s, D), F32), _sds((s, D_FF), _MXU), _sds((s, D_FF), _MXU), _sds((1, D), F32)],
                 args=(dx2, x1, g, u, ln2_g, w_g, w_u, w_d), big=True)


def _dx_call(x, dx1, dp_ret, dp_gate, dp_attn, ln1_g, w_in, tm, exchange):
    s = x.shape[0]

    def body(x_ref, dx1_ref, dr_ref, dg_ref, da_ref, g1_ref, w_ref, dx_ref, dg1_ref):
        @pl.when(pl.program_id(0) == 0)
        def _():
            dg1_ref[...] = jnp.zeros_like(dg1_ref)

        dh = (_dot(dr_ref[...], w_ref[0:2048, :]) + _dot(dg_ref[:, 0:1024], w_ref[2048:3072, :])
              + _dot(da_ref[...], w_ref[3072:4352, :]) + _dot(dg_ref[:, 1024:3072], w_ref[4352:6400, :]))
        xv = x_ref[...]
        r = lax.rsqrt(jnp.mean(xv * xv, axis=-1, keepdims=True) + EPS)
        xhat = xv * r
        dg1_ref[...] += jnp.sum(dh * xhat, axis=0, keepdims=True)
        dxh = dh * g1_ref[...]
        dx_ref[...] = dx1_ref[...] + r * (dxh - xhat * jnp.mean(dxh * xhat, axis=-1, keepdims=True))

    return _call(body, name="dx", grid=(s // tm,),
                 in_specs=[_rows(tm, D), _rows(tm, D), _rows(tm, N_RET), _rows(tm, N_GATE), _rows(tm, N_ATTN), _acc(D),
                           _const((D_IN, D))],
                 out_specs=[_rows(tm, D), _acc(D)],
                 out_shape=[_sds((s, D), F32), _sds((1, D), F32)],
                 args=(x, dx1, dp_ret, dp_gate, dp_attn, ln1_g, w_in), big=True, exchange=exchange)


def _adamw(g, w, m, v):
    m_new = B1 * m + (1.0 - B1) * g
    v_new = B2 * v + (1.0 - B2) * (g * g)
    m_hat = m_new / (1.0 - B1 ** STEP)
    v_hat = v_new / (1.0 - B2 ** STEP)
    return -LR * (m_hat / (jnp.sqrt(v_hat) + ADAM_EPS) + WD * w), m_new, v_new


def _slot_sum(p_ref):
    g = p_ref[0].astype(F32)
    for k in range(1, N_DEV):
        g = g + p_ref[k].astype(F32)
    return g


def _adamw_call(parts, w, m, v, name, tr):
    rows, cols = w.shape

    def body(p_ref, w_ref, m_ref, v_ref, g_ref, dw_ref, nm_ref, nv_ref):
        g = _slot_sum(p_ref)
        g_ref[...] = g
        dw_ref[...], nm_ref[...], nv_ref[...] = _adamw(g, w_ref[...], m_ref[...], v_ref[...])

    p_spec = pl.BlockSpec((N_DEV, tr, cols), lambda i: (0, i, 0))
    spec = pl.BlockSpec((tr, cols), lambda i: (i, 0))
    return _call(body, name=name, grid=(rows // tr,), in_specs=[p_spec, spec, spec, spec], out_specs=[spec] * 4,
                 out_shape=[_sds((rows, cols), F32)] * 4, args=(parts, w, m, v))


SMALL_WIDTHS = [1024, 6400, 1024, 16, 1024, 1024]
SMALL_OFFSETS = [0, 1024, 7424, 8448, 8576, 9600]
LOSS_OFFSET = 10624
SMALL_LEN = 10752


def _pack_small(grads, loss):
    pieces = []
    for gr, width in zip(grads, SMALL_WIDTHS):
        pieces.append(jnp.pad(gr.reshape(1, width), ((0, 0), (0, -width % 128))))
    pieces.append(jnp.pad(loss.reshape(1, 1), ((0, 0), (0, 127))))
    return jnp.concatenate(pieces, axis=1)


def _adamw_small_call(parts, ws, ms, vs):
    n = len(ws)

    def body(*refs):
        p_ref, w_refs, m_refs, v_refs = refs[0], refs[1:1 + n], refs[1 + n:1 + 2 * n], refs[1 + 2 * n:1 + 3 * n]
        outs = refs[1 + 3 * n:]
        g_all = _slot_sum(p_ref)
        for i, (off, width) in enumerate(zip(SMALL_OFFSETS, SMALL_WIDTHS)):
            g = g_all[:, off:off + width]
            outs[i][...] = g
            outs[n + i][...], outs[2 * n + i][...], outs[3 * n + i][...] = _adamw(
                g, w_refs[i][...], m_refs[i][...], v_refs[i][...])
        outs[4 * n][...] = g_all[:, LOSS_OFFSET:LOSS_OFFSET + 128]

    whole = lambda shape: pl.BlockSpec(shape, lambda i: (0,) * len(shape))
    small = [whole((1, w)) for w in SMALL_WIDTHS]
    res = _call(body, name="adamw_small", grid=(1,), in_specs=[whole((N_DEV, 1, SMALL_LEN))] + small * 3,
                out_specs=small * 4 + [whole((1, 128))],
                out_shape=[_sds((1, w), F32) for w in SMALL_WIDTHS] * 4 + [_sds((1, 128), F32)],
                args=(parts, *ws, *ms, *vs))
    return [res[k * n:(k + 1) * n] for k in range(4)], res[4 * n]


def kernel(x, ln1_g, w_in, b_in, ret_norm_g, w_ret_out, attn_sinks, w_attn_out, w_out, ln2_g, w_ffn_gate, w_ffn_up, w_ffn_down, lnf_g, loss_target, m_ln1_g, m_w_in, m_b_in, m_ret_norm_g, m_w_ret_out, m_attn_sinks, m_w_attn_out, m_w_out, m_ln2_g, m_w_ffn_gate, m_w_ffn_up, m_w_ffn_down, m_lnf_g, v_ln1_g, v_w_in, v_b_in, v_ret_norm_g, v_w_ret_out, v_attn_sinks, v_w_attn_out, v_w_out, v_ln2_g, v_w_ffn_gate, v_w_ffn_up, v_w_ffn_down, v_lnf_g):
    cast = lambda a: a.astype(_MXU)
    xs, target = x[0], loss_target[0]
    s = xs.shape[0]
    r_sq = w_ret_out.shape[1]
    r_dn = w_ffn_down.shape[1]
    c_in = w_in.shape[2]
    c_ff = w_ffn_gate.shape[2]
    tm, tk = min(256, s), min(2048, s)
    lnf_row = lnf_g.reshape(1, D)
    cos_r, sin_r = _rope_tables(s, RET_DK // 2)
    cos_a, sin_a = _rope_tables(s, HEAD_DIM // 2)
    decays = _retention_decays()
    tr_shard = lambda a: a[0].T
    per_dev = lambda a, n: a.reshape(N_DEV, n, D)

    (h,), (all_in,) = _ln_call(xs, ln1_g, tm, _AllGather([cast(tr_shard(w_in))]))
    wt_in = all_in.reshape(N_DEV * c_in, D)
    rest = [tr_shard(w_ffn_gate), tr_shard(w_ffn_up), w_ret_out[0], w_attn_out[0], w_out[0], w_ffn_down[0]]
    (proj,), gathered = _proj_call(h, wt_in, b_in, _AllGather([cast(a) for a in rest]))
    wt_g, wt_u, full_ro, full_ao, full_o, full_d = (a.reshape(N_DEV * a.shape[1], D) for a in gathered)
    ry, qr, kr, states = _ret_fwd_call(proj, cos_r, sin_r, decays)
    ay, aqr, akr, avb = _attn_fwd_call(proj, attn_sinks, cos_a, sin_a)
    a_in, br_a, br_b, merged, x1 = _merge_fwd_call(xs, ry, proj, ay, ret_norm_g, full_ro, full_ao, full_o, tm)
    h2, g, u, f, dx2, loss, d_lnf = _ffn_fwd_call(x1, target, ln2_g, lnf_row, wt_g, wt_u, full_d, tm)

    dx1, dg, du, d_ln2 = _ffn_bwd_call(dx2, x1, g, u, ln2_g, wt_g, wt_u, full_d, tm)
    dw_d = _mm_tn(f, dx2, "dw_ffn_down", 1408, 1024, tk)
    dwt_g, (got_d,) = _mm_tn(dg, h2, "dw_ffn_gate", 1408, 1024, tk, _AllToAll([per_dev(dw_d, r_dn)]))
    dwt_u = _mm_tn(du, h2, "dw_ffn_up", 1408, 1024, tk)
    (d_a, d_b, dp_gate, day, dry, db_gate, d_gn), (got_g, got_u) = _merge_bwd_call(
        dx1, br_a, br_b, ry, proj, ret_norm_g, full_ro, full_ao, full_o, tm,
        _AllToAll([per_dev(dwt_g, c_ff), per_dev(dwt_u, c_ff)]))
    dw_o = _mm_tn(merged, dx1, "dw_out", 1024, 1024, tk)
    dw_ro = _mm_tn(a_in, d_a, "dw_ret_out", 1024, 1024, tk)
    dw_ao = _mm_tn(ay, d_b, "dw_attn_out", 1024, 1024, tk)
    dq, dkv_cur, dkv_prev, d_sinks, db_aq = _attn_bwd_call(aqr, akr, avb, day, attn_sinks, cos_a, sin_a)
    dp_attn, db_akv = _attn_combine_call(dq, dkv_cur, dkv_prev, cos_a, sin_a)
    dp_ret, db_ret = _ret_bwd_call(qr, kr, proj, states, dry, cos_r, sin_r, decays)
    dwt_ret = _mm_tn(dp_ret, h, "dw_in_ret", 1024, 1024, tk)
    dwt_gate, (got_ro, got_ao, got_o) = _mm_tn(
        dp_gate, h, "dw_in_gate", 1024, 1024, tk,
        _AllToAll([per_dev(dw_ro, r_sq), per_dev(dw_ao, r_sq), per_dev(dw_o, r_sq)]))
    dwt_attn = _mm_tn(dp_attn, h, "dw_in_attn", 1280, 1024, tk)
    dwt_in = jnp.concatenate([dwt_ret, dwt_gate[0:1024], dwt_attn, dwt_gate[1024:3072]], axis=0)
    (dx, d_ln1), (got_in,) = _dx_call(xs, dx1, dp_ret, dp_gate, dp_attn, ln1_g, wt_in, tm,
                                      _AllToAll([per_dev(dwt_in, c_in)]))
    db_in = jnp.concatenate([db_ret, db_gate[:, 0:1024], db_aq, db_akv, db_gate[:, 1024:3072]], axis=1)
    small = [d_ln1, db_in, d_gn, d_sinks[:, 0:Q_HEADS], d_ln2, d_lnf]
    (got_small,) = _exchange_call(_AllGather([_pack_small(small, loss[0, 0])]), "gather_small")

    transposed = ("w_in", "w_ffn_gate", "w_ffn_up")
    res = {}
    res["w_in"] = _adamw_call(got_in, tr_shard(w_in), tr_shard(m_w_in), tr_shard(v_w_in), "adamw_w_in", 160)
    res["w_ffn_gate"] = _adamw_call(got_g, tr_shard(w_ffn_gate), tr_shard(m_w_ffn_gate), tr_shard(v_w_ffn_gate),
                                    "adamw_ffn_gate", 176)
    res["w_ffn_up"] = _adamw_call(got_u, tr_shard(w_ffn_up), tr_shard(m_w_ffn_up), tr_shard(v_w_ffn_up),
                                  "adamw_ffn_up", 176)
    res["w_ret_out"] = _adamw_call(got_ro, w_ret_out[0], m_w_ret_out[0], v_w_ret_out[0], "adamw_ret_out", r_sq)
    res["w_attn_out"] = _adamw_call(got_ao, w_attn_out[0], m_w_attn_out[0], v_w_attn_out[0], "adamw_attn_out", r_sq)
    res["w_out"] = _adamw_call(got_o, w_out[0], m_w_out[0], v_w_out[0], "adamw_out", r_sq)
    res["w_ffn_down"] = _adamw_call(got_d, w_ffn_down[0], m_w_ffn_down[0], v_w_ffn_down[0], "adamw_ffn_down", 176)
    small_names = ["ln1_g", "b_in", "ret_norm_g", "attn_sinks", "ln2_g", "lnf_g"]
    small_res, loss_row = _adamw_small_call(
        got_small, [ln1_g, b_in, ret_norm_g, attn_sinks, ln2_g, lnf_row],
        [m_ln1_g, m_b_in, m_ret_norm_g, m_attn_sinks, m_ln2_g, m_lnf_g.reshape(1, D)],
        [v_ln1_g, v_b_in, v_ret_norm_g, v_attn_sinks, v_ln2_g, v_lnf_g.reshape(1, D)])
    for i, nm in enumerate(small_names):
        res[nm] = [small_res[kind][i] for kind in range(4)]

    order = ["ln1_g", "w_in", "b_in", "ret_norm_g", "w_ret_out", "attn_sinks", "w_attn_out", "w_out", "ln2_g",
             "w_ffn_gate", "w_ffn_up", "w_ffn_down", "lnf_g"]
    outs = [loss_row[0, 0], dx[None]]
    for kind in range(4):
        for nm in order:
            val = res[nm][kind]
            if nm in transposed:
                val = val.T
            outs.append(val[None] if nm.startswith("w_") else val.reshape(D) if nm == "lnf_g" else val)
    return tuple(outs)
```

```python
import math

import numpy as np
import jax
import jax.numpy as jnp
from jax import lax
from jax.experimental import pallas as pl
from jax.experimental.pallas import tpu as pltpu

F32 = jnp.float32
_MXU = jnp.bfloat16

N_DEV = 8
D = 1024
RET_HEADS, RET_DK, RET_DV = 4, 128, 256
BLK = 128
Q_HEADS, KV_HEADS, HEAD_DIM = 16, 2, 64
GROUP = Q_HEADS // KV_HEADS
D_FF = 2816
N_RET, N_GATE, N_ATTN = 2048, 3072, 1280
D_IN = N_RET + N_GATE + N_ATTN
ROPE_THETA = 10000.0
EPS = 1e-6
RET_SCALE = RET_DK ** -0.5
ATTN_SCALE = HEAD_DIM ** -0.5
LR, B1, B2, ADAM_EPS, WD, STEP = 0.001, 0.9, 0.999, 1e-08, 0.01, 10
VMEM_LIMIT_MB = 56
MESH = pl.DeviceIdType.MESH


def _dot(a, b):
    return jnp.dot(a.astype(_MXU), b.astype(_MXU), preferred_element_type=F32)


def _dot_nt(a, b):
    return lax.dot_general(a.astype(_MXU), b.astype(_MXU), (((1,), (1,)), ((), ())), preferred_element_type=F32)


def _dot_tn(a, b):
    return lax.dot_general(a.astype(_MXU), b.astype(_MXU), (((0,), (0,)), ((), ())), preferred_element_type=F32)


def _sigmoid(x):
    return 1.0 / (1.0 + jnp.exp(-x))


def _cparams(n_axes, big=False):
    kw = dict(dimension_semantics=("arbitrary",) * n_axes)
    if big:
        kw["vmem_limit_bytes"] = VMEM_LIMIT_MB * 2**20
    return pltpu.CompilerParams(**kw)


def _rows(tm, width, col=0):
    return pl.BlockSpec((tm, width), lambda i: (i, col))


def _const(shape):
    nd = len(shape)
    return pl.BlockSpec(shape, lambda *_: (0,) * nd, pipeline_mode=pl.Buffered(1))


def _acc(width):
    return pl.BlockSpec((1, width), lambda *_: (0, 0))


def _sds(shape, dtype):
    return jax.ShapeDtypeStruct(shape, dtype)


def _swap_halves(x, half):
    w = x.shape[-1]
    if 2 * half == w:
        return pltpu.roll(x, half, 1)
    lane = lax.broadcasted_iota(jnp.int32, x.shape, 1)
    return jnp.where(lane % (2 * half) < half, pltpu.roll(x, w - half, 1), pltpu.roll(x, half, 1))


def _rope_tables(seq, half):
    lane = jnp.arange(128, dtype=jnp.int32)
    inv_freq = ROPE_THETA ** (-(lane % half).astype(F32) / half)
    sign = jnp.where(lane % (2 * half) < half, -1.0, 1.0).astype(F32)
    ang = jnp.arange(seq, dtype=jnp.int32).astype(F32)[:, None] * inv_freq[None, :]
    return jnp.cos(ang), jnp.sin(ang) * sign[None, :]


def _retention_decays():
    log_gamma = np.log1p(-np.exp2(-5.0 - np.arange(RET_HEADS, dtype=np.float32))).astype(np.float32)
    idx = np.arange(BLK, dtype=np.float32)
    rel = idx[:, None] - idx[None, :]
    intra = np.where(rel[None] >= 0, np.exp(log_gamma[:, None, None] * np.maximum(rel, 0.0)[None]), 0.0)
    q_decay = np.exp(log_gamma[:, None] * (idx + 1.0))[:, :, None]
    k_decay = np.exp(log_gamma[:, None] * (BLK - 1.0 - idx))[:, :, None]
    chunk_decay = [float(np.exp(np.float32(lg * BLK))) for lg in log_gamma]
    return (jnp.asarray(intra, F32), jnp.asarray(q_decay, F32), jnp.asarray(k_decay, F32), chunk_decay)


def _position():
    return lax.axis_index("x"), lax.axis_index("y"), lax.axis_index("c")


def _slot(px, py, pc):
    return 4 * px + 2 * py + pc


class _AllGather:
    def __init__(self, blocks):
        self.blocks = list(blocks)
        nb = len(self.blocks)
        self.out_shape = [_sds((N_DEV,) + b.shape, b.dtype) for b in self.blocks]
        self.scratch = [pltpu.SemaphoreType.DMA((nb, 7)), pltpu.SemaphoreType.DMA((nb, 7)),
                        pltpu.SemaphoreType.DMA((nb,))]

    def phases(self, ins, outs, send_sems, recv_sems, local_sems):
        nb = len(ins)
        x, y, c = _position()
        me, sibling = (x, y, c), (x, y, 1 - c)
        chips = [(1 - x, y), (x, 1 - y), (1 - x, 1 - y)]

        def copy(b, k, block, to, src=None):
            dst = outs[b].at[_slot(*block)]
            return pltpu.make_async_remote_copy(
                src_ref=dst if src is None else src, dst_ref=dst, send_sem=send_sems.at[b, k],
                recv_sem=recv_sems.at[b, k], device_id=to, device_id_type=MESH)

        def own(b):
            return pltpu.make_async_copy(ins[b], outs[b].at[_slot(*me)], local_sems.at[b])

        def first(b):
            return [copy(b, 0, me, sibling, src=ins[b])] + [
                copy(b, 1 + j, me, (*chip, c), src=ins[b]) for j, chip in enumerate(chips)]

        def start():
            for b in range(nb):
                own(b).start()
                for cp in first(b):
                    cp.start()

        def forward():
            for b in range(nb):
                for j, chip in enumerate(chips):
                    copy(b, 1 + j, (*chip, c), me).wait_recv()
                    copy(b, 4 + j, (*chip, c), sibling).start()

        def finish():
            for b in range(nb):
                copy(b, 0, sibling, me).wait_recv()
                for j, chip in enumerate(chips):
                    copy(b, 4 + j, (*chip, 1 - c), me).wait_recv()
            for b in range(nb):
                for cp in first(b):
                    cp.wait_send()
                for j, chip in enumerate(chips):
                    copy(b, 4 + j, (*chip, c), sibling).wait_send()
                own(b).wait()

        return start, forward, finish


class _AllToAll:
    def __init__(self, blocks):
        self.blocks = list(blocks)
        nb = len(self.blocks)
        self.out_shape = [_sds(b.shape, b.dtype) for b in self.blocks]
        self.scratch = [pltpu.SemaphoreType.DMA((nb, 7)), pltpu.SemaphoreType.DMA((nb, 7)),
                        pltpu.SemaphoreType.DMA((nb,))]

    def phases(self, ins, outs, send_sems, recv_sems, local_sems):
        nb = len(ins)
        x, y, c = _position()
        flip = lambda v, bit: 1 - v if bit else v
        peers = [(flip(x, k >> 2 & 1), flip(y, k >> 1 & 1), flip(c, k & 1)) for k in range(1, N_DEV)]

        def copy(b, k, peer, landed=False):
            return pltpu.make_async_remote_copy(
                src_ref=ins[b].at[_slot(*peer)], dst_ref=outs[b].at[_slot(*peer) if landed else _slot(x, y, c)],
                send_sem=send_sems.at[b, k], recv_sem=recv_sems.at[b, k], device_id=peer, device_id_type=MESH)

        def own(b):
            return pltpu.make_async_copy(ins[b].at[_slot(x, y, c)], outs[b].at[_slot(x, y, c)], local_sems.at[b])

        def start():
            for b in range(nb):
                own(b).start()
                for k, peer in enumerate(peers):
                    copy(b, k, peer).start()

        def forward():
            pass

        def finish():
            for b in range(nb):
                for k, peer in enumerate(peers):
                    copy(b, k, peer, landed=True).wait_recv()
            for b in range(nb):
                for k, peer in enumerate(peers):
                    copy(b, k, peer).wait_send()
                own(b).wait()

        return start, forward, finish


def _call(body, *, name, grid, in_specs, out_specs, out_shape, args, scratch_shapes=(), big=False, exchange=None):
    params = _cparams(len(grid), big)
    if exchange is None:
        return pl.pallas_call(body, name=name, grid=grid, in_specs=in_specs, out_specs=out_specs, out_shape=out_shape,
                              scratch_shapes=list(scratch_shapes), compiler_params=params)(*args)
    n_in, n_out, n_scr, nb = len(in_specs), len(out_specs), len(scratch_shapes), len(exchange.blocks)
    steps = math.prod(grid)

    def carried(*refs):
        pos = 0
        parts = []
        for n in (n_in, nb, n_out, nb, n_scr, len(exchange.scratch)):
            parts.append(refs[pos:pos + n])
            pos += n
        ins, x_ins, outs, x_outs, scr, sems = parts
        step = pl.program_id(0)
        for axis in range(1, len(grid)):
            step = step * grid[axis] + pl.program_id(axis)
        start, forward, finish = exchange.phases(x_ins, x_outs, *sems)
        pl.when(step == 0)(start)
        body(*ins, *outs, *scr)

        @pl.when(step == steps - 1)
        def _():
            forward()
            finish()

    any_spec = pl.BlockSpec(memory_space=pl.ANY)
    res = pl.pallas_call(
        carried, name=name, grid=grid, in_specs=list(in_specs) + [any_spec] * nb,
        out_specs=list(out_specs) + [any_spec] * nb, out_shape=list(out_shape) + exchange.out_shape,
        scratch_shapes=list(scratch_shapes) + exchange.scratch, compiler_params=params)(*args, *exchange.blocks)
    return res[:n_out], res[n_out:]


def _exchange_call(exchange, name):
    nb = len(exchange.blocks)

    def body(*refs):
        start, forward, finish = exchange.phases(refs[:nb], refs[nb:2 * nb], *refs[2 * nb:])
        start()
        forward()
        finish()

    any_spec = pl.BlockSpec(memory_space=pl.ANY)
    return pl.pallas_call(body, name=name, in_specs=[any_spec] * nb, out_specs=[any_spec] * nb,
                          out_shape=exchange.out_shape, scratch_shapes=exchange.scratch)(*exchange.blocks)


def _ln_call(x, g, tm, exchange):
    s = x.shape[0]

    def body(x_ref, g_ref, h_ref):
        xv = x_ref[...]
        r = lax.rsqrt(jnp.mean(xv * xv, axis=-1, keepdims=True) + EPS)
        h_ref[...] = ((xv * r) * g_ref[...]).astype(h_ref.dtype)

    return _call(body, name="ln1", grid=(s // tm,), in_specs=[_rows(tm, D), _acc(D)], out_specs=[_rows(tm, D)],
                 out_shape=[_sds((s, D), _MXU)], args=(x, g), exchange=exchange)


PROJ_TILE = 256


def _proj_source_tile(j):
    gate_end, attn_end, end = 3072 // PROJ_TILE, 4352 // PROJ_TILE, 6400 // PROJ_TILE
    n_gates = end - attn_end
    return jnp.where(j < gate_end, j, jnp.where(j < gate_end + n_gates, j + (attn_end - gate_end), j - n_gates))


def _proj_call(a, wt, bias, exchange):
    s, k = a.shape
    n = wt.shape[0]
    rows = min(1024, s)

    def body(a_ref, w_ref, b_ref, o_ref):
        for r in range(0, s, rows):
            o_ref[r:r + rows, :] = _dot_nt(a_ref[r:r + rows, :], w_ref[...]) + b_ref[...]

    return _call(body, name="proj", grid=(n // PROJ_TILE,),
                 in_specs=[_const((s, k)), pl.BlockSpec((PROJ_TILE, k), lambda j: (_proj_source_tile(j), 0)),
                           pl.BlockSpec((1, PROJ_TILE), lambda j: (0, _proj_source_tile(j)))],
                 out_specs=[pl.BlockSpec((s, PROJ_TILE), lambda j: (0, j))], out_shape=[_sds((s, n), F32)],
                 args=(a, wt, bias), big=True, exchange=exchange)


def _mm_tn(a, b, name, tm, tn, tk, exchange=None):
    s, m = a.shape
    n = b.shape[1]
    last = s // tk - 1

    def body(a_ref, b_ref, o_ref, acc):
        k = pl.program_id(2)
        part = _dot_tn(a_ref[...], b_ref[...])

        @pl.when(k == 0)
        def _():
            acc[...] = part

        @pl.when(k > 0)
        def _():
            acc[...] += part

        @pl.when(k == last)
        def _():
            o_ref[...] = acc[...].astype(o_ref.dtype)

    res = _call(body, name=name, grid=(m // tm, n // tn, s // tk),
                in_specs=[pl.BlockSpec((tk, tm), lambda i, j, k: (k, i)), pl.BlockSpec((tk, tn), lambda i, j, k: (k, j))],
                out_specs=[pl.BlockSpec((tm, tn), lambda i, j, k: (i, j))], out_shape=[_sds((m, n), _MXU)],
                scratch_shapes=[pltpu.VMEM((tm, tn), F32)], args=(a, b), big=True, exchange=exchange)
    return res[0] if exchange is None else (res[0][0], res[1])


RET_CHUNKS = 2


def _ret_fwd_call(proj, cos, sin, decays):
    s = proj.shape[0]
    nblk = s // BLK
    per = min(RET_CHUNKS, nblk)
    rows = per * BLK
    intra, q_decay, k_decay, chunk_decay = decays

    def body(rq_ref, rk_ref, rv_ref, cos_ref, sin_ref, intra_ref, qd_ref, kd_ref,
             ry_ref, qr_ref, kr_ref, st_ref, state):
        @pl.when(pl.program_id(0) == 0)
        def _():
            state[...] = jnp.zeros_like(state)

        for c in range(per):
            rc = slice(c * BLK, (c + 1) * BLK)
            cos_v, sin_v = cos_ref[rc, :], sin_ref[rc, :]
            for h in range(RET_HEADS):
                hk = slice(h * RET_DK, (h + 1) * RET_DK)
                hv = slice(h * RET_DV, (h + 1) * RET_DV)
                q, k = rq_ref[rc, hk], rk_ref[rc, hk]
                qr = (q * cos_v + _swap_halves(q, RET_DK // 2) * sin_v) * RET_SCALE
                kr = k * cos_v + _swap_halves(k, RET_DK // 2) * sin_v
                v = rv_ref[rc, hv]
                s_h = state[h]
                st_ref[c, h] = s_h.astype(st_ref.dtype)
                scores = _dot_nt(qr, kr) * intra_ref[h]
                ry_ref[rc, hv] = _dot(scores, v) + _dot(qr, s_h) * qd_ref[h]
                state[h] = s_h * chunk_decay[h] + _dot_tn(kr * kd_ref[h], v)
                qr_ref[rc, hk] = qr.astype(qr_ref.dtype)
                kr_ref[rc, hk] = kr.astype(kr_ref.dtype)

    blk = lambda w, c: pl.BlockSpec((rows, w), lambda n: (n, c))
    return _call(body, name="ret_fwd", grid=(nblk // per,),
                 in_specs=[blk(512, 0), blk(512, 1), blk(1024, 1), blk(128, 0), blk(128, 0),
                           _const(intra.shape), _const(q_decay.shape), _const(k_decay.shape)],
                 out_specs=[blk(1024, 0), blk(512, 0), blk(512, 0),
                            pl.BlockSpec((per, RET_HEADS, RET_DK, RET_DV), lambda n: (n, 0, 0, 0))],
                 out_shape=[_sds((s, 1024), F32), _sds((s, 512), _MXU), _sds((s, 512), _MXU),
                            _sds((nblk, RET_HEADS, RET_DK, RET_DV), _MXU)],
                 scratch_shapes=[pltpu.VMEM((RET_HEADS, RET_DK, RET_DV), F32)],
                 args=(proj, proj, proj, cos, sin, intra, q_decay, k_decay))


def _ret_bwd_call(qr, kr, proj, states, dry, cos, sin, decays):
    s = qr.shape[0]
    nblk = s // BLK
    per = min(RET_CHUNKS, nblk)
    rows = per * BLK
    steps = nblk // per
    intra, q_decay, k_decay, chunk_decay = decays

    def body(qr_ref, kr_ref, rv_ref, st_ref, dry_ref, cos_ref, sin_ref, intra_ref, qd_ref, kd_ref,
             dp_ref, db_ref, dstate):
        @pl.when(pl.program_id(0) == 0)
        def _():
            dstate[...] = jnp.zeros_like(dstate)
            db_ref[...] = jnp.zeros_like(db_ref)

        for c in reversed(range(per)):
            rc = slice(c * BLK, (c + 1) * BLK)
            cos_v, sin_v = cos_ref[rc, :], sin_ref[rc, :]
            for h in range(RET_HEADS):
                hk = slice(h * RET_DK, (h + 1) * RET_DK)
                hv = slice(h * RET_DV, (h + 1) * RET_DV)
                q, k, v, d_out = qr_ref[rc, hk], kr_ref[rc, hk], rv_ref[rc, hv], dry_ref[rc, hv]
                d_next = dstate[h]
                scores = _dot_nt(q, k) * intra_ref[h]
                d_scores = _dot_nt(d_out, v) * intra_ref[h]
                d_cross = d_out * qd_ref[h]
                dq = _dot(d_scores, k) + _dot_nt(d_cross, st_ref[c, h])
                dk = _dot_tn(d_scores, q) + _dot_nt(v, d_next) * kd_ref[h]
                dv = _dot_tn(scores, d_out) + _dot(k.astype(F32) * kd_ref[h], d_next)
                dstate[h] = d_next * chunk_decay[h] + _dot_tn(q, d_cross)
                dq = (dq * cos_v - _swap_halves(dq, RET_DK // 2) * sin_v) * RET_SCALE
                dk = dk * cos_v - _swap_halves(dk, RET_DK // 2) * sin_v
                kcols = slice(512 + h * RET_DK, 512 + (h + 1) * RET_DK)
                vcols = slice(1024 + h * RET_DV, 1024 + (h + 1) * RET_DV)
                dp_ref[rc, hk] = dq.astype(dp_ref.dtype)
                dp_ref[rc, kcols] = dk.astype(dp_ref.dtype)
                dp_ref[rc, vcols] = dv.astype(dp_ref.dtype)
                db_ref[:, hk] += jnp.sum(dq, axis=0, keepdims=True)
                db_ref[:, kcols] += jnp.sum(dk, axis=0, keepdims=True)
                db_ref[:, vcols] += jnp.sum(dv, axis=0, keepdims=True)

    rblk = lambda w, c: pl.BlockSpec((rows, w), lambda n: (steps - 1 - n, c))
    return _call(body, name="ret_bwd", grid=(steps,),
                 in_specs=[rblk(512, 0), rblk(512, 0), rblk(1024, 1),
                           pl.BlockSpec((per, RET_HEADS, RET_DK, RET_DV), lambda n: (steps - 1 - n, 0, 0, 0)),
                           rblk(1024, 0), rblk(128, 0), rblk(128, 0),
                           _const(intra.shape), _const(q_decay.shape), _const(k_decay.shape)],
                 out_specs=[rblk(N_RET, 0), _acc(N_RET)],
                 out_shape=[_sds((s, N_RET), _MXU), _sds((1, N_RET), F32)],
                 scratch_shapes=[pltpu.VMEM((RET_HEADS, RET_DK, RET_DV), F32)],
                 args=(qr, kr, proj, states, dry, cos, sin, intra, q_decay, k_decay))


def _both_halves(x, g):
    lane = lax.broadcasted_iota(jnp.int32, x.shape, 1)
    keep = lane < HEAD_DIM if g == 0 else lane >= HEAD_DIM
    return jnp.where(keep, x, pltpu.roll(x, HEAD_DIM, 1))


def _stack_heads(ref, g):
    lane = lax.broadcasted_iota(jnp.int32, (BLK, 128), 1)
    pieces = []
    for j in range(g * 4, g * 4 + 4):
        chunk = ref[:, j * 128:(j + 1) * 128]
        pieces += [jnp.where(lane < HEAD_DIM, chunk, jnp.zeros_like(chunk)),
                   jnp.where(lane >= HEAD_DIM, chunk, jnp.zeros_like(chunk))]
    return jnp.concatenate(pieces, axis=0)


def _window_bias(first_block):
    kj = lax.broadcasted_iota(jnp.int32, (2 * BLK, BLK), 0)
    qi = lax.broadcasted_iota(jnp.int32, (2 * BLK, BLK), 1)
    first_key = jnp.where(first_block, BLK, 0)
    seen = (kj > qi) & (kj <= qi + BLK) & (kj >= first_key)
    return jnp.where(seen, 0.0, -1e30)


def _sink_softmax(scores, sink):
    m = jnp.maximum(jnp.max(scores, axis=0, keepdims=True), sink)
    e = jnp.exp(scores - m)
    e_sink = jnp.exp(sink - m)
    return e, e_sink, 1.0 / (jnp.sum(e, axis=0, keepdims=True) + e_sink)


def _head_pair(stacked_t, jj):
    even = stacked_t[0:HEAD_DIM, 2 * jj * BLK:(2 * jj + 1) * BLK]
    odd = stacked_t[HEAD_DIM:128, (2 * jj + 1) * BLK:(2 * jj + 2) * BLK]
    return jnp.concatenate([even, odd], axis=0).T


def _attn_fwd_call(proj, sinks, cos, sin):
    s = proj.shape[0]
    nblk = s // BLK

    def body(sink_ref, q_ref, k_ref, v_ref, cos_ref, sin_ref, ay_ref, qr_ref, kr_ref, vb_ref,
             kwin, vwin, bias, s_scr, p_scr):
        n = pl.program_id(0)

        @pl.when(n == 0)
        def _():
            kwin[...] = jnp.zeros_like(kwin)
            vwin[...] = jnp.zeros_like(vwin)

        @pl.when(n > 0)
        def _():
            kwin[0:BLK] = kwin[BLK:2 * BLK]
            vwin[0:BLK] = vwin[BLK:2 * BLK]

        cos_v, sin_v = cos_ref[...], sin_ref[...]
        k = k_ref[...]
        kr = (k * cos_v + _swap_halves(k, HEAD_DIM // 2) * sin_v).astype(kwin.dtype)
        kwin[BLK:2 * BLK] = kr
        vwin[BLK:2 * BLK] = v_ref[...].astype(vwin.dtype)
        kr_ref[...] = kr
        vb_ref[...] = vwin[BLK:2 * BLK]
        for j in range(Q_HEADS // 2):
            cols = slice(j * 128, (j + 1) * 128)
            q = q_ref[:, cols]
            qr_ref[:, cols] = ((q * cos_v + _swap_halves(q, HEAD_DIM // 2) * sin_v) * ATTN_SCALE).astype(qr_ref.dtype)
        bias[...] = _window_bias(n == 0)
        for g in range(KV_HEADS):
            kg = _both_halves(kwin[...], g)
            vg_t = _both_halves(vwin[...], g).astype(F32).T
            s_scr[...] = _dot_nt(kg, _stack_heads(qr_ref, g))
            for i in range(GROUP):
                cols = slice(i * BLK, (i + 1) * BLK)
                e, _, inv = _sink_softmax(s_scr[:, cols] + bias[...], sink_ref[0, g * GROUP + i])
                p_scr[:, cols] = (e * inv).astype(p_scr.dtype)
            out_t = _dot(vg_t, p_scr[...])
            for jj in range(4):
                j = g * 4 + jj
                ay_ref[:, j * 128:(j + 1) * 128] = _head_pair(out_t, jj).astype(ay_ref.dtype)

    blk = lambda w, c: pl.BlockSpec((BLK, w), lambda n: (n, c))
    off = (N_RET + N_GATE) // 128
    wide = (2 * BLK, GROUP * BLK)
    return _call(body, name="attn_fwd", grid=(nblk,),
                 in_specs=[pl.BlockSpec(memory_space=pltpu.SMEM), blk(1024, off // 8), blk(128, off + 8), blk(128, off + 9),
                           blk(128, 0), blk(128, 0)],
                 out_specs=[blk(1024, 0), blk(1024, 0), blk(128, 0), blk(128, 0)],
                 out_shape=[_sds((s, 1024), _MXU), _sds((s, 1024), _MXU), _sds((s, 128), _MXU), _sds((s, 128), _MXU)],
                 scratch_shapes=[pltpu.VMEM((2 * BLK, 128), _MXU), pltpu.VMEM((2 * BLK, 128), _MXU),
                                 pltpu.VMEM((2 * BLK, BLK), F32), pltpu.VMEM(wide, F32), pltpu.VMEM(wide, _MXU)],
                 args=(sinks, proj, proj, proj, cos, sin))


def _attn_bwd_call(qr, kr, vb, day, sinks, cos, sin):
    s = qr.shape[0]
    nblk = s // BLK

    def body(sink_ref, q_ref, kc_ref, kp_ref, vc_ref, vp_ref, do_ref, cos_ref, sin_ref, cosp_ref, sinp_ref,
             dp_ref, dsink_ref, db_ref, bias, s_scr, dp_scr, p_scr, ds_scr, dq_held, kv_held, kv_prev, kv_new):
        n = pl.program_id(0)
        valid = (n < nblk).astype(F32)

        @pl.when(n == 0)
        def _():
            dsink_ref[...] = jnp.zeros_like(dsink_ref)
            db_ref[...] = jnp.zeros_like(db_ref)

        @pl.when(n >= 1)
        def _():
            dp_ref[:, 0:1024] = dq_held[...]

        cos_v, sin_v = cos_ref[...], sin_ref[...]
        bias[...] = _window_bias(n == 0)
        lane1 = lax.broadcasted_iota(jnp.int32, (1, 128), 1)
        kwin = jnp.concatenate([kp_ref[...], kc_ref[...]], axis=0)
        vwin = jnp.concatenate([vp_ref[...], vc_ref[...]], axis=0)
        dk_heads, dv_heads = [], []
        dsink = jnp.zeros((1, 128), F32)
        for g in range(KV_HEADS):
            kg = _both_halves(kwin, g)
            vg = _both_halves(vwin, g)
            q_all = _stack_heads(q_ref, g)
            do_all = _stack_heads(do_ref, g)
            s_scr[...] = _dot_nt(kg, q_all)
            dp_scr[...] = _dot_nt(vg, do_all)
            for i in range(GROUP):
                head = g * GROUP + i
                cols = slice(i * BLK, (i + 1) * BLK)
                e, e_sink, inv = _sink_softmax(s_scr[:, cols] + bias[...], sink_ref[0, head])
                p = e * inv
                dp = dp_scr[:, cols]
                delta = jnp.sum(p * dp, axis=0, keepdims=True)
                p_scr[:, cols] = p.astype(p_scr.dtype)
                ds_scr[:, cols] = (p * (dp - delta)).astype(ds_scr.dtype)
                dsink = dsink + jnp.where(lane1 == head, -jnp.sum(e_sink * inv * delta, axis=1, keepdims=True), 0.0)
            dv_both = _dot(p_scr[...], do_all)
            dk_both = _dot(ds_scr[...], q_all)
            dv_heads.append(dv_both + pltpu.roll(dv_both, HEAD_DIM, 1))
            dk_heads.append(dk_both + pltpu.roll(dk_both, HEAD_DIM, 1))
            dq_t = _dot(kg.astype(F32).T, ds_scr[...])
            for jj in range(4):
                cols = slice((g * 4 + jj) * 128, (g * 4 + jj + 1) * 128)
                dq = _head_pair(dq_t, jj)
                dq = (dq * cos_v - _swap_halves(dq, HEAD_DIM // 2) * sin_v) * ATTN_SCALE
                dq_held[:, cols] = dq.astype(dq_held.dtype)
                db_ref[:, cols] += jnp.sum(dq, axis=0, keepdims=True) * valid
        dsink_ref[...] += dsink * valid
        lane2 = lax.broadcasted_iota(jnp.int32, (2 * BLK, 128), 1)
        dk_all = jnp.where(lane2 < HEAD_DIM, dk_heads[0], dk_heads[1])
        dv_all = jnp.where(lane2 < HEAD_DIM, dv_heads[0], dv_heads[1])
        kv_prev[:, 0:128] = dk_all[0:BLK] * valid
        kv_prev[:, 128:256] = dv_all[0:BLK] * valid
        kv_new[:, 0:128] = dk_all[BLK:2 * BLK]
        kv_new[:, 128:256] = dv_all[BLK:2 * BLK]

        @pl.when(n >= 1)
        def _():
            dkv = kv_held[...] + kv_prev[...]
            dk = dkv[:, 0:128]
            dk = dk * cosp_ref[...] - _swap_halves(dk, HEAD_DIM // 2) * sinp_ref[...]
            dv = dkv[:, 128:256]
            dp_ref[:, 1024:1152] = dk.astype(dp_ref.dtype)
            dp_ref[:, 1152:1280] = dv.astype(dp_ref.dtype)
            db_ref[:, 1024:1152] += jnp.sum(dk, axis=0, keepdims=True)
            db_ref[:, 1152:1280] += jnp.sum(dv, axis=0, keepdims=True)

        kv_held[...] = kv_new[...]

    blk = lambda w: pl.BlockSpec((BLK, w), lambda n: (jnp.minimum(n, nblk - 1), 0))
    pblk = lambda w: pl.BlockSpec((BLK, w), lambda n: (jnp.maximum(n - 1, 0), 0))
    wide = (2 * BLK, GROUP * BLK)
    return _call(body, name="attn_bwd", grid=(nblk + 1,),
                 in_specs=[pl.BlockSpec(memory_space=pltpu.SMEM), blk(1024), blk(128), pblk(128), blk(128), pblk(128),
                           blk(1024), blk(128), blk(128), pblk(128), pblk(128)],
                 out_specs=[pblk(N_ATTN), _acc(128), _acc(N_ATTN)],
                 out_shape=[_sds((s, N_ATTN), _MXU), _sds((1, 128), F32), _sds((1, N_ATTN), F32)],
                 scratch_shapes=[pltpu.VMEM((2 * BLK, BLK), F32), pltpu.VMEM(wide, F32), pltpu.VMEM(wide, F32),
                                 pltpu.VMEM(wide, _MXU), pltpu.VMEM(wide, _MXU), pltpu.VMEM((BLK, 1024), _MXU),
                                 pltpu.VMEM((BLK, 256), F32), pltpu.VMEM((BLK, 256), F32), pltpu.VMEM((BLK, 256), F32)],
                 args=(sinks, qr, kr, kr, vb, vb, day, cos, sin, cos, sin))


def _group_norm(y):
    mu = jnp.mean(y, axis=-1, keepdims=True)
    yc = y - mu
    rs = lax.rsqrt(jnp.mean(yc * yc, axis=-1, keepdims=True) + EPS)
    return yc * rs, rs


GATE_COL = N_RET // 1024


def _merge_fwd_call(x, ry, proj, ay, gn_g, w_ro, w_ao, w_o, tm):
    s = x.shape[0]

    def body(x_ref, ry_ref, rg_ref, ga_ref, gb_ref, ay_ref, gn_ref, wro_ref, wao_ref, wo_ref,
             ain_ref, a_ref, b_ref, mg_ref, x1_ref):
        for h in range(RET_HEADS):
            hv = slice(h * RET_DV, (h + 1) * RET_DV)
            yhat, _ = _group_norm(ry_ref[:, hv])
            rg = rg_ref[:, hv]
            ain_ref[:, hv] = ((rg * _sigmoid(rg)) * (yhat * gn_ref[:, hv])).astype(ain_ref.dtype)
        a = _dot(ain_ref[...], wro_ref[...])
        b = _dot(ay_ref[...], wao_ref[...])
        a_ref[...] = a
        b_ref[...] = b
        merged = (_sigmoid(ga_ref[...]) * a + _sigmoid(gb_ref[...]) * b).astype(mg_ref.dtype)
        mg_ref[...] = merged
        x1_ref[...] = x_ref[...] + _dot(merged, wo_ref[...])

    return _call(body, name="merge_fwd", grid=(s // tm,),
                 in_specs=[_rows(tm, D), _rows(tm, 1024), _rows(tm, 1024, GATE_COL), _rows(tm, 1024, GATE_COL + 1),
                           _rows(tm, 1024, GATE_COL + 2), _rows(tm, 1024), _acc(1024),
                           _const((D, D)), _const((D, D)), _const((D, D))],
                 out_specs=[_rows(tm, D)] * 5,
                 out_shape=[_sds((s, D), _MXU), _sds((s, D), F32), _sds((s, D), F32), _sds((s, D), _MXU), _sds((s, D), F32)],
                 args=(x, ry, proj, proj, proj, ay, gn_g, w_ro, w_ao, w_o), big=True)


def _merge_bwd_call(dx1, a, b, ry, proj, gn_g, w_ro, w_ao, w_o, tm, exchange):
    s = dx1.shape[0]

    def body(dx1_ref, a_ref, b_ref, ry_ref, rg_ref, ga_ref, gb_ref, gn_ref, wro_ref, wao_ref, wo_ref,
             da_ref, dbr_ref, dp_ref, day_ref, dry_ref, dbias_ref, dgn_ref):
        @pl.when(pl.program_id(0) == 0)
        def _():
            dbias_ref[...] = jnp.zeros_like(dbias_ref)
            dgn_ref[...] = jnp.zeros_like(dgn_ref)

        d_merged = _dot_nt(dx1_ref[...], wo_ref[...])
        sa, sb = _sigmoid(ga_ref[...]), _sigmoid(gb_ref[...])
        d_a = d_merged * sa
        d_b = d_merged * sb
        da_ref[...] = d_a.astype(da_ref.dtype)
        dbr_ref[...] = d_b.astype(dbr_ref.dtype)
        d_ga = d_merged * a_ref[...] * (sa * (1.0 - sa))
        d_gb = d_merged * b_ref[...] * (sb * (1.0 - sb))
        dp_ref[:, 1024:2048] = d_ga.astype(dp_ref.dtype)
        dp_ref[:, 2048:3072] = d_gb.astype(dp_ref.dtype)
        dbias_ref[:, 1024:2048] += jnp.sum(d_ga, axis=0, keepdims=True)
        dbias_ref[:, 2048:3072] += jnp.sum(d_gb, axis=0, keepdims=True)
        day_ref[...] = _dot_nt(d_b, wao_ref[...]).astype(day_ref.dtype)
        d_ain = _dot_nt(d_a, wro_ref[...])
        for h in range(RET_HEADS):
            hv = slice(h * RET_DV, (h + 1) * RET_DV)
            yhat, rs = _group_norm(ry_ref[:, hv])
            rg = rg_ref[:, hv]
            sg = _sigmoid(rg)
            gn = gn_ref[:, hv]
            d_h = d_ain[:, hv]
            d_rg = d_h * (yhat * gn) * (sg * (1.0 + rg * (1.0 - sg)))
            d_ryn = d_h * (rg * sg)
            dgn_ref[:, hv] += jnp.sum(d_ryn * yhat, axis=0, keepdims=True)
            d_yhat = d_ryn * gn
            dry_ref[:, hv] = rs * (d_yhat - jnp.mean(d_yhat, axis=-1, keepdims=True)
                                   - yhat * jnp.mean(d_yhat * yhat, axis=-1, keepdims=True))
            dp_ref[:, hv] = d_rg.astype(dp_ref.dtype)
            dbias_ref[:, hv] += jnp.sum(d_rg, axis=0, keepdims=True)

    return _call(body, name="merge_bwd", grid=(s // tm,),
                 in_specs=[_rows(tm, D), _rows(tm, D), _rows(tm, D), _rows(tm, 1024), _rows(tm, 1024, GATE_COL),
                           _rows(tm, 1024, GATE_COL + 1), _rows(tm, 1024, GATE_COL + 2), _acc(1024),
                           _const((D, D)), _const((D, D)), _const((D, D))],
                 out_specs=[_rows(tm, D), _rows(tm, D), _rows(tm, N_GATE), _rows(tm, D), _rows(tm, D), _acc(N_GATE),
                            _acc(1024)],
                 out_shape=[_sds((s, D), _MXU), _sds((s, D), _MXU), _sds((s, N_GATE), _MXU), _sds((s, D), _MXU),
                            _sds((s, D), F32), _sds((1, N_GATE), F32), _sds((1, 1024), F32)],
                 args=(dx1, a, b, ry, proj, proj, proj, gn_g, w_ro, w_ao, w_o), big=True, exchange=exchange)


def _ffn_fwd_call(x1, target, ln2_g, lnf_g, w_g, w_u, w_d, tm):
    s = x1.shape[0]

    def body(x1_ref, t_ref, g2_ref, gf_ref, wg_ref, wu_ref, wd_ref,
             h2_ref, g_ref, u_ref, f_ref, dx2_ref, loss_ref, dgf_ref):
        @pl.when(pl.program_id(0) == 0)
        def _():
            loss_ref[...] = jnp.zeros_like(loss_ref)
            dgf_ref[...] = jnp.zeros_like(dgf_ref)

        x1v = x1_ref[...]
        r1 = lax.rsqrt(jnp.mean(x1v * x1v, axis=-1, keepdims=True) + EPS)
        h2 = ((x1v * r1) * g2_ref[...]).astype(h2_ref.dtype)
        h2_ref[...] = h2
        g = _dot_nt(h2, wg_ref[...])
        u = _dot_nt(h2, wu_ref[...])
        g_ref[...] = g
        u_ref[...] = u
        f = ((g * _sigmoid(g)) * u).astype(f_ref.dtype)
        f_ref[...] = f
        x2 = x1v + _dot(f, wd_ref[...])
        r2 = lax.rsqrt(jnp.mean(x2 * x2, axis=-1, keepdims=True) + EPS)
        xhat = x2 * r2
        err = xhat * gf_ref[...] - t_ref[...]
        loss_ref[...] += 0.5 * jnp.sum(jnp.mean(err * err, axis=-1, keepdims=True))
        dy = err * (1.0 / D)
        dgf_ref[...] += jnp.sum(dy * xhat, axis=0, keepdims=True)
        dxh = dy * gf_ref[...]
        dx2_ref[...] = r2 * (dxh - xhat * jnp.mean(dxh * xhat, axis=-1, keepdims=True))

    return _call(body, name="ffn_fwd", grid=(s // tm,),
                 in_specs=[_rows(tm, D), _rows(tm, D), _acc(D), _acc(D), _const((D_FF, D)), _const((D_FF, D)),
                           _const((D_FF, D))],
                 out_specs=[_rows(tm, D), _rows(tm, D_FF), _rows(tm, D_FF), _rows(tm, D_FF), _rows(tm, D), _acc(128),
                            _acc(D)],
                 out_shape=[_sds((s, D), _MXU), _sds((s, D_FF), F32), _sds((s, D_FF), F32), _sds((s, D_FF), _MXU),
                            _sds((s, D), F32), _sds((1, 128), F32), _sds((1, D), F32)],
                 args=(x1, target, ln2_g, lnf_g, w_g, w_u, w_d), big=True)


def _ffn_bwd_call(dx2, x1, g, u, ln2_g, w_g, w_u, w_d, tm):
    s = dx2.shape[0]

    def body(dx2_ref, x1_ref, g_ref, u_ref, g2_ref, wg_ref, wu_ref, wd_ref, dx1_ref, dg_ref, du_ref, dg2_ref):
        @pl.when(pl.program_id(0) == 0)
        def _():
            dg2_ref[...] = jnp.zeros_like(dg2_ref)

        dx2v = dx2_ref[...]
        df = _dot_nt(dx2v, wd_ref[...])
        gv, uv = g_ref[...], u_ref[...]
        sg = _sigmoid(gv)
        du = (df * (gv * sg)).astype(du_ref.dtype)
        dg = (df * uv * (sg * (1.0 + gv * (1.0 - sg)))).astype(dg_ref.dtype)
        du_ref[...] = du
        dg_ref[...] = dg
        dh2 = _dot(dg, wg_ref[...]) + _dot(du, wu_ref[...])
        x1v = x1_ref[...]
        r1 = lax.rsqrt(jnp.mean(x1v * x1v, axis=-1, keepdims=True) + EPS)
        xhat = x1v * r1
        dg2_ref[...] += jnp.sum(dh2 * xhat, axis=0, keepdims=True)
        dxh = dh2 * g2_ref[...]
        dx1_ref[...] = dx2v + r1 * (dxh - xhat * jnp.mean(dxh * xhat, axis=-1, keepdims=True))

    return _call(body, name="ffn_bwd", grid=(s // tm,),
                 in_specs=[_rows(tm, D), _rows(tm, D), _rows(tm, D_FF), _rows(tm, D_FF), _acc(D),
                           _const((D_FF, D)), _const((D_FF, D)), _const((D_FF, D))],
                 out_specs=[_rows(tm, D), _rows(tm, D_FF), _rows(tm, D_FF), _acc(D)],
                 out_shape=[_sds((s, D), F32), _sds((s, D_FF), _MXU), _sds((s, D_FF), _MXU), _sds((1, D), F32)],
                 args=(dx2, x1, g, u, ln2_g, w_g, w_u, w_d), big=True)


def _dx_call(x, dx1, dp_ret, dp_gate, dp_attn, ln1_g, w_in, tm, exchange):
    s = x.shape[0]

    def body(x_ref, dx1_ref, dr_ref, dg_ref, da_ref, g1_ref, w_ref, dx_ref, dg1_ref):
        @pl.when(pl.program_id(0) == 0)
        def _():
            dg1_ref[...] = jnp.zeros_like(dg1_ref)

        dh = (_dot(dr_ref[...], w_ref[0:2048, :]) + _dot(dg_ref[:, 0:1024], w_ref[2048:3072, :])
              + _dot(da_ref[...], w_ref[3072:4352, :]) + _dot(dg_ref[:, 1024:3072], w_ref[4352:6400, :]))
        xv = x_ref[...]
        r = lax.rsqrt(jnp.mean(xv * xv, axis=-1, keepdims=True) + EPS)
        xhat = xv * r
        dg1_ref[...] += jnp.sum(dh * xhat, axis=0, keepdims=True)
        dxh = dh * g1_ref[...]
        dx_ref[...] = dx1_ref[...] + r * (dxh - xhat * jnp.mean(dxh * xhat, axis=-1, keepdims=True))

    return _call(body, name="dx", grid=(s // tm,),
                 in_specs=[_rows(tm, D), _rows(tm, D), _rows(tm, N_RET), _rows(tm, N_GATE), _rows(tm, N_ATTN), _acc(D),
                           _const((D_IN, D))],
                 out_specs=[_rows(tm, D), _acc(D)],
                 out_shape=[_sds((s, D), F32), _sds((1, D), F32)],
                 args=(x, dx1, dp_ret, dp_gate, dp_attn, ln1_g, w_in), big=True, exchange=exchange)


def _adamw(g, w, m, v):
    m_new = B1 * m + (1.0 - B1) * g
    v_new = B2 * v + (1.0 - B2) * (g * g)
    m_hat = m_new / (1.0 - B1 ** STEP)
    v_hat = v_new / (1.0 - B2 ** STEP)
    return -LR * (m_hat / (jnp.sqrt(v_hat) + ADAM_EPS) + WD * w), m_new, v_new


def _slot_sum(p_ref):
    g = p_ref[0].astype(F32)
    for k in range(1, N_DEV):
        g = g + p_ref[k].astype(F32)
    return g


def _adamw_call(parts, w, m, v, name, tr):
    rows, cols = w.shape

    def body(p_ref, w_ref, m_ref, v_ref, g_ref, dw_ref, nm_ref, nv_ref):
        g = _slot_sum(p_ref)
        g_ref[...] = g
        dw_ref[...], nm_ref[...], nv_ref[...] = _adamw(g, w_ref[...], m_ref[...], v_ref[...])

    p_spec = pl.BlockSpec((N_DEV, tr, cols), lambda i: (0, i, 0))
    spec = pl.BlockSpec((tr, cols), lambda i: (i, 0))
    return _call(body, name=name, grid=(rows // tr,), in_specs=[p_spec, spec, spec, spec], out_specs=[spec] * 4,
                 out_shape=[_sds((rows, cols), F32)] * 4, args=(parts, w, m, v))


SMALL_WIDTHS = [1024, 6400, 1024, 16, 1024, 1024]
SMALL_OFFSETS = [0, 1024, 7424, 8448, 8576, 9600]
LOSS_OFFSET = 10624
SMALL_LEN = 10752


def _pack_small(grads, loss):
    pieces = []
    for gr, width in zip(grads, SMALL_WIDTHS):
        pieces.append(jnp.pad(gr.reshape(1, width), ((0, 0), (0, -width % 128))))
    pieces.append(jnp.pad(loss.reshape(1, 1), ((0, 0), (0, 127))))
    return jnp.concatenate(pieces, axis=1)


def _adamw_small_call(parts, ws, ms, vs):
    n = len(ws)

    def body(*refs):
        p_ref, w_refs, m_refs, v_refs = refs[0], refs[1:1 + n], refs[1 + n:1 + 2 * n], refs[1 + 2 * n:1 + 3 * n]
        outs = refs[1 + 3 * n:]
        g_all = _slot_sum(p_ref)
        for i, (off, width) in enumerate(zip(SMALL_OFFSETS, SMALL_WIDTHS)):
            g = g_all[:, off:off + width]
            outs[i][...] = g
            outs[n + i][...], outs[2 * n + i][...], outs[3 * n + i][...] = _adamw(
                g, w_refs[i][...], m_refs[i][...], v_refs[i][...])
        outs[4 * n][...] = g_all[:, LOSS_OFFSET:LOSS_OFFSET + 128]

    whole = lambda shape: pl.BlockSpec(shape, lambda i: (0,) * len(shape))
    small = [whole((1, w)) for w in SMALL_WIDTHS]
    res = _call(body, name="adamw_small", grid=(1,), in_specs=[whole((N_DEV, 1, SMALL_LEN))] + small * 3,
                out_specs=small * 4 + [whole((1, 128))],
                out_shape=[_sds((1, w), F32) for w in SMALL_WIDTHS] * 4 + [_sds((1, 128), F32)],
                args=(parts, *ws, *ms, *vs))
    return [res[k * n:(k + 1) * n] for k in range(4)], res[4 * n]


def kernel(x, ln1_g, w_in, b_in, ret_norm_g, w_ret_out, attn_sinks, w_attn_out, w_out, ln2_g, w_ffn_gate, w_ffn_up, w_ffn_down, lnf_g, loss_target, m_ln1_g, m_w_in, m_b_in, m_ret_norm_g, m_w_ret_out, m_attn_sinks, m_w_attn_out, m_w_out, m_ln2_g, m_w_ffn_gate, m_w_ffn_up, m_w_ffn_down, m_lnf_g, v_ln1_g, v_w_in, v_b_in, v_ret_norm_g, v_w_ret_out, v_attn_sinks, v_w_attn_out, v_w_out, v_ln2_g, v_w_ffn_gate, v_w_ffn_up, v_w_ffn_down, v_lnf_g):
    cast = lambda a: a.astype(_MXU)
    xs, target = x[0], loss_target[0]
    s = xs.shape[0]
    r_sq = w_ret_out.shape[1]
    r_dn = w_ffn_down.shape[1]
    c_in = w_in.shape[2]
    c_ff = w_ffn_gate.shape[2]
    tm, tk = min(256, s), min(2048, s)
    lnf_row = lnf_g.reshape(1, D)
    cos_r, sin_r = _rope_tables(s, RET_DK // 2)
    cos_a, sin_a = _rope_tables(s, HEAD_DIM // 2)
    decays = _retention_decays()
    tr_shard = lambda a: a[0].T
    per_dev = lambda a, n: a.reshape(N_DEV, n, D)

    (h,), (all_in,) = _ln_call(xs, ln1_g, tm, _AllGather([cast(tr_shard(w_in))]))
    wt_in = all_in.reshape(N_DEV * c_in, D)
    rest = [tr_shard(w_ffn_gate), tr_shard(w_ffn_up), w_ret_out[0], w_attn_out[0], w_out[0], w_ffn_down[0]]
    (proj,), gathered = _proj_call(h, wt_in, b_in, _AllGather([cast(a) for a in rest]))
    wt_g, wt_u, full_ro, full_ao, full_o, full_d = (a.reshape(N_DEV * a.shape[1], D) for a in gathered)
    ry, qr, kr, states = _ret_fwd_call(proj, cos_r, sin_r, decays)
    ay, aqr, akr, avb = _attn_fwd_call(proj, attn_sinks, cos_a, sin_a)
    a_in, br_a, br_b, merged, x1 = _merge_fwd_call(xs, ry, proj, ay, ret_norm_g, full_ro, full_ao, full_o, tm)
    h2, g, u, f, dx2, loss, d_lnf = _ffn_fwd_call(x1, target, ln2_g, lnf_row, wt_g, wt_u, full_d, tm)

    dx1, dg, du, d_ln2 = _ffn_bwd_call(dx2, x1, g, u, ln2_g, wt_g, wt_u, full_d, tm)
    dw_d = _mm_tn(f, dx2, "dw_ffn_down", 1408, 1024, tk)
    dwt_g, (got_d,) = _mm_tn(dg, h2, "dw_ffn_gate", 1408, 1024, tk, _AllToAll([per_dev(dw_d, r_dn)]))
    dwt_u = _mm_tn(du, h2, "dw_ffn_up", 1408, 1024, tk)
    (d_a, d_b, dp_gate, day, dry, db_gate, d_gn), (got_g, got_u) = _merge_bwd_call(
        dx1, br_a, br_b, ry, proj, ret_norm_g, full_ro, full_ao, full_o, tm,
        _AllToAll([per_dev(dwt_g, c_ff), per_dev(dwt_u, c_ff)]))
    dw_o = _mm_tn(merged, dx1, "dw_out", 1024, 1024, tk)
    dw_ro = _mm_tn(a_in, d_a, "dw_ret_out", 1024, 1024, tk)
    dw_ao = _mm_tn(ay, d_b, "dw_attn_out", 1024, 1024, tk)
    dp_attn, d_sinks, db_attn = _attn_bwd_call(aqr, akr, avb, day, attn_sinks, cos_a, sin_a)
    dp_ret, db_ret = _ret_bwd_call(qr, kr, proj, states, dry, cos_r, sin_r, decays)
    dwt_ret = _mm_tn(dp_ret, h, "dw_in_ret", 1024, 1024, tk)
    dwt_gate, (got_ro, got_ao, got_o) = _mm_tn(
        dp_gate, h, "dw_in_gate", 1024, 1024, tk,
        _AllToAll([per_dev(dw_ro, r_sq), per_dev(dw_ao, r_sq), per_dev(dw_o, r_sq)]))
    dwt_attn = _mm_tn(dp_attn, h, "dw_in_attn", 1280, 1024, tk)
    dwt_in = jnp.concatenate([dwt_ret, dwt_gate[0:1024], dwt_attn, dwt_gate[1024:3072]], axis=0)
    (dx, d_ln1), (got_in,) = _dx_call(xs, dx1, dp_ret, dp_gate, dp_attn, ln1_g, wt_in, tm,
                                      _AllToAll([per_dev(dwt_in, c_in)]))
    db_in = jnp.concatenate([db_ret, db_gate[:, 0:1024], db_attn, db_gate[:, 1024:3072]], axis=1)
    small = [d_ln1, db_in, d_gn, d_sinks[:, 0:Q_HEADS], d_ln2, d_lnf]
    (got_small,) = _exchange_call(_AllGather([_pack_small(small, loss[0, 0])]), "gather_small")

    transposed = ("w_in", "w_ffn_gate", "w_ffn_up")
    res = {}
    res["w_in"] = _adamw_call(got_in, tr_shard(w_in), tr_shard(m_w_in), tr_shard(v_w_in), "adamw_w_in", 160)
    res["w_ffn_gate"] = _adamw_call(got_g, tr_shard(w_ffn_gate), tr_shard(m_w_ffn_gate), tr_shard(v_w_ffn_gate),
                                    "adamw_ffn_gate", 176)
    res["w_ffn_up"] = _adamw_call(got_u, tr_shard(w_ffn_up), tr_shard(m_w_ffn_up), tr_shard(v_w_ffn_up),
                                  "adamw_ffn_up", 176)
    res["w_ret_out"] = _adamw_call(got_ro, w_ret_out[0], m_w_ret_out[0], v_w_ret_out[0], "adamw_ret_out", r_sq)
    res["w_attn_out"] = _adamw_call(got_ao, w_attn_out[0], m_w_attn_out[0], v_w_attn_out[0], "adamw_attn_out", r_sq)
    res["w_out"] = _adamw_call(got_o, w_out[0], m_w_out[0], v_w_out[0], "adamw_out", r_sq)
    res["w_ffn_down"] = _adamw_call(got_d, w_ffn_down[0], m_w_ffn_down[0], v_w_ffn_down[0], "adamw_ffn_down", 176)
    small_names = ["ln1_g", "b_in", "ret_norm_g", "attn_sinks", "ln2_g", "lnf_g"]
    small_res, loss_row = _adamw_small_call(
        got_small, [ln1_g, b_in, ret_norm_g, attn_sinks, ln2_g, lnf_row],
        [m_ln1_g, m_b_in, m_ret_norm_g, m_attn_sinks, m_ln2_g, m_lnf_g.reshape(1, D)],
        [v_ln1_g, v_b_in, v_ret_norm_g, v_attn_sinks, v_ln2_g, v_lnf_g.reshape(1, D)])
    for i, nm in enumerate(small_names):
        res[nm] = [small_res[kind][i] for kind in range(4)]

    order = ["ln1_g", "w_in", "b_in", "ret_norm_g", "w_ret_out", "attn_sinks", "w_attn_out", "w_out", "ln2_g",
             "w_ffn_gate", "w_ffn_up", "w_ffn_down", "lnf_g"]
    outs = [loss_row[0, 0], dx[None]]
    for kind in range(4):
        for nm in order:
            val = res[nm][kind]
            if nm in transposed:
                val = val.T
            outs.append(val[None] if nm.startswith("w_") else val.reshape(D) if nm == "lnf_g" else val)
    return tuple(outs)
```

```python
import math

import numpy as np
import jax
import jax.numpy as jnp
from jax import lax
from jax.experimental import pallas as pl
from jax.experimental.pallas import tpu as pltpu

F32 = jnp.float32
_MXU = jnp.bfloat16
_STORE = jnp.bfloat16

N_DEV = 8
D = 1024
RET_HEADS, RET_DK, RET_DV = 4, 128, 256
BLK = 128
Q_HEADS, KV_HEADS, HEAD_DIM = 16, 2, 64
GROUP = Q_HEADS // KV_HEADS
D_FF = 2816
N_RET, N_GATE, N_ATTN = 2048, 3072, 1280
D_IN = N_RET + N_GATE + N_ATTN
ROPE_THETA = 10000.0
EPS = 1e-6
RET_SCALE = RET_DK ** -0.5
ATTN_SCALE = HEAD_DIM ** -0.5
LR, B1, B2, ADAM_EPS, WD, STEP = 0.001, 0.9, 0.999, 1e-08, 0.01, 10
VMEM_LIMIT_MB = 56
MESH = pl.DeviceIdType.MESH


def _dot(a, b):
    return jnp.dot(a.astype(_MXU), b.astype(_MXU), preferred_element_type=F32)


def _dot_nt(a, b):
    return lax.dot_general(a.astype(_MXU), b.astype(_MXU), (((1,), (1,)), ((), ())), preferred_element_type=F32)


def _dot_tn(a, b):
    return lax.dot_general(a.astype(_MXU), b.astype(_MXU), (((0,), (0,)), ((), ())), preferred_element_type=F32)


def _sigmoid(x):
    return 1.0 / (1.0 + jnp.exp(-x))


def _cparams(n_axes, big=False):
    kw = dict(dimension_semantics=("arbitrary",) * n_axes)
    if big:
        kw["vmem_limit_bytes"] = VMEM_LIMIT_MB * 2**20
    return pltpu.CompilerParams(**kw)


def _rows(tm, width, col=0):
    return pl.BlockSpec((tm, width), lambda i: (i, col))


def _const(shape):
    nd = len(shape)
    return pl.BlockSpec(shape, lambda *_: (0,) * nd, pipeline_mode=pl.Buffered(1))


def _acc(width):
    return pl.BlockSpec((1, width), lambda *_: (0, 0))


def _sds(shape, dtype):
    return jax.ShapeDtypeStruct(shape, dtype)


def _swap_halves(x, half):
    w = x.shape[-1]
    if 2 * half == w:
        return pltpu.roll(x, half, 1)
    lane = lax.broadcasted_iota(jnp.int32, x.shape, 1)
    return jnp.where(lane % (2 * half) < half, pltpu.roll(x, w - half, 1), pltpu.roll(x, half, 1))


def _rope_tables(seq, half):
    lane = jnp.arange(128, dtype=jnp.int32)
    inv_freq = ROPE_THETA ** (-(lane % half).astype(F32) / half)
    sign = jnp.where(lane % (2 * half) < half, -1.0, 1.0).astype(F32)
    ang = jnp.arange(seq, dtype=jnp.int32).astype(F32)[:, None] * inv_freq[None, :]
    return jnp.cos(ang), jnp.sin(ang) * sign[None, :]


def _retention_decays():
    log_gamma = np.log1p(-np.exp2(-5.0 - np.arange(RET_HEADS, dtype=np.float32))).astype(np.float32)
    idx = np.arange(BLK, dtype=np.float32)
    rel = idx[:, None] - idx[None, :]
    intra = np.where(rel[None] >= 0, np.exp(log_gamma[:, None, None] * np.maximum(rel, 0.0)[None]), 0.0)
    q_decay = np.exp(log_gamma[:, None] * (idx + 1.0))[:, :, None]
    k_decay = np.exp(log_gamma[:, None] * (BLK - 1.0 - idx))[:, :, None]
    chunk_decay = [float(np.exp(np.float32(lg * BLK))) for lg in log_gamma]
    return (jnp.asarray(intra, F32), jnp.asarray(q_decay, F32), jnp.asarray(k_decay, F32), chunk_decay)


def _position():
    return lax.axis_index("x"), lax.axis_index("y"), lax.axis_index("c")


def _slot(px, py, pc):
    return 4 * px + 2 * py + pc


class _AllGather:
    def __init__(self, blocks):
        self.blocks = list(blocks)
        nb = len(self.blocks)
        self.out_shape = [_sds((N_DEV,) + b.shape, b.dtype) for b in self.blocks]
        self.scratch = [pltpu.SemaphoreType.DMA((nb, 7)), pltpu.SemaphoreType.DMA((nb, 7)),
                        pltpu.SemaphoreType.DMA((nb,))]

    def phases(self, ins, outs, send_sems, recv_sems, local_sems):
        nb = len(ins)
        x, y, c = _position()
        me, sibling = (x, y, c), (x, y, 1 - c)
        chips = [(1 - x, y), (x, 1 - y), (1 - x, 1 - y)]

        def copy(b, k, block, to, src=None):
            dst = outs[b].at[_slot(*block)]
            return pltpu.make_async_remote_copy(
                src_ref=dst if src is None else src, dst_ref=dst, send_sem=send_sems.at[b, k],
                recv_sem=recv_sems.at[b, k], device_id=to, device_id_type=MESH)

        def own(b):
            return pltpu.make_async_copy(ins[b], outs[b].at[_slot(*me)], local_sems.at[b])

        def first(b):
            return [copy(b, 0, me, sibling, src=ins[b])] + [
                copy(b, 1 + j, me, (*chip, c), src=ins[b]) for j, chip in enumerate(chips)]

        def start():
            for b in range(nb):
                own(b).start()
                for cp in first(b):
                    cp.start()

        def forward():
            for b in range(nb):
                for j, chip in enumerate(chips):
                    copy(b, 1 + j, (*chip, c), me).wait_recv()
                    copy(b, 4 + j, (*chip, c), sibling).start()

        def finish():
            for b in range(nb):
                copy(b, 0, sibling, me).wait_recv()
                for j, chip in enumerate(chips):
                    copy(b, 4 + j, (*chip, 1 - c), me).wait_recv()
            for b in range(nb):
                for cp in first(b):
                    cp.wait_send()
                for j, chip in enumerate(chips):
                    copy(b, 4 + j, (*chip, c), sibling).wait_send()
                own(b).wait()

        return start, forward, finish


class _AllToAll:
    def __init__(self, blocks):
        self.blocks = list(blocks)
        nb = len(self.blocks)
        self.out_shape = [_sds(b.shape, b.dtype) for b in self.blocks]
        self.scratch = [pltpu.SemaphoreType.DMA((nb, 7)), pltpu.SemaphoreType.DMA((nb, 7)),
                        pltpu.SemaphoreType.DMA((nb,))]

    def phases(self, ins, outs, send_sems, recv_sems, local_sems):
        nb = len(ins)
        x, y, c = _position()
        flip = lambda v, bit: 1 - v if bit else v
        peers = [(flip(x, k >> 2 & 1), flip(y, k >> 1 & 1), flip(c, k & 1)) for k in range(1, N_DEV)]

        def copy(b, k, peer, landed=False):
            return pltpu.make_async_remote_copy(
                src_ref=ins[b].at[_slot(*peer)], dst_ref=outs[b].at[_slot(*peer) if landed else _slot(x, y, c)],
                send_sem=send_sems.at[b, k], recv_sem=recv_sems.at[b, k], device_id=peer, device_id_type=MESH)

        def own(b):
            return pltpu.make_async_copy(ins[b].at[_slot(x, y, c)], outs[b].at[_slot(x, y, c)], local_sems.at[b])

        def start():
            for b in range(nb):
                own(b).start()
                for k, peer in enumerate(peers):
                    copy(b, k, peer).start()

        def forward():
            pass

        def finish():
            for b in range(nb):
                for k, peer in enumerate(peers):
                    copy(b, k, peer, landed=True).wait_recv()
            for b in range(nb):
                for k, peer in enumerate(peers):
                    copy(b, k, peer).wait_send()
                own(b).wait()

        return start, forward, finish


class _RowScatter:
    def __init__(self, arrays, pieces, n):
        self.blocks = list(arrays)
        self.pieces, self.n = pieces, n
        self.out_shape = [_sds((N_DEV, n, D), arrays[0].dtype)]
        self.scratch = [pltpu.SemaphoreType.DMA((N_DEV,)), pltpu.SemaphoreType.DMA((N_DEV,)), pltpu.SemaphoreType.DMA]

    def _parts(self, k):
        lo, hi, pos, res = k * self.n, (k + 1) * self.n, 0, []
        for arr, first, last in self.pieces:
            a, b = max(lo, pos), min(hi, pos + last - first)
            if a < b:
                res.append((arr, first + a - pos, b - a, a - lo))
            pos += last - first
        return res

    def phases(self, ins, outs, send_sems, recv_sems, local_sem):
        (out,) = outs
        x, y, c = _position()
        me = _slot(x, y, c)

        def start():
            for k in range(N_DEV):
                dist = jnp.bitwise_xor(me, k)

                @pl.when(me != k)
                def _():
                    for arr, first, rows, at in self._parts(k):
                        pltpu.make_async_remote_copy(
                            src_ref=ins[arr].at[pl.ds(first, rows)], dst_ref=out.at[me, pl.ds(at, rows)],
                            send_sem=send_sems.at[dist], recv_sem=recv_sems.at[dist],
                            device_id=(k >> 2 & 1, k >> 1 & 1, k & 1), device_id_type=MESH).start()

                @pl.when(me == k)
                def _():
                    for arr, first, rows, at in self._parts(k):
                        pltpu.make_async_copy(ins[arr].at[pl.ds(first, rows)], out.at[me, pl.ds(at, rows)], local_sem).start()

        def forward():
            pass

        def whole_block(dist):
            return pltpu.make_async_remote_copy(
                src_ref=out.at[me], dst_ref=out.at[jnp.bitwise_xor(me, dist)], send_sem=send_sems.at[dist],
                recv_sem=recv_sems.at[dist], device_id=(x, y, c), device_id_type=MESH)

        def finish():
            for dist in range(1, N_DEV):
                whole_block(dist).wait_recv()
            for dist in range(1, N_DEV):
                whole_block(dist).wait_send()
            pltpu.make_async_copy(out.at[me], out.at[me], local_sem).wait()

        return start, forward, finish


def _call(body, *, name, grid, in_specs, out_specs, out_shape, args, scratch_shapes=(), big=False, exchange=None):
    params = _cparams(len(grid), big)
    if exchange is None:
        return pl.pallas_call(body, name=name, grid=grid, in_specs=in_specs, out_specs=out_specs, out_shape=out_shape,
                              scratch_shapes=list(scratch_shapes), compiler_params=params)(*args)
    n_in, n_out, n_scr = len(in_specs), len(out_specs), len(scratch_shapes)
    nb, nb_out = len(exchange.blocks), len(exchange.out_shape)
    steps = math.prod(grid)

    def carried(*refs):
        pos = 0
        parts = []
        for n in (n_in, nb, n_out, nb_out, n_scr, len(exchange.scratch)):
            parts.append(refs[pos:pos + n])
            pos += n
        ins, x_ins, outs, x_outs, scr, sems = parts
        step = pl.program_id(0)
        for axis in range(1, len(grid)):
            step = step * grid[axis] + pl.program_id(axis)
        start, forward, finish = exchange.phases(x_ins, x_outs, *sems)
        pl.when(step == 0)(start)
        body(*ins, *outs, *scr)

        @pl.when(step == steps - 1)
        def _():
            forward()
            finish()

    any_spec = pl.BlockSpec(memory_space=pl.ANY)
    res = pl.pallas_call(
        carried, name=name, grid=grid, in_specs=list(in_specs) + [any_spec] * nb,
        out_specs=list(out_specs) + [any_spec] * nb_out, out_shape=list(out_shape) + exchange.out_shape,
        scratch_shapes=list(scratch_shapes) + exchange.scratch, compiler_params=params)(*args, *exchange.blocks)
    return res[:n_out], res[n_out:]


def _exchange_call(exchange, name):
    nb = len(exchange.blocks)

    def body(*refs):
        start, forward, finish = exchange.phases(refs[:nb], refs[nb:2 * nb], *refs[2 * nb:])
        start()
        forward()
        finish()

    any_spec = pl.BlockSpec(memory_space=pl.ANY)
    return pl.pallas_call(body, name=name, in_specs=[any_spec] * nb, out_specs=[any_spec] * nb,
                          out_shape=exchange.out_shape, scratch_shapes=exchange.scratch)(*exchange.blocks)


def _ln_call(x, g, tm, exchange):
    s = x.shape[0]

    def body(x_ref, g_ref, h_ref):
        xv = x_ref[...]
        r = lax.rsqrt(jnp.mean(xv * xv, axis=-1, keepdims=True) + EPS)
        h_ref[...] = ((xv * r) * g_ref[...]).astype(h_ref.dtype)

    return _call(body, name="ln1", grid=(s // tm,), in_specs=[_rows(tm, D), _acc(D)], out_specs=[_rows(tm, D)],
                 out_shape=[_sds((s, D), _MXU)], args=(x, g), exchange=exchange)


PROJ_TILE = 256


def _proj_source_tile(j):
    gate_end, attn_end, end = 3072 // PROJ_TILE, 4352 // PROJ_TILE, 6400 // PROJ_TILE
    n_gates = end - attn_end
    return jnp.where(j < gate_end, j, jnp.where(j < gate_end + n_gates, j + (attn_end - gate_end), j - n_gates))


def _proj_call(a, wt, bias, exchange):
    s, k = a.shape
    n = wt.shape[0]
    rows = min(1024, s)

    def body(a_ref, w_ref, b_ref, o_ref):
        for r in range(0, s, rows):
            o_ref[r:r + rows, :] = _dot_nt(a_ref[r:r + rows, :], w_ref[...]) + b_ref[...]

    return _call(body, name="proj", grid=(n // PROJ_TILE,),
                 in_specs=[_const((s, k)), pl.BlockSpec((PROJ_TILE, k), lambda j: (_proj_source_tile(j), 0)),
                           pl.BlockSpec((1, PROJ_TILE), lambda j: (0, _proj_source_tile(j)))],
                 out_specs=[pl.BlockSpec((s, PROJ_TILE), lambda j: (0, j))], out_shape=[_sds((s, n), F32)],
                 args=(a, wt, bias), big=True, exchange=exchange)


def _mm_tn(a, b, name, tm, tn, tk, exchange=None):
    s, m = a.shape
    n = b.shape[1]
    last = s // tk - 1

    def body(a_ref, b_ref, o_ref, acc):
        k = pl.program_id(2)
        part = _dot_tn(a_ref[...], b_ref[...])

        @pl.when(k == 0)
        def _():
            acc[...] = part

        @pl.when(k > 0)
        def _():
            acc[...] += part

        @pl.when(k == last)
        def _():
            o_ref[...] = acc[...].astype(o_ref.dtype)

    res = _call(body, name=name, grid=(m // tm, n // tn, s // tk),
                in_specs=[pl.BlockSpec((tk, tm), lambda i, j, k: (k, i)), pl.BlockSpec((tk, tn), lambda i, j, k: (k, j))],
                out_specs=[pl.BlockSpec((tm, tn), lambda i, j, k: (i, j))], out_shape=[_sds((m, n), _MXU)],
                scratch_shapes=[pltpu.VMEM((tm, tn), F32)], args=(a, b), big=True, exchange=exchange)
    return res[0] if exchange is None else (res[0][0], res[1])


RET_CHUNKS = 2


def _ret_fwd_call(proj, cos, sin, decays):
    s = proj.shape[0]
    nblk = s // BLK
    per = min(RET_CHUNKS, nblk)
    rows = per * BLK
    intra, q_decay, k_decay, chunk_decay = decays

    def body(rq_ref, rk_ref, rv_ref, cos_ref, sin_ref, intra_ref, qd_ref, kd_ref,
             ry_ref, qr_ref, kr_ref, st_ref, state):
        @pl.when(pl.program_id(0) == 0)
        def _():
            state[...] = jnp.zeros_like(state)

        for c in range(per):
            rc = slice(c * BLK, (c + 1) * BLK)
            cos_v, sin_v = cos_ref[rc, :], sin_ref[rc, :]
            for h in range(RET_HEADS):
                hk = slice(h * RET_DK, (h + 1) * RET_DK)
                hv = slice(h * RET_DV, (h + 1) * RET_DV)
                q, k = rq_ref[rc, hk], rk_ref[rc, hk]
                qr = (q * cos_v + _swap_halves(q, RET_DK // 2) * sin_v) * RET_SCALE
                kr = k * cos_v + _swap_halves(k, RET_DK // 2) * sin_v
                v = rv_ref[rc, hv]
                s_h = state[h]
                st_ref[c, h] = s_h.astype(st_ref.dtype)
                scores = _dot_nt(qr, kr) * intra_ref[h]
                ry_ref[rc, hv] = _dot(scores, v) + _dot(qr, s_h) * qd_ref[h]
                state[h] = s_h * chunk_decay[h] + _dot_tn(kr * kd_ref[h], v)
                qr_ref[rc, hk] = qr.astype(qr_ref.dtype)
                kr_ref[rc, hk] = kr.astype(kr_ref.dtype)

    blk = lambda w, c: pl.BlockSpec((rows, w), lambda n: (n, c))
    return _call(body, name="ret_fwd", grid=(nblk // per,),
                 in_specs=[blk(512, 0), blk(512, 1), blk(1024, 1), blk(128, 0), blk(128, 0),
                           _const(intra.shape), _const(q_decay.shape), _const(k_decay.shape)],
                 out_specs=[blk(1024, 0), blk(512, 0), blk(512, 0),
                            pl.BlockSpec((per, RET_HEADS, RET_DK, RET_DV), lambda n: (n, 0, 0, 0))],
                 out_shape=[_sds((s, 1024), F32), _sds((s, 512), _MXU), _sds((s, 512), _MXU),
                            _sds((nblk, RET_HEADS, RET_DK, RET_DV), _MXU)],
                 scratch_shapes=[pltpu.VMEM((RET_HEADS, RET_DK, RET_DV), F32)],
                 args=(proj, proj, proj, cos, sin, intra, q_decay, k_decay))


def _ret_bwd_call(qr, kr, proj, states, dry, cos, sin, decays):
    s = qr.shape[0]
    nblk = s // BLK
    per = min(RET_CHUNKS, nblk)
    rows = per * BLK
    steps = nblk // per
    intra, q_decay, k_decay, chunk_decay = decays

    def body(qr_ref, kr_ref, rv_ref, st_ref, dry_ref, cos_ref, sin_ref, intra_ref, qd_ref, kd_ref,
             dp_ref, db_ref, dstate):
        @pl.when(pl.program_id(0) == 0)
        def _():
            dstate[...] = jnp.zeros_like(dstate)
            db_ref[...] = jnp.zeros_like(db_ref)

        for c in reversed(range(per)):
            rc = slice(c * BLK, (c + 1) * BLK)
            cos_v, sin_v = cos_ref[rc, :], sin_ref[rc, :]
            for h in range(RET_HEADS):
                hk = slice(h * RET_DK, (h + 1) * RET_DK)
                hv = slice(h * RET_DV, (h + 1) * RET_DV)
                q, k, v, d_out = qr_ref[rc, hk], kr_ref[rc, hk], rv_ref[rc, hv], dry_ref[rc, hv]
                d_next = dstate[h]
                scores = _dot_nt(q, k) * intra_ref[h]
                d_scores = _dot_nt(d_out, v) * intra_ref[h]
                d_cross = d_out * qd_ref[h]
                dq = _dot(d_scores, k) + _dot_nt(d_cross, st_ref[c, h])
                dk = _dot_tn(d_scores, q) + _dot_nt(v, d_next) * kd_ref[h]
                dv = _dot_tn(scores, d_out) + _dot(k.astype(F32) * kd_ref[h], d_next)
                dstate[h] = d_next * chunk_decay[h] + _dot_tn(q, d_cross)
                dq = (dq * cos_v - _swap_halves(dq, RET_DK // 2) * sin_v) * RET_SCALE
                dk = dk * cos_v - _swap_halves(dk, RET_DK // 2) * sin_v
                kcols = slice(512 + h * RET_DK, 512 + (h + 1) * RET_DK)
                vcols = slice(1024 + h * RET_DV, 1024 + (h + 1) * RET_DV)
                dp_ref[rc, hk] = dq.astype(dp_ref.dtype)
                dp_ref[rc, kcols] = dk.astype(dp_ref.dtype)
                dp_ref[rc, vcols] = dv.astype(dp_ref.dtype)
                db_ref[:, hk] += jnp.sum(dq, axis=0, keepdims=True)
                db_ref[:, kcols] += jnp.sum(dk, axis=0, keepdims=True)
                db_ref[:, vcols] += jnp.sum(dv, axis=0, keepdims=True)

    rblk = lambda w, c: pl.BlockSpec((rows, w), lambda n: (steps - 1 - n, c))
    return _call(body, name="ret_bwd", grid=(steps,),
                 in_specs=[rblk(512, 0), rblk(512, 0), rblk(1024, 1),
                           pl.BlockSpec((per, RET_HEADS, RET_DK, RET_DV), lambda n: (steps - 1 - n, 0, 0, 0)),
                           rblk(1024, 0), rblk(128, 0), rblk(128, 0),
                           _const(intra.shape), _const(q_decay.shape), _const(k_decay.shape)],
                 out_specs=[rblk(N_RET, 0), _acc(N_RET)],
                 out_shape=[_sds((s, N_RET), _MXU), _sds((1, N_RET), F32)],
                 scratch_shapes=[pltpu.VMEM((RET_HEADS, RET_DK, RET_DV), F32)],
                 args=(qr, kr, proj, states, dry, cos, sin, intra, q_decay, k_decay))


def _both_halves(x, g):
    lane = lax.broadcasted_iota(jnp.int32, x.shape, 1)
    keep = lane < HEAD_DIM if g == 0 else lane >= HEAD_DIM
    return jnp.where(keep, x, pltpu.roll(x, HEAD_DIM, 1))


def _stack_heads(ref, g):
    lane = lax.broadcasted_iota(jnp.int32, (BLK, 128), 1)
    pieces = []
    for j in range(g * 4, g * 4 + 4):
        chunk = ref[:, j * 128:(j + 1) * 128]
        pieces += [jnp.where(lane < HEAD_DIM, chunk, jnp.zeros_like(chunk)),
                   jnp.where(lane >= HEAD_DIM, chunk, jnp.zeros_like(chunk))]
    return jnp.concatenate(pieces, axis=0)


def _window_bias(first_block):
    kj = lax.broadcasted_iota(jnp.int32, (2 * BLK, BLK), 0)
    qi = lax.broadcasted_iota(jnp.int32, (2 * BLK, BLK), 1)
    first_key = jnp.where(first_block, BLK, 0)
    seen = (kj > qi) & (kj <= qi + BLK) & (kj >= first_key)
    return jnp.where(seen, 0.0, -1e30)


def _sink_softmax(scores, sink):
    m = jnp.maximum(jnp.max(scores, axis=0, keepdims=True), sink)
    e = jnp.exp(scores - m)
    e_sink = jnp.exp(sink - m)
    return e, e_sink, 1.0 / (jnp.sum(e, axis=0, keepdims=True) + e_sink)


def _head_pair(stacked_t, jj):
    even = stacked_t[0:HEAD_DIM, 2 * jj * BLK:(2 * jj + 1) * BLK]
    odd = stacked_t[HEAD_DIM:128, (2 * jj + 1) * BLK:(2 * jj + 2) * BLK]
    return jnp.concatenate([even, odd], axis=0).T


def _attn_fwd_call(proj, sinks, cos, sin):
    s = proj.shape[0]
    nblk = s // BLK

    def body(sink_ref, q_ref, k_ref, v_ref, cos_ref, sin_ref, ay_ref, qr_ref, kr_ref, vb_ref,
             kwin, vwin, bias, s_scr, p_scr):
        n = pl.program_id(0)

        @pl.when(n == 0)
        def _():
            kwin[...] = jnp.zeros_like(kwin)
            vwin[...] = jnp.zeros_like(vwin)

        @pl.when(n > 0)
        def _():
            kwin[0:BLK] = kwin[BLK:2 * BLK]
            vwin[0:BLK] = vwin[BLK:2 * BLK]

        cos_v, sin_v = cos_ref[...], sin_ref[...]
        k = k_ref[...]
        kr = (k * cos_v + _swap_halves(k, HEAD_DIM // 2) * sin_v).astype(kwin.dtype)
        kwin[BLK:2 * BLK] = kr
        vwin[BLK:2 * BLK] = v_ref[...].astype(vwin.dtype)
        kr_ref[...] = kr
        vb_ref[...] = vwin[BLK:2 * BLK]
        for j in range(Q_HEADS // 2):
            cols = slice(j * 128, (j + 1) * 128)
            q = q_ref[:, cols]
            qr_ref[:, cols] = ((q * cos_v + _swap_halves(q, HEAD_DIM // 2) * sin_v) * ATTN_SCALE).astype(qr_ref.dtype)
        bias[...] = _window_bias(n == 0)
        for g in range(KV_HEADS):
            kg = _both_halves(kwin[...], g)
            vg_t = _both_halves(vwin[...], g).astype(F32).T
            s_scr[...] = _dot_nt(kg, _stack_heads(qr_ref, g))
            for i in range(GROUP):
                cols = slice(i * BLK, (i + 1) * BLK)
                e, _, inv = _sink_softmax(s_scr[:, cols] + bias[...], sink_ref[0, g * GROUP + i])
                p_scr[:, cols] = (e * inv).astype(p_scr.dtype)
            out_t = _dot(vg_t, p_scr[...])
            for jj in range(4):
                j = g * 4 + jj
                ay_ref[:, j * 128:(j + 1) * 128] = _head_pair(out_t, jj).astype(ay_ref.dtype)

    blk = lambda w, c: pl.BlockSpec((BLK, w), lambda n: (n, c))
    off = (N_RET + N_GATE) // 128
    wide = (2 * BLK, GROUP * BLK)
    return _call(body, name="attn_fwd", grid=(nblk,),
                 in_specs=[pl.BlockSpec(memory_space=pltpu.SMEM), blk(1024, off // 8), blk(128, off + 8), blk(128, off + 9),
                           blk(128, 0), blk(128, 0)],
                 out_specs=[blk(1024, 0), blk(1024, 0), blk(128, 0), blk(128, 0)],
                 out_shape=[_sds((s, 1024), _MXU), _sds((s, 1024), _MXU), _sds((s, 128), _MXU), _sds((s, 128), _MXU)],
                 scratch_shapes=[pltpu.VMEM((2 * BLK, 128), _MXU), pltpu.VMEM((2 * BLK, 128), _MXU),
                                 pltpu.VMEM((2 * BLK, BLK), F32), pltpu.VMEM(wide, F32), pltpu.VMEM(wide, _MXU)],
                 args=(sinks, proj, proj, proj, cos, sin))


def _attn_bwd_call(qr, kr, vb, day, sinks, cos, sin):
    s = qr.shape[0]
    nblk = s // BLK

    def body(sink_ref, q_ref, kc_ref, kp_ref, vc_ref, vp_ref, do_ref, cos_ref, sin_ref, cosp_ref, sinp_ref,
             dp_ref, dsink_ref, db_ref, bias, s_scr, dp_scr, p_scr, ds_scr, dq_held, kv_held, kv_prev, kv_new):
        n = pl.program_id(0)
        valid = (n < nblk).astype(F32)

        @pl.when(n == 0)
        def _():
            dsink_ref[...] = jnp.zeros_like(dsink_ref)
            db_ref[...] = jnp.zeros_like(db_ref)

        @pl.when(n >= 1)
        def _():
            dp_ref[:, 0:1024] = dq_held[...]

        cos_v, sin_v = cos_ref[...], sin_ref[...]
        bias[...] = _window_bias(n == 0)
        lane1 = lax.broadcasted_iota(jnp.int32, (1, 128), 1)
        kwin = jnp.concatenate([kp_ref[...], kc_ref[...]], axis=0)
        vwin = jnp.concatenate([vp_ref[...], vc_ref[...]], axis=0)
        dk_heads, dv_heads = [], []
        dsink = jnp.zeros((1, 128), F32)
        for g in range(KV_HEADS):
            kg = _both_halves(kwin, g)
            vg = _both_halves(vwin, g)
            q_all = _stack_heads(q_ref, g)
            do_all = _stack_heads(do_ref, g)
            s_scr[...] = _dot_nt(kg, q_all)
            dp_scr[...] = _dot_nt(vg, do_all)
            for i in range(GROUP):
                head = g * GROUP + i
                cols = slice(i * BLK, (i + 1) * BLK)
                e, e_sink, inv = _sink_softmax(s_scr[:, cols] + bias[...], sink_ref[0, head])
                p = e * inv
                dp = dp_scr[:, cols]
                delta = jnp.sum(p * dp, axis=0, keepdims=True)
                p_scr[:, cols] = p.astype(p_scr.dtype)
                ds_scr[:, cols] = (p * (dp - delta)).astype(ds_scr.dtype)
                dsink = dsink + jnp.where(lane1 == head, -jnp.sum(e_sink * inv * delta, axis=1, keepdims=True), 0.0)
            dv_both = _dot(p_scr[...], do_all)
            dk_both = _dot(ds_scr[...], q_all)
            dv_heads.append(dv_both + pltpu.roll(dv_both, HEAD_DIM, 1))
            dk_heads.append(dk_both + pltpu.roll(dk_both, HEAD_DIM, 1))
            dq_t = _dot(kg.astype(F32).T, ds_scr[...])
            for jj in range(4):
                cols = slice((g * 4 + jj) * 128, (g * 4 + jj + 1) * 128)
                dq = _head_pair(dq_t, jj)
                dq = (dq * cos_v - _swap_halves(dq, HEAD_DIM // 2) * sin_v) * ATTN_SCALE
                dq_held[:, cols] = dq.astype(dq_held.dtype)
                db_ref[:, cols] += jnp.sum(dq, axis=0, keepdims=True) * valid
        dsink_ref[...] += dsink * valid
        lane2 = lax.broadcasted_iota(jnp.int32, (2 * BLK, 128), 1)
        dk_all = jnp.where(lane2 < HEAD_DIM, dk_heads[0], dk_heads[1])
        dv_all = jnp.where(lane2 < HEAD_DIM, dv_heads[0], dv_heads[1])
        kv_prev[:, 0:128] = dk_all[0:BLK] * valid
        kv_prev[:, 128:256] = dv_all[0:BLK] * valid
        kv_new[:, 0:128] = dk_all[BLK:2 * BLK]
        kv_new[:, 128:256] = dv_all[BLK:2 * BLK]

        @pl.when(n >= 1)
        def _():
            dkv = kv_held[...] + kv_prev[...]
            dk = dkv[:, 0:128]
            dk = dk * cosp_ref[...] - _swap_halves(dk, HEAD_DIM // 2) * sinp_ref[...]
            dv = dkv[:, 128:256]
            dp_ref[:, 1024:1152] = dk.astype(dp_ref.dtype)
            dp_ref[:, 1152:1280] = dv.astype(dp_ref.dtype)
            db_ref[:, 1024:1152] += jnp.sum(dk, axis=0, keepdims=True)
            db_ref[:, 1152:1280] += jnp.sum(dv, axis=0, keepdims=True)

        kv_held[...] = kv_new[...]

    blk = lambda w: pl.BlockSpec((BLK, w), lambda n: (jnp.minimum(n, nblk - 1), 0))
    pblk = lambda w: pl.BlockSpec((BLK, w), lambda n: (jnp.maximum(n - 1, 0), 0))
    wide = (2 * BLK, GROUP * BLK)
    return _call(body, name="attn_bwd", grid=(nblk + 1,),
                 in_specs=[pl.BlockSpec(memory_space=pltpu.SMEM), blk(1024), blk(128), pblk(128), blk(128), pblk(128),
                           blk(1024), blk(128), blk(128), pblk(128), pblk(128)],
                 out_specs=[pblk(N_ATTN), _acc(128), _acc(N_ATTN)],
                 out_shape=[_sds((s, N_ATTN), _MXU), _sds((1, 128), F32), _sds((1, N_ATTN), F32)],
                 scratch_shapes=[pltpu.VMEM((2 * BLK, BLK), F32), pltpu.VMEM(wide, F32), pltpu.VMEM(wide, F32),
                                 pltpu.VMEM(wide, _MXU), pltpu.VMEM(wide, _MXU), pltpu.VMEM((BLK, 1024), _MXU),
                                 pltpu.VMEM((BLK, 256), F32), pltpu.VMEM((BLK, 256), F32), pltpu.VMEM((BLK, 256), F32)],
                 args=(sinks, qr, kr, kr, vb, vb, day, cos, sin, cos, sin))


def _group_norm(y):
    mu = jnp.mean(y, axis=-1, keepdims=True)
    yc = y - mu
    rs = lax.rsqrt(jnp.mean(yc * yc, axis=-1, keepdims=True) + EPS)
    return yc * rs, rs


GATE_COL = N_RET // 1024


def _merge_fwd_call(x, ry, proj, ay, gn_g, w_ro, w_ao, w_o, tm):
    s = x.shape[0]

    def body(x_ref, ry_ref, rg_ref, ga_ref, gb_ref, ay_ref, gn_ref, wro_ref, wao_ref, wo_ref,
             ain_ref, a_ref, b_ref, mg_ref, x1_ref):
        for h in range(RET_HEADS):
            hv = slice(h * RET_DV, (h + 1) * RET_DV)
            yhat, _ = _group_norm(ry_ref[:, hv])
            rg = rg_ref[:, hv]
            ain_ref[:, hv] = ((rg * _sigmoid(rg)) * (yhat * gn_ref[:, hv])).astype(ain_ref.dtype)
        a = _dot(ain_ref[...], wro_ref[...])
        b = _dot(ay_ref[...], wao_ref[...])
        a_ref[...] = a.astype(a_ref.dtype)
        b_ref[...] = b.astype(b_ref.dtype)
        merged = (_sigmoid(ga_ref[...]) * a + _sigmoid(gb_ref[...]) * b).astype(mg_ref.dtype)
        mg_ref[...] = merged
        x1_ref[...] = x_ref[...] + _dot(merged, wo_ref[...])

    return _call(body, name="merge_fwd", grid=(s // tm,),
                 in_specs=[_rows(tm, D), _rows(tm, 1024), _rows(tm, 1024, GATE_COL), _rows(tm, 1024, GATE_COL + 1),
                           _rows(tm, 1024, GATE_COL + 2), _rows(tm, 1024), _acc(1024),
                           _const((D, D)), _const((D, D)), _const((D, D))],
                 out_specs=[_rows(tm, D)] * 5,
                 out_shape=[_sds((s, D), _MXU), _sds((s, D), _STORE), _sds((s, D), _STORE), _sds((s, D), _MXU),
                            _sds((s, D), F32)],
                 args=(x, ry, proj, proj, proj, ay, gn_g, w_ro, w_ao, w_o), big=True)


def _merge_bwd_call(dx1, a, b, ry, proj, gn_g, w_ro, w_ao, w_o, tm, exchange):
    s = dx1.shape[0]

    def body(dx1_ref, a_ref, b_ref, ry_ref, rg_ref, ga_ref, gb_ref, gn_ref, wro_ref, wao_ref, wo_ref,
             da_ref, dbr_ref, dp_ref, day_ref, dry_ref, dbias_ref, dgn_ref):
        @pl.when(pl.program_id(0) == 0)
        def _():
            dbias_ref[...] = jnp.zeros_like(dbias_ref)
            dgn_ref[...] = jnp.zeros_like(dgn_ref)

        d_merged = _dot_nt(dx1_ref[...], wo_ref[...])
        sa, sb = _sigmoid(ga_ref[...]), _sigmoid(gb_ref[...])
        d_a = d_merged * sa
        d_b = d_merged * sb
        da_ref[...] = d_a.astype(da_ref.dtype)
        dbr_ref[...] = d_b.astype(dbr_ref.dtype)
        d_ga = d_merged * a_ref[...] * (sa * (1.0 - sa))
        d_gb = d_merged * b_ref[...] * (sb * (1.0 - sb))
        dp_ref[:, 1024:2048] = d_ga.astype(dp_ref.dtype)
        dp_ref[:, 2048:3072] = d_gb.astype(dp_ref.dtype)
        dbias_ref[:, 1024:2048] += jnp.sum(d_ga, axis=0, keepdims=True)
        dbias_ref[:, 2048:3072] += jnp.sum(d_gb, axis=0, keepdims=True)
        day_ref[...] = _dot_nt(d_b, wao_ref[...]).astype(day_ref.dtype)
        d_ain = _dot_nt(d_a, wro_ref[...])
        for h in range(RET_HEADS):
            hv = slice(h * RET_DV, (h + 1) * RET_DV)
            yhat, rs = _group_norm(ry_ref[:, hv])
            rg = rg_ref[:, hv]
            sg = _sigmoid(rg)
            gn = gn_ref[:, hv]
            d_h = d_ain[:, hv]
            d_rg = d_h * (yhat * gn) * (sg * (1.0 + rg * (1.0 - sg)))
            d_ryn = d_h * (rg * sg)
            dgn_ref[:, hv] += jnp.sum(d_ryn * yhat, axis=0, keepdims=True)
            d_yhat = d_ryn * gn
            dry_ref[:, hv] = (rs * (d_yhat - jnp.mean(d_yhat, axis=-1, keepdims=True)
                                    - yhat * jnp.mean(d_yhat * yhat, axis=-1, keepdims=True))).astype(dry_ref.dtype)
            dp_ref[:, hv] = d_rg.astype(dp_ref.dtype)
            dbias_ref[:, hv] += jnp.sum(d_rg, axis=0, keepdims=True)

    return _call(body, name="merge_bwd", grid=(s // tm,),
                 in_specs=[_rows(tm, D), _rows(tm, D), _rows(tm, D), _rows(tm, 1024), _rows(tm, 1024, GATE_COL),
                           _rows(tm, 1024, GATE_COL + 1), _rows(tm, 1024, GATE_COL + 2), _acc(1024),
                           _const((D, D)), _const((D, D)), _const((D, D))],
                 out_specs=[_rows(tm, D), _rows(tm, D), _rows(tm, N_GATE), _rows(tm, D), _rows(tm, D), _acc(N_GATE),
                            _acc(1024)],
                 out_shape=[_sds((s, D), _MXU), _sds((s, D), _MXU), _sds((s, N_GATE), _MXU), _sds((s, D), _MXU),
                            _sds((s, D), _STORE), _sds((1, N_GATE), F32), _sds((1, 1024), F32)],
                 args=(dx1, a, b, ry, proj, proj, proj, gn_g, w_ro, w_ao, w_o), big=True, exchange=exchange)


def _ffn_fwd_call(x1, target, ln2_g, lnf_g, w_g, w_u, w_d, tm):
    s = x1.shape[0]

    def body(x1_ref, t_ref, g2_ref, gf_ref, wg_ref, wu_ref, wd_ref,
             h2_ref, g_ref, u_ref, f_ref, dx2_ref, loss_ref, dgf_ref):
        @pl.when(pl.program_id(0) == 0)
        def _():
            loss_ref[...] = jnp.zeros_like(loss_ref)
            dgf_ref[...] = jnp.zeros_like(dgf_ref)

        x1v = x1_ref[...]
        r1 = lax.rsqrt(jnp.mean(x1v * x1v, axis=-1, keepdims=True) + EPS)
        h2 = ((x1v * r1) * g2_ref[...]).astype(h2_ref.dtype)
        h2_ref[...] = h2
        g = _dot_nt(h2, wg_ref[...])
        u = _dot_nt(h2, wu_ref[...])
        g_ref[...] = g
        u_ref[...] = u
        f = ((g * _sigmoid(g)) * u).astype(f_ref.dtype)
        f_ref[...] = f
        x2 = x1v + _dot(f, wd_ref[...])
        r2 = lax.rsqrt(jnp.mean(x2 * x2, axis=-1, keepdims=True) + EPS)
        xhat = x2 * r2
        err = xhat * gf_ref[...] - t_ref[...]
        loss_ref[...] += 0.5 * jnp.sum(jnp.mean(err * err, axis=-1, keepdims=True))
        dy = err * (1.0 / D)
        dgf_ref[...] += jnp.sum(dy * xhat, axis=0, keepdims=True)
        dxh = dy * gf_ref[...]
        dx2_ref[...] = r2 * (dxh - xhat * jnp.mean(dxh * xhat, axis=-1, keepdims=True))

    return _call(body, name="ffn_fwd", grid=(s // tm,),
                 in_specs=[_rows(tm, D), _rows(tm, D), _acc(D), _acc(D), _const((D_FF, D)), _const((D_FF, D)),
                           _const((D_FF, D))],
                 out_specs=[_rows(tm, D), _rows(tm, D_FF), _rows(tm, D_FF), _rows(tm, D_FF), _rows(tm, D), _acc(128),
                            _acc(D)],
                 out_shape=[_sds((s, D), _MXU), _sds((s, D_FF), F32), _sds((s, D_FF), F32), _sds((s, D_FF), _MXU),
                            _sds((s, D), F32), _sds((1, 128), F32), _sds((1, D), F32)],
                 args=(x1, target, ln2_g, lnf_g, w_g, w_u, w_d), big=True)


def _ffn_bwd_call(dx2, x1, g, u, ln2_g, w_g, w_u, w_d, tm):
    s = dx2.shape[0]

    def body(dx2_ref, x1_ref, g_ref, u_ref, g2_ref, wg_ref, wu_ref, wd_ref, dx1_ref, dg_ref, du_ref, dg2_ref):
        @pl.when(pl.program_id(0) == 0)
        def _():
            dg2_ref[...] = jnp.zeros_like(dg2_ref)

        dx2v = dx2_ref[...]
        df = _dot_nt(dx2v, wd_ref[...])
        gv, uv = g_ref[...], u_ref[...]
        sg = _sigmoid(gv)
        du = (df * (gv * sg)).astype(du_ref.dtype)
        dg = (df * uv * (sg * (1.0 + gv * (1.0 - sg)))).astype(dg_ref.dtype)
        du_ref[...] = du
        dg_ref[...] = dg
        dh2 = _dot(dg, wg_ref[...]) + _dot(du, wu_ref[...])
        x1v = x1_ref[...]
        r1 = lax.rsqrt(jnp.mean(x1v * x1v, axis=-1, keepdims=True) + EPS)
        xhat = x1v * r1
        dg2_ref[...] += jnp.sum(dh2 * xhat, axis=0, keepdims=True)
        dxh = dh2 * g2_ref[...]
        dx1_ref[...] = dx2v + r1 * (dxh - xhat * jnp.mean(dxh * xhat, axis=-1, keepdims=True))

    return _call(body, name="ffn_bwd", grid=(s // tm,),
                 in_specs=[_rows(tm, D), _rows(tm, D), _rows(tm, D_FF), _rows(tm, D_FF), _acc(D),
                           _const((D_FF, D)), _const((D_FF, D)), _const((D_FF, D))],
                 out_specs=[_rows(tm, D), _rows(tm, D_FF), _rows(tm, D_FF), _acc(D)],
                 out_shape=[_sds((s, D), F32), _sds((s, D_FF), _MXU), _sds((s, D_FF), _MXU), _sds((1, D), F32)],
                 args=(dx2, x1, g, u, ln2_g, w_g, w_u, w_d), big=True)


def _dx_call(x, dx1, dp_ret, dp_gate, dp_attn, ln1_g, w_in, tm, exchange):
    s = x.shape[0]

    def body(x_ref, dx1_ref, dr_ref, dg_ref, da_ref, g1_ref, w_ref, dx_ref, dg1_ref):
        @pl.when(pl.program_id(0) == 0)
        def _():
            dg1_ref[...] = jnp.zeros_like(dg1_ref)

        dh = (_dot(dr_ref[...], w_ref[0:2048, :]) + _dot(dg_ref[:, 0:1024], w_ref[2048:3072, :])
              + _dot(da_ref[...], w_ref[3072:4352, :]) + _dot(dg_ref[:, 1024:3072], w_ref[4352:6400, :]))
        xv = x_ref[...]
        r = lax.rsqrt(jnp.mean(xv * xv, axis=-1, keepdims=True) + EPS)
        xhat = xv * r
        dg1_ref[...] += jnp.sum(dh * xhat, axis=0, keepdims=True)
        dxh = dh * g1_ref[...]
        dx_ref[...] = dx1_ref[...] + r * (dxh - xhat * jnp.mean(dxh * xhat, axis=-1, keepdims=True))

    return _call(body, name="dx", grid=(s // tm,),
                 in_specs=[_rows(tm, D), _rows(tm, D), _rows(tm, N_RET), _rows(tm, N_GATE), _rows(tm, N_ATTN), _acc(D),
                           _const((D_IN, D))],
                 out_specs=[_rows(tm, D), _acc(D)],
                 out_shape=[_sds((s, D), F32), _sds((1, D), F32)],
                 args=(x, dx1, dp_ret, dp_gate, dp_attn, ln1_g, w_in), big=True, exchange=exchange)


def _adamw(g, w, m, v):
    m_new = B1 * m + (1.0 - B1) * g
    v_new = B2 * v + (1.0 - B2) * (g * g)
    m_hat = m_new / (1.0 - B1 ** STEP)
    v_hat = v_new / (1.0 - B2 ** STEP)
    return -LR * (m_hat / (jnp.sqrt(v_hat) + ADAM_EPS) + WD * w), m_new, v_new


def _slot_sum(p_ref):
    g = p_ref[0].astype(F32)
    for k in range(1, N_DEV):
        g = g + p_ref[k].astype(F32)
    return g


def _adamw_call(parts, w, m, v, name, tr):
    rows, cols = w.shape

    def body(p_ref, w_ref, m_ref, v_ref, g_ref, dw_ref, nm_ref, nv_ref):
        g = _slot_sum(p_ref)
        g_ref[...] = g
        dw_ref[...], nm_ref[...], nv_ref[...] = _adamw(g, w_ref[...], m_ref[...], v_ref[...])

    p_spec = pl.BlockSpec((N_DEV, tr, cols), lambda i: (0, i, 0))
    spec = pl.BlockSpec((tr, cols), lambda i: (i, 0))
    return _call(body, name=name, grid=(rows // tr,), in_specs=[p_spec, spec, spec, spec], out_specs=[spec] * 4,
                 out_shape=[_sds((rows, cols), F32)] * 4, args=(parts, w, m, v))


SMALL_WIDTHS = [1024, 6400, 1024, 16, 1024, 1024]
SMALL_OFFSETS = [0, 1024, 7424, 8448, 8576, 9600]
LOSS_OFFSET = 10624
SMALL_LEN = 10752


def _pack_small(grads, loss):
    pieces = []
    for gr, width in zip(grads, SMALL_WIDTHS):
        pieces.append(jnp.pad(gr.reshape(1, width), ((0, 0), (0, -width % 128))))
    pieces.append(jnp.pad(loss.reshape(1, 1), ((0, 0), (0, 127))))
    return jnp.concatenate(pieces, axis=1)


def _adamw_small_call(parts, ws, ms, vs):
    n = len(ws)

    def body(*refs):
        p_ref, w_refs, m_refs, v_refs = refs[0], refs[1:1 + n], refs[1 + n:1 + 2 * n], refs[1 + 2 * n:1 + 3 * n]
        outs = refs[1 + 3 * n:]
        g_all = _slot_sum(p_ref)
        for i, (off, width) in enumerate(zip(SMALL_OFFSETS, SMALL_WIDTHS)):
            g = g_all[:, off:off + width]
            outs[i][...] = g
            outs[n + i][...], outs[2 * n + i][...], outs[3 * n + i][...] = _adamw(
                g, w_refs[i][...], m_refs[i][...], v_refs[i][...])
        outs[4 * n][...] = g_all[:, LOSS_OFFSET:LOSS_OFFSET + 128]

    whole = lambda shape: pl.BlockSpec(shape, lambda i: (0,) * len(shape))
    small = [whole((1, w)) for w in SMALL_WIDTHS]
    res = _call(body, name="adamw_small", grid=(1,), in_specs=[whole((N_DEV, 1, SMALL_LEN))] + small * 3,
                out_specs=small * 4 + [whole((1, 128))],
                out_shape=[_sds((1, w), F32) for w in SMALL_WIDTHS] * 4 + [_sds((1, 128), F32)],
                args=(parts, *ws, *ms, *vs))
    return [res[k * n:(k + 1) * n] for k in range(4)], res[4 * n]


def kernel(x, ln1_g, w_in, b_in, ret_norm_g, w_ret_out, attn_sinks, w_attn_out, w_out, ln2_g, w_ffn_gate, w_ffn_up, w_ffn_down, lnf_g, loss_target, m_ln1_g, m_w_in, m_b_in, m_ret_norm_g, m_w_ret_out, m_attn_sinks, m_w_attn_out, m_w_out, m_ln2_g, m_w_ffn_gate, m_w_ffn_up, m_w_ffn_down, m_lnf_g, v_ln1_g, v_w_in, v_b_in, v_ret_norm_g, v_w_ret_out, v_attn_sinks, v_w_attn_out, v_w_out, v_ln2_g, v_w_ffn_gate, v_w_ffn_up, v_w_ffn_down, v_lnf_g):
    cast = lambda a: a.astype(_MXU)
    xs, target = x[0], loss_target[0]
    s = xs.shape[0]
    r_sq = w_ret_out.shape[1]
    r_dn = w_ffn_down.shape[1]
    c_in = w_in.shape[2]
    c_ff = w_ffn_gate.shape[2]
    tm, tk = min(256, s), min(2048, s)
    lnf_row = lnf_g.reshape(1, D)
    cos_r, sin_r = _rope_tables(s, RET_DK // 2)
    cos_a, sin_a = _rope_tables(s, HEAD_DIM // 2)
    decays = _retention_decays()
    tr_shard = lambda a: a[0].T
    per_dev = lambda a, n: a.reshape(N_DEV, n, D)

    (h,), (all_in,) = _ln_call(xs, ln1_g, tm, _AllGather([cast(tr_shard(w_in))]))
    wt_in = all_in.reshape(N_DEV * c_in, D)
    rest = [tr_shard(w_ffn_gate), tr_shard(w_ffn_up), w_ret_out[0], w_attn_out[0], w_out[0], w_ffn_down[0]]
    (proj,), gathered = _proj_call(h, wt_in, b_in, _AllGather([cast(a) for a in rest]))
    wt_g, wt_u, full_ro, full_ao, full_o, full_d = (a.reshape(N_DEV * a.shape[1], D) for a in gathered)
    ry, qr, kr, states = _ret_fwd_call(proj, cos_r, sin_r, decays)
    ay, aqr, akr, avb = _attn_fwd_call(proj, attn_sinks, cos_a, sin_a)
    a_in, br_a, br_b, merged, x1 = _merge_fwd_call(xs, ry, proj, ay, ret_norm_g, full_ro, full_ao, full_o, tm)
    h2, g, u, f, dx2, loss, d_lnf = _ffn_fwd_call(x1, target, ln2_g, lnf_row, wt_g, wt_u, full_d, tm)

    dx1, dg, du, d_ln2 = _ffn_bwd_call(dx2, x1, g, u, ln2_g, wt_g, wt_u, full_d, tm)
    dw_d = _mm_tn(f, dx2, "dw_ffn_down", 1408, 1024, tk)
    dwt_g, (got_d,) = _mm_tn(dg, h2, "dw_ffn_gate", 1408, 1024, tk, _AllToAll([per_dev(dw_d, r_dn)]))
    dwt_u = _mm_tn(du, h2, "dw_ffn_up", 1408, 1024, tk)
    (d_a, d_b, dp_gate, day, dry, db_gate, d_gn), (got_g, got_u) = _merge_bwd_call(
        dx1, br_a, br_b, ry, proj, ret_norm_g, full_ro, full_ao, full_o, tm,
        _AllToAll([per_dev(dwt_g, c_ff), per_dev(dwt_u, c_ff)]))
    dw_o = _mm_tn(merged, dx1, "dw_out", 1024, 1024, tk)
    dw_ro = _mm_tn(a_in, d_a, "dw_ret_out", 1024, 1024, tk)
    dw_ao = _mm_tn(ay, d_b, "dw_attn_out", 1024, 1024, tk)
    dp_attn, d_sinks, db_attn = _attn_bwd_call(aqr, akr, avb, day, attn_sinks, cos_a, sin_a)
    dp_ret, db_ret = _ret_bwd_call(qr, kr, proj, states, dry, cos_r, sin_r, decays)
    dwt_ret = _mm_tn(dp_ret, h, "dw_in_ret", 1024, 1024, tk)
    dwt_gate, (got_ro, got_ao, got_o) = _mm_tn(
        dp_gate, h, "dw_in_gate", 1024, 1024, tk,
        _AllToAll([per_dev(dw_ro, r_sq), per_dev(dw_ao, r_sq), per_dev(dw_o, r_sq)]))
    dwt_attn = _mm_tn(dp_attn, h, "dw_in_attn", 1280, 1024, tk)
    in_pieces = [(0, 0, 2048), (1, 0, 1024), (2, 0, 1280), (1, 1024, 3072)]
    (dx, d_ln1), (got_in,) = _dx_call(xs, dx1, dp_ret, dp_gate, dp_attn, ln1_g, wt_in, tm,
                                      _RowScatter([dwt_ret, dwt_gate, dwt_attn], in_pieces, c_in))
    db_in = jnp.concatenate([db_ret, db_gate[:, 0:1024], db_attn, db_gate[:, 1024:3072]], axis=1)
    small = [d_ln1, db_in, d_gn, d_sinks[:, 0:Q_HEADS], d_ln2, d_lnf]
    (got_small,) = _exchange_call(_AllGather([_pack_small(small, loss[0, 0])]), "gather_small")

    transposed = ("w_in", "w_ffn_gate", "w_ffn_up")
    res = {}
    res["w_in"] = _adamw_call(got_in, tr_shard(w_in), tr_shard(m_w_in), tr_shard(v_w_in), "adamw_w_in", 160)
    res["w_ffn_gate"] = _adamw_call(got_g, tr_shard(w_ffn_gate), tr_shard(m_w_ffn_gate), tr_shard(v_w_ffn_gate),
                                    "adamw_ffn_gate", 176)
    res["w_ffn_up"] = _adamw_call(got_u, tr_shard(w_ffn_up), tr_shard(m_w_ffn_up), tr_shard(v_w_ffn_up),
                                  "adamw_ffn_up", 176)
    res["w_ret_out"] = _adamw_call(got_ro, w_ret_out[0], m_w_ret_out[0], v_w_ret_out[0], "adamw_ret_out", r_sq)
    res["w_attn_out"] = _adamw_call(got_ao, w_attn_out[0], m_w_attn_out[0], v_w_attn_out[0], "adamw_attn_out", r_sq)
    res["w_out"] = _adamw_call(got_o, w_out[0], m_w_out[0], v_w_out[0], "adamw_out", r_sq)
    res["w_ffn_down"] = _adamw_call(got_d, w_ffn_down[0], m_w_ffn_down[0], v_w_ffn_down[0], "adamw_ffn_down", 176)
    small_names = ["ln1_g", "b_in", "ret_norm_g", "attn_sinks", "ln2_g", "lnf_g"]
    small_res, loss_row = _adamw_small_call(
        got_small, [ln1_g, b_in, ret_norm_g, attn_sinks, ln2_g, lnf_row],
        [m_ln1_g, m_b_in, m_ret_norm_g, m_attn_sinks, m_ln2_g, m_lnf_g.reshape(1, D)],
        [v_ln1_g, v_b_in, v_ret_norm_g, v_attn_sinks, v_ln2_g, v_lnf_g.reshape(1, D)])
    for i, nm in enumerate(small_names):
        res[nm] = [small_res[kind][i] for kind in range(4)]

    order = ["ln1_g", "w_in", "b_in", "ret_norm_g", "w_ret_out", "attn_sinks", "w_attn_out", "w_out", "ln2_g",
             "w_ffn_gate", "w_ffn_up", "w_ffn_down", "lnf_g"]
    outs = [loss_row[0, 0], dx[None]]
    for kind in range(4):
        for nm in order:
            val = res[nm][kind]
            if nm in transposed:
                val = val.T
            outs.append(val[None] if nm.startswith("w_") else val.reshape(D) if nm == "lnf_g" else val)
    return tuple(outs)
```

```python
import math

import numpy as np
import jax
import jax.numpy as jnp
from jax import lax
from jax.experimental import pallas as pl
from jax.experimental.pallas import tpu as pltpu

F32 = jnp.float32
_MXU = jnp.bfloat16
_STORE = jnp.bfloat16

N_DEV = 8
D = 1024
RET_HEADS, RET_DK, RET_DV = 4, 128, 256
BLK = 128
Q_HEADS, KV_HEADS, HEAD_DIM = 16, 2, 64
GROUP = Q_HEADS // KV_HEADS
D_FF = 2816
N_RET, N_GATE, N_ATTN = 2048, 3072, 1280
D_IN = N_RET + N_GATE + N_ATTN
ROPE_THETA = 10000.0
EPS = 1e-6
RET_SCALE = RET_DK ** -0.5
ATTN_SCALE = HEAD_DIM ** -0.5
LR, B1, B2, ADAM_EPS, WD, STEP = 0.001, 0.9, 0.999, 1e-08, 0.01, 10
VMEM_LIMIT_MB = 56
MESH = pl.DeviceIdType.MESH


def _dot(a, b):
    return jnp.dot(a.astype(_MXU), b.astype(_MXU), preferred_element_type=F32)


def _dot_nt(a, b):
    return lax.dot_general(a.astype(_MXU), b.astype(_MXU), (((1,), (1,)), ((), ())), preferred_element_type=F32)


def _dot_tn(a, b):
    return lax.dot_general(a.astype(_MXU), b.astype(_MXU), (((0,), (0,)), ((), ())), preferred_element_type=F32)


def _sigmoid(x):
    return 1.0 / (1.0 + jnp.exp(-x))


def _cparams(n_axes, big=False):
    kw = dict(dimension_semantics=("arbitrary",) * n_axes)
    if big:
        kw["vmem_limit_bytes"] = VMEM_LIMIT_MB * 2**20
    return pltpu.CompilerParams(**kw)


def _rows(tm, width, col=0):
    return pl.BlockSpec((tm, width), lambda i: (i, col))


def _const(shape):
    nd = len(shape)
    return pl.BlockSpec(shape, lambda *_: (0,) * nd, pipeline_mode=pl.Buffered(1))


def _acc(width):
    return pl.BlockSpec((1, width), lambda *_: (0, 0))


def _sds(shape, dtype):
    return jax.ShapeDtypeStruct(shape, dtype)


def _swap_halves(x, half):
    w = x.shape[-1]
    if 2 * half == w:
        return pltpu.roll(x, half, 1)
    lane = lax.broadcasted_iota(jnp.int32, x.shape, 1)
    return jnp.where(lane % (2 * half) < half, pltpu.roll(x, w - half, 1), pltpu.roll(x, half, 1))


def _rope_tables(seq):
    lane = jnp.arange(128, dtype=jnp.int32)
    ret_freq = ROPE_THETA ** (-(lane % 64).astype(F32) / 64)
    attn_freq = ROPE_THETA ** (-(lane % 32).astype(F32) / 32)
    ang = jnp.arange(seq, dtype=jnp.int32).astype(F32)[:, None] * jnp.where(lane < 64, ret_freq, attn_freq)[None, :]
    return jnp.cos(ang), jnp.sin(ang)


def _ret_rope(cos, sin):
    low = lax.broadcasted_iota(jnp.int32, cos.shape, 1) < RET_DK // 2
    return jnp.where(low, cos, pltpu.roll(cos, RET_DK // 2, 1)), jnp.where(low, -sin, pltpu.roll(sin, RET_DK // 2, 1))


def _attn_rope(cos, sin):
    lane = lax.broadcasted_iota(jnp.int32, cos.shape, 1)
    half = HEAD_DIM // 2

    def spread(t):
        t = pltpu.roll(t, 64, 1)
        t = jnp.where(lane < half, t, pltpu.roll(t, half, 1))
        return jnp.where(lane < HEAD_DIM, t, pltpu.roll(t, HEAD_DIM, 1))

    return spread(cos), jnp.where(lane % HEAD_DIM < half, -spread(sin), spread(sin))


def _retention_decays():
    log_gamma = np.log1p(-np.exp2(-5.0 - np.arange(RET_HEADS, dtype=np.float32))).astype(np.float32)
    idx = np.arange(BLK, dtype=np.float32)
    rel = idx[:, None] - idx[None, :]
    intra = np.where(rel[None] >= 0, np.exp(log_gamma[:, None, None] * np.maximum(rel, 0.0)[None]), 0.0)
    q_decay = np.exp(log_gamma[:, None] * (idx + 1.0))[:, :, None]
    k_decay = np.exp(log_gamma[:, None] * (BLK - 1.0 - idx))[:, :, None]
    chunk_decay = [float(np.exp(np.float32(lg * BLK))) for lg in log_gamma]
    return (jnp.asarray(intra, F32), jnp.asarray(q_decay, F32), jnp.asarray(k_decay, F32), chunk_decay)


def _position():
    return lax.axis_index("x"), lax.axis_index("y"), lax.axis_index("c")


def _slot(px, py, pc):
    return 4 * px + 2 * py + pc


class _AllGather:
    def __init__(self, blocks):
        self.blocks = list(blocks)
        nb = len(self.blocks)
        self.out_shape = [_sds((N_DEV,) + b.shape, b.dtype) for b in self.blocks]
        self.scratch = [pltpu.SemaphoreType.DMA((nb, 7)), pltpu.SemaphoreType.DMA((nb, 7)),
                        pltpu.SemaphoreType.DMA((nb,))]

    def phases(self, ins, outs, send_sems, recv_sems, local_sems):
        nb = len(ins)
        x, y, c = _position()
        me, sibling = (x, y, c), (x, y, 1 - c)
        chips = [(1 - x, y), (x, 1 - y), (1 - x, 1 - y)]

        def copy(b, k, block, to, src=None):
            dst = outs[b].at[_slot(*block)]
            return pltpu.make_async_remote_copy(
                src_ref=dst if src is None else src, dst_ref=dst, send_sem=send_sems.at[b, k],
                recv_sem=recv_sems.at[b, k], device_id=to, device_id_type=MESH)

        def own(b):
            return pltpu.make_async_copy(ins[b], outs[b].at[_slot(*me)], local_sems.at[b])

        def first(b):
            return [copy(b, 0, me, sibling, src=ins[b])] + [
                copy(b, 1 + j, me, (*chip, c), src=ins[b]) for j, chip in enumerate(chips)]

        def start():
            for b in range(nb):
                own(b).start()
                for cp in first(b):
                    cp.start()

        def forward():
            for b in range(nb):
                for j, chip in enumerate(chips):
                    copy(b, 1 + j, (*chip, c), me).wait_recv()
                    copy(b, 4 + j, (*chip, c), sibling).start()

        def finish():
            for b in range(nb):
                copy(b, 0, sibling, me).wait_recv()
                for j, chip in enumerate(chips):
                    copy(b, 4 + j, (*chip, 1 - c), me).wait_recv()
            for b in range(nb):
                for cp in first(b):
                    cp.wait_send()
                for j, chip in enumerate(chips):
                    copy(b, 4 + j, (*chip, c), sibling).wait_send()
                own(b).wait()

        return start, forward, finish


class _AllToAll:
    def __init__(self, blocks):
        self.blocks = list(blocks)
        nb = len(self.blocks)
        self.out_shape = [_sds(b.shape, b.dtype) for b in self.blocks]
        self.scratch = [pltpu.SemaphoreType.DMA((nb, 7)), pltpu.SemaphoreType.DMA((nb, 7)),
                        pltpu.SemaphoreType.DMA((nb,))]

    def phases(self, ins, outs, send_sems, recv_sems, local_sems):
        nb = len(ins)
        x, y, c = _position()
        flip = lambda v, bit: 1 - v if bit else v
        peers = [(flip(x, k >> 2 & 1), flip(y, k >> 1 & 1), flip(c, k & 1)) for k in range(1, N_DEV)]

        def copy(b, k, peer, landed=False):
            return pltpu.make_async_remote_copy(
                src_ref=ins[b].at[_slot(*peer)], dst_ref=outs[b].at[_slot(*peer) if landed else _slot(x, y, c)],
                send_sem=send_sems.at[b, k], recv_sem=recv_sems.at[b, k], device_id=peer, device_id_type=MESH)

        def own(b):
            return pltpu.make_async_copy(ins[b].at[_slot(x, y, c)], outs[b].at[_slot(x, y, c)], local_sems.at[b])

        def start():
            for b in range(nb):
                own(b).start()
                for k, peer in enumerate(peers):
                    copy(b, k, peer).start()

        def forward():
            pass

        def finish():
            for b in range(nb):
                for k, peer in enumerate(peers):
                    copy(b, k, peer, landed=True).wait_recv()
            for b in range(nb):
                for k, peer in enumerate(peers):
                    copy(b, k, peer).wait_send()
                own(b).wait()

        return start, forward, finish


class _RowScatter:
    def __init__(self, arrays, pieces, n):
        self.blocks = list(arrays)
        self.pieces, self.n = pieces, n
        self.out_shape = [_sds((N_DEV, n, D), arrays[0].dtype)]
        self.scratch = [pltpu.SemaphoreType.DMA((N_DEV,)), pltpu.SemaphoreType.DMA((N_DEV,)), pltpu.SemaphoreType.DMA]

    def _parts(self, k):
        lo, hi, pos, res = k * self.n, (k + 1) * self.n, 0, []
        for arr, first, last in self.pieces:
            a, b = max(lo, pos), min(hi, pos + last - first)
            if a < b:
                res.append((arr, first + a - pos, b - a, a - lo))
            pos += last - first
        return res

    def phases(self, ins, outs, send_sems, recv_sems, local_sem):
        (out,) = outs
        x, y, c = _position()
        me = _slot(x, y, c)

        def start():
            for k in range(N_DEV):
                dist = jnp.bitwise_xor(me, k)

                @pl.when(me != k)
                def _():
                    for arr, first, rows, at in self._parts(k):
                        pltpu.make_async_remote_copy(
                            src_ref=ins[arr].at[pl.ds(first, rows)], dst_ref=out.at[me, pl.ds(at, rows)],
                            send_sem=send_sems.at[dist], recv_sem=recv_sems.at[dist],
                            device_id=(k >> 2 & 1, k >> 1 & 1, k & 1), device_id_type=MESH).start()

                @pl.when(me == k)
                def _():
                    for arr, first, rows, at in self._parts(k):
                        pltpu.make_async_copy(ins[arr].at[pl.ds(first, rows)], out.at[me, pl.ds(at, rows)], local_sem).start()

        def forward():
            pass

        def whole_block(dist):
            return pltpu.make_async_remote_copy(
                src_ref=out.at[me], dst_ref=out.at[jnp.bitwise_xor(me, dist)], send_sem=send_sems.at[dist],
                recv_sem=recv_sems.at[dist], device_id=(x, y, c), device_id_type=MESH)

        def finish():
            for dist in range(1, N_DEV):
                whole_block(dist).wait_recv()
            for dist in range(1, N_DEV):
                whole_block(dist).wait_send()
            pltpu.make_async_copy(out.at[me], out.at[me], local_sem).wait()

        return start, forward, finish


def _call(body, *, name, grid, in_specs, out_specs, out_shape, args, scratch_shapes=(), big=False, exchange=None):
    params = _cparams(len(grid), big)
    if exchange is None:
        return pl.pallas_call(body, name=name, grid=grid, in_specs=in_specs, out_specs=out_specs, out_shape=out_shape,
                              scratch_shapes=list(scratch_shapes), compiler_params=params)(*args)
    n_in, n_out, n_scr = len(in_specs), len(out_specs), len(scratch_shapes)
    nb, nb_out = len(exchange.blocks), len(exchange.out_shape)
    steps = math.prod(grid)

    def carried(*refs):
        pos = 0
        parts = []
        for n in (n_in, nb, n_out, nb_out, n_scr, len(exchange.scratch)):
            parts.append(refs[pos:pos + n])
            pos += n
        ins, x_ins, outs, x_outs, scr, sems = parts
        step = pl.program_id(0)
        for axis in range(1, len(grid)):
            step = step * grid[axis] + pl.program_id(axis)
        start, forward, finish = exchange.phases(x_ins, x_outs, *sems)
        pl.when(step == 0)(start)
        body(*ins, *outs, *scr)

        @pl.when(step == steps - 1)
        def _():
            forward()
            finish()

    any_spec = pl.BlockSpec(memory_space=pl.ANY)
    res = pl.pallas_call(
        carried, name=name, grid=grid, in_specs=list(in_specs) + [any_spec] * nb,
        out_specs=list(out_specs) + [any_spec] * nb_out, out_shape=list(out_shape) + exchange.out_shape,
        scratch_shapes=list(scratch_shapes) + exchange.scratch, compiler_params=params)(*args, *exchange.blocks)
    return res[:n_out], res[n_out:]


def _exchange_call(exchange, name):
    nb = len(exchange.blocks)

    def body(*refs):
        start, forward, finish = exchange.phases(refs[:nb], refs[nb:2 * nb], *refs[2 * nb:])
        start()
        forward()
        finish()

    any_spec = pl.BlockSpec(memory_space=pl.ANY)
    return pl.pallas_call(body, name=name, in_specs=[any_spec] * nb, out_specs=[any_spec] * nb,
                          out_shape=exchange.out_shape, scratch_shapes=exchange.scratch)(*exchange.blocks)


def _ln_call(x, g, tm, exchange):
    s = x.shape[0]

    def body(x_ref, g_ref, h_ref):
        xv = x_ref[...]
        r = lax.rsqrt(jnp.mean(xv * xv, axis=-1, keepdims=True) + EPS)
        h_ref[...] = ((xv * r) * g_ref[...]).astype(h_ref.dtype)

    return _call(body, name="ln1", grid=(s // tm,), in_specs=[_rows(tm, D), _acc(D)], out_specs=[_rows(tm, D)],
                 out_shape=[_sds((s, D), _MXU)], args=(x, g), exchange=exchange)


PROJ_TILE = 256


def _proj_source_tile(j):
    gate_end, attn_end, end = 3072 // PROJ_TILE, 4352 // PROJ_TILE, 6400 // PROJ_TILE
    n_gates = end - attn_end
    return jnp.where(j < gate_end, j, jnp.where(j < gate_end + n_gates, j + (attn_end - gate_end), j - n_gates))


def _proj_call(a, wt, bias, exchange):
    s, k = a.shape
    n = wt.shape[0]
    rows = min(1024, s)

    def body(a_ref, w_ref, b_ref, o_ref):
        for r in range(0, s, rows):
            o_ref[r:r + rows, :] = _dot_nt(a_ref[r:r + rows, :], w_ref[...]) + b_ref[...]

    return _call(body, name="proj", grid=(n // PROJ_TILE,),
                 in_specs=[_const((s, k)), pl.BlockSpec((PROJ_TILE, k), lambda j: (_proj_source_tile(j), 0)),
                           pl.BlockSpec((1, PROJ_TILE), lambda j: (0, _proj_source_tile(j)))],
                 out_specs=[pl.BlockSpec((s, PROJ_TILE), lambda j: (0, j))], out_shape=[_sds((s, n), F32)],
                 args=(a, wt, bias), big=True, exchange=exchange)


def _mm_tn(a, b, name, tm, tn, tk, exchange=None):
    s, m = a.shape
    n = b.shape[1]
    last = s // tk - 1

    def body(a_ref, b_ref, o_ref, acc):
        k = pl.program_id(2)
        part = _dot_tn(a_ref[...], b_ref[...])

        @pl.when(k == 0)
        def _():
            acc[...] = part

        @pl.when(k > 0)
        def _():
            acc[...] += part

        @pl.when(k == last)
        def _():
            o_ref[...] = acc[...].astype(o_ref.dtype)

    res = _call(body, name=name, grid=(m // tm, n // tn, s // tk),
                in_specs=[pl.BlockSpec((tk, tm), lambda i, j, k: (k, i)), pl.BlockSpec((tk, tn), lambda i, j, k: (k, j))],
                out_specs=[pl.BlockSpec((tm, tn), lambda i, j, k: (i, j))], out_shape=[_sds((m, n), _MXU)],
                scratch_shapes=[pltpu.VMEM((tm, tn), F32)], args=(a, b), big=True, exchange=exchange)
    return res[0] if exchange is None else (res[0][0], res[1])


RET_CHUNKS = 2


def _ret_fwd_call(proj, cos, sin, decays):
    s = proj.shape[0]
    nblk = s // BLK
    per = min(RET_CHUNKS, nblk)
    rows = per * BLK
    intra, q_decay, k_decay, chunk_decay = decays

    def body(rq_ref, rk_ref, rv_ref, cos_ref, sin_ref, intra_ref, qd_ref, kd_ref,
             ry_ref, qr_ref, kr_ref, st_ref, state):
        @pl.when(pl.program_id(0) == 0)
        def _():
            state[...] = jnp.zeros_like(state)

        for c in range(per):
            rc = slice(c * BLK, (c + 1) * BLK)
            cos_v, sin_v = _ret_rope(cos_ref[rc, :], sin_ref[rc, :])
            for h in range(RET_HEADS):
                hk = slice(h * RET_DK, (h + 1) * RET_DK)
                hv = slice(h * RET_DV, (h + 1) * RET_DV)
                q, k = rq_ref[rc, hk], rk_ref[rc, hk]
                qr = (q * cos_v + _swap_halves(q, RET_DK // 2) * sin_v) * RET_SCALE
                kr = k * cos_v + _swap_halves(k, RET_DK // 2) * sin_v
                v = rv_ref[rc, hv]
                s_h = state[h]
                st_ref[c, h] = s_h.astype(st_ref.dtype)
                scores = _dot_nt(qr, kr) * intra_ref[h]
                ry_ref[rc, hv] = _dot(scores, v) + _dot(qr, s_h) * qd_ref[h]
                state[h] = s_h * chunk_decay[h] + _dot_tn(kr * kd_ref[h], v)
                qr_ref[rc, hk] = qr.astype(qr_ref.dtype)
                kr_ref[rc, hk] = kr.astype(kr_ref.dtype)

    blk = lambda w, c: pl.BlockSpec((rows, w), lambda n: (n, c))
    return _call(body, name="ret_fwd", grid=(nblk // per,),
                 in_specs=[blk(512, 0), blk(512, 1), blk(1024, 1), blk(128, 0), blk(128, 0),
                           _const(intra.shape), _const(q_decay.shape), _const(k_decay.shape)],
                 out_specs=[blk(1024, 0), blk(512, 0), blk(512, 0),
                            pl.BlockSpec((per, RET_HEADS, RET_DK, RET_DV), lambda n: (n, 0, 0, 0))],
                 out_shape=[_sds((s, 1024), F32), _sds((s, 512), _MXU), _sds((s, 512), _MXU),
                            _sds((nblk, RET_HEADS, RET_DK, RET_DV), _MXU)],
                 scratch_shapes=[pltpu.VMEM((RET_HEADS, RET_DK, RET_DV), F32)],
                 args=(proj, proj, proj, cos, sin, intra, q_decay, k_decay))


def _ret_bwd_call(qr, kr, proj, states, dry, cos, sin, decays):
    s = qr.shape[0]
    nblk = s // BLK
    per = min(RET_CHUNKS, nblk)
    rows = per * BLK
    steps = nblk // per
    intra, q_decay, k_decay, chunk_decay = decays

    def body(qr_ref, kr_ref, rv_ref, st_ref, dry_ref, cos_ref, sin_ref, intra_ref, qd_ref, kd_ref,
             dp_ref, db_ref, dstate):
        @pl.when(pl.program_id(0) == 0)
        def _():
            dstate[...] = jnp.zeros_like(dstate)
            db_ref[...] = jnp.zeros_like(db_ref)

        for c in reversed(range(per)):
            rc = slice(c * BLK, (c + 1) * BLK)
            cos_v, sin_v = _ret_rope(cos_ref[rc, :], sin_ref[rc, :])
            for h in range(RET_HEADS):
                hk = slice(h * RET_DK, (h + 1) * RET_DK)
                hv = slice(h * RET_DV, (h + 1) * RET_DV)
                q, k, v, d_out = qr_ref[rc, hk], kr_ref[rc, hk], rv_ref[rc, hv], dry_ref[rc, hv]
                d_next = dstate[h]
                scores = _dot_nt(q, k) * intra_ref[h]
                d_scores = _dot_nt(d_out, v) * intra_ref[h]
                d_cross = d_out * qd_ref[h]
                dq = _dot(d_scores, k) + _dot_nt(d_cross, st_ref[c, h])
                dk = _dot_tn(d_scores, q) + _dot_nt(v, d_next) * kd_ref[h]
                dv = _dot_tn(scores, d_out) + _dot(k.astype(F32) * kd_ref[h], d_next)
                dstate[h] = d_next * chunk_decay[h] + _dot_tn(q, d_cross)
                dq = (dq * cos_v - _swap_halves(dq, RET_DK // 2) * sin_v) * RET_SCALE
                dk = dk * cos_v - _swap_halves(dk, RET_DK // 2) * sin_v
                kcols = slice(512 + h * RET_DK, 512 + (h + 1) * RET_DK)
                vcols = slice(1024 + h * RET_DV, 1024 + (h + 1) * RET_DV)
                dp_ref[rc, hk] = dq.astype(dp_ref.dtype)
                dp_ref[rc, kcols] = dk.astype(dp_ref.dtype)
                dp_ref[rc, vcols] = dv.astype(dp_ref.dtype)
                db_ref[:, hk] += jnp.sum(dq, axis=0, keepdims=True)
                db_ref[:, kcols] += jnp.sum(dk, axis=0, keepdims=True)
                db_ref[:, vcols] += jnp.sum(dv, axis=0, keepdims=True)

    rblk = lambda w, c: pl.BlockSpec((rows, w), lambda n: (steps - 1 - n, c))
    return _call(body, name="ret_bwd", grid=(steps,),
                 in_specs=[rblk(512, 0), rblk(512, 0), rblk(1024, 1),
                           pl.BlockSpec((per, RET_HEADS, RET_DK, RET_DV), lambda n: (steps - 1 - n, 0, 0, 0)),
                           rblk(1024, 0), rblk(128, 0), rblk(128, 0),
                           _const(intra.shape), _const(q_decay.shape), _const(k_decay.shape)],
                 out_specs=[rblk(N_RET, 0), _acc(N_RET)],
                 out_shape=[_sds((s, N_RET), _MXU), _sds((1, N_RET), F32)],
                 scratch_shapes=[pltpu.VMEM((RET_HEADS, RET_DK, RET_DV), F32)],
                 args=(qr, kr, proj, states, dry, cos, sin, intra, q_decay, k_decay))


def _both_halves(x, g):
    lane = lax.broadcasted_iota(jnp.int32, x.shape, 1)
    keep = lane < HEAD_DIM if g == 0 else lane >= HEAD_DIM
    return jnp.where(keep, x, pltpu.roll(x, HEAD_DIM, 1))


def _stack_heads(ref, g):
    lane = lax.broadcasted_iota(jnp.int32, (BLK, 128), 1)
    pieces = []
    for j in range(g * 4, g * 4 + 4):
        chunk = ref[:, j * 128:(j + 1) * 128]
        pieces += [jnp.where(lane < HEAD_DIM, chunk, jnp.zeros_like(chunk)),
                   jnp.where(lane >= HEAD_DIM, chunk, jnp.zeros_like(chunk))]
    return jnp.concatenate(pieces, axis=0)


def _window_bias(first_block):
    kj = lax.broadcasted_iota(jnp.int32, (2 * BLK, BLK), 0)
    qi = lax.broadcasted_iota(jnp.int32, (2 * BLK, BLK), 1)
    first_key = jnp.where(first_block, BLK, 0)
    seen = (kj > qi) & (kj <= qi + BLK) & (kj >= first_key)
    return jnp.where(seen, 0.0, -1e30)


def _sink_softmax(scores, sink):
    m = jnp.maximum(jnp.max(scores, axis=0, keepdims=True), sink)
    e = jnp.exp(scores - m)
    e_sink = jnp.exp(sink - m)
    return e, e_sink, 1.0 / (jnp.sum(e, axis=0, keepdims=True) + e_sink)


def _head_pair(stacked_t, jj):
    even = stacked_t[0:HEAD_DIM, 2 * jj * BLK:(2 * jj + 1) * BLK]
    odd = stacked_t[HEAD_DIM:128, (2 * jj + 1) * BLK:(2 * jj + 2) * BLK]
    return jnp.concatenate([even, odd], axis=0).T


def _attn_fwd_call(proj, sinks, cos, sin):
    s = proj.shape[0]
    nblk = s // BLK

    def body(sink_ref, q_ref, k_ref, v_ref, cos_ref, sin_ref, ay_ref, qr_ref, kr_ref, vb_ref,
             kwin, vwin, bias, s_scr, p_scr):
        n = pl.program_id(0)

        @pl.when(n == 0)
        def _():
            kwin[...] = jnp.zeros_like(kwin)
            vwin[...] = jnp.zeros_like(vwin)

        @pl.when(n > 0)
        def _():
            kwin[0:BLK] = kwin[BLK:2 * BLK]
            vwin[0:BLK] = vwin[BLK:2 * BLK]

        cos_v, sin_v = _attn_rope(cos_ref[...], sin_ref[...])
        k = k_ref[...]
        kr = (k * cos_v + _swap_halves(k, HEAD_DIM // 2) * sin_v).astype(kwin.dtype)
        kwin[BLK:2 * BLK] = kr
        vwin[BLK:2 * BLK] = v_ref[...].astype(vwin.dtype)
        kr_ref[...] = kr
        vb_ref[...] = vwin[BLK:2 * BLK]
        for j in range(Q_HEADS // 2):
            cols = slice(j * 128, (j + 1) * 128)
            q = q_ref[:, cols]
            qr_ref[:, cols] = ((q * cos_v + _swap_halves(q, HEAD_DIM // 2) * sin_v) * ATTN_SCALE).astype(qr_ref.dtype)
        bias[...] = _window_bias(n == 0)
        for g in range(KV_HEADS):
            kg = _both_halves(kwin[...], g)
            vg_t = _both_halves(vwin[...], g).astype(F32).T
            s_scr[...] = _dot_nt(kg, _stack_heads(qr_ref, g))
            for i in range(GROUP):
                cols = slice(i * BLK, (i + 1) * BLK)
                e, _, inv = _sink_softmax(s_scr[:, cols] + bias[...], sink_ref[0, g * GROUP + i])
                p_scr[:, cols] = (e * inv).astype(p_scr.dtype)
            out_t = _dot(vg_t, p_scr[...])
            for jj in range(4):
                j = g * 4 + jj
                ay_ref[:, j * 128:(j + 1) * 128] = _head_pair(out_t, jj).astype(ay_ref.dtype)

    blk = lambda w, c: pl.BlockSpec((BLK, w), lambda n: (n, c))
    off = (N_RET + N_GATE) // 128
    wide = (2 * BLK, GROUP * BLK)
    return _call(body, name="attn_fwd", grid=(nblk,),
                 in_specs=[pl.BlockSpec(memory_space=pltpu.SMEM), blk(1024, off // 8), blk(128, off + 8), blk(128, off + 9),
                           blk(128, 0), blk(128, 0)],
                 out_specs=[blk(1024, 0), blk(1024, 0), blk(128, 0), blk(128, 0)],
                 out_shape=[_sds((s, 1024), _MXU), _sds((s, 1024), _MXU), _sds((s, 128), _MXU), _sds((s, 128), _MXU)],
                 scratch_shapes=[pltpu.VMEM((2 * BLK, 128), _MXU), pltpu.VMEM((2 * BLK, 128), _MXU),
                                 pltpu.VMEM((2 * BLK, BLK), F32), pltpu.VMEM(wide, F32), pltpu.VMEM(wide, _MXU)],
                 args=(sinks, proj, proj, proj, cos, sin))


def _attn_bwd_call(qr, kr, vb, day, sinks, cos, sin):
    s = qr.shape[0]
    nblk = s // BLK

    def body(sink_ref, q_ref, kc_ref, kp_ref, vc_ref, vp_ref, do_ref, cos_ref, sin_ref, cosp_ref, sinp_ref,
             dp_ref, dsink_ref, db_ref, bias, s_scr, dp_scr, p_scr, ds_scr, dq_held, kv_held, kv_prev, kv_new):
        n = pl.program_id(0)
        valid = (n < nblk).astype(F32)

        @pl.when(n == 0)
        def _():
            dsink_ref[...] = jnp.zeros_like(dsink_ref)
            db_ref[...] = jnp.zeros_like(db_ref)

        @pl.when(n >= 1)
        def _():
            dp_ref[:, 0:1024] = dq_held[...]

        cos_v, sin_v = _attn_rope(cos_ref[...], sin_ref[...])
        bias[...] = _window_bias(n == 0)
        lane1 = lax.broadcasted_iota(jnp.int32, (1, 128), 1)
        kwin = jnp.concatenate([kp_ref[...], kc_ref[...]], axis=0)
        vwin = jnp.concatenate([vp_ref[...], vc_ref[...]], axis=0)
        dk_heads, dv_heads = [], []
        dsink = jnp.zeros((1, 128), F32)
        for g in range(KV_HEADS):
            kg = _both_halves(kwin, g)
            vg = _both_halves(vwin, g)
            q_all = _stack_heads(q_ref, g)
            do_all = _stack_heads(do_ref, g)
            s_scr[...] = _dot_nt(kg, q_all)
            dp_scr[...] = _dot_nt(vg, do_all)
            for i in range(GROUP):
                head = g * GROUP + i
                cols = slice(i * BLK, (i + 1) * BLK)
                e, e_sink, inv = _sink_softmax(s_scr[:, cols] + bias[...], sink_ref[0, head])
                p = e * inv
                dp = dp_scr[:, cols]
                delta = jnp.sum(p * dp, axis=0, keepdims=True)
                p_scr[:, cols] = p.astype(p_scr.dtype)
                ds_scr[:, cols] = (p * (dp - delta)).astype(ds_scr.dtype)
                dsink = dsink + jnp.where(lane1 == head, -jnp.sum(e_sink * inv * delta, axis=1, keepdims=True), 0.0)
            dv_both = _dot(p_scr[...], do_all)
            dk_both = _dot(ds_scr[...], q_all)
            dv_heads.append(dv_both + pltpu.roll(dv_both, HEAD_DIM, 1))
            dk_heads.append(dk_both + pltpu.roll(dk_both, HEAD_DIM, 1))
            dq_t = _dot(kg.astype(F32).T, ds_scr[...])
            for jj in range(4):
                cols = slice((g * 4 + jj) * 128, (g * 4 + jj + 1) * 128)
                dq = _head_pair(dq_t, jj)
                dq = (dq * cos_v - _swap_halves(dq, HEAD_DIM // 2) * sin_v) * ATTN_SCALE
                dq_held[:, cols] = dq.astype(dq_held.dtype)
                db_ref[:, cols] += jnp.sum(dq, axis=0, keepdims=True) * valid
        dsink_ref[...] += dsink * valid
        lane2 = lax.broadcasted_iota(jnp.int32, (2 * BLK, 128), 1)
        dk_all = jnp.where(lane2 < HEAD_DIM, dk_heads[0], dk_heads[1])
        dv_all = jnp.where(lane2 < HEAD_DIM, dv_heads[0], dv_heads[1])
        kv_prev[:, 0:128] = dk_all[0:BLK] * valid
        kv_prev[:, 128:256] = dv_all[0:BLK] * valid
        kv_new[:, 0:128] = dk_all[BLK:2 * BLK]
        kv_new[:, 128:256] = dv_all[BLK:2 * BLK]

        @pl.when(n >= 1)
        def _():
            dkv = kv_held[...] + kv_prev[...]
            dk = dkv[:, 0:128]
            cos_p, sin_p = _attn_rope(cosp_ref[...], sinp_ref[...])
            dk = dk * cos_p - _swap_halves(dk, HEAD_DIM // 2) * sin_p
            dv = dkv[:, 128:256]
            dp_ref[:, 1024:1152] = dk.astype(dp_ref.dtype)
            dp_ref[:, 1152:1280] = dv.astype(dp_ref.dtype)
            db_ref[:, 1024:1152] += jnp.sum(dk, axis=0, keepdims=True)
            db_ref[:, 1152:1280] += jnp.sum(dv, axis=0, keepdims=True)

        kv_held[...] = kv_new[...]

    blk = lambda w: pl.BlockSpec((BLK, w), lambda n: (jnp.minimum(n, nblk - 1), 0))
    pblk = lambda w: pl.BlockSpec((BLK, w), lambda n: (jnp.maximum(n - 1, 0), 0))
    wide = (2 * BLK, GROUP * BLK)
    return _call(body, name="attn_bwd", grid=(nblk + 1,),
                 in_specs=[pl.BlockSpec(memory_space=pltpu.SMEM), blk(1024), blk(128), pblk(128), blk(128), pblk(128),
                           blk(1024), blk(128), blk(128), pblk(128), pblk(128)],
                 out_specs=[pblk(N_ATTN), _acc(128), _acc(N_ATTN)],
                 out_shape=[_sds((s, N_ATTN), _MXU), _sds((1, 128), F32), _sds((1, N_ATTN), F32)],
                 scratch_shapes=[pltpu.VMEM((2 * BLK, BLK), F32), pltpu.VMEM(wide, F32), pltpu.VMEM(wide, F32),
                                 pltpu.VMEM(wide, _MXU), pltpu.VMEM(wide, _MXU), pltpu.VMEM((BLK, 1024), _MXU),
                                 pltpu.VMEM((BLK, 256), F32), pltpu.VMEM((BLK, 256), F32), pltpu.VMEM((BLK, 256), F32)],
                 args=(sinks, qr, kr, kr, vb, vb, day, cos, sin, cos, sin))


def _group_norm(y):
    mu = jnp.mean(y, axis=-1, keepdims=True)
    yc = y - mu
    rs = lax.rsqrt(jnp.mean(yc * yc, axis=-1, keepdims=True) + EPS)
    return yc * rs, rs


GATE_COL = N_RET // 1024


def _merge_fwd_call(x, ry, proj, ay, gn_g, w_ro, w_ao, w_o, tm):
    s = x.shape[0]

    def body(x_ref, ry_ref, rg_ref, ga_ref, gb_ref, ay_ref, gn_ref, wro_ref, wao_ref, wo_ref,
             ain_ref, a_ref, b_ref, mg_ref, x1_ref):
        for h in range(RET_HEADS):
            hv = slice(h * RET_DV, (h + 1) * RET_DV)
            yhat, _ = _group_norm(ry_ref[:, hv])
            rg = rg_ref[:, hv]
            ain_ref[:, hv] = ((rg * _sigmoid(rg)) * (yhat * gn_ref[:, hv])).astype(ain_ref.dtype)
        a = _dot(ain_ref[...], wro_ref[...])
        b = _dot(ay_ref[...], wao_ref[...])
        a_ref[...] = a.astype(a_ref.dtype)
        b_ref[...] = b.astype(b_ref.dtype)
        merged = (_sigmoid(ga_ref[...]) * a + _sigmoid(gb_ref[...]) * b).astype(mg_ref.dtype)
        mg_ref[...] = merged
        x1_ref[...] = x_ref[...] + _dot(merged, wo_ref[...])

    return _call(body, name="merge_fwd", grid=(s // tm,),
                 in_specs=[_rows(tm, D), _rows(tm, 1024), _rows(tm, 1024, GATE_COL), _rows(tm, 1024, GATE_COL + 1),
                           _rows(tm, 1024, GATE_COL + 2), _rows(tm, 1024), _acc(1024),
                           _const((D, D)), _const((D, D)), _const((D, D))],
                 out_specs=[_rows(tm, D)] * 5,
                 out_shape=[_sds((s, D), _MXU), _sds((s, D), _STORE), _sds((s, D), _STORE), _sds((s, D), _MXU),
                            _sds((s, D), F32)],
                 args=(x, ry, proj, proj, proj, ay, gn_g, w_ro, w_ao, w_o), big=True)


def _merge_bwd_call(dx1, a, b, ry, proj, gn_g, w_ro, w_ao, w_o, tm, exchange):
    s = dx1.shape[0]

    def body(dx1_ref, a_ref, b_ref, ry_ref, rg_ref, ga_ref, gb_ref, gn_ref, wro_ref, wao_ref, wo_ref,
             da_ref, dbr_ref, dp_ref, day_ref, dry_ref, dbias_ref, dgn_ref):
        @pl.when(pl.program_id(0) == 0)
        def _():
            dbias_ref[...] = jnp.zeros_like(dbias_ref)
            dgn_ref[...] = jnp.zeros_like(dgn_ref)

        d_merged = _dot_nt(dx1_ref[...], wo_ref[...])
        sa, sb = _sigmoid(ga_ref[...]), _sigmoid(gb_ref[...])
        d_a = d_merged * sa
        d_b = d_merged * sb
        da_ref[...] = d_a.astype(da_ref.dtype)
        dbr_ref[...] = d_b.astype(dbr_ref.dtype)
        d_ga = d_merged * a_ref[...] * (sa * (1.0 - sa))
        d_gb = d_merged * b_ref[...] * (sb * (1.0 - sb))
        dp_ref[:, 1024:2048] = d_ga.astype(dp_ref.dtype)
        dp_ref[:, 2048:3072] = d_gb.astype(dp_ref.dtype)
        dbias_ref[:, 1024:2048] += jnp.sum(d_ga, axis=0, keepdims=True)
        dbias_ref[:, 2048:3072] += jnp.sum(d_gb, axis=0, keepdims=True)
        day_ref[...] = _dot_nt(d_b, wao_ref[...]).astype(day_ref.dtype)
        d_ain = _dot_nt(d_a, wro_ref[...])
        for h in range(RET_HEADS):
            hv = slice(h * RET_DV, (h + 1) * RET_DV)
            yhat, rs = _group_norm(ry_ref[:, hv])
            rg = rg_ref[:, hv]
            sg = _sigmoid(rg)
            gn = gn_ref[:, hv]
            d_h = d_ain[:, hv]
            d_rg = d_h * (yhat * gn) * (sg * (1.0 + rg * (1.0 - sg)))
            d_ryn = d_h * (rg * sg)
            dgn_ref[:, hv] += jnp.sum(d_ryn * yhat, axis=0, keepdims=True)
            d_yhat = d_ryn * gn
            dry_ref[:, hv] = (rs * (d_yhat - jnp.mean(d_yhat, axis=-1, keepdims=True)
                                    - yhat * jnp.mean(d_yhat * yhat, axis=-1, keepdims=True))).astype(dry_ref.dtype)
            dp_ref[:, hv] = d_rg.astype(dp_ref.dtype)
            dbias_ref[:, hv] += jnp.sum(d_rg, axis=0, keepdims=True)

    return _call(body, name="merge_bwd", grid=(s // tm,),
                 in_specs=[_rows(tm, D), _rows(tm, D), _rows(tm, D), _rows(tm, 1024), _rows(tm, 1024, GATE_COL),
                           _rows(tm, 1024, GATE_COL + 1), _rows(tm, 1024, GATE_COL + 2), _acc(1024),
                           _const((D, D)), _const((D, D)), _const((D, D))],
                 out_specs=[_rows(tm, D), _rows(tm, D), _rows(tm, N_GATE), _rows(tm, D), _rows(tm, D), _acc(N_GATE),
                            _acc(1024)],
                 out_shape=[_sds((s, D), _MXU), _sds((s, D), _MXU), _sds((s, N_GATE), _MXU), _sds((s, D), _MXU),
                            _sds((s, D), _STORE), _sds((1, N_GATE), F32), _sds((1, 1024), F32)],
                 args=(dx1, a, b, ry, proj, proj, proj, gn_g, w_ro, w_ao, w_o), big=True, exchange=exchange)


def _ffn_fwd_call(x1, target, ln2_g, lnf_g, w_g, w_u, w_d, tm):
    s = x1.shape[0]

    def body(x1_ref, t_ref, g2_ref, gf_ref, wg_ref, wu_ref, wd_ref,
             h2_ref, g_ref, u_ref, f_ref, dx2_ref, loss_ref, dgf_ref):
        @pl.when(pl.program_id(0) == 0)
        def _():
            loss_ref[...] = jnp.zeros_like(loss_ref)
            dgf_ref[...] = jnp.zeros_like(dgf_ref)

        x1v = x1_ref[...]
        r1 = lax.rsqrt(jnp.mean(x1v * x1v, axis=-1, keepdims=True) + EPS)
        h2 = ((x1v * r1) * g2_ref[...]).astype(h2_ref.dtype)
        h2_ref[...] = h2
        g = _dot_nt(h2, wg_ref[...])
        u = _dot_nt(h2, wu_ref[...])
        g_ref[...] = g
        u_ref[...] = u
        f = ((g * _sigmoid(g)) * u).astype(f_ref.dtype)
        f_ref[...] = f
        x2 = x1v + _dot(f, wd_ref[...])
        r2 = lax.rsqrt(jnp.mean(x2 * x2, axis=-1, keepdims=True) + EPS)
        xhat = x2 * r2
        err = xhat * gf_ref[...] - t_ref[...]
        loss_ref[...] += 0.5 * jnp.sum(jnp.mean(err * err, axis=-1, keepdims=True))
        dy = err * (1.0 / D)
        dgf_ref[...] += jnp.sum(dy * xhat, axis=0, keepdims=True)
        dxh = dy * gf_ref[...]
        dx2_ref[...] = r2 * (dxh - xhat * jnp.mean(dxh * xhat, axis=-1, keepdims=True))

    return _call(body, name="ffn_fwd", grid=(s // tm,),
                 in_specs=[_rows(tm, D), _rows(tm, D), _acc(D), _acc(D), _const((D_FF, D)), _const((D_FF, D)),
                           _const((D_FF, D))],
                 out_specs=[_rows(tm, D), _rows(tm, D_FF), _rows(tm, D_FF), _rows(tm, D_FF), _rows(tm, D), _acc(128),
                            _acc(D)],
                 out_shape=[_sds((s, D), _MXU), _sds((s, D_FF), F32), _sds((s, D_FF), F32), _sds((s, D_FF), _MXU),
                            _sds((s, D), F32), _sds((1, 128), F32), _sds((1, D), F32)],
                 args=(x1, target, ln2_g, lnf_g, w_g, w_u, w_d), big=True)


def _ffn_bwd_call(dx2, x1, g, u, ln2_g, w_g, w_u, w_d, tm):
    s = dx2.shape[0]

    def body(dx2_ref, x1_ref, g_ref, u_ref, g2_ref, wg_ref, wu_ref, wd_ref, dx1_ref, dg_ref, du_ref, dg2_ref):
        @pl.when(pl.program_id(0) == 0)
        def _():
            dg2_ref[...] = jnp.zeros_like(dg2_ref)

        dx2v = dx2_ref[...]
        df = _dot_nt(dx2v, wd_ref[...])
        gv, uv = g_ref[...], u_ref[...]
        sg = _sigmoid(gv)
        du = (df * (gv * sg)).astype(du_ref.dtype)
        dg = (df * uv * (sg * (1.0 + gv * (1.0 - sg)))).astype(dg_ref.dtype)
        du_ref[...] = du
        dg_ref[...] = dg
        dh2 = _dot(dg, wg_ref[...]) + _dot(du, wu_ref[...])
        x1v = x1_ref[...]
        r1 = lax.rsqrt(jnp.mean(x1v * x1v, axis=-1, keepdims=True) + EPS)
        xhat = x1v * r1
        dg2_ref[...] += jnp.sum(dh2 * xhat, axis=0, keepdims=True)
        dxh = dh2 * g2_ref[...]
        dx1_ref[...] = dx2v + r1 * (dxh - xhat * jnp.mean(dxh * xhat, axis=-1, keepdims=True))

    return _call(body, name="ffn_bwd", grid=(s // tm,),
                 in_specs=[_rows(tm, D), _rows(tm, D), _rows(tm, D_FF), _rows(tm, D_FF), _acc(D),
                           _const((D_FF, D)), _const((D_FF, D)), _const((D_FF, D))],
                 out_specs=[_rows(tm, D), _rows(tm, D_FF), _rows(tm, D_FF), _acc(D)],
                 out_shape=[_sds((s, D), F32), _sds((s, D_FF), _MXU), _sds((s, D_FF), _MXU), _sds((1, D), F32)],
                 args=(dx2, x1, g, u, ln2_g, w_g, w_u, w_d), big=True)


def _dx_call(x, dx1, dp_ret, dp_gate, dp_attn, ln1_g, w_in, tm, exchange):
    s = x.shape[0]

    def body(x_ref, dx1_ref, dr_ref, dg_ref, da_ref, g1_ref, w_ref, dx_ref, dg1_ref):
        @pl.when(pl.program_id(0) == 0)
        def _():
            dg1_ref[...] = jnp.zeros_like(dg1_ref)

        dh = (_dot(dr_ref[...], w_ref[0:2048, :]) + _dot(dg_ref[:, 0:1024], w_ref[2048:3072, :])
              + _dot(da_ref[...], w_ref[3072:4352, :]) + _dot(dg_ref[:, 1024:3072], w_ref[4352:6400, :]))
        xv = x_ref[...]
        r = lax.rsqrt(jnp.mean(xv * xv, axis=-1, keepdims=True) + EPS)
        xhat = xv * r
        dg1_ref[...] += jnp.sum(dh * xhat, axis=0, keepdims=True)
        dxh = dh * g1_ref[...]
        dx_ref[...] = dx1_ref[...] + r * (dxh - xhat * jnp.mean(dxh * xhat, axis=-1, keepdims=True))

    return _call(body, name="dx", grid=(s // tm,),
                 in_specs=[_rows(tm, D), _rows(tm, D), _rows(tm, N_RET), _rows(tm, N_GATE), _rows(tm, N_ATTN), _acc(D),
                           _const((D_IN, D))],
                 out_specs=[_rows(tm, D), _acc(D)],
                 out_shape=[_sds((s, D), F32), _sds((1, D), F32)],
                 args=(x, dx1, dp_ret, dp_gate, dp_attn, ln1_g, w_in), big=True, exchange=exchange)


def _adamw(g, w, m, v):
    m_new = B1 * m + (1.0 - B1) * g
    v_new = B2 * v + (1.0 - B2) * (g * g)
    m_hat = m_new / (1.0 - B1 ** STEP)
    v_hat = v_new / (1.0 - B2 ** STEP)
    return -LR * (m_hat / (jnp.sqrt(v_hat) + ADAM_EPS) + WD * w), m_new, v_new


def _slot_sum(p_ref):
    g = p_ref[0].astype(F32)
    for k in range(1, N_DEV):
        g = g + p_ref[k].astype(F32)
    return g


def _adamw_call(parts, w, m, v, name, tr):
    rows, cols = w.shape

    def body(p_ref, w_ref, m_ref, v_ref, g_ref, dw_ref, nm_ref, nv_ref):
        g = _slot_sum(p_ref)
        g_ref[...] = g
        dw_ref[...], nm_ref[...], nv_ref[...] = _adamw(g, w_ref[...], m_ref[...], v_ref[...])

    p_spec = pl.BlockSpec((N_DEV, tr, cols), lambda i: (0, i, 0))
    spec = pl.BlockSpec((tr, cols), lambda i: (i, 0))
    return _call(body, name=name, grid=(rows // tr,), in_specs=[p_spec, spec, spec, spec], out_specs=[spec] * 4,
                 out_shape=[_sds((rows, cols), F32)] * 4, args=(parts, w, m, v))


SMALL_WIDTHS = [1024, 6400, 1024, 16, 1024, 1024]
SMALL_OFFSETS = [0, 1024, 7424, 8448, 8576, 9600]
LOSS_OFFSET = 10624
SMALL_LEN = 10752


def _pack_small(grads, loss):
    pieces = []
    for gr, width in zip(grads, SMALL_WIDTHS):
        pieces.append(jnp.pad(gr.reshape(1, width), ((0, 0), (0, -width % 128))))
    pieces.append(jnp.pad(loss.reshape(1, 1), ((0, 0), (0, 127))))
    return jnp.concatenate(pieces, axis=1)


def _adamw_small_call(parts, ws, ms, vs):
    n = len(ws)

    def body(*refs):
        p_ref, w_refs, m_refs, v_refs = refs[0], refs[1:1 + n], refs[1 + n:1 + 2 * n], refs[1 + 2 * n:1 + 3 * n]
        outs = refs[1 + 3 * n:]
        g_all = _slot_sum(p_ref)
        for i, (off, width) in enumerate(zip(SMALL_OFFSETS, SMALL_WIDTHS)):
            g = g_all[:, off:off + width]
            outs[i][...] = g
            outs[n + i][...], outs[2 * n + i][...], outs[3 * n + i][...] = _adamw(
                g, w_refs[i][...], m_refs[i][...], v_refs[i][...])
        outs[4 * n][...] = g_all[:, LOSS_OFFSET:LOSS_OFFSET + 128]

    whole = lambda shape: pl.BlockSpec(shape, lambda i: (0,) * len(shape))
    small = [whole((1, w)) for w in SMALL_WIDTHS]
    res = _call(body, name="adamw_small", grid=(1,), in_specs=[whole((N_DEV, 1, SMALL_LEN))] + small * 3,
                out_specs=small * 4 + [whole((1, 128))],
                out_shape=[_sds((1, w), F32) for w in SMALL_WIDTHS] * 4 + [_sds((1, 128), F32)],
                args=(parts, *ws, *ms, *vs))
    return [res[k * n:(k + 1) * n] for k in range(4)], res[4 * n]


def kernel(x, ln1_g, w_in, b_in, ret_norm_g, w_ret_out, attn_sinks, w_attn_out, w_out, ln2_g, w_ffn_gate, w_ffn_up, w_ffn_down, lnf_g, loss_target, m_ln1_g, m_w_in, m_b_in, m_ret_norm_g, m_w_ret_out, m_attn_sinks, m_w_attn_out, m_w_out, m_ln2_g, m_w_ffn_gate, m_w_ffn_up, m_w_ffn_down, m_lnf_g, v_ln1_g, v_w_in, v_b_in, v_ret_norm_g, v_w_ret_out, v_attn_sinks, v_w_attn_out, v_w_out, v_ln2_g, v_w_ffn_gate, v_w_ffn_up, v_w_ffn_down, v_lnf_g):
    cast = lambda a: a.astype(_MXU)
    xs, target = x[0], loss_target[0]
    s = xs.shape[0]
    r_sq = w_ret_out.shape[1]
    r_dn = w_ffn_down.shape[1]
    c_in = w_in.shape[2]
    c_ff = w_ffn_gate.shape[2]
    tm, tk = min(256, s), min(2048, s)
    lnf_row = lnf_g.reshape(1, D)
    cos_t, sin_t = _rope_tables(s)
    decays = _retention_decays()
    tr_shard = lambda a: a[0].T
    per_dev = lambda a, n: a.reshape(N_DEV, n, D)

    (h,), (all_in,) = _ln_call(xs, ln1_g, tm, _AllGather([cast(tr_shard(w_in))]))
    wt_in = all_in.reshape(N_DEV * c_in, D)
    rest = [tr_shard(w_ffn_gate), tr_shard(w_ffn_up), w_ret_out[0], w_attn_out[0], w_out[0], w_ffn_down[0]]
    (proj,), gathered = _proj_call(h, wt_in, b_in, _AllGather([cast(a) for a in rest]))
    wt_g, wt_u, full_ro, full_ao, full_o, full_d = (a.reshape(N_DEV * a.shape[1], D) for a in gathered)
    ry, qr, kr, states = _ret_fwd_call(proj, cos_t, sin_t, decays)
    ay, aqr, akr, avb = _attn_fwd_call(proj, attn_sinks, cos_t, sin_t)
    a_in, br_a, br_b, merged, x1 = _merge_fwd_call(xs, ry, proj, ay, ret_norm_g, full_ro, full_ao, full_o, tm)
    h2, g, u, f, dx2, loss, d_lnf = _ffn_fwd_call(x1, target, ln2_g, lnf_row, wt_g, wt_u, full_d, tm)

    dx1, dg, du, d_ln2 = _ffn_bwd_call(dx2, x1, g, u, ln2_g, wt_g, wt_u, full_d, tm)
    dw_d = _mm_tn(f, dx2, "dw_ffn_down", 1408, 1024, tk)
    dwt_g, (got_d,) = _mm_tn(dg, h2, "dw_ffn_gate", 1408, 1024, tk, _AllToAll([per_dev(dw_d, r_dn)]))
    dwt_u = _mm_tn(du, h2, "dw_ffn_up", 1408, 1024, tk)
    (d_a, d_b, dp_gate, day, dry, db_gate, d_gn), (got_g, got_u) = _merge_bwd_call(
        dx1, br_a, br_b, ry, proj, ret_norm_g, full_ro, full_ao, full_o, tm,
        _AllToAll([per_dev(dwt_g, c_ff), per_dev(dwt_u, c_ff)]))
    dw_o = _mm_tn(merged, dx1, "dw_out", 1024, 1024, tk)
    dw_ro = _mm_tn(a_in, d_a, "dw_ret_out", 1024, 1024, tk)
    dw_ao = _mm_tn(ay, d_b, "dw_attn_out", 1024, 1024, tk)
    dp_attn, d_sinks, db_attn = _attn_bwd_call(aqr, akr, avb, day, attn_sinks, cos_t, sin_t)
    dp_ret, db_ret = _ret_bwd_call(qr, kr, proj, states, dry, cos_t, sin_t, decays)
    dwt_ret = _mm_tn(dp_ret, h, "dw_in_ret", 1024, 1024, tk)
    dwt_gate, (got_ro, got_ao, got_o) = _mm_tn(
        dp_gate, h, "dw_in_gate", 1024, 1024, tk,
        _AllToAll([per_dev(dw_ro, r_sq), per_dev(dw_ao, r_sq), per_dev(dw_o, r_sq)]))
    dwt_attn = _mm_tn(dp_attn, h, "dw_in_attn", 1280, 1024, tk)
    in_pieces = [(0, 0, 2048), (1, 0, 1024), (2, 0, 1280), (1, 1024, 3072)]
    (dx, d_ln1), (got_in,) = _dx_call(xs, dx1, dp_ret, dp_gate, dp_attn, ln1_g, wt_in, tm,
                                      _RowScatter([dwt_ret, dwt_gate, dwt_attn], in_pieces, c_in))
    db_in = jnp.concatenate([db_ret, db_gate[:, 0:1024], db_attn, db_gate[:, 1024:3072]], axis=1)
    small = [d_ln1, db_in, d_gn, d_sinks[:, 0:Q_HEADS], d_ln2, d_lnf]
    (got_small,) = _exchange_call(_AllGather([_pack_small(small, loss[0, 0])]), "gather_small")

    transposed = ("w_in", "w_ffn_gate", "w_ffn_up")
    res = {}
    res["w_in"] = _adamw_call(got_in, tr_shard(w_in), tr_shard(m_w_in), tr_shard(v_w_in), "adamw_w_in", 160)
    res["w_ffn_gate"] = _adamw_call(got_g, tr_shard(w_ffn_gate), tr_shard(m_w_ffn_gate), tr_shard(v_w_ffn_gate),
                                    "adamw_ffn_gate", 176)
    res["w_ffn_up"] = _adamw_call(got_u, tr_shard(w_ffn_up), tr_shard(m_w_ffn_up), tr_shard(v_w_ffn_up),
                                  "adamw_ffn_up", 176)
    res["w_ret_out"] = _adamw_call(got_ro, w_ret_out[0], m_w_ret_out[0], v_w_ret_out[0], "adamw_ret_out", r_sq)
    res["w_attn_out"] = _adamw_call(got_ao, w_attn_out[0], m_w_attn_out[0], v_w_attn_out[0], "adamw_attn_out", r_sq)
    res["w_out"] = _adamw_call(got_o, w_out[0], m_w_out[0], v_w_out[0], "adamw_out", r_sq)
    res["w_ffn_down"] = _adamw_call(got_d, w_ffn_down[0], m_w_ffn_down[0], v_w_ffn_down[0], "adamw_ffn_down", 176)
    small_names = ["ln1_g", "b_in", "ret_norm_g", "attn_sinks", "ln2_g", "lnf_g"]
    small_res, loss_row = _adamw_small_call(
        got_small, [ln1_g, b_in, ret_norm_g, attn_sinks, ln2_g, lnf_row],
        [m_ln1_g, m_b_in, m_ret_norm_g, m_attn_sinks, m_ln2_g, m_lnf_g.reshape(1, D)],
        [v_ln1_g, v_b_in, v_ret_norm_g, v_attn_sinks, v_ln2_g, v_lnf_g.reshape(1, D)])
    for i, nm in enumerate(small_names):
        res[nm] = [small_res[kind][i] for kind in range(4)]

    order = ["ln1_g", "w_in", "b_in", "ret_norm_g", "w_ret_out", "attn_sinks", "w_attn_out", "w_out", "ln2_g",
             "w_ffn_gate", "w_ffn_up", "w_ffn_down", "lnf_g"]
    outs = [loss_row[0, 0], dx[None]]
    for kind in range(4):
        for nm in order:
            val = res[nm][kind]
            if nm in transposed:
                val = val.T
            outs.append(val[None] if nm.startswith("w_") else val.reshape(D) if nm == "lnf_g" else val)
    return tuple(outs)
```

```python
import math

import numpy as np
import jax
import jax.numpy as jnp
from jax import lax
from jax.experimental import pallas as pl
from jax.experimental.pallas import tpu as pltpu

F32 = jnp.float32
_MXU = jnp.bfloat16
_STORE = jnp.bfloat16

N_DEV = 8
D = 1024
RET_HEADS, RET_DK, RET_DV = 4, 128, 256
BLK = 128
Q_HEADS, KV_HEADS, HEAD_DIM = 16, 2, 64
GROUP = Q_HEADS // KV_HEADS
D_FF = 2816
N_RET, N_GATE, N_ATTN = 2048, 3072, 1280
D_IN = N_RET + N_GATE + N_ATTN
ROPE_THETA = 10000.0
EPS = 1e-6
RET_SCALE = RET_DK ** -0.5
ATTN_SCALE = HEAD_DIM ** -0.5
LR, B1, B2, ADAM_EPS, WD, STEP = 0.001, 0.9, 0.999, 1e-08, 0.01, 10
VMEM_LIMIT_MB = 56
MESH = pl.DeviceIdType.MESH


def _dot(a, b):
    return jnp.dot(a.astype(_MXU), b.astype(_MXU), preferred_element_type=F32)


def _dot_nt(a, b):
    return lax.dot_general(a.astype(_MXU), b.astype(_MXU), (((1,), (1,)), ((), ())), preferred_element_type=F32)


def _dot_tn(a, b):
    return lax.dot_general(a.astype(_MXU), b.astype(_MXU), (((0,), (0,)), ((), ())), preferred_element_type=F32)


def _sigmoid(x):
    return 1.0 / (1.0 + jnp.exp(-x))


def _cparams(n_axes, big=False):
    kw = dict(dimension_semantics=("arbitrary",) * n_axes)
    if big:
        kw["vmem_limit_bytes"] = VMEM_LIMIT_MB * 2**20
    return pltpu.CompilerParams(**kw)


def _rows(tm, width, col=0):
    return pl.BlockSpec((tm, width), lambda i: (i, col))


def _const(shape):
    nd = len(shape)
    return pl.BlockSpec(shape, lambda *_: (0,) * nd, pipeline_mode=pl.Buffered(1))


def _acc(width):
    return pl.BlockSpec((1, width), lambda *_: (0, 0))


def _sds(shape, dtype):
    return jax.ShapeDtypeStruct(shape, dtype)


def _swap_halves(x, half):
    w = x.shape[-1]
    if 2 * half == w:
        return pltpu.roll(x, half, 1)
    lane = lax.broadcasted_iota(jnp.int32, x.shape, 1)
    return jnp.where(lane % (2 * half) < half, pltpu.roll(x, w - half, 1), pltpu.roll(x, half, 1))


def _rope_tables(seq):
    lane = jnp.arange(128, dtype=jnp.int32)
    ret_freq = ROPE_THETA ** (-(lane % 64).astype(F32) / 64)
    attn_freq = ROPE_THETA ** (-(lane % 32).astype(F32) / 32)
    ang = jnp.arange(seq, dtype=jnp.int32).astype(F32)[:, None] * jnp.where(lane < 64, ret_freq, attn_freq)[None, :]
    return jnp.cos(ang), jnp.sin(ang)


def _ret_rope(cos, sin):
    low = lax.broadcasted_iota(jnp.int32, cos.shape, 1) < RET_DK // 2
    return jnp.where(low, cos, pltpu.roll(cos, RET_DK // 2, 1)), jnp.where(low, -sin, pltpu.roll(sin, RET_DK // 2, 1))


def _attn_rope(cos, sin):
    lane = lax.broadcasted_iota(jnp.int32, cos.shape, 1)
    half = HEAD_DIM // 2

    def spread(t):
        t = pltpu.roll(t, 64, 1)
        t = jnp.where(lane < half, t, pltpu.roll(t, half, 1))
        return jnp.where(lane < HEAD_DIM, t, pltpu.roll(t, HEAD_DIM, 1))

    return spread(cos), jnp.where(lane % HEAD_DIM < half, -spread(sin), spread(sin))


def _retention_decays():
    log_gamma = np.log1p(-np.exp2(-5.0 - np.arange(RET_HEADS, dtype=np.float32))).astype(np.float32)
    idx = np.arange(BLK, dtype=np.float32)
    rel = idx[:, None] - idx[None, :]
    intra = np.where(rel[None] >= 0, np.exp(log_gamma[:, None, None] * np.maximum(rel, 0.0)[None]), 0.0)
    q_decay = np.exp(log_gamma[:, None] * (idx + 1.0))[:, :, None]
    k_decay = np.exp(log_gamma[:, None] * (BLK - 1.0 - idx))[:, :, None]
    chunk_decay = [float(np.exp(np.float32(lg * BLK))) for lg in log_gamma]
    return (jnp.asarray(intra, F32), jnp.asarray(q_decay, F32), jnp.asarray(k_decay, F32), chunk_decay)


def _position():
    return lax.axis_index("x"), lax.axis_index("y"), lax.axis_index("c")


def _slot(px, py, pc):
    return 4 * px + 2 * py + pc


class _AllGather:
    def __init__(self, blocks):
        self.blocks = list(blocks)
        nb = len(self.blocks)
        self.out_shape = [_sds((N_DEV,) + b.shape, b.dtype) for b in self.blocks]
        self.scratch = [pltpu.SemaphoreType.DMA((nb, 7)), pltpu.SemaphoreType.DMA((nb, 7)),
                        pltpu.SemaphoreType.DMA((nb,))]

    def phases(self, ins, outs, send_sems, recv_sems, local_sems):
        nb = len(ins)
        x, y, c = _position()
        me, sibling = (x, y, c), (x, y, 1 - c)
        chips = [(1 - x, y), (x, 1 - y), (1 - x, 1 - y)]

        def copy(b, k, block, to, src=None):
            dst = outs[b].at[_slot(*block)]
            return pltpu.make_async_remote_copy(
                src_ref=dst if src is None else src, dst_ref=dst, send_sem=send_sems.at[b, k],
                recv_sem=recv_sems.at[b, k], device_id=to, device_id_type=MESH)

        def own(b):
            return pltpu.make_async_copy(ins[b], outs[b].at[_slot(*me)], local_sems.at[b])

        def first(b):
            return [copy(b, 0, me, sibling, src=ins[b])] + [
                copy(b, 1 + j, me, (*chip, c), src=ins[b]) for j, chip in enumerate(chips)]

        def start():
            for b in range(nb):
                own(b).start()
                for cp in first(b):
                    cp.start()

        def forward():
            for b in range(nb):
                for j, chip in enumerate(chips):
                    copy(b, 1 + j, (*chip, c), me).wait_recv()
                    copy(b, 4 + j, (*chip, c), sibling).start()

        def finish():
            for b in range(nb):
                copy(b, 0, sibling, me).wait_recv()
                for j, chip in enumerate(chips):
                    copy(b, 4 + j, (*chip, 1 - c), me).wait_recv()
            for b in range(nb):
                for cp in first(b):
                    cp.wait_send()
                for j, chip in enumerate(chips):
                    copy(b, 4 + j, (*chip, c), sibling).wait_send()
                own(b).wait()

        return start, forward, finish


class _AllToAll:
    def __init__(self, blocks):
        self.blocks = list(blocks)
        nb = len(self.blocks)
        self.out_shape = [_sds(b.shape, b.dtype) for b in self.blocks]
        self.scratch = [pltpu.SemaphoreType.DMA((nb, 7)), pltpu.SemaphoreType.DMA((nb, 7)),
                        pltpu.SemaphoreType.DMA((nb,))]

    def phases(self, ins, outs, send_sems, recv_sems, local_sems):
        nb = len(ins)
        x, y, c = _position()
        flip = lambda v, bit: 1 - v if bit else v
        peers = [(flip(x, k >> 2 & 1), flip(y, k >> 1 & 1), flip(c, k & 1)) for k in range(1, N_DEV)]

        def copy(b, k, peer, landed=False):
            return pltpu.make_async_remote_copy(
                src_ref=ins[b].at[_slot(*peer)], dst_ref=outs[b].at[_slot(*peer) if landed else _slot(x, y, c)],
                send_sem=send_sems.at[b, k], recv_sem=recv_sems.at[b, k], device_id=peer, device_id_type=MESH)

        def own(b):
            return pltpu.make_async_copy(ins[b].at[_slot(x, y, c)], outs[b].at[_slot(x, y, c)], local_sems.at[b])

        def start():
            for b in range(nb):
                own(b).start()
                for k, peer in enumerate(peers):
                    copy(b, k, peer).start()

        def forward():
            pass

        def finish():
            for b in range(nb):
                for k, peer in enumerate(peers):
                    copy(b, k, peer, landed=True).wait_recv()
            for b in range(nb):
                for k, peer in enumerate(peers):
                    copy(b, k, peer).wait_send()
                own(b).wait()

        return start, forward, finish


class _RowScatter:
    def __init__(self, arrays, pieces, n):
        self.blocks = list(arrays)
        self.pieces, self.n = pieces, n
        self.out_shape = [_sds((N_DEV, n, D), arrays[0].dtype)]
        self.scratch = [pltpu.SemaphoreType.DMA((N_DEV,)), pltpu.SemaphoreType.DMA((N_DEV,)), pltpu.SemaphoreType.DMA]

    def _parts(self, k):
        lo, hi, pos, res = k * self.n, (k + 1) * self.n, 0, []
        for arr, first, last in self.pieces:
            a, b = max(lo, pos), min(hi, pos + last - first)
            if a < b:
                res.append((arr, first + a - pos, b - a, a - lo))
            pos += last - first
        return res

    def phases(self, ins, outs, send_sems, recv_sems, local_sem):
        (out,) = outs
        x, y, c = _position()
        me = _slot(x, y, c)

        def start():
            for k in range(N_DEV):
                dist = jnp.bitwise_xor(me, k)

                @pl.when(me != k)
                def _():
                    for arr, first, rows, at in self._parts(k):
                        pltpu.make_async_remote_copy(
                            src_ref=ins[arr].at[pl.ds(first, rows)], dst_ref=out.at[me, pl.ds(at, rows)],
                            send_sem=send_sems.at[dist], recv_sem=recv_sems.at[dist],
                            device_id=(k >> 2 & 1, k >> 1 & 1, k & 1), device_id_type=MESH).start()

                @pl.when(me == k)
                def _():
                    for arr, first, rows, at in self._parts(k):
                        pltpu.make_async_copy(ins[arr].at[pl.ds(first, rows)], out.at[me, pl.ds(at, rows)], local_sem).start()

        def forward():
            pass

        def whole_block(dist):
            return pltpu.make_async_remote_copy(
                src_ref=out.at[me], dst_ref=out.at[jnp.bitwise_xor(me, dist)], send_sem=send_sems.at[dist],
                recv_sem=recv_sems.at[dist], device_id=(x, y, c), device_id_type=MESH)

        def finish():
            for dist in range(1, N_DEV):
                whole_block(dist).wait_recv()
            for dist in range(1, N_DEV):
                whole_block(dist).wait_send()
            pltpu.make_async_copy(out.at[me], out.at[me], local_sem).wait()

        return start, forward, finish


def _call(body, *, name, grid, in_specs, out_specs, out_shape, args, scratch_shapes=(), big=False, exchange=None):
    params = _cparams(len(grid), big)
    if exchange is None:
        return pl.pallas_call(body, name=name, grid=grid, in_specs=in_specs, out_specs=out_specs, out_shape=out_shape,
                              scratch_shapes=list(scratch_shapes), compiler_params=params)(*args)
    n_in, n_out, n_scr = len(in_specs), len(out_specs), len(scratch_shapes)
    nb, nb_out = len(exchange.blocks), len(exchange.out_shape)
    steps = math.prod(grid)

    def carried(*refs):
        pos = 0
        parts = []
        for n in (n_in, nb, n_out, nb_out, n_scr, len(exchange.scratch)):
            parts.append(refs[pos:pos + n])
            pos += n
        ins, x_ins, outs, x_outs, scr, sems = parts
        step = pl.program_id(0)
        for axis in range(1, len(grid)):
            step = step * grid[axis] + pl.program_id(axis)
        start, forward, finish = exchange.phases(x_ins, x_outs, *sems)
        pl.when(step == 0)(start)
        body(*ins, *outs, *scr)

        @pl.when(step == steps - 1)
        def _():
            forward()
            finish()

    any_spec = pl.BlockSpec(memory_space=pl.ANY)
    res = pl.pallas_call(
        carried, name=name, grid=grid, in_specs=list(in_specs) + [any_spec] * nb,
        out_specs=list(out_specs) + [any_spec] * nb_out, out_shape=list(out_shape) + exchange.out_shape,
        scratch_shapes=list(scratch_shapes) + exchange.scratch, compiler_params=params)(*args, *exchange.blocks)
    return res[:n_out], res[n_out:]


def _exchange_call(exchange, name):
    nb = len(exchange.blocks)

    def body(*refs):
        start, forward, finish = exchange.phases(refs[:nb], refs[nb:2 * nb], *refs[2 * nb:])
        start()
        forward()
        finish()

    any_spec = pl.BlockSpec(memory_space=pl.ANY)
    return pl.pallas_call(body, name=name, in_specs=[any_spec] * nb, out_specs=[any_spec] * nb,
                          out_shape=exchange.out_shape, scratch_shapes=exchange.scratch)(*exchange.blocks)


def _ln_call(x, g, tm, exchange):
    s = x.shape[0]

    def body(x_ref, g_ref, h_ref):
        xv = x_ref[...]
        r = lax.rsqrt(jnp.mean(xv * xv, axis=-1, keepdims=True) + EPS)
        h_ref[...] = ((xv * r) * g_ref[...]).astype(h_ref.dtype)

    return _call(body, name="ln1", grid=(s // tm,), in_specs=[_rows(tm, D), _acc(D)], out_specs=[_rows(tm, D)],
                 out_shape=[_sds((s, D), _MXU)], args=(x, g), exchange=exchange)


PROJ_TILE = 256


def _proj_source_tile(j):
    gate_end, attn_end, end = 3072 // PROJ_TILE, 4352 // PROJ_TILE, 6400 // PROJ_TILE
    n_gates = end - attn_end
    return jnp.where(j < gate_end, j, jnp.where(j < gate_end + n_gates, j + (attn_end - gate_end), j - n_gates))


def _proj_call(a, wt, bias, exchange):
    s, k = a.shape
    n = wt.shape[0]
    rows = min(1024, s)

    def body(a_ref, w_ref, b_ref, o_ref):
        for r in range(0, s, rows):
            o_ref[r:r + rows, :] = _dot_nt(a_ref[r:r + rows, :], w_ref[...]) + b_ref[...]

    return _call(body, name="proj", grid=(n // PROJ_TILE,),
                 in_specs=[_const((s, k)), pl.BlockSpec((PROJ_TILE, k), lambda j: (_proj_source_tile(j), 0)),
                           pl.BlockSpec((1, PROJ_TILE), lambda j: (0, _proj_source_tile(j)))],
                 out_specs=[pl.BlockSpec((s, PROJ_TILE), lambda j: (0, j))], out_shape=[_sds((s, n), F32)],
                 args=(a, wt, bias), big=True, exchange=exchange)


def _mm_tn(a, b, name, tm, tn, tk, exchange=None):
    s, m = a.shape
    n = b.shape[1]
    last = s // tk - 1

    def body(a_ref, b_ref, o_ref, acc):
        k = pl.program_id(2)
        part = _dot_tn(a_ref[...], b_ref[...])

        @pl.when(k == 0)
        def _():
            acc[...] = part

        @pl.when(k > 0)
        def _():
            acc[...] += part

        @pl.when(k == last)
        def _():
            o_ref[...] = acc[...].astype(o_ref.dtype)

    res = _call(body, name=name, grid=(m // tm, n // tn, s // tk),
                in_specs=[pl.BlockSpec((tk, tm), lambda i, j, k: (k, i)), pl.BlockSpec((tk, tn), lambda i, j, k: (k, j))],
                out_specs=[pl.BlockSpec((tm, tn), lambda i, j, k: (i, j))], out_shape=[_sds((m, n), _MXU)],
                scratch_shapes=[pltpu.VMEM((tm, tn), F32)], args=(a, b), big=True, exchange=exchange)
    return res[0] if exchange is None else (res[0][0], res[1])


RET_CHUNKS = 2


def _ret_fwd_call(proj, cos, sin, decays):
    s = proj.shape[0]
    nblk = s // BLK
    per = min(RET_CHUNKS, nblk)
    rows = per * BLK
    intra, q_decay, k_decay, chunk_decay = decays

    def body(rq_ref, rk_ref, rv_ref, cos_ref, sin_ref, intra_ref, qd_ref, kd_ref,
             ry_ref, qr_ref, kr_ref, st_ref, state):
        @pl.when(pl.program_id(0) == 0)
        def _():
            state[...] = jnp.zeros_like(state)

        for c in range(per):
            rc = slice(c * BLK, (c + 1) * BLK)
            cos_v, sin_v = _ret_rope(cos_ref[rc, :], sin_ref[rc, :])
            for h in range(RET_HEADS):
                hk = slice(h * RET_DK, (h + 1) * RET_DK)
                hv = slice(h * RET_DV, (h + 1) * RET_DV)
                q, k = rq_ref[rc, hk], rk_ref[rc, hk]
                qr = (q * cos_v + _swap_halves(q, RET_DK // 2) * sin_v) * RET_SCALE
                kr = k * cos_v + _swap_halves(k, RET_DK // 2) * sin_v
                v = rv_ref[rc, hv]
                s_h = state[h]
                st_ref[c, h] = s_h.astype(st_ref.dtype)
                scores = _dot_nt(qr, kr) * intra_ref[h]
                ry_ref[rc, hv] = _dot(scores, v) + _dot(qr, s_h) * qd_ref[h]
                state[h] = s_h * chunk_decay[h] + _dot_tn(kr * kd_ref[h], v)
                qr_ref[rc, hk] = qr.astype(qr_ref.dtype)
                kr_ref[rc, hk] = kr.astype(kr_ref.dtype)

    blk = lambda w, c: pl.BlockSpec((rows, w), lambda n: (n, c))
    return _call(body, name="ret_fwd", grid=(nblk // per,),
                 in_specs=[blk(512, 0), blk(512, 1), blk(1024, 1), blk(128, 0), blk(128, 0),
                           _const(intra.shape), _const(q_decay.shape), _const(k_decay.shape)],
                 out_specs=[blk(1024, 0), blk(512, 0), blk(512, 0),
                            pl.BlockSpec((per, RET_HEADS, RET_DK, RET_DV), lambda n: (n, 0, 0, 0))],
                 out_shape=[_sds((s, 1024), F32), _sds((s, 512), _MXU), _sds((s, 512), _MXU),
                            _sds((nblk, RET_HEADS, RET_DK, RET_DV), _MXU)],
                 scratch_shapes=[pltpu.VMEM((RET_HEADS, RET_DK, RET_DV), F32)],
                 args=(proj, proj, proj, cos, sin, intra, q_decay, k_decay))


def _ret_bwd_call(qr, kr, proj, states, dry, cos, sin, decays, exchange):
    s = qr.shape[0]
    nblk = s // BLK
    per = min(RET_CHUNKS, nblk)
    rows = per * BLK
    steps = nblk // per
    intra, q_decay, k_decay, chunk_decay = decays

    def body(qr_ref, kr_ref, rv_ref, st_ref, dry_ref, cos_ref, sin_ref, intra_ref, qd_ref, kd_ref,
             dp_ref, db_ref, dstate):
        @pl.when(pl.program_id(0) == 0)
        def _():
            dstate[...] = jnp.zeros_like(dstate)
            db_ref[...] = jnp.zeros_like(db_ref)

        for c in reversed(range(per)):
            rc = slice(c * BLK, (c + 1) * BLK)
            cos_v, sin_v = _ret_rope(cos_ref[rc, :], sin_ref[rc, :])
            for h in range(RET_HEADS):
                hk = slice(h * RET_DK, (h + 1) * RET_DK)
                hv = slice(h * RET_DV, (h + 1) * RET_DV)
                q, k, v, d_out = qr_ref[rc, hk], kr_ref[rc, hk], rv_ref[rc, hv], dry_ref[rc, hv]
                d_next = dstate[h]
                scores = _dot_nt(q, k) * intra_ref[h]
                d_scores = _dot_nt(d_out, v) * intra_ref[h]
                d_cross = d_out * qd_ref[h]
                dq = _dot(d_scores, k) + _dot_nt(d_cross, st_ref[c, h])
                dk = _dot_tn(d_scores, q) + _dot_nt(v, d_next) * kd_ref[h]
                dv = _dot_tn(scores, d_out) + _dot(k.astype(F32) * kd_ref[h], d_next)
                dstate[h] = d_next * chunk_decay[h] + _dot_tn(q, d_cross)
                dq = (dq * cos_v - _swap_halves(dq, RET_DK // 2) * sin_v) * RET_SCALE
                dk = dk * cos_v - _swap_halves(dk, RET_DK // 2) * sin_v
                kcols = slice(512 + h * RET_DK, 512 + (h + 1) * RET_DK)
                vcols = slice(1024 + h * RET_DV, 1024 + (h + 1) * RET_DV)
                dp_ref[rc, hk] = dq.astype(dp_ref.dtype)
                dp_ref[rc, kcols] = dk.astype(dp_ref.dtype)
                dp_ref[rc, vcols] = dv.astype(dp_ref.dtype)
                db_ref[:, hk] += jnp.sum(dq, axis=0, keepdims=True)
                db_ref[:, kcols] += jnp.sum(dk, axis=0, keepdims=True)
                db_ref[:, vcols] += jnp.sum(dv, axis=0, keepdims=True)

    rblk = lambda w, c: pl.BlockSpec((rows, w), lambda n: (steps - 1 - n, c))
    return _call(body, name="ret_bwd", grid=(steps,),
                 in_specs=[rblk(512, 0), rblk(512, 0), rblk(1024, 1),
                           pl.BlockSpec((per, RET_HEADS, RET_DK, RET_DV), lambda n: (steps - 1 - n, 0, 0, 0)),
                           rblk(1024, 0), rblk(128, 0), rblk(128, 0),
                           _const(intra.shape), _const(q_decay.shape), _const(k_decay.shape)],
                 out_specs=[rblk(N_RET, 0), _acc(N_RET)],
                 out_shape=[_sds((s, N_RET), _MXU), _sds((1, N_RET), F32)],
                 scratch_shapes=[pltpu.VMEM((RET_HEADS, RET_DK, RET_DV), F32)],
                 args=(qr, kr, proj, states, dry, cos, sin, intra, q_decay, k_decay), exchange=exchange)


def _both_halves(x, g):
    lane = lax.broadcasted_iota(jnp.int32, x.shape, 1)
    keep = lane < HEAD_DIM if g == 0 else lane >= HEAD_DIM
    return jnp.where(keep, x, pltpu.roll(x, HEAD_DIM, 1))


def _stack_heads(ref, g):
    lane = lax.broadcasted_iota(jnp.int32, (BLK, 128), 1)
    pieces = []
    for j in range(g * 4, g * 4 + 4):
        chunk = ref[:, j * 128:(j + 1) * 128]
        pieces += [jnp.where(lane < HEAD_DIM, chunk, jnp.zeros_like(chunk)),
                   jnp.where(lane >= HEAD_DIM, chunk, jnp.zeros_like(chunk))]
    return jnp.concatenate(pieces, axis=0)


def _window_bias(first_block):
    kj = lax.broadcasted_iota(jnp.int32, (2 * BLK, BLK), 0)
    qi = lax.broadcasted_iota(jnp.int32, (2 * BLK, BLK), 1)
    first_key = jnp.where(first_block, BLK, 0)
    seen = (kj > qi) & (kj <= qi + BLK) & (kj >= first_key)
    return jnp.where(seen, 0.0, -1e30)


def _sink_softmax(scores, sink):
    m = jnp.maximum(jnp.max(scores, axis=0, keepdims=True), sink)
    e = jnp.exp(scores - m)
    e_sink = jnp.exp(sink - m)
    return e, e_sink, 1.0 / (jnp.sum(e, axis=0, keepdims=True) + e_sink)


def _head_pair(stacked_t, jj):
    even = stacked_t[0:HEAD_DIM, 2 * jj * BLK:(2 * jj + 1) * BLK]
    odd = stacked_t[HEAD_DIM:128, (2 * jj + 1) * BLK:(2 * jj + 2) * BLK]
    return jnp.concatenate([even, odd], axis=0).T


def _attn_fwd_call(proj, sinks, cos, sin):
    s = proj.shape[0]
    nblk = s // BLK

    def body(sink_ref, q_ref, k_ref, v_ref, cos_ref, sin_ref, ay_ref, qr_ref, kr_ref, vb_ref,
             kwin, vwin, bias, s_scr, p_scr):
        n = pl.program_id(0)

        @pl.when(n == 0)
        def _():
            kwin[...] = jnp.zeros_like(kwin)
            vwin[...] = jnp.zeros_like(vwin)

        @pl.when(n > 0)
        def _():
            kwin[0:BLK] = kwin[BLK:2 * BLK]
            vwin[0:BLK] = vwin[BLK:2 * BLK]

        cos_v, sin_v = _attn_rope(cos_ref[...], sin_ref[...])
        k = k_ref[...]
        kr = (k * cos_v + _swap_halves(k, HEAD_DIM // 2) * sin_v).astype(kwin.dtype)
        kwin[BLK:2 * BLK] = kr
        vwin[BLK:2 * BLK] = v_ref[...].astype(vwin.dtype)
        kr_ref[...] = kr
        vb_ref[...] = vwin[BLK:2 * BLK]
        for j in range(Q_HEADS // 2):
            cols = slice(j * 128, (j + 1) * 128)
            q = q_ref[:, cols]
            qr_ref[:, cols] = ((q * cos_v + _swap_halves(q, HEAD_DIM // 2) * sin_v) * ATTN_SCALE).astype(qr_ref.dtype)
        bias[...] = _window_bias(n == 0)
        for g in range(KV_HEADS):
            kg = _both_halves(kwin[...], g)
            vg_t = _both_halves(vwin[...], g).astype(F32).T
            s_scr[...] = _dot_nt(kg, _stack_heads(qr_ref, g))
            for i in range(GROUP):
                cols = slice(i * BLK, (i + 1) * BLK)
                e, _, inv = _sink_softmax(s_scr[:, cols] + bias[...], sink_ref[0, g * GROUP + i])
                p_scr[:, cols] = (e * inv).astype(p_scr.dtype)
            out_t = _dot(vg_t, p_scr[...])
            for jj in range(4):
                j = g * 4 + jj
                ay_ref[:, j * 128:(j + 1) * 128] = _head_pair(out_t, jj).astype(ay_ref.dtype)

    blk = lambda w, c: pl.BlockSpec((BLK, w), lambda n: (n, c))
    off = (N_RET + N_GATE) // 128
    wide = (2 * BLK, GROUP * BLK)
    return _call(body, name="attn_fwd", grid=(nblk,),
                 in_specs=[pl.BlockSpec(memory_space=pltpu.SMEM), blk(1024, off // 8), blk(128, off + 8), blk(128, off + 9),
                           blk(128, 0), blk(128, 0)],
                 out_specs=[blk(1024, 0), blk(1024, 0), blk(128, 0), blk(128, 0)],
                 out_shape=[_sds((s, 1024), _MXU), _sds((s, 1024), _MXU), _sds((s, 128), _MXU), _sds((s, 128), _MXU)],
                 scratch_shapes=[pltpu.VMEM((2 * BLK, 128), _MXU), pltpu.VMEM((2 * BLK, 128), _MXU),
                                 pltpu.VMEM((2 * BLK, BLK), F32), pltpu.VMEM(wide, F32), pltpu.VMEM(wide, _MXU)],
                 args=(sinks, proj, proj, proj, cos, sin))


def _attn_bwd_call(qr, kr, vb, day, sinks, cos, sin, exchange):
    s = qr.shape[0]
    nblk = s // BLK

    def body(sink_ref, q_ref, kc_ref, kp_ref, vc_ref, vp_ref, do_ref, cos_ref, sin_ref, cosp_ref, sinp_ref,
             dp_ref, dsink_ref, db_ref, bias, s_scr, dp_scr, p_scr, ds_scr, dq_held, kv_held, kv_prev, kv_new):
        n = pl.program_id(0)
        valid = (n < nblk).astype(F32)

        @pl.when(n == 0)
        def _():
            dsink_ref[...] = jnp.zeros_like(dsink_ref)
            db_ref[...] = jnp.zeros_like(db_ref)

        @pl.when(n >= 1)
        def _():
            dp_ref[:, 0:1024] = dq_held[...]

        cos_v, sin_v = _attn_rope(cos_ref[...], sin_ref[...])
        bias[...] = _window_bias(n == 0)
        lane1 = lax.broadcasted_iota(jnp.int32, (1, 128), 1)
        kwin = jnp.concatenate([kp_ref[...], kc_ref[...]], axis=0)
        vwin = jnp.concatenate([vp_ref[...], vc_ref[...]], axis=0)
        dk_heads, dv_heads = [], []
        dsink = jnp.zeros((1, 128), F32)
        for g in range(KV_HEADS):
            kg = _both_halves(kwin, g)
            vg = _both_halves(vwin, g)
            q_all = _stack_heads(q_ref, g)
            do_all = _stack_heads(do_ref, g)
            s_scr[...] = _dot_nt(kg, q_all)
            dp_scr[...] = _dot_nt(vg, do_all)
            for i in range(GROUP):
                head = g * GROUP + i
                cols = slice(i * BLK, (i + 1) * BLK)
                e, e_sink, inv = _sink_softmax(s_scr[:, cols] + bias[...], sink_ref[0, head])
                p = e * inv
                dp = dp_scr[:, cols]
                delta = jnp.sum(p * dp, axis=0, keepdims=True)
                p_scr[:, cols] = p.astype(p_scr.dtype)
                ds_scr[:, cols] = (p * (dp - delta)).astype(ds_scr.dtype)
                dsink = dsink + jnp.where(lane1 == head, -jnp.sum(e_sink * inv * delta, axis=1, keepdims=True), 0.0)
            dv_both = _dot(p_scr[...], do_all)
            dk_both = _dot(ds_scr[...], q_all)
            dv_heads.append(dv_both + pltpu.roll(dv_both, HEAD_DIM, 1))
            dk_heads.append(dk_both + pltpu.roll(dk_both, HEAD_DIM, 1))
            dq_t = _dot(kg.astype(F32).T, ds_scr[...])
            for jj in range(4):
                cols = slice((g * 4 + jj) * 128, (g * 4 + jj + 1) * 128)
                dq = _head_pair(dq_t, jj)
                dq = (dq * cos_v - _swap_halves(dq, HEAD_DIM // 2) * sin_v) * ATTN_SCALE
                dq_held[:, cols] = dq.astype(dq_held.dtype)
                db_ref[:, cols] += jnp.sum(dq, axis=0, keepdims=True) * valid
        dsink_ref[...] += dsink * valid
        lane2 = lax.broadcasted_iota(jnp.int32, (2 * BLK, 128), 1)
        dk_all = jnp.where(lane2 < HEAD_DIM, dk_heads[0], dk_heads[1])
        dv_all = jnp.where(lane2 < HEAD_DIM, dv_heads[0], dv_heads[1])
        kv_prev[:, 0:128] = dk_all[0:BLK] * valid
        kv_prev[:, 128:256] = dv_all[0:BLK] * valid
        kv_new[:, 0:128] = dk_all[BLK:2 * BLK]
        kv_new[:, 128:256] = dv_all[BLK:2 * BLK]

        @pl.when(n >= 1)
        def _():
            dkv = kv_held[...] + kv_prev[...]
            dk = dkv[:, 0:128]
            cos_p, sin_p = _attn_rope(cosp_ref[...], sinp_ref[...])
            dk = dk * cos_p - _swap_halves(dk, HEAD_DIM // 2) * sin_p
            dv = dkv[:, 128:256]
            dp_ref[:, 1024:1152] = dk.astype(dp_ref.dtype)
            dp_ref[:, 1152:1280] = dv.astype(dp_ref.dtype)
            db_ref[:, 1024:1152] += jnp.sum(dk, axis=0, keepdims=True)
            db_ref[:, 1152:1280] += jnp.sum(dv, axis=0, keepdims=True)

        kv_held[...] = kv_new[...]

    blk = lambda w: pl.BlockSpec((BLK, w), lambda n: (jnp.minimum(n, nblk - 1), 0))
    pblk = lambda w: pl.BlockSpec((BLK, w), lambda n: (jnp.maximum(n - 1, 0), 0))
    wide = (2 * BLK, GROUP * BLK)
    return _call(body, name="attn_bwd", grid=(nblk + 1,),
                 in_specs=[pl.BlockSpec(memory_space=pltpu.SMEM), blk(1024), blk(128), pblk(128), blk(128), pblk(128),
                           blk(1024), blk(128), blk(128), pblk(128), pblk(128)],
                 out_specs=[pblk(N_ATTN), _acc(128), _acc(N_ATTN)],
                 out_shape=[_sds((s, N_ATTN), _MXU), _sds((1, 128), F32), _sds((1, N_ATTN), F32)],
                 scratch_shapes=[pltpu.VMEM((2 * BLK, BLK), F32), pltpu.VMEM(wide, F32), pltpu.VMEM(wide, F32),
                                 pltpu.VMEM(wide, _MXU), pltpu.VMEM(wide, _MXU), pltpu.VMEM((BLK, 1024), _MXU),
                                 pltpu.VMEM((BLK, 256), F32), pltpu.VMEM((BLK, 256), F32), pltpu.VMEM((BLK, 256), F32)],
                 args=(sinks, qr, kr, kr, vb, vb, day, cos, sin, cos, sin), exchange=exchange)


def _group_norm(y):
    mu = jnp.mean(y, axis=-1, keepdims=True)
    yc = y - mu
    rs = lax.rsqrt(jnp.mean(yc * yc, axis=-1, keepdims=True) + EPS)
    return yc * rs, rs


GATE_COL = N_RET // 1024


def _merge_fwd_call(x, ry, proj, ay, gn_g, w_ro, w_ao, w_o, tm):
    s = x.shape[0]

    def body(x_ref, ry_ref, rg_ref, ga_ref, gb_ref, ay_ref, gn_ref, wro_ref, wao_ref, wo_ref,
             ain_ref, a_ref, b_ref, mg_ref, x1_ref):
        for h in range(RET_HEADS):
            hv = slice(h * RET_DV, (h + 1) * RET_DV)
            yhat, _ = _group_norm(ry_ref[:, hv])
            rg = rg_ref[:, hv]
            ain_ref[:, hv] = ((rg * _sigmoid(rg)) * (yhat * gn_ref[:, hv])).astype(ain_ref.dtype)
        a = _dot(ain_ref[...], wro_ref[...])
        b = _dot(ay_ref[...], wao_ref[...])
        a_ref[...] = a.astype(a_ref.dtype)
        b_ref[...] = b.astype(b_ref.dtype)
        merged = (_sigmoid(ga_ref[...]) * a + _sigmoid(gb_ref[...]) * b).astype(mg_ref.dtype)
        mg_ref[...] = merged
        x1_ref[...] = x_ref[...] + _dot(merged, wo_ref[...])

    return _call(body, name="merge_fwd", grid=(s // tm,),
                 in_specs=[_rows(tm, D), _rows(tm, 1024), _rows(tm, 1024, GATE_COL), _rows(tm, 1024, GATE_COL + 1),
                           _rows(tm, 1024, GATE_COL + 2), _rows(tm, 1024), _acc(1024),
                           _const((D, D)), _const((D, D)), _const((D, D))],
                 out_specs=[_rows(tm, D)] * 5,
                 out_shape=[_sds((s, D), _MXU), _sds((s, D), _STORE), _sds((s, D), _STORE), _sds((s, D), _MXU),
                            _sds((s, D), F32)],
                 args=(x, ry, proj, proj, proj, ay, gn_g, w_ro, w_ao, w_o), big=True)


def _merge_bwd_call(dx1, a, b, ry, proj, gn_g, w_ro, w_ao, w_o, tm, exchange):
    s = dx1.shape[0]

    def body(dx1_ref, a_ref, b_ref, ry_ref, rg_ref, ga_ref, gb_ref, gn_ref, wro_ref, wao_ref, wo_ref,
             da_ref, dbr_ref, dp_ref, day_ref, dry_ref, dbias_ref, dgn_ref):
        @pl.when(pl.program_id(0) == 0)
        def _():
            dbias_ref[...] = jnp.zeros_like(dbias_ref)
            dgn_ref[...] = jnp.zeros_like(dgn_ref)

        d_merged = _dot_nt(dx1_ref[...], wo_ref[...])
        sa, sb = _sigmoid(ga_ref[...]), _sigmoid(gb_ref[...])
        d_a = d_merged * sa
        d_b = d_merged * sb
        da_ref[...] = d_a.astype(da_ref.dtype)
        dbr_ref[...] = d_b.astype(dbr_ref.dtype)
        d_ga = d_merged * a_ref[...] * (sa * (1.0 - sa))
        d_gb = d_merged * b_ref[...] * (sb * (1.0 - sb))
        dp_ref[:, 1024:2048] = d_ga.astype(dp_ref.dtype)
        dp_ref[:, 2048:3072] = d_gb.astype(dp_ref.dtype)
        dbias_ref[:, 1024:2048] += jnp.sum(d_ga, axis=0, keepdims=True)
        dbias_ref[:, 2048:3072] += jnp.sum(d_gb, axis=0, keepdims=True)
        day_ref[...] = _dot_nt(d_b, wao_ref[...]).astype(day_ref.dtype)
        d_ain = _dot_nt(d_a, wro_ref[...])
        for h in range(RET_HEADS):
            hv = slice(h * RET_DV, (h + 1) * RET_DV)
            yhat, rs = _group_norm(ry_ref[:, hv])
            rg = rg_ref[:, hv]
            sg = _sigmoid(rg)
            gn = gn_ref[:, hv]
            d_h = d_ain[:, hv]
            d_rg = d_h * (yhat * gn) * (sg * (1.0 + rg * (1.0 - sg)))
            d_ryn = d_h * (rg * sg)
            dgn_ref[:, hv] += jnp.sum(d_ryn * yhat, axis=0, keepdims=True)
            d_yhat = d_ryn * gn
            dry_ref[:, hv] = (rs * (d_yhat - jnp.mean(d_yhat, axis=-1, keepdims=True)
                                    - yhat * jnp.mean(d_yhat * yhat, axis=-1, keepdims=True))).astype(dry_ref.dtype)
            dp_ref[:, hv] = d_rg.astype(dp_ref.dtype)
            dbias_ref[:, hv] += jnp.sum(d_rg, axis=0, keepdims=True)

    return _call(body, name="merge_bwd", grid=(s // tm,),
                 in_specs=[_rows(tm, D), _rows(tm, D), _rows(tm, D), _rows(tm, 1024), _rows(tm, 1024, GATE_COL),
                           _rows(tm, 1024, GATE_COL + 1), _rows(tm, 1024, GATE_COL + 2), _acc(1024),
                           _const((D, D)), _const((D, D)), _const((D, D))],
                 out_specs=[_rows(tm, D), _rows(tm, D), _rows(tm, N_GATE), _rows(tm, D), _rows(tm, D), _acc(N_GATE),
                            _acc(1024)],
                 out_shape=[_sds((s, D), _MXU), _sds((s, D), _MXU), _sds((s, N_GATE), _MXU), _sds((s, D), _MXU),
                            _sds((s, D), _STORE), _sds((1, N_GATE), F32), _sds((1, 1024), F32)],
                 args=(dx1, a, b, ry, proj, proj, proj, gn_g, w_ro, w_ao, w_o), big=True, exchange=exchange)


def _ffn_fwd_call(x1, target, ln2_g, lnf_g, w_g, w_u, w_d, tm):
    s = x1.shape[0]

    def body(x1_ref, t_ref, g2_ref, gf_ref, wg_ref, wu_ref, wd_ref,
             h2_ref, g_ref, u_ref, f_ref, dx2_ref, loss_ref, dgf_ref):
        @pl.when(pl.program_id(0) == 0)
        def _():
            loss_ref[...] = jnp.zeros_like(loss_ref)
            dgf_ref[...] = jnp.zeros_like(dgf_ref)

        x1v = x1_ref[...]
        r1 = lax.rsqrt(jnp.mean(x1v * x1v, axis=-1, keepdims=True) + EPS)
        h2 = ((x1v * r1) * g2_ref[...]).astype(h2_ref.dtype)
        h2_ref[...] = h2
        g = _dot_nt(h2, wg_ref[...])
        u = _dot_nt(h2, wu_ref[...])
        g_ref[...] = g
        u_ref[...] = u
        f = ((g * _sigmoid(g)) * u).astype(f_ref.dtype)
        f_ref[...] = f
        x2 = x1v + _dot(f, wd_ref[...])
        r2 = lax.rsqrt(jnp.mean(x2 * x2, axis=-1, keepdims=True) + EPS)
        xhat = x2 * r2
        err = xhat * gf_ref[...] - t_ref[...]
        loss_ref[...] += 0.5 * jnp.sum(jnp.mean(err * err, axis=-1, keepdims=True))
        dy = err * (1.0 / D)
        dgf_ref[...] += jnp.sum(dy * xhat, axis=0, keepdims=True)
        dxh = dy * gf_ref[...]
        dx2_ref[...] = r2 * (dxh - xhat * jnp.mean(dxh * xhat, axis=-1, keepdims=True))

    return _call(body, name="ffn_fwd", grid=(s // tm,),
                 in_specs=[_rows(tm, D), _rows(tm, D), _acc(D), _acc(D), _const((D_FF, D)), _const((D_FF, D)),
                           _const((D_FF, D))],
                 out_specs=[_rows(tm, D), _rows(tm, D_FF), _rows(tm, D_FF), _rows(tm, D_FF), _rows(tm, D), _acc(128),
                            _acc(D)],
                 out_shape=[_sds((s, D), _MXU), _sds((s, D_FF), F32), _sds((s, D_FF), F32), _sds((s, D_FF), _MXU),
                            _sds((s, D), F32), _sds((1, 128), F32), _sds((1, D), F32)],
                 args=(x1, target, ln2_g, lnf_g, w_g, w_u, w_d), big=True)


def _ffn_bwd_call(dx2, x1, g, u, ln2_g, w_g, w_u, w_d, tm):
    s = dx2.shape[0]

    def body(dx2_ref, x1_ref, g_ref, u_ref, g2_ref, wg_ref, wu_ref, wd_ref, dx1_ref, dg_ref, du_ref, dg2_ref):
        @pl.when(pl.program_id(0) == 0)
        def _():
            dg2_ref[...] = jnp.zeros_like(dg2_ref)

        dx2v = dx2_ref[...]
        df = _dot_nt(dx2v, wd_ref[...])
        gv, uv = g_ref[...], u_ref[...]
        sg = _sigmoid(gv)
        du = (df * (gv * sg)).astype(du_ref.dtype)
        dg = (df * uv * (sg * (1.0 + gv * (1.0 - sg)))).astype(dg_ref.dtype)
        du_ref[...] = du
        dg_ref[...] = dg
        dh2 = _dot(dg, wg_ref[...]) + _dot(du, wu_ref[...])
        x1v = x1_ref[...]
        r1 = lax.rsqrt(jnp.mean(x1v * x1v, axis=-1, keepdims=True) + EPS)
        xhat = x1v * r1
        dg2_ref[...] += jnp.sum(dh2 * xhat, axis=0, keepdims=True)
        dxh = dh2 * g2_ref[...]
        dx1_ref[...] = dx2v + r1 * (dxh - xhat * jnp.mean(dxh * xhat, axis=-1, keepdims=True))

    return _call(body, name="ffn_bwd", grid=(s // tm,),
                 in_specs=[_rows(tm, D), _rows(tm, D), _rows(tm, D_FF), _rows(tm, D_FF), _acc(D),
                           _const((D_FF, D)), _const((D_FF, D)), _const((D_FF, D))],
                 out_specs=[_rows(tm, D), _rows(tm, D_FF), _rows(tm, D_FF), _acc(D)],
                 out_shape=[_sds((s, D), F32), _sds((s, D_FF), _MXU), _sds((s, D_FF), _MXU), _sds((1, D), F32)],
                 args=(dx2, x1, g, u, ln2_g, w_g, w_u, w_d), big=True)


def _dx_call(x, dx1, dp_ret, dp_gate, dp_attn, ln1_g, w_in, tm, exchange):
    s = x.shape[0]

    def body(x_ref, dx1_ref, dr_ref, dg_ref, da_ref, g1_ref, w_ref, dx_ref, dg1_ref):
        @pl.when(pl.program_id(0) == 0)
        def _():
            dg1_ref[...] = jnp.zeros_like(dg1_ref)

        dh = (_dot(dr_ref[...], w_ref[0:2048, :]) + _dot(dg_ref[:, 0:1024], w_ref[2048:3072, :])
              + _dot(da_ref[...], w_ref[3072:4352, :]) + _dot(dg_ref[:, 1024:3072], w_ref[4352:6400, :]))
        xv = x_ref[...]
        r = lax.rsqrt(jnp.mean(xv * xv, axis=-1, keepdims=True) + EPS)
        xhat = xv * r
        dg1_ref[...] += jnp.sum(dh * xhat, axis=0, keepdims=True)
        dxh = dh * g1_ref[...]
        dx_ref[...] = dx1_ref[...] + r * (dxh - xhat * jnp.mean(dxh * xhat, axis=-1, keepdims=True))

    return _call(body, name="dx", grid=(s // tm,),
                 in_specs=[_rows(tm, D), _rows(tm, D), _rows(tm, N_RET), _rows(tm, N_GATE), _rows(tm, N_ATTN), _acc(D),
                           _const((D_IN, D))],
                 out_specs=[_rows(tm, D), _acc(D)],
                 out_shape=[_sds((s, D), F32), _sds((1, D), F32)],
                 args=(x, dx1, dp_ret, dp_gate, dp_attn, ln1_g, w_in), big=True, exchange=exchange)


def _adamw(g, w, m, v):
    m_new = B1 * m + (1.0 - B1) * g
    v_new = B2 * v + (1.0 - B2) * (g * g)
    m_hat = m_new / (1.0 - B1 ** STEP)
    v_hat = v_new / (1.0 - B2 ** STEP)
    return -LR * (m_hat / (jnp.sqrt(v_hat) + ADAM_EPS) + WD * w), m_new, v_new


def _slot_sum(p_ref):
    g = p_ref[0].astype(F32)
    for k in range(1, N_DEV):
        g = g + p_ref[k].astype(F32)
    return g


def _adamw_call(parts, w, m, v, name, tr):
    rows, cols = w.shape

    def body(p_ref, w_ref, m_ref, v_ref, g_ref, dw_ref, nm_ref, nv_ref):
        g = _slot_sum(p_ref)
        g_ref[...] = g
        dw_ref[...], nm_ref[...], nv_ref[...] = _adamw(g, w_ref[...], m_ref[...], v_ref[...])

    p_spec = pl.BlockSpec((N_DEV, tr, cols), lambda i: (0, i, 0))
    spec = pl.BlockSpec((tr, cols), lambda i: (i, 0))
    return _call(body, name=name, grid=(rows // tr,), in_specs=[p_spec, spec, spec, spec], out_specs=[spec] * 4,
                 out_shape=[_sds((rows, cols), F32)] * 4, args=(parts, w, m, v))


SMALL_WIDTHS = [1024, 6400, 1024, 16, 1024, 1024]
SMALL_OFFSETS = [0, 1024, 7424, 8448, 8576, 9600]
LOSS_OFFSET = 10624
SMALL_LEN = 10752


def _pack_small(grads, loss):
    pieces = []
    for gr, width in zip(grads, SMALL_WIDTHS):
        pieces.append(jnp.pad(gr.reshape(1, width), ((0, 0), (0, -width % 128))))
    pieces.append(jnp.pad(loss.reshape(1, 1), ((0, 0), (0, 127))))
    return jnp.concatenate(pieces, axis=1)


def _adamw_small_call(parts, ws, ms, vs):
    n = len(ws)

    def body(*refs):
        p_ref, w_refs, m_refs, v_refs = refs[0], refs[1:1 + n], refs[1 + n:1 + 2 * n], refs[1 + 2 * n:1 + 3 * n]
        outs = refs[1 + 3 * n:]
        g_all = _slot_sum(p_ref)
        for i, (off, width) in enumerate(zip(SMALL_OFFSETS, SMALL_WIDTHS)):
            g = g_all[:, off:off + width]
            outs[i][...] = g
            outs[n + i][...], outs[2 * n + i][...], outs[3 * n + i][...] = _adamw(
                g, w_refs[i][...], m_refs[i][...], v_refs[i][...])
        outs[4 * n][...] = g_all[:, LOSS_OFFSET:LOSS_OFFSET + 128]

    whole = lambda shape: pl.BlockSpec(shape, lambda i: (0,) * len(shape))
    small = [whole((1, w)) for w in SMALL_WIDTHS]
    res = _call(body, name="adamw_small", grid=(1,), in_specs=[whole((N_DEV, 1, SMALL_LEN))] + small * 3,
                out_specs=small * 4 + [whole((1, 128))],
                out_shape=[_sds((1, w), F32) for w in SMALL_WIDTHS] * 4 + [_sds((1, 128), F32)],
                args=(parts, *ws, *ms, *vs))
    return [res[k * n:(k + 1) * n] for k in range(4)], res[4 * n]


def kernel(x, ln1_g, w_in, b_in, ret_norm_g, w_ret_out, attn_sinks, w_attn_out, w_out, ln2_g, w_ffn_gate, w_ffn_up, w_ffn_down, lnf_g, loss_target, m_ln1_g, m_w_in, m_b_in, m_ret_norm_g, m_w_ret_out, m_attn_sinks, m_w_attn_out, m_w_out, m_ln2_g, m_w_ffn_gate, m_w_ffn_up, m_w_ffn_down, m_lnf_g, v_ln1_g, v_w_in, v_b_in, v_ret_norm_g, v_w_ret_out, v_attn_sinks, v_w_attn_out, v_w_out, v_ln2_g, v_w_ffn_gate, v_w_ffn_up, v_w_ffn_down, v_lnf_g):
    cast = lambda a: a.astype(_MXU)
    xs, target = x[0], loss_target[0]
    s = xs.shape[0]
    r_sq = w_ret_out.shape[1]
    r_dn = w_ffn_down.shape[1]
    c_in = w_in.shape[2]
    c_ff = w_ffn_gate.shape[2]
    tm, tk = min(256, s), min(2048, s)
    lnf_row = lnf_g.reshape(1, D)
    cos_t, sin_t = _rope_tables(s)
    decays = _retention_decays()
    tr_shard = lambda a: a[0].T
    per_dev = lambda a, n: a.reshape(N_DEV, n, D)

    (h,), (all_in,) = _ln_call(xs, ln1_g, tm, _AllGather([cast(tr_shard(w_in))]))
    wt_in = all_in.reshape(N_DEV * c_in, D)
    rest = [tr_shard(w_ffn_gate), tr_shard(w_ffn_up), w_ret_out[0], w_attn_out[0], w_out[0], w_ffn_down[0]]
    (proj,), gathered = _proj_call(h, wt_in, b_in, _AllGather([cast(a) for a in rest]))
    wt_g, wt_u, full_ro, full_ao, full_o, full_d = (a.reshape(N_DEV * a.shape[1], D) for a in gathered)
    ry, qr, kr, states = _ret_fwd_call(proj, cos_t, sin_t, decays)
    ay, aqr, akr, avb = _attn_fwd_call(proj, attn_sinks, cos_t, sin_t)
    a_in, br_a, br_b, merged, x1 = _merge_fwd_call(xs, ry, proj, ay, ret_norm_g, full_ro, full_ao, full_o, tm)
    h2, g, u, f, dx2, loss, d_lnf = _ffn_fwd_call(x1, target, ln2_g, lnf_row, wt_g, wt_u, full_d, tm)

    dx1, dg, du, d_ln2 = _ffn_bwd_call(dx2, x1, g, u, ln2_g, wt_g, wt_u, full_d, tm)
    dw_d = _mm_tn(f, dx2, "dw_ffn_down", 1408, 1024, tk)
    dwt_g = _mm_tn(dg, h2, "dw_ffn_gate", 1408, 1024, tk)
    dwt_u = _mm_tn(du, h2, "dw_ffn_up", 1408, 1024, tk)
    (d_a, d_b, dp_gate, day, dry, db_gate, d_gn), (got_d,) = _merge_bwd_call(
        dx1, br_a, br_b, ry, proj, ret_norm_g, full_ro, full_ao, full_o, tm, _AllToAll([per_dev(dw_d, r_dn)]))
    dw_o = _mm_tn(merged, dx1, "dw_out", 1024, 1024, tk)
    dw_ro = _mm_tn(a_in, d_a, "dw_ret_out", 1024, 1024, tk)
    dw_ao = _mm_tn(ay, d_b, "dw_attn_out", 1024, 1024, tk)
    (dp_attn, d_sinks, db_attn), (got_g, got_u) = _attn_bwd_call(
        aqr, akr, avb, day, attn_sinks, cos_t, sin_t, _AllToAll([per_dev(dwt_g, c_ff), per_dev(dwt_u, c_ff)]))
    (dp_ret, db_ret), (got_ro, got_ao, got_o) = _ret_bwd_call(
        qr, kr, proj, states, dry, cos_t, sin_t, decays,
        _AllToAll([per_dev(dw_ro, r_sq), per_dev(dw_ao, r_sq), per_dev(dw_o, r_sq)]))
    tk_in = min(4096, s)
    dwt_ret = _mm_tn(dp_ret, h, "dw_in_ret", 1024, 1024, tk_in)
    dwt_gate = _mm_tn(dp_gate, h, "dw_in_gate", 1024, 1024, tk_in)
    dwt_attn = _mm_tn(dp_attn, h, "dw_in_attn", 1280, 1024, tk_in)
    in_pieces = [(0, 0, 2048), (1, 0, 1024), (2, 0, 1280), (1, 1024, 3072)]
    (dx, d_ln1), (got_in,) = _dx_call(xs, dx1, dp_ret, dp_gate, dp_attn, ln1_g, wt_in, tm,
                                      _RowScatter([dwt_ret, dwt_gate, dwt_attn], in_pieces, c_in))
    db_in = jnp.concatenate([db_ret, db_gate[:, 0:1024], db_attn, db_gate[:, 1024:3072]], axis=1)
    small = [d_ln1, db_in, d_gn, d_sinks[:, 0:Q_HEADS], d_ln2, d_lnf]
    (got_small,) = _exchange_call(_AllGather([_pack_small(small, loss[0, 0])]), "gather_small")

    transposed = ("w_in", "w_ffn_gate", "w_ffn_up")
    res = {}
    res["w_in"] = _adamw_call(got_in, tr_shard(w_in), tr_shard(m_w_in), tr_shard(v_w_in), "adamw_w_in", 160)
    res["w_ffn_gate"] = _adamw_call(got_g, tr_shard(w_ffn_gate), tr_shard(m_w_ffn_gate), tr_shard(v_w_ffn_gate),
                                    "adamw_ffn_gate", 176)
    res["w_ffn_up"] = _adamw_call(got_u, tr_shard(w_ffn_up), tr_shard(m_w_ffn_up), tr_shard(v_w_ffn_up),
                                  "adamw_ffn_up", 176)
    res["w_ret_out"] = _adamw_call(got_ro, w_ret_out[0], m_w_ret_out[0], v_w_ret_out[0], "adamw_ret_out", r_sq)
    res["w_attn_out"] = _adamw_call(got_ao, w_attn_out[0], m_w_attn_out[0], v_w_attn_out[0], "adamw_attn_out", r_sq)
    res["w_out"] = _adamw_call(got_o, w_out[0], m_w_out[0], v_w_out[0], "adamw_out", r_sq)
    res["w_ffn_down"] = _adamw_call(got_d, w_ffn_down[0], m_w_ffn_down[0], v_w_ffn_down[0], "adamw_ffn_down", 176)
    small_names = ["ln1_g", "b_in", "ret_norm_g", "attn_sinks", "ln2_g", "lnf_g"]
    small_res, loss_row = _adamw_small_call(
        got_small, [ln1_g, b_in, ret_norm_g, attn_sinks, ln2_g, lnf_row],
        [m_ln1_g, m_b_in, m_ret_norm_g, m_attn_sinks, m_ln2_g, m_lnf_g.reshape(1, D)],
        [v_ln1_g, v_b_in, v_ret_norm_g, v_attn_sinks, v_ln2_g, v_lnf_g.reshape(1, D)])
    for i, nm in enumerate(small_names):
        res[nm] = [small_res[kind][i] for kind in range(4)]

    order = ["ln1_g", "w_in", "b_in", "ret_norm_g", "w_ret_out", "attn_sinks", "w_attn_out", "w_out", "ln2_g",
             "w_ffn_gate", "w_ffn_up", "w_ffn_down", "lnf_g"]
    outs = [loss_row[0, 0], dx[None]]
    for kind in range(4):
        for nm in order:
            val = res[nm][kind]
            if nm in transposed:
                val = val.T
            outs.append(val[None] if nm.startswith("w_") else val.reshape(D) if nm == "lnf_g" else val)
    return tuple(outs)
```

```python
import math

import numpy as np
import jax
import jax.numpy as jnp
from jax import lax
from jax.experimental import pallas as pl
from jax.experimental.pallas import tpu as pltpu

F32 = jnp.float32
_MXU = jnp.bfloat16
_STORE = jnp.bfloat16

N_DEV = 8
D = 1024
RET_HEADS, RET_DK, RET_DV = 4, 128, 256
BLK = 128
Q_HEADS, KV_HEADS, HEAD_DIM = 16, 2, 64
GROUP = Q_HEADS // KV_HEADS
D_FF = 2816
N_RET, N_GATE, N_ATTN = 2048, 3072, 1280
D_IN = N_RET + N_GATE + N_ATTN
ROPE_THETA = 10000.0
EPS = 1e-6
RET_SCALE = RET_DK ** -0.5
ATTN_SCALE = HEAD_DIM ** -0.5
LR, B1, B2, ADAM_EPS, WD, STEP = 0.001, 0.9, 0.999, 1e-08, 0.01, 10
VMEM_LIMIT_MB = 56
MESH = pl.DeviceIdType.MESH


def _dot(a, b):
    return jnp.dot(a.astype(_MXU), b.astype(_MXU), preferred_element_type=F32)


def _dot_nt(a, b):
    return lax.dot_general(a.astype(_MXU), b.astype(_MXU), (((1,), (1,)), ((), ())), preferred_element_type=F32)


def _dot_tn(a, b):
    return lax.dot_general(a.astype(_MXU), b.astype(_MXU), (((0,), (0,)), ((), ())), preferred_element_type=F32)


def _sigmoid(x):
    return 1.0 / (1.0 + jnp.exp(-x))


def _cparams(n_axes, big=False):
    kw = dict(dimension_semantics=("arbitrary",) * n_axes)
    if big:
        kw["vmem_limit_bytes"] = VMEM_LIMIT_MB * 2**20
    return pltpu.CompilerParams(**kw)


def _rows(tm, width, col=0):
    return pl.BlockSpec((tm, width), lambda i: (i, col))


def _const(shape):
    nd = len(shape)
    return pl.BlockSpec(shape, lambda *_: (0,) * nd, pipeline_mode=pl.Buffered(1))


def _acc(width):
    return pl.BlockSpec((1, width), lambda *_: (0, 0))


def _sds(shape, dtype):
    return jax.ShapeDtypeStruct(shape, dtype)


def _swap_halves(x, half):
    w = x.shape[-1]
    if 2 * half == w:
        return pltpu.roll(x, half, 1)
    lane = lax.broadcasted_iota(jnp.int32, x.shape, 1)
    return jnp.where(lane % (2 * half) < half, pltpu.roll(x, w - half, 1), pltpu.roll(x, half, 1))


def _rope_tables(seq):
    lane = jnp.arange(128, dtype=jnp.int32)
    ret_freq = ROPE_THETA ** (-(lane % 64).astype(F32) / 64)
    attn_freq = ROPE_THETA ** (-(lane % 32).astype(F32) / 32)
    ang = jnp.arange(seq, dtype=jnp.int32).astype(F32)[:, None] * jnp.where(lane < 64, ret_freq, attn_freq)[None, :]
    return jnp.cos(ang), jnp.sin(ang)


def _ret_rope(cos, sin):
    low = lax.broadcasted_iota(jnp.int32, cos.shape, 1) < RET_DK // 2
    return jnp.where(low, cos, pltpu.roll(cos, RET_DK // 2, 1)), jnp.where(low, -sin, pltpu.roll(sin, RET_DK // 2, 1))


def _attn_rope(cos, sin):
    lane = lax.broadcasted_iota(jnp.int32, cos.shape, 1)
    half = HEAD_DIM // 2

    def spread(t):
        t = pltpu.roll(t, 64, 1)
        t = jnp.where(lane < half, t, pltpu.roll(t, half, 1))
        return jnp.where(lane < HEAD_DIM, t, pltpu.roll(t, HEAD_DIM, 1))

    return spread(cos), jnp.where(lane % HEAD_DIM < half, -spread(sin), spread(sin))


def _retention_decays():
    log_gamma = np.log1p(-np.exp2(-5.0 - np.arange(RET_HEADS, dtype=np.float32))).astype(np.float32)
    idx = np.arange(BLK, dtype=np.float32)
    rel = idx[:, None] - idx[None, :]
    intra = np.where(rel[None] >= 0, np.exp(log_gamma[:, None, None] * np.maximum(rel, 0.0)[None]), 0.0)
    q_decay = np.exp(log_gamma[:, None] * (idx + 1.0))[:, :, None]
    k_decay = np.exp(log_gamma[:, None] * (BLK - 1.0 - idx))[:, :, None]
    chunk_decay = [float(np.exp(np.float32(lg * BLK))) for lg in log_gamma]
    return (jnp.asarray(intra, F32), jnp.asarray(q_decay, F32), jnp.asarray(k_decay, F32), chunk_decay)


def _position():
    return lax.axis_index("x"), lax.axis_index("y"), lax.axis_index("c")


def _slot(px, py, pc):
    return 4 * px + 2 * py + pc


class _AllGather:
    def __init__(self, blocks):
        self.blocks = list(blocks)
        nb = len(self.blocks)
        self.out_shape = [_sds((N_DEV,) + b.shape, b.dtype) for b in self.blocks]
        self.scratch = [pltpu.SemaphoreType.DMA((nb, 7)), pltpu.SemaphoreType.DMA((nb, 7)),
                        pltpu.SemaphoreType.DMA((nb,))]

    def phases(self, ins, outs, send_sems, recv_sems, local_sems):
        nb = len(ins)
        x, y, c = _position()
        me, sibling = (x, y, c), (x, y, 1 - c)
        chips = [(1 - x, y), (x, 1 - y), (1 - x, 1 - y)]

        def copy(b, k, block, to, src=None):
            dst = outs[b].at[_slot(*block)]
            return pltpu.make_async_remote_copy(
                src_ref=dst if src is None else src, dst_ref=dst, send_sem=send_sems.at[b, k],
                recv_sem=recv_sems.at[b, k], device_id=to, device_id_type=MESH)

        def own(b):
            return pltpu.make_async_copy(ins[b], outs[b].at[_slot(*me)], local_sems.at[b])

        def first(b):
            return [copy(b, 0, me, sibling, src=ins[b])] + [
                copy(b, 1 + j, me, (*chip, c), src=ins[b]) for j, chip in enumerate(chips)]

        def start():
            for b in range(nb):
                own(b).start()
                for cp in first(b):
                    cp.start()

        def forward():
            for b in range(nb):
                for j, chip in enumerate(chips):
                    copy(b, 1 + j, (*chip, c), me).wait_recv()
                    copy(b, 4 + j, (*chip, c), sibling).start()

        def finish():
            for b in range(nb):
                copy(b, 0, sibling, me).wait_recv()
                for j, chip in enumerate(chips):
                    copy(b, 4 + j, (*chip, 1 - c), me).wait_recv()
            for b in range(nb):
                for cp in first(b):
                    cp.wait_send()
                for j, chip in enumerate(chips):
                    copy(b, 4 + j, (*chip, c), sibling).wait_send()
                own(b).wait()

        return start, forward, finish


class _AllToAll:
    def __init__(self, blocks):
        self.blocks = list(blocks)
        nb = len(self.blocks)
        self.out_shape = [_sds(b.shape, b.dtype) for b in self.blocks]
        self.scratch = [pltpu.SemaphoreType.DMA((nb, 7)), pltpu.SemaphoreType.DMA((nb, 7)),
                        pltpu.SemaphoreType.DMA((nb,))]

    def phases(self, ins, outs, send_sems, recv_sems, local_sems):
        nb = len(ins)
        x, y, c = _position()
        flip = lambda v, bit: 1 - v if bit else v
        peers = [(flip(x, k >> 2 & 1), flip(y, k >> 1 & 1), flip(c, k & 1)) for k in range(1, N_DEV)]

        def copy(b, k, peer, landed=False):
            return pltpu.make_async_remote_copy(
                src_ref=ins[b].at[_slot(*peer)], dst_ref=outs[b].at[_slot(*peer) if landed else _slot(x, y, c)],
                send_sem=send_sems.at[b, k], recv_sem=recv_sems.at[b, k], device_id=peer, device_id_type=MESH)

        def own(b):
            return pltpu.make_async_copy(ins[b].at[_slot(x, y, c)], outs[b].at[_slot(x, y, c)], local_sems.at[b])

        def start():
            for b in range(nb):
                own(b).start()
                for k, peer in enumerate(peers):
                    copy(b, k, peer).start()

        def forward():
            pass

        def finish():
            for b in range(nb):
                for k, peer in enumerate(peers):
                    copy(b, k, peer, landed=True).wait_recv()
            for b in range(nb):
                for k, peer in enumerate(peers):
                    copy(b, k, peer).wait_send()
                own(b).wait()

        return start, forward, finish


class _RowScatter:
    def __init__(self, arrays, pieces, n):
        self.blocks = list(arrays)
        self.pieces, self.n = pieces, n
        self.out_shape = [_sds((N_DEV, n, D), arrays[0].dtype)]
        self.scratch = [pltpu.SemaphoreType.DMA((N_DEV,)), pltpu.SemaphoreType.DMA((N_DEV,)), pltpu.SemaphoreType.DMA]

    def _parts(self, k):
        lo, hi, pos, res = k * self.n, (k + 1) * self.n, 0, []
        for arr, first, last in self.pieces:
            a, b = max(lo, pos), min(hi, pos + last - first)
            if a < b:
                res.append((arr, first + a - pos, b - a, a - lo))
            pos += last - first
        return res

    def phases(self, ins, outs, send_sems, recv_sems, local_sem):
        (out,) = outs
        x, y, c = _position()
        me = _slot(x, y, c)

        def start():
            for k in range(N_DEV):
                dist = jnp.bitwise_xor(me, k)

                @pl.when(me != k)
                def _():
                    for arr, first, rows, at in self._parts(k):
                        pltpu.make_async_remote_copy(
                            src_ref=ins[arr].at[pl.ds(first, rows)], dst_ref=out.at[me, pl.ds(at, rows)],
                            send_sem=send_sems.at[dist], recv_sem=recv_sems.at[dist],
                            device_id=(k >> 2 & 1, k >> 1 & 1, k & 1), device_id_type=MESH).start()

                @pl.when(me == k)
                def _():
                    for arr, first, rows, at in self._parts(k):
                        pltpu.make_async_copy(ins[arr].at[pl.ds(first, rows)], out.at[me, pl.ds(at, rows)], local_sem).start()

        def forward():
            pass

        def whole_block(dist):
            return pltpu.make_async_remote_copy(
                src_ref=out.at[me], dst_ref=out.at[jnp.bitwise_xor(me, dist)], send_sem=send_sems.at[dist],
                recv_sem=recv_sems.at[dist], device_id=(x, y, c), device_id_type=MESH)

        def finish():
            for dist in range(1, N_DEV):
                whole_block(dist).wait_recv()
            for dist in range(1, N_DEV):
                whole_block(dist).wait_send()
            pltpu.make_async_copy(out.at[me], out.at[me], local_sem).wait()

        return start, forward, finish


def _call(body, *, name, grid, in_specs, out_specs, out_shape, args, scratch_shapes=(), big=False, exchange=None):
    params = _cparams(len(grid), big)
    if exchange is None:
        return pl.pallas_call(body, name=name, grid=grid, in_specs=in_specs, out_specs=out_specs, out_shape=out_shape,
                              scratch_shapes=list(scratch_shapes), compiler_params=params)(*args)
    n_in, n_out, n_scr = len(in_specs), len(out_specs), len(scratch_shapes)
    nb, nb_out = len(exchange.blocks), len(exchange.out_shape)
    steps = math.prod(grid)

    def carried(*refs):
        pos = 0
        parts = []
        for n in (n_in, nb, n_out, nb_out, n_scr, len(exchange.scratch)):
            parts.append(refs[pos:pos + n])
            pos += n
        ins, x_ins, outs, x_outs, scr, sems = parts
        step = pl.program_id(0)
        for axis in range(1, len(grid)):
            step = step * grid[axis] + pl.program_id(axis)
        start, forward, finish = exchange.phases(x_ins, x_outs, *sems)
        pl.when(step == 0)(start)
        body(*ins, *outs, *scr)

        @pl.when(step == steps - 1)
        def _():
            forward()
            finish()

    any_spec = pl.BlockSpec(memory_space=pl.ANY)
    res = pl.pallas_call(
        carried, name=name, grid=grid, in_specs=list(in_specs) + [any_spec] * nb,
        out_specs=list(out_specs) + [any_spec] * nb_out, out_shape=list(out_shape) + exchange.out_shape,
        scratch_shapes=list(scratch_shapes) + exchange.scratch, compiler_params=params)(*args, *exchange.blocks)
    return res[:n_out], res[n_out:]


def _exchange_call(exchange, name):
    nb = len(exchange.blocks)

    def body(*refs):
        start, forward, finish = exchange.phases(refs[:nb], refs[nb:2 * nb], *refs[2 * nb:])
        start()
        forward()
        finish()

    any_spec = pl.BlockSpec(memory_space=pl.ANY)
    return pl.pallas_call(body, name=name, in_specs=[any_spec] * nb, out_specs=[any_spec] * nb,
                          out_shape=exchange.out_shape, scratch_shapes=exchange.scratch)(*exchange.blocks)


def _ln_call(x, g, tm, exchange):
    s = x.shape[0]

    def body(x_ref, g_ref, h_ref):
        xv = x_ref[...]
        r = lax.rsqrt(jnp.mean(xv * xv, axis=-1, keepdims=True) + EPS)
        h_ref[...] = ((xv * r) * g_ref[...]).astype(h_ref.dtype)

    return _call(body, name="ln1", grid=(s // tm,), in_specs=[_rows(tm, D), _acc(D)], out_specs=[_rows(tm, D)],
                 out_shape=[_sds((s, D), _MXU)], args=(x, g), exchange=exchange)


PROJ_TILE = 256


def _proj_source_tile(j):
    gate_end, attn_end, end = 3072 // PROJ_TILE, 4352 // PROJ_TILE, 6400 // PROJ_TILE
    n_gates = end - attn_end
    return jnp.where(j < gate_end, j, jnp.where(j < gate_end + n_gates, j + (attn_end - gate_end), j - n_gates))


def _proj_call(a, wt, bias, exchange):
    s, k = a.shape
    n = wt.shape[0]
    rows = min(1024, s)

    def body(a_ref, w_ref, b_ref, o_ref):
        for r in range(0, s, rows):
            o_ref[r:r + rows, :] = _dot_nt(a_ref[r:r + rows, :], w_ref[...]) + b_ref[...]

    return _call(body, name="proj", grid=(n // PROJ_TILE,),
                 in_specs=[_const((s, k)), pl.BlockSpec((PROJ_TILE, k), lambda j: (_proj_source_tile(j), 0)),
                           pl.BlockSpec((1, PROJ_TILE), lambda j: (0, _proj_source_tile(j)))],
                 out_specs=[pl.BlockSpec((s, PROJ_TILE), lambda j: (0, j))], out_shape=[_sds((s, n), F32)],
                 args=(a, wt, bias), big=True, exchange=exchange)


def _mm_tn(a, b, name, tm, tn, tk, exchange=None):
    s, m = a.shape
    n = b.shape[1]
    last = s // tk - 1

    def body(a_ref, b_ref, o_ref, acc):
        k = pl.program_id(2)
        part = _dot_tn(a_ref[...], b_ref[...])

        @pl.when(k == 0)
        def _():
            acc[...] = part

        @pl.when(k > 0)
        def _():
            acc[...] += part

        @pl.when(k == last)
        def _():
            o_ref[...] = acc[...].astype(o_ref.dtype)

    res = _call(body, name=name, grid=(m // tm, n // tn, s // tk),
                in_specs=[pl.BlockSpec((tk, tm), lambda i, j, k: (k, i)), pl.BlockSpec((tk, tn), lambda i, j, k: (k, j))],
                out_specs=[pl.BlockSpec((tm, tn), lambda i, j, k: (i, j))], out_shape=[_sds((m, n), _MXU)],
                scratch_shapes=[pltpu.VMEM((tm, tn), F32)], args=(a, b), big=True, exchange=exchange)
    return res[0] if exchange is None else (res[0][0], res[1])


RET_CHUNKS = 2


def _ret_fwd_call(proj, cos, sin, decays):
    s = proj.shape[0]
    nblk = s // BLK
    per = min(RET_CHUNKS, nblk)
    rows = per * BLK
    intra, q_decay, k_decay, chunk_decay = decays

    def body(rq_ref, rk_ref, rv_ref, cos_ref, sin_ref, intra_ref, qd_ref, kd_ref,
             ry_ref, qr_ref, kr_ref, st_ref, state):
        @pl.when(pl.program_id(0) == 0)
        def _():
            state[...] = jnp.zeros_like(state)

        for c in range(per):
            rc = slice(c * BLK, (c + 1) * BLK)
            cos_v, sin_v = _ret_rope(cos_ref[rc, :], sin_ref[rc, :])
            for h in range(RET_HEADS):
                hk = slice(h * RET_DK, (h + 1) * RET_DK)
                hv = slice(h * RET_DV, (h + 1) * RET_DV)
                q, k = rq_ref[rc, hk], rk_ref[rc, hk]
                qr = (q * cos_v + _swap_halves(q, RET_DK // 2) * sin_v) * RET_SCALE
                kr = k * cos_v + _swap_halves(k, RET_DK // 2) * sin_v
                v = rv_ref[rc, hv]
                s_h = state[h]
                st_ref[c, h] = s_h.astype(st_ref.dtype)
                scores = _dot_nt(qr, kr) * intra_ref[h]
                ry_ref[rc, hv] = _dot(scores, v) + _dot(qr, s_h) * qd_ref[h]
                state[h] = s_h * chunk_decay[h] + _dot_tn(kr * kd_ref[h], v)
                qr_ref[rc, hk] = qr.astype(qr_ref.dtype)
                kr_ref[rc, hk] = kr.astype(kr_ref.dtype)

    blk = lambda w, c: pl.BlockSpec((rows, w), lambda n: (n, c))
    return _call(body, name="ret_fwd", grid=(nblk // per,),
                 in_specs=[blk(512, 0), blk(512, 1), blk(1024, 1), blk(128, 0), blk(128, 0),
                           _const(intra.shape), _const(q_decay.shape), _const(k_decay.shape)],
                 out_specs=[blk(1024, 0), blk(512, 0), blk(512, 0),
                            pl.BlockSpec((per, RET_HEADS, RET_DK, RET_DV), lambda n: (n, 0, 0, 0))],
                 out_shape=[_sds((s, 1024), F32), _sds((s, 512), _MXU), _sds((s, 512), _MXU),
                            _sds((nblk, RET_HEADS, RET_DK, RET_DV), _MXU)],
                 scratch_shapes=[pltpu.VMEM((RET_HEADS, RET_DK, RET_DV), F32)],
                 args=(proj, proj, proj, cos, sin, intra, q_decay, k_decay))


def _ret_bwd_call(qr, kr, proj, states, dry, cos, sin, decays, exchange):
    s = qr.shape[0]
    nblk = s // BLK
    per = min(RET_CHUNKS, nblk)
    rows = per * BLK
    steps = nblk // per
    intra, q_decay, k_decay, chunk_decay = decays

    def body(qr_ref, kr_ref, rv_ref, st_ref, dry_ref, cos_ref, sin_ref, intra_ref, qd_ref, kd_ref,
             dp_ref, db_ref, dstate):
        @pl.when(pl.program_id(0) == 0)
        def _():
            dstate[...] = jnp.zeros_like(dstate)
            db_ref[...] = jnp.zeros_like(db_ref)

        for c in reversed(range(per)):
            rc = slice(c * BLK, (c + 1) * BLK)
            cos_v, sin_v = _ret_rope(cos_ref[rc, :], sin_ref[rc, :])
            for h in range(RET_HEADS):
                hk = slice(h * RET_DK, (h + 1) * RET_DK)
                hv = slice(h * RET_DV, (h + 1) * RET_DV)
                q, k, v, d_out = qr_ref[rc, hk], kr_ref[rc, hk], rv_ref[rc, hv], dry_ref[rc, hv]
                d_next = dstate[h]
                scores = _dot_nt(q, k) * intra_ref[h]
                d_scores = _dot_nt(d_out, v) * intra_ref[h]
                d_cross = d_out * qd_ref[h]
                dq = _dot(d_scores, k) + _dot_nt(d_cross, st_ref[c, h])
                dk = _dot_tn(d_scores, q) + _dot_nt(v, d_next) * kd_ref[h]
                dv = _dot_tn(scores, d_out) + _dot(k.astype(F32) * kd_ref[h], d_next)
                dstate[h] = d_next * chunk_decay[h] + _dot_tn(q, d_cross)
                dq = (dq * cos_v - _swap_halves(dq, RET_DK // 2) * sin_v) * RET_SCALE
                dk = dk * cos_v - _swap_halves(dk, RET_DK // 2) * sin_v
                kcols = slice(512 + h * RET_DK, 512 + (h + 1) * RET_DK)
                vcols = slice(1024 + h * RET_DV, 1024 + (h + 1) * RET_DV)
                dp_ref[rc, hk] = dq.astype(dp_ref.dtype)
                dp_ref[rc, kcols] = dk.astype(dp_ref.dtype)
                dp_ref[rc, vcols] = dv.astype(dp_ref.dtype)
                db_ref[:, hk] += jnp.sum(dq, axis=0, keepdims=True)
                db_ref[:, kcols] += jnp.sum(dk, axis=0, keepdims=True)
                db_ref[:, vcols] += jnp.sum(dv, axis=0, keepdims=True)

    rblk = lambda w, c: pl.BlockSpec((rows, w), lambda n: (steps - 1 - n, c))
    return _call(body, name="ret_bwd", grid=(steps,),
                 in_specs=[rblk(512, 0), rblk(512, 0), rblk(1024, 1),
                           pl.BlockSpec((per, RET_HEADS, RET_DK, RET_DV), lambda n: (steps - 1 - n, 0, 0, 0)),
                           rblk(1024, 0), rblk(128, 0), rblk(128, 0),
                           _const(intra.shape), _const(q_decay.shape), _const(k_decay.shape)],
                 out_specs=[rblk(N_RET, 0), _acc(N_RET)],
                 out_shape=[_sds((s, N_RET), _MXU), _sds((1, N_RET), F32)],
                 scratch_shapes=[pltpu.VMEM((RET_HEADS, RET_DK, RET_DV), F32)],
                 args=(qr, kr, proj, states, dry, cos, sin, intra, q_decay, k_decay), exchange=exchange)


def _both_halves(x, g):
    lane = lax.broadcasted_iota(jnp.int32, x.shape, 1)
    keep = lane < HEAD_DIM if g == 0 else lane >= HEAD_DIM
    return jnp.where(keep, x, pltpu.roll(x, HEAD_DIM, 1))


def _stack_heads(ref, g):
    lane = lax.broadcasted_iota(jnp.int32, (BLK, 128), 1)
    pieces = []
    for j in range(g * 4, g * 4 + 4):
        chunk = ref[:, j * 128:(j + 1) * 128]
        pieces += [jnp.where(lane < HEAD_DIM, chunk, jnp.zeros_like(chunk)),
                   jnp.where(lane >= HEAD_DIM, chunk, jnp.zeros_like(chunk))]
    return jnp.concatenate(pieces, axis=0)


def _window_bias(first_block):
    kj = lax.broadcasted_iota(jnp.int32, (2 * BLK, BLK), 0)
    qi = lax.broadcasted_iota(jnp.int32, (2 * BLK, BLK), 1)
    first_key = jnp.where(first_block, BLK, 0)
    seen = (kj > qi) & (kj <= qi + BLK) & (kj >= first_key)
    return jnp.where(seen, 0.0, -1e30)


def _sink_softmax(scores, sink):
    m = jnp.maximum(jnp.max(scores, axis=0, keepdims=True), sink)
    e = jnp.exp(scores - m)
    e_sink = jnp.exp(sink - m)
    return e, e_sink, 1.0 / (jnp.sum(e, axis=0, keepdims=True) + e_sink)


def _head_pair(stacked_t, jj):
    even = stacked_t[0:HEAD_DIM, 2 * jj * BLK:(2 * jj + 1) * BLK]
    odd = stacked_t[HEAD_DIM:128, (2 * jj + 1) * BLK:(2 * jj + 2) * BLK]
    return jnp.concatenate([even, odd], axis=0).T


def _attn_fwd_call(proj, sinks, cos, sin):
    s = proj.shape[0]
    nblk = s // BLK

    def body(sink_ref, q_ref, k_ref, v_ref, cos_ref, sin_ref, ay_ref, qr_ref, kr_ref, vb_ref,
             kwin, vwin, bias, s_scr, p_scr):
        n = pl.program_id(0)

        @pl.when(n == 0)
        def _():
            kwin[...] = jnp.zeros_like(kwin)
            vwin[...] = jnp.zeros_like(vwin)

        @pl.when(n > 0)
        def _():
            kwin[0:BLK] = kwin[BLK:2 * BLK]
            vwin[0:BLK] = vwin[BLK:2 * BLK]

        cos_v, sin_v = _attn_rope(cos_ref[...], sin_ref[...])
        k = k_ref[...]
        kr = (k * cos_v + _swap_halves(k, HEAD_DIM // 2) * sin_v).astype(kwin.dtype)
        kwin[BLK:2 * BLK] = kr
        vwin[BLK:2 * BLK] = v_ref[...].astype(vwin.dtype)
        kr_ref[...] = kr
        vb_ref[...] = vwin[BLK:2 * BLK]
        for j in range(Q_HEADS // 2):
            cols = slice(j * 128, (j + 1) * 128)
            q = q_ref[:, cols]
            qr_ref[:, cols] = ((q * cos_v + _swap_halves(q, HEAD_DIM // 2) * sin_v) * ATTN_SCALE).astype(qr_ref.dtype)
        bias[...] = _window_bias(n == 0)
        for g in range(KV_HEADS):
            kg = _both_halves(kwin[...], g)
            vg_t = _both_halves(vwin[...], g).astype(F32).T
            s_scr[...] = _dot_nt(kg, _stack_heads(qr_ref, g))
            for i in range(GROUP):
                cols = slice(i * BLK, (i + 1) * BLK)
                e, _, inv = _sink_softmax(s_scr[:, cols] + bias[...], sink_ref[0, g * GROUP + i])
                p_scr[:, cols] = (e * inv).astype(p_scr.dtype)
            out_t = _dot(vg_t, p_scr[...])
            for jj in range(4):
                j = g * 4 + jj
                ay_ref[:, j * 128:(j + 1) * 128] = _head_pair(out_t, jj).astype(ay_ref.dtype)

    blk = lambda w, c: pl.BlockSpec((BLK, w), lambda n: (n, c))
    off = (N_RET + N_GATE) // 128
    wide = (2 * BLK, GROUP * BLK)
    return _call(body, name="attn_fwd", grid=(nblk,),
                 in_specs=[pl.BlockSpec(memory_space=pltpu.SMEM), blk(1024, off // 8), blk(128, off + 8), blk(128, off + 9),
                           blk(128, 0), blk(128, 0)],
                 out_specs=[blk(1024, 0), blk(1024, 0), blk(128, 0), blk(128, 0)],
                 out_shape=[_sds((s, 1024), _MXU), _sds((s, 1024), _MXU), _sds((s, 128), _MXU), _sds((s, 128), _MXU)],
                 scratch_shapes=[pltpu.VMEM((2 * BLK, 128), _MXU), pltpu.VMEM((2 * BLK, 128), _MXU),
                                 pltpu.VMEM((2 * BLK, BLK), F32), pltpu.VMEM(wide, F32), pltpu.VMEM(wide, _MXU)],
                 args=(sinks, proj, proj, proj, cos, sin))


def _attn_bwd_call(qr, kr, vb, day, sinks, cos, sin, exchange):
    s = qr.shape[0]
    nblk = s // BLK

    def body(sink_ref, q_ref, kc_ref, kp_ref, vc_ref, vp_ref, do_ref, cos_ref, sin_ref, cosp_ref, sinp_ref,
             dp_ref, dsink_ref, db_ref, bias, s_scr, dp_scr, p_scr, ds_scr, dq_held, kv_held, kv_prev, kv_new):
        n = pl.program_id(0)
        valid = (n < nblk).astype(F32)

        @pl.when(n == 0)
        def _():
            dsink_ref[...] = jnp.zeros_like(dsink_ref)
            db_ref[...] = jnp.zeros_like(db_ref)

        @pl.when(n >= 1)
        def _():
            dp_ref[:, 0:1024] = dq_held[...]

        cos_v, sin_v = _attn_rope(cos_ref[...], sin_ref[...])
        bias[...] = _window_bias(n == 0)
        lane1 = lax.broadcasted_iota(jnp.int32, (1, 128), 1)
        kwin = jnp.concatenate([kp_ref[...], kc_ref[...]], axis=0)
        vwin = jnp.concatenate([vp_ref[...], vc_ref[...]], axis=0)
        dk_heads, dv_heads = [], []
        dsink = jnp.zeros((1, 128), F32)
        for g in range(KV_HEADS):
            kg = _both_halves(kwin, g)
            vg = _both_halves(vwin, g)
            q_all = _stack_heads(q_ref, g)
            do_all = _stack_heads(do_ref, g)
            s_scr[...] = _dot_nt(kg, q_all)
            dp_scr[...] = _dot_nt(vg, do_all)
            for i in range(GROUP):
                head = g * GROUP + i
                cols = slice(i * BLK, (i + 1) * BLK)
                e, e_sink, inv = _sink_softmax(s_scr[:, cols] + bias[...], sink_ref[0, head])
                p = e * inv
                dp = dp_scr[:, cols]
                delta = jnp.sum(p * dp, axis=0, keepdims=True)
                p_scr[:, cols] = p.astype(p_scr.dtype)
                ds_scr[:, cols] = (p * (dp - delta)).astype(ds_scr.dtype)
                dsink = dsink + jnp.where(lane1 == head, -jnp.sum(e_sink * inv * delta, axis=1, keepdims=True), 0.0)
            dv_both = _dot(p_scr[...], do_all)
            dk_both = _dot(ds_scr[...], q_all)
            dv_heads.append(dv_both + pltpu.roll(dv_both, HEAD_DIM, 1))
            dk_heads.append(dk_both + pltpu.roll(dk_both, HEAD_DIM, 1))
            dq_t = _dot(kg.astype(F32).T, ds_scr[...])
            for jj in range(4):
                cols = slice((g * 4 + jj) * 128, (g * 4 + jj + 1) * 128)
                dq = _head_pair(dq_t, jj)
                dq = (dq * cos_v - _swap_halves(dq, HEAD_DIM // 2) * sin_v) * ATTN_SCALE
                dq_held[:, cols] = dq.astype(dq_held.dtype)
                db_ref[:, cols] += jnp.sum(dq, axis=0, keepdims=True) * valid
        dsink_ref[...] += dsink * valid
        lane2 = lax.broadcasted_iota(jnp.int32, (2 * BLK, 128), 1)
        dk_all = jnp.where(lane2 < HEAD_DIM, dk_heads[0], dk_heads[1])
        dv_all = jnp.where(lane2 < HEAD_DIM, dv_heads[0], dv_heads[1])
        kv_prev[:, 0:128] = dk_all[0:BLK] * valid
        kv_prev[:, 128:256] = dv_all[0:BLK] * valid
        kv_new[:, 0:128] = dk_all[BLK:2 * BLK]
        kv_new[:, 128:256] = dv_all[BLK:2 * BLK]

        @pl.when(n >= 1)
        def _():
            dkv = kv_held[...] + kv_prev[...]
            dk = dkv[:, 0:128]
            cos_p, sin_p = _attn_rope(cosp_ref[...], sinp_ref[...])
            dk = dk * cos_p - _swap_halves(dk, HEAD_DIM // 2) * sin_p
            dv = dkv[:, 128:256]
            dp_ref[:, 1024:1152] = dk.astype(dp_ref.dtype)
            dp_ref[:, 1152:1280] = dv.astype(dp_ref.dtype)
            db_ref[:, 1024:1152] += jnp.sum(dk, axis=0, keepdims=True)
            db_ref[:, 1152:1280] += jnp.sum(dv, axis=0, keepdims=True)

        kv_held[...] = kv_new[...]

    blk = lambda w: pl.BlockSpec((BLK, w), lambda n: (jnp.minimum(n, nblk - 1), 0))
    pblk = lambda w: pl.BlockSpec((BLK, w), lambda n: (jnp.maximum(n - 1, 0), 0))
    wide = (2 * BLK, GROUP * BLK)
    return _call(body, name="attn_bwd", grid=(nblk + 1,),
                 in_specs=[pl.BlockSpec(memory_space=pltpu.SMEM), blk(1024), blk(128), pblk(128), blk(128), pblk(128),
                           blk(1024), blk(128), blk(128), pblk(128), pblk(128)],
                 out_specs=[pblk(N_ATTN), _acc(128), _acc(N_ATTN)],
                 out_shape=[_sds((s, N_ATTN), _MXU), _sds((1, 128), F32), _sds((1, N_ATTN), F32)],
                 scratch_shapes=[pltpu.VMEM((2 * BLK, BLK), F32), pltpu.VMEM(wide, F32), pltpu.VMEM(wide, F32),
                                 pltpu.VMEM(wide, _MXU), pltpu.VMEM(wide, _MXU), pltpu.VMEM((BLK, 1024), _MXU),
                                 pltpu.VMEM((BLK, 256), F32), pltpu.VMEM((BLK, 256), F32), pltpu.VMEM((BLK, 256), F32)],
                 args=(sinks, qr, kr, kr, vb, vb, day, cos, sin, cos, sin), exchange=exchange)


def _group_norm(y):
    mu = jnp.mean(y, axis=-1, keepdims=True)
    yc = y - mu
    rs = lax.rsqrt(jnp.mean(yc * yc, axis=-1, keepdims=True) + EPS)
    return yc * rs, rs


GATE_COL = N_RET // 1024


def _merge_fwd_call(x, ry, proj, ay, gn_g, w_ro, w_ao, w_o, tm):
    s = x.shape[0]

    def body(x_ref, ry_ref, rg_ref, ga_ref, gb_ref, ay_ref, gn_ref, wro_ref, wao_ref, wo_ref,
             ain_ref, a_ref, b_ref, mg_ref, x1_ref):
        for h in range(RET_HEADS):
            hv = slice(h * RET_DV, (h + 1) * RET_DV)
            yhat, _ = _group_norm(ry_ref[:, hv])
            rg = rg_ref[:, hv]
            ain_ref[:, hv] = ((rg * _sigmoid(rg)) * (yhat * gn_ref[:, hv])).astype(ain_ref.dtype)
        a = _dot(ain_ref[...], wro_ref[...])
        b = _dot(ay_ref[...], wao_ref[...])
        a_ref[...] = a.astype(a_ref.dtype)
        b_ref[...] = b.astype(b_ref.dtype)
        merged = (_sigmoid(ga_ref[...]) * a + _sigmoid(gb_ref[...]) * b).astype(mg_ref.dtype)
        mg_ref[...] = merged
        x1_ref[...] = x_ref[...] + _dot(merged, wo_ref[...])

    return _call(body, name="merge_fwd", grid=(s // tm,),
                 in_specs=[_rows(tm, D), _rows(tm, 1024), _rows(tm, 1024, GATE_COL), _rows(tm, 1024, GATE_COL + 1),
                           _rows(tm, 1024, GATE_COL + 2), _rows(tm, 1024), _acc(1024),
                           _const((D, D)), _const((D, D)), _const((D, D))],
                 out_specs=[_rows(tm, D)] * 5,
                 out_shape=[_sds((s, D), _MXU), _sds((s, D), _STORE), _sds((s, D), _STORE), _sds((s, D), _MXU),
                            _sds((s, D), F32)],
                 args=(x, ry, proj, proj, proj, ay, gn_g, w_ro, w_ao, w_o), big=True)


def _merge_bwd_call(dx1, a, b, ry, proj, gn_g, w_ro, w_ao, w_o, tm):
    s = dx1.shape[0]

    def body(dx1_ref, a_ref, b_ref, ry_ref, rg_ref, ga_ref, gb_ref, gn_ref, wro_ref, wao_ref, wo_ref,
             da_ref, dbr_ref, dp_ref, day_ref, dry_ref, dbias_ref, dgn_ref):
        @pl.when(pl.program_id(0) == 0)
        def _():
            dbias_ref[...] = jnp.zeros_like(dbias_ref)
            dgn_ref[...] = jnp.zeros_like(dgn_ref)

        d_merged = _dot_nt(dx1_ref[...], wo_ref[...])
        sa, sb = _sigmoid(ga_ref[...]), _sigmoid(gb_ref[...])
        d_a = d_merged * sa
        d_b = d_merged * sb
        da_ref[...] = d_a.astype(da_ref.dtype)
        dbr_ref[...] = d_b.astype(dbr_ref.dtype)
        d_ga = d_merged * a_ref[...] * (sa * (1.0 - sa))
        d_gb = d_merged * b_ref[...] * (sb * (1.0 - sb))
        dp_ref[:, 1024:2048] = d_ga.astype(dp_ref.dtype)
        dp_ref[:, 2048:3072] = d_gb.astype(dp_ref.dtype)
        dbias_ref[:, 1024:2048] += jnp.sum(d_ga, axis=0, keepdims=True)
        dbias_ref[:, 2048:3072] += jnp.sum(d_gb, axis=0, keepdims=True)
        day_ref[...] = _dot_nt(d_b, wao_ref[...]).astype(day_ref.dtype)
        d_ain = _dot_nt(d_a, wro_ref[...])
        for h in range(RET_HEADS):
            hv = slice(h * RET_DV, (h + 1) * RET_DV)
            yhat, rs = _group_norm(ry_ref[:, hv])
            rg = rg_ref[:, hv]
            sg = _sigmoid(rg)
            gn = gn_ref[:, hv]
            d_h = d_ain[:, hv]
            d_rg = d_h * (yhat * gn) * (sg * (1.0 + rg * (1.0 - sg)))
            d_ryn = d_h * (rg * sg)
            dgn_ref[:, hv] += jnp.sum(d_ryn * yhat, axis=0, keepdims=True)
            d_yhat = d_ryn * gn
            dry_ref[:, hv] = (rs * (d_yhat - jnp.mean(d_yhat, axis=-1, keepdims=True)
                                    - yhat * jnp.mean(d_yhat * yhat, axis=-1, keepdims=True))).astype(dry_ref.dtype)
            dp_ref[:, hv] = d_rg.astype(dp_ref.dtype)
            dbias_ref[:, hv] += jnp.sum(d_rg, axis=0, keepdims=True)

    return _call(body, name="merge_bwd", grid=(s // tm,),
                 in_specs=[_rows(tm, D), _rows(tm, D), _rows(tm, D), _rows(tm, 1024), _rows(tm, 1024, GATE_COL),
                           _rows(tm, 1024, GATE_COL + 1), _rows(tm, 1024, GATE_COL + 2), _acc(1024),
                           _const((D, D)), _const((D, D)), _const((D, D))],
                 out_specs=[_rows(tm, D), _rows(tm, D), _rows(tm, N_GATE), _rows(tm, D), _rows(tm, D), _acc(N_GATE),
                            _acc(1024)],
                 out_shape=[_sds((s, D), _MXU), _sds((s, D), _MXU), _sds((s, N_GATE), _MXU), _sds((s, D), _MXU),
                            _sds((s, D), _STORE), _sds((1, N_GATE), F32), _sds((1, 1024), F32)],
                 args=(dx1, a, b, ry, proj, proj, proj, gn_g, w_ro, w_ao, w_o), big=True)


def _ffn_fwd_call(x1, target, ln2_g, lnf_g, w_g, w_u, w_d, tm):
    s = x1.shape[0]

    def body(x1_ref, t_ref, g2_ref, gf_ref, wg_ref, wu_ref, wd_ref,
             h2_ref, g_ref, u_ref, f_ref, dx2_ref, loss_ref, dgf_ref):
        @pl.when(pl.program_id(0) == 0)
        def _():
            loss_ref[...] = jnp.zeros_like(loss_ref)
            dgf_ref[...] = jnp.zeros_like(dgf_ref)

        x1v = x1_ref[...]
        r1 = lax.rsqrt(jnp.mean(x1v * x1v, axis=-1, keepdims=True) + EPS)
        h2 = ((x1v * r1) * g2_ref[...]).astype(h2_ref.dtype)
        h2_ref[...] = h2
        g = _dot_nt(h2, wg_ref[...])
        u = _dot_nt(h2, wu_ref[...])
        g_ref[...] = g
        u_ref[...] = u
        f = ((g * _sigmoid(g)) * u).astype(f_ref.dtype)
        f_ref[...] = f
        x2 = x1v + _dot(f, wd_ref[...])
        r2 = lax.rsqrt(jnp.mean(x2 * x2, axis=-1, keepdims=True) + EPS)
        xhat = x2 * r2
        err = xhat * gf_ref[...] - t_ref[...]
        loss_ref[...] += 0.5 * jnp.sum(jnp.mean(err * err, axis=-1, keepdims=True))
        dy = err * (1.0 / D)
        dgf_ref[...] += jnp.sum(dy * xhat, axis=0, keepdims=True)
        dxh = dy * gf_ref[...]
        dx2_ref[...] = r2 * (dxh - xhat * jnp.mean(dxh * xhat, axis=-1, keepdims=True))

    return _call(body, name="ffn_fwd", grid=(s // tm,),
                 in_specs=[_rows(tm, D), _rows(tm, D), _acc(D), _acc(D), _const((D_FF, D)), _const((D_FF, D)),
                           _const((D_FF, D))],
                 out_specs=[_rows(tm, D), _rows(tm, D_FF), _rows(tm, D_FF), _rows(tm, D_FF), _rows(tm, D), _acc(128),
                            _acc(D)],
                 out_shape=[_sds((s, D), _MXU), _sds((s, D_FF), F32), _sds((s, D_FF), F32), _sds((s, D_FF), _MXU),
                            _sds((s, D), F32), _sds((1, 128), F32), _sds((1, D), F32)],
                 args=(x1, target, ln2_g, lnf_g, w_g, w_u, w_d), big=True)


def _ffn_bwd_call(dx2, x1, g, u, ln2_g, w_g, w_u, w_d, tm):
    s = dx2.shape[0]

    def body(dx2_ref, x1_ref, g_ref, u_ref, g2_ref, wg_ref, wu_ref, wd_ref, dx1_ref, dg_ref, du_ref, dg2_ref):
        @pl.when(pl.program_id(0) == 0)
        def _():
            dg2_ref[...] = jnp.zeros_like(dg2_ref)

        dx2v = dx2_ref[...]
        df = _dot_nt(dx2v, wd_ref[...])
        gv, uv = g_ref[...], u_ref[...]
        sg = _sigmoid(gv)
        du = (df * (gv * sg)).astype(du_ref.dtype)
        dg = (df * uv * (sg * (1.0 + gv * (1.0 - sg)))).astype(dg_ref.dtype)
        du_ref[...] = du
        dg_ref[...] = dg
        dh2 = _dot(dg, wg_ref[...]) + _dot(du, wu_ref[...])
        x1v = x1_ref[...]
        r1 = lax.rsqrt(jnp.mean(x1v * x1v, axis=-1, keepdims=True) + EPS)
        xhat = x1v * r1
        dg2_ref[...] += jnp.sum(dh2 * xhat, axis=0, keepdims=True)
        dxh = dh2 * g2_ref[...]
        dx1_ref[...] = dx2v + r1 * (dxh - xhat * jnp.mean(dxh * xhat, axis=-1, keepdims=True))

    return _call(body, name="ffn_bwd", grid=(s // tm,),
                 in_specs=[_rows(tm, D), _rows(tm, D), _rows(tm, D_FF), _rows(tm, D_FF), _acc(D),
                           _const((D_FF, D)), _const((D_FF, D)), _const((D_FF, D))],
                 out_specs=[_rows(tm, D), _rows(tm, D_FF), _rows(tm, D_FF), _acc(D)],
                 out_shape=[_sds((s, D), F32), _sds((s, D_FF), _MXU), _sds((s, D_FF), _MXU), _sds((1, D), F32)],
                 args=(dx2, x1, g, u, ln2_g, w_g, w_u, w_d), big=True)


def _dx_call(x, dx1, dp_ret, dp_gate, dp_attn, ln1_g, w_in, tm, exchange):
    s = x.shape[0]

    def body(x_ref, dx1_ref, dr_ref, dg_ref, da_ref, g1_ref, w_ref, dx_ref, dg1_ref):
        @pl.when(pl.program_id(0) == 0)
        def _():
            dg1_ref[...] = jnp.zeros_like(dg1_ref)

        dh = (_dot(dr_ref[...], w_ref[0:2048, :]) + _dot(dg_ref[:, 0:1024], w_ref[2048:3072, :])
              + _dot(da_ref[...], w_ref[3072:4352, :]) + _dot(dg_ref[:, 1024:3072], w_ref[4352:6400, :]))
        xv = x_ref[...]
        r = lax.rsqrt(jnp.mean(xv * xv, axis=-1, keepdims=True) + EPS)
        xhat = xv * r
        dg1_ref[...] += jnp.sum(dh * xhat, axis=0, keepdims=True)
        dxh = dh * g1_ref[...]
        dx_ref[...] = dx1_ref[...] + r * (dxh - xhat * jnp.mean(dxh * xhat, axis=-1, keepdims=True))

    return _call(body, name="dx", grid=(s // tm,),
                 in_specs=[_rows(tm, D), _rows(tm, D), _rows(tm, N_RET), _rows(tm, N_GATE), _rows(tm, N_ATTN), _acc(D),
                           _const((D_IN, D))],
                 out_specs=[_rows(tm, D), _acc(D)],
                 out_shape=[_sds((s, D), F32), _sds((1, D), F32)],
                 args=(x, dx1, dp_ret, dp_gate, dp_attn, ln1_g, w_in), big=True, exchange=exchange)


def _adamw(g, w, m, v):
    m_new = B1 * m + (1.0 - B1) * g
    v_new = B2 * v + (1.0 - B2) * (g * g)
    m_hat = m_new / (1.0 - B1 ** STEP)
    v_hat = v_new / (1.0 - B2 ** STEP)
    return -LR * (m_hat / (jnp.sqrt(v_hat) + ADAM_EPS) + WD * w), m_new, v_new


def _slot_sum(p_ref):
    g = p_ref[0].astype(F32)
    for k in range(1, N_DEV):
        g = g + p_ref[k].astype(F32)
    return g


def _adamw_call(parts, w, m, v, name, tr):
    rows, cols = w.shape

    def body(p_ref, w_ref, m_ref, v_ref, g_ref, dw_ref, nm_ref, nv_ref):
        g = _slot_sum(p_ref)
        g_ref[...] = g
        dw_ref[...], nm_ref[...], nv_ref[...] = _adamw(g, w_ref[...], m_ref[...], v_ref[...])

    p_spec = pl.BlockSpec((N_DEV, tr, cols), lambda i: (0, i, 0))
    spec = pl.BlockSpec((tr, cols), lambda i: (i, 0))
    return _call(body, name=name, grid=(rows // tr,), in_specs=[p_spec, spec, spec, spec], out_specs=[spec] * 4,
                 out_shape=[_sds((rows, cols), F32)] * 4, args=(parts, w, m, v))


SMALL_WIDTHS = [1024, 6400, 1024, 16, 1024, 1024]
SMALL_OFFSETS = [0, 1024, 7424, 8448, 8576, 9600]
LOSS_OFFSET = 10624
SMALL_LEN = 10752


def _pack_small(grads, loss):
    pieces = []
    for gr, width in zip(grads, SMALL_WIDTHS):
        pieces.append(jnp.pad(gr.reshape(1, width), ((0, 0), (0, -width % 128))))
    pieces.append(jnp.pad(loss.reshape(1, 1), ((0, 0), (0, 127))))
    return jnp.concatenate(pieces, axis=1)


def _adamw_small_call(parts, ws, ms, vs):
    n = len(ws)

    def body(*refs):
        p_ref, w_refs, m_refs, v_refs = refs[0], refs[1:1 + n], refs[1 + n:1 + 2 * n], refs[1 + 2 * n:1 + 3 * n]
        outs = refs[1 + 3 * n:]
        g_all = _slot_sum(p_ref)
        for i, (off, width) in enumerate(zip(SMALL_OFFSETS, SMALL_WIDTHS)):
            g = g_all[:, off:off + width]
            outs[i][...] = g
            outs[n + i][...], outs[2 * n + i][...], outs[3 * n + i][...] = _adamw(
                g, w_refs[i][...], m_refs[i][...], v_refs[i][...])
        outs[4 * n][...] = g_all[:, LOSS_OFFSET:LOSS_OFFSET + 128]

    whole = lambda shape: pl.BlockSpec(shape, lambda i: (0,) * len(shape))
    small = [whole((1, w)) for w in SMALL_WIDTHS]
    res = _call(body, name="adamw_small", grid=(1,), in_specs=[whole((N_DEV, 1, SMALL_LEN))] + small * 3,
                out_specs=small * 4 + [whole((1, 128))],
                out_shape=[_sds((1, w), F32) for w in SMALL_WIDTHS] * 4 + [_sds((1, 128), F32)],
                args=(parts, *ws, *ms, *vs))
    return [res[k * n:(k + 1) * n] for k in range(4)], res[4 * n]


def kernel(x, ln1_g, w_in, b_in, ret_norm_g, w_ret_out, attn_sinks, w_attn_out, w_out, ln2_g, w_ffn_gate, w_ffn_up, w_ffn_down, lnf_g, loss_target, m_ln1_g, m_w_in, m_b_in, m_ret_norm_g, m_w_ret_out, m_attn_sinks, m_w_attn_out, m_w_out, m_ln2_g, m_w_ffn_gate, m_w_ffn_up, m_w_ffn_down, m_lnf_g, v_ln1_g, v_w_in, v_b_in, v_ret_norm_g, v_w_ret_out, v_attn_sinks, v_w_attn_out, v_w_out, v_ln2_g, v_w_ffn_gate, v_w_ffn_up, v_w_ffn_down, v_lnf_g):
    cast = lambda a: a.astype(_MXU)
    xs, target = x[0], loss_target[0]
    s = xs.shape[0]
    r_sq = w_ret_out.shape[1]
    r_dn = w_ffn_down.shape[1]
    c_in = w_in.shape[2]
    c_ff = w_ffn_gate.shape[2]
    tm, tm_wide, tk = min(256, s), min(512, s), min(2048, s)
    lnf_row = lnf_g.reshape(1, D)
    cos_t, sin_t = _rope_tables(s)
    decays = _retention_decays()
    tr_shard = lambda a: a[0].T
    per_dev = lambda a, n: a.reshape(N_DEV, n, D)

    (h,), (all_in,) = _ln_call(xs, ln1_g, tm, _AllGather([cast(tr_shard(w_in))]))
    wt_in = all_in.reshape(N_DEV * c_in, D)
    rest = [tr_shard(w_ffn_gate), tr_shard(w_ffn_up), w_ret_out[0], w_attn_out[0], w_out[0], w_ffn_down[0]]
    (proj,), gathered = _proj_call(h, wt_in, b_in, _AllGather([cast(a) for a in rest]))
    wt_g, wt_u, full_ro, full_ao, full_o, full_d = (a.reshape(N_DEV * a.shape[1], D) for a in gathered)
    ry, qr, kr, states = _ret_fwd_call(proj, cos_t, sin_t, decays)
    ay, aqr, akr, avb = _attn_fwd_call(proj, attn_sinks, cos_t, sin_t)
    a_in, br_a, br_b, merged, x1 = _merge_fwd_call(xs, ry, proj, ay, ret_norm_g, full_ro, full_ao, full_o, tm_wide)
    h2, g, u, f, dx2, loss, d_lnf = _ffn_fwd_call(x1, target, ln2_g, lnf_row, wt_g, wt_u, full_d, tm)

    dx1, dg, du, d_ln2 = _ffn_bwd_call(dx2, x1, g, u, ln2_g, wt_g, wt_u, full_d, tm)
    dw_d = _mm_tn(f, dx2, "dw_ffn_down", 1408, 1024, tk)
    dwt_g = _mm_tn(dg, h2, "dw_ffn_gate", 1408, 1024, tk)
    dwt_u = _mm_tn(du, h2, "dw_ffn_up", 1408, 1024, tk)
    d_a, d_b, dp_gate, day, dry, db_gate, d_gn = _merge_bwd_call(
        dx1, br_a, br_b, ry, proj, ret_norm_g, full_ro, full_ao, full_o, tm_wide)
    dw_o = _mm_tn(merged, dx1, "dw_out", 1024, 1024, tk)
    dw_ro = _mm_tn(a_in, d_a, "dw_ret_out", 1024, 1024, tk)
    dw_ao = _mm_tn(ay, d_b, "dw_attn_out", 1024, 1024, tk)
    (dp_attn, d_sinks, db_attn), (got_g, got_u, got_d) = _attn_bwd_call(
        aqr, akr, avb, day, attn_sinks, cos_t, sin_t,
        _AllToAll([per_dev(dwt_g, c_ff), per_dev(dwt_u, c_ff), per_dev(dw_d, r_dn)]))
    (dp_ret, db_ret), (got_ro, got_ao, got_o) = _ret_bwd_call(
        qr, kr, proj, states, dry, cos_t, sin_t, decays,
        _AllToAll([per_dev(dw_ro, r_sq), per_dev(dw_ao, r_sq), per_dev(dw_o, r_sq)]))
    tk_in = min(4096, s)
    dwt_ret = _mm_tn(dp_ret, h, "dw_in_ret", 1024, 1024, tk_in)
    dwt_gate = _mm_tn(dp_gate, h, "dw_in_gate", 1024, 1024, tk_in)
    dwt_attn = _mm_tn(dp_attn, h, "dw_in_attn", 1280, 1024, tk_in)
    in_pieces = [(0, 0, 2048), (1, 0, 1024), (2, 0, 1280), (1, 1024, 3072)]
    (dx, d_ln1), (got_in,) = _dx_call(xs, dx1, dp_ret, dp_gate, dp_attn, ln1_g, wt_in, tm_wide,
                                      _RowScatter([dwt_ret, dwt_gate, dwt_attn], in_pieces, c_in))
    db_in = jnp.concatenate([db_ret, db_gate[:, 0:1024], db_attn, db_gate[:, 1024:3072]], axis=1)
    small = [d_ln1, db_in, d_gn, d_sinks[:, 0:Q_HEADS], d_ln2, d_lnf]
    (got_small,) = _exchange_call(_AllGather([_pack_small(small, loss[0, 0])]), "gather_small")

    transposed = ("w_in", "w_ffn_gate", "w_ffn_up")
    res = {}
    res["w_in"] = _adamw_call(got_in, tr_shard(w_in), tr_shard(m_w_in), tr_shard(v_w_in), "adamw_w_in", 160)
    res["w_ffn_gate"] = _adamw_call(got_g, tr_shard(w_ffn_gate), tr_shard(m_w_ffn_gate), tr_shard(v_w_ffn_gate),
                                    "adamw_ffn_gate", 176)
    res["w_ffn_up"] = _adamw_call(got_u, tr_shard(w_ffn_up), tr_shard(m_w_ffn_up), tr_shard(v_w_ffn_up),
                                  "adamw_ffn_up", 176)
    res["w_ret_out"] = _adamw_call(got_ro, w_ret_out[0], m_w_ret_out[0], v_w_ret_out[0], "adamw_ret_out", r_sq)
    res["w_attn_out"] = _adamw_call(got_ao, w_attn_out[0], m_w_attn_out[0], v_w_attn_out[0], "adamw_attn_out", r_sq)
    res["w_out"] = _adamw_call(got_o, w_out[0], m_w_out[0], v_w_out[0], "adamw_out", r_sq)
    res["w_ffn_down"] = _adamw_call(got_d, w_ffn_down[0], m_w_ffn_down[0], v_w_ffn_down[0], "adamw_ffn_down", 176)
    small_names = ["ln1_g", "b_in", "ret_norm_g", "attn_sinks", "ln2_g", "lnf_g"]
    small_res, loss_row = _adamw_small_call(
        got_small, [ln1_g, b_in, ret_norm_g, attn_sinks, ln2_g, lnf_row],
        [m_ln1_g, m_b_in, m_ret_norm_g, m_attn_sinks, m_ln2_g, m_lnf_g.reshape(1, D)],
        [v_ln1_g, v_b_in, v_ret_norm_g, v_attn_sinks, v_ln2_g, v_lnf_g.reshape(1, D)])
    for i, nm in enumerate(small_names):
        res[nm] = [small_res[kind][i] for kind in range(4)]

    order = ["ln1_g", "w_in", "b_in", "ret_norm_g", "w_ret_out", "attn_sinks", "w_attn_out", "w_out", "ln2_g",
             "w_ffn_gate", "w_ffn_up", "w_ffn_down", "lnf_g"]
    outs = [loss_row[0, 0], dx[None]]
    for kind in range(4):
        for nm in order:
            val = res[nm][kind]
            if nm in transposed:
                val = val.T
            outs.append(val[None] if nm.startswith("w_") else val.reshape(D) if nm == "lnf_g" else val)
    return tuple(outs)
```

```python
import math

import numpy as np
import jax
import jax.numpy as jnp
from jax import lax
from jax.experimental import pallas as pl
from jax.experimental.pallas import tpu as pltpu

F32 = jnp.float32
_MXU = jnp.bfloat16
_STORE = jnp.bfloat16

N_DEV = 8
D = 1024
RET_HEADS, RET_DK, RET_DV = 4, 128, 256
BLK = 128
Q_HEADS, KV_HEADS, HEAD_DIM = 16, 2, 64
GROUP = Q_HEADS // KV_HEADS
D_FF = 2816
N_RET, N_GATE, N_ATTN = 2048, 3072, 1280
D_IN = N_RET + N_GATE + N_ATTN
ROPE_THETA = 10000.0
EPS = 1e-6
RET_SCALE = RET_DK ** -0.5
ATTN_SCALE = HEAD_DIM ** -0.5
LR, B1, B2, ADAM_EPS, WD, STEP = 0.001, 0.9, 0.999, 1e-08, 0.01, 10
VMEM_LIMIT_MB = 56
MESH = pl.DeviceIdType.MESH


def _dot(a, b):
    return jnp.dot(a.astype(_MXU), b.astype(_MXU), preferred_element_type=F32)


def _dot_nt(a, b):
    return lax.dot_general(a.astype(_MXU), b.astype(_MXU), (((1,), (1,)), ((), ())), preferred_element_type=F32)


def _dot_tn(a, b):
    return lax.dot_general(a.astype(_MXU), b.astype(_MXU), (((0,), (0,)), ((), ())), preferred_element_type=F32)


def _sigmoid(x):
    return 1.0 / (1.0 + jnp.exp(-x))


def _cparams(n_axes, big=False):
    kw = dict(dimension_semantics=("arbitrary",) * n_axes)
    if big:
        kw["vmem_limit_bytes"] = VMEM_LIMIT_MB * 2**20
    return pltpu.CompilerParams(**kw)


def _rows(tm, width, col=0):
    return pl.BlockSpec((tm, width), lambda i: (i, col))


def _const(shape):
    nd = len(shape)
    return pl.BlockSpec(shape, lambda *_: (0,) * nd, pipeline_mode=pl.Buffered(1))


def _acc(width):
    return pl.BlockSpec((1, width), lambda *_: (0, 0))


def _sds(shape, dtype):
    return jax.ShapeDtypeStruct(shape, dtype)


def _swap_halves(x, half):
    w = x.shape[-1]
    if 2 * half == w:
        return pltpu.roll(x, half, 1)
    lane = lax.broadcasted_iota(jnp.int32, x.shape, 1)
    return jnp.where(lane % (2 * half) < half, pltpu.roll(x, w - half, 1), pltpu.roll(x, half, 1))


def _rope_tables(seq):
    lane = jnp.arange(128, dtype=jnp.int32)
    ret_freq = ROPE_THETA ** (-(lane % 64).astype(F32) / 64)
    attn_freq = ROPE_THETA ** (-(lane % 32).astype(F32) / 32)
    ang = jnp.arange(seq, dtype=jnp.int32).astype(F32)[:, None] * jnp.where(lane < 64, ret_freq, attn_freq)[None, :]
    return jnp.cos(ang), jnp.sin(ang)


def _ret_rope(cos, sin):
    low = lax.broadcasted_iota(jnp.int32, cos.shape, 1) < RET_DK // 2
    return jnp.where(low, cos, pltpu.roll(cos, RET_DK // 2, 1)), jnp.where(low, -sin, pltpu.roll(sin, RET_DK // 2, 1))


def _attn_rope(cos, sin):
    lane = lax.broadcasted_iota(jnp.int32, cos.shape, 1)
    half = HEAD_DIM // 2

    def spread(t):
        t = pltpu.roll(t, 64, 1)
        t = jnp.where(lane < half, t, pltpu.roll(t, half, 1))
        return jnp.where(lane < HEAD_DIM, t, pltpu.roll(t, HEAD_DIM, 1))

    return spread(cos), jnp.where(lane % HEAD_DIM < half, -spread(sin), spread(sin))


def _retention_decays():
    log_gamma = np.log1p(-np.exp2(-5.0 - np.arange(RET_HEADS, dtype=np.float32))).astype(np.float32)
    idx = np.arange(BLK, dtype=np.float32)
    rel = idx[:, None] - idx[None, :]
    intra = np.where(rel[None] >= 0, np.exp(log_gamma[:, None, None] * np.maximum(rel, 0.0)[None]), 0.0)
    q_decay = np.exp(log_gamma[:, None] * (idx + 1.0))[:, :, None]
    k_decay = np.exp(log_gamma[:, None] * (BLK - 1.0 - idx))[:, :, None]
    chunk_decay = [float(np.exp(np.float32(lg * BLK))) for lg in log_gamma]
    return (jnp.asarray(intra, F32), jnp.asarray(q_decay, F32), jnp.asarray(k_decay, F32), chunk_decay)


def _position():
    return lax.axis_index("x"), lax.axis_index("y"), lax.axis_index("c")


def _slot(px, py, pc):
    return 4 * px + 2 * py + pc


class _AllGather:
    def __init__(self, blocks):
        self.blocks = list(blocks)
        nb = len(self.blocks)
        self.out_shape = [_sds((N_DEV,) + b.shape, b.dtype) for b in self.blocks]
        self.scratch = [pltpu.SemaphoreType.DMA((nb, 7)), pltpu.SemaphoreType.DMA((nb, 7)),
                        pltpu.SemaphoreType.DMA((nb,))]

    def phases(self, ins, outs, send_sems, recv_sems, local_sems):
        nb = len(ins)
        x, y, c = _position()
        me, sibling = (x, y, c), (x, y, 1 - c)
        chips = [(1 - x, y), (x, 1 - y), (1 - x, 1 - y)]

        def copy(b, k, block, to, src=None):
            dst = outs[b].at[_slot(*block)]
            return pltpu.make_async_remote_copy(
                src_ref=dst if src is None else src, dst_ref=dst, send_sem=send_sems.at[b, k],
                recv_sem=recv_sems.at[b, k], device_id=to, device_id_type=MESH)

        def own(b):
            return pltpu.make_async_copy(ins[b], outs[b].at[_slot(*me)], local_sems.at[b])

        def first(b):
            return [copy(b, 0, me, sibling, src=ins[b])] + [
                copy(b, 1 + j, me, (*chip, c), src=ins[b]) for j, chip in enumerate(chips)]

        def start():
            for b in range(nb):
                own(b).start()
                for cp in first(b):
                    cp.start()

        def forward():
            for b in range(nb):
                for j, chip in enumerate(chips):
                    copy(b, 1 + j, (*chip, c), me).wait_recv()
                    copy(b, 4 + j, (*chip, c), sibling).start()

        def finish():
            for b in range(nb):
                copy(b, 0, sibling, me).wait_recv()
                for j, chip in enumerate(chips):
                    copy(b, 4 + j, (*chip, 1 - c), me).wait_recv()
            for b in range(nb):
                for cp in first(b):
                    cp.wait_send()
                for j, chip in enumerate(chips):
                    copy(b, 4 + j, (*chip, c), sibling).wait_send()
                own(b).wait()

        return start, forward, finish


class _AllToAll:
    def __init__(self, blocks):
        self.blocks = list(blocks)
        nb = len(self.blocks)
        self.out_shape = [_sds(b.shape, b.dtype) for b in self.blocks]
        self.scratch = [pltpu.SemaphoreType.DMA((nb, 7)), pltpu.SemaphoreType.DMA((nb, 7)),
                        pltpu.SemaphoreType.DMA((nb,))]

    def phases(self, ins, outs, send_sems, recv_sems, local_sems):
        nb = len(ins)
        x, y, c = _position()
        flip = lambda v, bit: 1 - v if bit else v
        peers = [(flip(x, k >> 2 & 1), flip(y, k >> 1 & 1), flip(c, k & 1)) for k in range(1, N_DEV)]

        def copy(b, k, peer, landed=False):
            return pltpu.make_async_remote_copy(
                src_ref=ins[b].at[_slot(*peer)], dst_ref=outs[b].at[_slot(*peer) if landed else _slot(x, y, c)],
                send_sem=send_sems.at[b, k], recv_sem=recv_sems.at[b, k], device_id=peer, device_id_type=MESH)

        def own(b):
            return pltpu.make_async_copy(ins[b].at[_slot(x, y, c)], outs[b].at[_slot(x, y, c)], local_sems.at[b])

        def start():
            for b in range(nb):
                own(b).start()
                for k, peer in enumerate(peers):
                    copy(b, k, peer).start()

        def forward():
            pass

        def finish():
            for b in range(nb):
                for k, peer in enumerate(peers):
                    copy(b, k, peer, landed=True).wait_recv()
            for b in range(nb):
                for k, peer in enumerate(peers):
                    copy(b, k, peer).wait_send()
                own(b).wait()

        return start, forward, finish


class _RowScatter:
    def __init__(self, arrays, pieces, n):
        self.blocks = list(arrays)
        self.pieces, self.n = pieces, n
        self.out_shape = [_sds((N_DEV, n, D), arrays[0].dtype)]
        self.scratch = [pltpu.SemaphoreType.DMA((N_DEV,)), pltpu.SemaphoreType.DMA((N_DEV,)), pltpu.SemaphoreType.DMA]

    def _parts(self, k):
        lo, hi, pos, res = k * self.n, (k + 1) * self.n, 0, []
        for arr, first, last in self.pieces:
            a, b = max(lo, pos), min(hi, pos + last - first)
            if a < b:
                res.append((arr, first + a - pos, b - a, a - lo))
            pos += last - first
        return res

    def phases(self, ins, outs, send_sems, recv_sems, local_sem):
        (out,) = outs
        x, y, c = _position()
        me = _slot(x, y, c)

        def start():
            for k in range(N_DEV):
                dist = jnp.bitwise_xor(me, k)

                @pl.when(me != k)
                def _():
                    for arr, first, rows, at in self._parts(k):
                        pltpu.make_async_remote_copy(
                            src_ref=ins[arr].at[pl.ds(first, rows)], dst_ref=out.at[me, pl.ds(at, rows)],
                            send_sem=send_sems.at[dist], recv_sem=recv_sems.at[dist],
                            device_id=(k >> 2 & 1, k >> 1 & 1, k & 1), device_id_type=MESH).start()

                @pl.when(me == k)
                def _():
                    for arr, first, rows, at in self._parts(k):
                        pltpu.make_async_copy(ins[arr].at[pl.ds(first, rows)], out.at[me, pl.ds(at, rows)], local_sem).start()

        def forward():
            pass

        def whole_block(dist):
            return pltpu.make_async_remote_copy(
                src_ref=out.at[me], dst_ref=out.at[jnp.bitwise_xor(me, dist)], send_sem=send_sems.at[dist],
                recv_sem=recv_sems.at[dist], device_id=(x, y, c), device_id_type=MESH)

        def finish():
            for dist in range(1, N_DEV):
                whole_block(dist).wait_recv()
            for dist in range(1, N_DEV):
                whole_block(dist).wait_send()
            pltpu.make_async_copy(out.at[me], out.at[me], local_sem).wait()

        return start, forward, finish


def _call(body, *, name, grid, in_specs, out_specs, out_shape, args, scratch_shapes=(), big=False, exchange=None):
    params = _cparams(len(grid), big)
    if exchange is None:
        return pl.pallas_call(body, name=name, grid=grid, in_specs=in_specs, out_specs=out_specs, out_shape=out_shape,
                              scratch_shapes=list(scratch_shapes), compiler_params=params)(*args)
    n_in, n_out, n_scr = len(in_specs), len(out_specs), len(scratch_shapes)
    nb, nb_out = len(exchange.blocks), len(exchange.out_shape)
    steps = math.prod(grid)

    def carried(*refs):
        pos = 0
        parts = []
        for n in (n_in, nb, n_out, nb_out, n_scr, len(exchange.scratch)):
            parts.append(refs[pos:pos + n])
            pos += n
        ins, x_ins, outs, x_outs, scr, sems = parts
        step = pl.program_id(0)
        for axis in range(1, len(grid)):
            step = step * grid[axis] + pl.program_id(axis)
        start, forward, finish = exchange.phases(x_ins, x_outs, *sems)
        pl.when(step == 0)(start)
        body(*ins, *outs, *scr)

        @pl.when(step == steps - 1)
        def _():
            forward()
            finish()

    any_spec = pl.BlockSpec(memory_space=pl.ANY)
    res = pl.pallas_call(
        carried, name=name, grid=grid, in_specs=list(in_specs) + [any_spec] * nb,
        out_specs=list(out_specs) + [any_spec] * nb_out, out_shape=list(out_shape) + exchange.out_shape,
        scratch_shapes=list(scratch_shapes) + exchange.scratch, compiler_params=params)(*args, *exchange.blocks)
    return res[:n_out], res[n_out:]


def _exchange_call(exchange, name):
    nb = len(exchange.blocks)

    def body(*refs):
        start, forward, finish = exchange.phases(refs[:nb], refs[nb:2 * nb], *refs[2 * nb:])
        start()
        forward()
        finish()

    any_spec = pl.BlockSpec(memory_space=pl.ANY)
    return pl.pallas_call(body, name=name, in_specs=[any_spec] * nb, out_specs=[any_spec] * nb,
                          out_shape=exchange.out_shape, scratch_shapes=exchange.scratch)(*exchange.blocks)


def _ln_call(x, g, tm, exchange):
    s = x.shape[0]

    def body(x_ref, g_ref, h_ref):
        xv = x_ref[...]
        r = lax.rsqrt(jnp.mean(xv * xv, axis=-1, keepdims=True) + EPS)
        h_ref[...] = ((xv * r) * g_ref[...]).astype(h_ref.dtype)

    return _call(body, name="ln1", grid=(s // tm,), in_specs=[_rows(tm, D), _acc(D)], out_specs=[_rows(tm, D)],
                 out_shape=[_sds((s, D), _MXU)], args=(x, g), exchange=exchange)


PROJ_TILE = 256


def _proj_source_tile(j):
    gate_end, attn_end, end = 3072 // PROJ_TILE, 4352 // PROJ_TILE, 6400 // PROJ_TILE
    n_gates = end - attn_end
    return jnp.where(j < gate_end, j, jnp.where(j < gate_end + n_gates, j + (attn_end - gate_end), j - n_gates))


def _proj_call(a, wt, bias, exchange):
    s, k = a.shape
    n = wt.shape[0]
    rows = min(1024, s)

    def body(a_ref, w_ref, b_ref, o_ref):
        for r in range(0, s, rows):
            o_ref[r:r + rows, :] = _dot_nt(a_ref[r:r + rows, :], w_ref[...]) + b_ref[...]

    return _call(body, name="proj", grid=(n // PROJ_TILE,),
                 in_specs=[_const((s, k)), pl.BlockSpec((PROJ_TILE, k), lambda j: (_proj_source_tile(j), 0)),
                           pl.BlockSpec((1, PROJ_TILE), lambda j: (0, _proj_source_tile(j)))],
                 out_specs=[pl.BlockSpec((s, PROJ_TILE), lambda j: (0, j))], out_shape=[_sds((s, n), F32)],
                 args=(a, wt, bias), big=True, exchange=exchange)


def _mm_tn(a, b, name, tm, tn, tk, exchange=None):
    s, m = a.shape
    n = b.shape[1]
    last = s // tk - 1

    def body(a_ref, b_ref, o_ref, acc):
        k = pl.program_id(2)
        part = _dot_tn(a_ref[...], b_ref[...])

        @pl.when(k == 0)
        def _():
            acc[...] = part

        @pl.when(k > 0)
        def _():
            acc[...] += part

        @pl.when(k == last)
        def _():
            o_ref[...] = acc[...].astype(o_ref.dtype)

    res = _call(body, name=name, grid=(m // tm, n // tn, s // tk),
                in_specs=[pl.BlockSpec((tk, tm), lambda i, j, k: (k, i)), pl.BlockSpec((tk, tn), lambda i, j, k: (k, j))],
                out_specs=[pl.BlockSpec((tm, tn), lambda i, j, k: (i, j))], out_shape=[_sds((m, n), _MXU)],
                scratch_shapes=[pltpu.VMEM((tm, tn), F32)], args=(a, b), big=True, exchange=exchange)
    return res[0] if exchange is None else (res[0][0], res[1])


RET_CHUNKS = 2


def _ret_fwd_call(proj, cos, sin, decays):
    s = proj.shape[0]
    nblk = s // BLK
    per = min(RET_CHUNKS, nblk)
    rows = per * BLK
    intra, q_decay, k_decay, chunk_decay = decays

    def body(rq_ref, rk_ref, rv_ref, cos_ref, sin_ref, intra_ref, qd_ref, kd_ref,
             ry_ref, qr_ref, kr_ref, st_ref, state):
        @pl.when(pl.program_id(0) == 0)
        def _():
            state[...] = jnp.zeros_like(state)

        for c in range(per):
            rc = slice(c * BLK, (c + 1) * BLK)
            cos_v, sin_v = _ret_rope(cos_ref[rc, :], sin_ref[rc, :])
            for h in range(RET_HEADS):
                hk = slice(h * RET_DK, (h + 1) * RET_DK)
                hv = slice(h * RET_DV, (h + 1) * RET_DV)
                q, k = rq_ref[rc, hk], rk_ref[rc, hk]
                qr = (q * cos_v + _swap_halves(q, RET_DK // 2) * sin_v) * RET_SCALE
                kr = k * cos_v + _swap_halves(k, RET_DK // 2) * sin_v
                v = rv_ref[rc, hv]
                s_h = state[h]
                st_ref[c, h] = s_h.astype(st_ref.dtype)
                scores = _dot_nt(qr, kr) * intra_ref[h]
                ry_ref[rc, hv] = _dot(scores, v) + _dot(qr, s_h) * qd_ref[h]
                state[h] = s_h * chunk_decay[h] + _dot_tn(kr * kd_ref[h], v)
                qr_ref[rc, hk] = qr.astype(qr_ref.dtype)
                kr_ref[rc, hk] = kr.astype(kr_ref.dtype)

    blk = lambda w, c: pl.BlockSpec((rows, w), lambda n: (n, c))
    return _call(body, name="ret_fwd", grid=(nblk // per,),
                 in_specs=[blk(512, 0), blk(512, 1), blk(1024, 1), blk(128, 0), blk(128, 0),
                           _const(intra.shape), _const(q_decay.shape), _const(k_decay.shape)],
                 out_specs=[blk(1024, 0), blk(512, 0), blk(512, 0),
                            pl.BlockSpec((per, RET_HEADS, RET_DK, RET_DV), lambda n: (n, 0, 0, 0))],
                 out_shape=[_sds((s, 1024), F32), _sds((s, 512), _MXU), _sds((s, 512), _MXU),
                            _sds((nblk, RET_HEADS, RET_DK, RET_DV), _MXU)],
                 scratch_shapes=[pltpu.VMEM((RET_HEADS, RET_DK, RET_DV), F32)],
                 args=(proj, proj, proj, cos, sin, intra, q_decay, k_decay))


def _ret_bwd_call(qr, kr, proj, states, dry, cos, sin, decays, exchange):
    s = qr.shape[0]
    nblk = s // BLK
    per = min(RET_CHUNKS, nblk)
    rows = per * BLK
    steps = nblk // per
    intra, q_decay, k_decay, chunk_decay = decays

    def body(qr_ref, kr_ref, rv_ref, st_ref, dry_ref, cos_ref, sin_ref, intra_ref, qd_ref, kd_ref,
             dp_ref, db_ref, dstate):
        @pl.when(pl.program_id(0) == 0)
        def _():
            dstate[...] = jnp.zeros_like(dstate)
            db_ref[...] = jnp.zeros_like(db_ref)

        for c in reversed(range(per)):
            rc = slice(c * BLK, (c + 1) * BLK)
            cos_v, sin_v = _ret_rope(cos_ref[rc, :], sin_ref[rc, :])
            for h in range(RET_HEADS):
                hk = slice(h * RET_DK, (h + 1) * RET_DK)
                hv = slice(h * RET_DV, (h + 1) * RET_DV)
                q, k, v, d_out = qr_ref[rc, hk], kr_ref[rc, hk], rv_ref[rc, hv], dry_ref[rc, hv]
                d_next = dstate[h]
                scores = _dot_nt(q, k) * intra_ref[h]
                d_scores = _dot_nt(d_out, v) * intra_ref[h]
                d_cross = d_out * qd_ref[h]
                dq = _dot(d_scores, k) + _dot_nt(d_cross, st_ref[c, h])
                dk = _dot_tn(d_scores, q) + _dot_nt(v, d_next) * kd_ref[h]
                dv = _dot_tn(scores, d_out) + _dot(k.astype(F32) * kd_ref[h], d_next)
                dstate[h] = d_next * chunk_decay[h] + _dot_tn(q, d_cross)
                dq = (dq * cos_v - _swap_halves(dq, RET_DK // 2) * sin_v) * RET_SCALE
                dk = dk * cos_v - _swap_halves(dk, RET_DK // 2) * sin_v
                kcols = slice(512 + h * RET_DK, 512 + (h + 1) * RET_DK)
                vcols = slice(1024 + h * RET_DV, 1024 + (h + 1) * RET_DV)
                dp_ref[rc, hk] = dq.astype(dp_ref.dtype)
                dp_ref[rc, kcols] = dk.astype(dp_ref.dtype)
                dp_ref[rc, vcols] = dv.astype(dp_ref.dtype)
                db_ref[:, hk] += jnp.sum(dq, axis=0, keepdims=True)
                db_ref[:, kcols] += jnp.sum(dk, axis=0, keepdims=True)
                db_ref[:, vcols] += jnp.sum(dv, axis=0, keepdims=True)

    rblk = lambda w, c: pl.BlockSpec((rows, w), lambda n: (steps - 1 - n, c))
    return _call(body, name="ret_bwd", grid=(steps,),
                 in_specs=[rblk(512, 0), rblk(512, 0), rblk(1024, 1),
                           pl.BlockSpec((per, RET_HEADS, RET_DK, RET_DV), lambda n: (steps - 1 - n, 0, 0, 0)),
                           rblk(1024, 0), rblk(128, 0), rblk(128, 0),
                           _const(intra.shape), _const(q_decay.shape), _const(k_decay.shape)],
                 out_specs=[rblk(N_RET, 0), _acc(N_RET)],
                 out_shape=[_sds((s, N_RET), _MXU), _sds((1, N_RET), F32)],
                 scratch_shapes=[pltpu.VMEM((RET_HEADS, RET_DK, RET_DV), F32)],
                 args=(qr, kr, proj, states, dry, cos, sin, intra, q_decay, k_decay), exchange=exchange)


def _both_halves(x, g):
    lane = lax.broadcasted_iota(jnp.int32, x.shape, 1)
    keep = lane < HEAD_DIM if g == 0 else lane >= HEAD_DIM
    return jnp.where(keep, x, pltpu.roll(x, HEAD_DIM, 1))


def _stack_heads(ref, g, rows=slice(None)):
    lane = lax.broadcasted_iota(jnp.int32, (BLK, 128), 1)
    pieces = []
    for j in range(g * 4, g * 4 + 4):
        chunk = ref[rows, j * 128:(j + 1) * 128]
        pieces += [jnp.where(lane < HEAD_DIM, chunk, jnp.zeros_like(chunk)),
                   jnp.where(lane >= HEAD_DIM, chunk, jnp.zeros_like(chunk))]
    return jnp.concatenate(pieces, axis=0)


def _window_bias(first_block):
    kj = lax.broadcasted_iota(jnp.int32, (2 * BLK, BLK), 0)
    qi = lax.broadcasted_iota(jnp.int32, (2 * BLK, BLK), 1)
    first_key = jnp.where(first_block, BLK, 0)
    seen = (kj > qi) & (kj <= qi + BLK) & (kj >= first_key)
    return jnp.where(seen, 0.0, -1e30)


def _sink_softmax(scores, sink):
    m = jnp.maximum(jnp.max(scores, axis=0, keepdims=True), sink)
    e = jnp.exp(scores - m)
    e_sink = jnp.exp(sink - m)
    return e, e_sink, 1.0 / (jnp.sum(e, axis=0, keepdims=True) + e_sink)


def _head_pair(stacked_t, jj):
    even = stacked_t[0:HEAD_DIM, 2 * jj * BLK:(2 * jj + 1) * BLK]
    odd = stacked_t[HEAD_DIM:128, (2 * jj + 1) * BLK:(2 * jj + 2) * BLK]
    return jnp.concatenate([even, odd], axis=0).T


ATTN_BLOCKS = 2


def _attn_fwd_call(proj, sinks, cos, sin):
    s = proj.shape[0]
    nblk = s // BLK
    per = min(ATTN_BLOCKS, nblk)
    rows = per * BLK

    def body(sink_ref, q_ref, k_ref, v_ref, cos_ref, sin_ref, ay_ref, qr_ref, kr_ref, vb_ref,
             kwin, vwin, bias, s_scr, p_scr):
        n = pl.program_id(0)

        @pl.when(n == 0)
        def _():
            kwin[...] = jnp.zeros_like(kwin)
            vwin[...] = jnp.zeros_like(vwin)

        for c in range(per):
            rc = slice(c * BLK, (c + 1) * BLK)
            kwin[0:BLK] = kwin[BLK:2 * BLK]
            vwin[0:BLK] = vwin[BLK:2 * BLK]
            cos_v, sin_v = _attn_rope(cos_ref[rc, :], sin_ref[rc, :])
            k = k_ref[rc, :]
            kr = (k * cos_v + _swap_halves(k, HEAD_DIM // 2) * sin_v).astype(kwin.dtype)
            kwin[BLK:2 * BLK] = kr
            vwin[BLK:2 * BLK] = v_ref[rc, :].astype(vwin.dtype)
            kr_ref[rc, :] = kr
            vb_ref[rc, :] = vwin[BLK:2 * BLK]
            for j in range(Q_HEADS // 2):
                cols = slice(j * 128, (j + 1) * 128)
                q = q_ref[rc, cols]
                qr_ref[rc, cols] = ((q * cos_v + _swap_halves(q, HEAD_DIM // 2) * sin_v) * ATTN_SCALE).astype(qr_ref.dtype)
            bias[...] = _window_bias(n == 0) if c == 0 else _window_bias(False)
            for g in range(KV_HEADS):
                kg = _both_halves(kwin[...], g)
                vg_t = _both_halves(vwin[...], g).astype(F32).T
                s_scr[...] = _dot_nt(kg, _stack_heads(qr_ref, g, rc))
                for i in range(GROUP):
                    cols = slice(i * BLK, (i + 1) * BLK)
                    e, _, inv = _sink_softmax(s_scr[:, cols] + bias[...], sink_ref[0, g * GROUP + i])
                    p_scr[:, cols] = (e * inv).astype(p_scr.dtype)
                out_t = _dot(vg_t, p_scr[...])
                for jj in range(4):
                    j = g * 4 + jj
                    ay_ref[rc, j * 128:(j + 1) * 128] = _head_pair(out_t, jj).astype(ay_ref.dtype)

    blk = lambda w, c: pl.BlockSpec((rows, w), lambda n: (n, c))
    off = (N_RET + N_GATE) // 128
    wide = (2 * BLK, GROUP * BLK)
    return _call(body, name="attn_fwd", grid=(nblk // per,),
                 in_specs=[pl.BlockSpec(memory_space=pltpu.SMEM), blk(1024, off // 8), blk(128, off + 8), blk(128, off + 9),
                           blk(128, 0), blk(128, 0)],
                 out_specs=[blk(1024, 0), blk(1024, 0), blk(128, 0), blk(128, 0)],
                 out_shape=[_sds((s, 1024), _MXU), _sds((s, 1024), _MXU), _sds((s, 128), _MXU), _sds((s, 128), _MXU)],
                 scratch_shapes=[pltpu.VMEM((2 * BLK, 128), _MXU), pltpu.VMEM((2 * BLK, 128), _MXU),
                                 pltpu.VMEM((2 * BLK, BLK), F32), pltpu.VMEM(wide, F32), pltpu.VMEM(wide, _MXU)],
                 args=(sinks, proj, proj, proj, cos, sin))


def _attn_bwd_call(qr, kr, vb, day, sinks, cos, sin, exchange):
    s = qr.shape[0]
    nblk = s // BLK

    def body(sink_ref, q_ref, kc_ref, kp_ref, vc_ref, vp_ref, do_ref, cos_ref, sin_ref, cosp_ref, sinp_ref,
             dp_ref, dsink_ref, db_ref, bias, s_scr, dp_scr, p_scr, ds_scr, dq_held, kv_held, kv_prev, kv_new):
        n = pl.program_id(0)
        valid = (n < nblk).astype(F32)

        @pl.when(n == 0)
        def _():
            dsink_ref[...] = jnp.zeros_like(dsink_ref)
            db_ref[...] = jnp.zeros_like(db_ref)

        @pl.when(n >= 1)
        def _():
            dp_ref[:, 0:1024] = dq_held[...]

        cos_v, sin_v = _attn_rope(cos_ref[...], sin_ref[...])
        bias[...] = _window_bias(n == 0)
        lane1 = lax.broadcasted_iota(jnp.int32, (1, 128), 1)
        kwin = jnp.concatenate([kp_ref[...], kc_ref[...]], axis=0)
        vwin = jnp.concatenate([vp_ref[...], vc_ref[...]], axis=0)
        dk_heads, dv_heads = [], []
        dsink = jnp.zeros((1, 128), F32)
        for g in range(KV_HEADS):
            kg = _both_halves(kwin, g)
            vg = _both_halves(vwin, g)
            q_all = _stack_heads(q_ref, g)
            do_all = _stack_heads(do_ref, g)
            s_scr[...] = _dot_nt(kg, q_all)
            dp_scr[...] = _dot_nt(vg, do_all)
            for i in range(GROUP):
                head = g * GROUP + i
                cols = slice(i * BLK, (i + 1) * BLK)
                e, e_sink, inv = _sink_softmax(s_scr[:, cols] + bias[...], sink_ref[0, head])
                p = e * inv
                dp = dp_scr[:, cols]
                delta = jnp.sum(p * dp, axis=0, keepdims=True)
                p_scr[:, cols] = p.astype(p_scr.dtype)
                ds_scr[:, cols] = (p * (dp - delta)).astype(ds_scr.dtype)
                dsink = dsink + jnp.where(lane1 == head, -jnp.sum(e_sink * inv * delta, axis=1, keepdims=True), 0.0)
            dv_both = _dot(p_scr[...], do_all)
            dk_both = _dot(ds_scr[...], q_all)
            dv_heads.append(dv_both + pltpu.roll(dv_both, HEAD_DIM, 1))
            dk_heads.append(dk_both + pltpu.roll(dk_both, HEAD_DIM, 1))
            dq_t = _dot(kg.astype(F32).T, ds_scr[...])
            for jj in range(4):
                cols = slice((g * 4 + jj) * 128, (g * 4 + jj + 1) * 128)
                dq = _head_pair(dq_t, jj)
                dq = (dq * cos_v - _swap_halves(dq, HEAD_DIM // 2) * sin_v) * ATTN_SCALE
                dq_held[:, cols] = dq.astype(dq_held.dtype)
                db_ref[:, cols] += jnp.sum(dq, axis=0, keepdims=True) * valid
        dsink_ref[...] += dsink * valid
        lane2 = lax.broadcasted_iota(jnp.int32, (2 * BLK, 128), 1)
        dk_all = jnp.where(lane2 < HEAD_DIM, dk_heads[0], dk_heads[1])
        dv_all = jnp.where(lane2 < HEAD_DIM, dv_heads[0], dv_heads[1])
        kv_prev[:, 0:128] = dk_all[0:BLK] * valid
        kv_prev[:, 128:256] = dv_all[0:BLK] * valid
        kv_new[:, 0:128] = dk_all[BLK:2 * BLK]
        kv_new[:, 128:256] = dv_all[BLK:2 * BLK]

        @pl.when(n >= 1)
        def _():
            dkv = kv_held[...] + kv_prev[...]
            dk = dkv[:, 0:128]
            cos_p, sin_p = _attn_rope(cosp_ref[...], sinp_ref[...])
            dk = dk * cos_p - _swap_halves(dk, HEAD_DIM // 2) * sin_p
            dv = dkv[:, 128:256]
            dp_ref[:, 1024:1152] = dk.astype(dp_ref.dtype)
            dp_ref[:, 1152:1280] = dv.astype(dp_ref.dtype)
            db_ref[:, 1024:1152] += jnp.sum(dk, axis=0, keepdims=True)
            db_ref[:, 1152:1280] += jnp.sum(dv, axis=0, keepdims=True)

        kv_held[...] = kv_new[...]

    blk = lambda w: pl.BlockSpec((BLK, w), lambda n: (jnp.minimum(n, nblk - 1), 0))
    pblk = lambda w: pl.BlockSpec((BLK, w), lambda n: (jnp.maximum(n - 1, 0), 0))
    wide = (2 * BLK, GROUP * BLK)
    return _call(body, name="attn_bwd", grid=(nblk + 1,),
                 in_specs=[pl.BlockSpec(memory_space=pltpu.SMEM), blk(1024), blk(128), pblk(128), blk(128), pblk(128),
                           blk(1024), blk(128), blk(128), pblk(128), pblk(128)],
                 out_specs=[pblk(N_ATTN), _acc(128), _acc(N_ATTN)],
                 out_shape=[_sds((s, N_ATTN), _MXU), _sds((1, 128), F32), _sds((1, N_ATTN), F32)],
                 scratch_shapes=[pltpu.VMEM((2 * BLK, BLK), F32), pltpu.VMEM(wide, F32), pltpu.VMEM(wide, F32),
                                 pltpu.VMEM(wide, _MXU), pltpu.VMEM(wide, _MXU), pltpu.VMEM((BLK, 1024), _MXU),
                                 pltpu.VMEM((BLK, 256), F32), pltpu.VMEM((BLK, 256), F32), pltpu.VMEM((BLK, 256), F32)],
                 args=(sinks, qr, kr, kr, vb, vb, day, cos, sin, cos, sin), exchange=exchange)


def _group_norm(y):
    mu = jnp.mean(y, axis=-1, keepdims=True)
    yc = y - mu
    rs = lax.rsqrt(jnp.mean(yc * yc, axis=-1, keepdims=True) + EPS)
    return yc * rs, rs


GATE_COL = N_RET // 1024


def _merge_fwd_call(x, ry, proj, ay, gn_g, w_ro, w_ao, w_o, tm):
    s = x.shape[0]

    def body(x_ref, ry_ref, rg_ref, ga_ref, gb_ref, ay_ref, gn_ref, wro_ref, wao_ref, wo_ref,
             ain_ref, a_ref, b_ref, mg_ref, x1_ref):
        for h in range(RET_HEADS):
            hv = slice(h * RET_DV, (h + 1) * RET_DV)
            yhat, _ = _group_norm(ry_ref[:, hv])
            rg = rg_ref[:, hv]
            ain_ref[:, hv] = ((rg * _sigmoid(rg)) * (yhat * gn_ref[:, hv])).astype(ain_ref.dtype)
        a = _dot(ain_ref[...], wro_ref[...])
        b = _dot(ay_ref[...], wao_ref[...])
        a_ref[...] = a.astype(a_ref.dtype)
        b_ref[...] = b.astype(b_ref.dtype)
        merged = (_sigmoid(ga_ref[...]) * a + _sigmoid(gb_ref[...]) * b).astype(mg_ref.dtype)
        mg_ref[...] = merged
        x1_ref[...] = x_ref[...] + _dot(merged, wo_ref[...])

    return _call(body, name="merge_fwd", grid=(s // tm,),
                 in_specs=[_rows(tm, D), _rows(tm, 1024), _rows(tm, 1024, GATE_COL), _rows(tm, 1024, GATE_COL + 1),
                           _rows(tm, 1024, GATE_COL + 2), _rows(tm, 1024), _acc(1024),
                           _const((D, D)), _const((D, D)), _const((D, D))],
                 out_specs=[_rows(tm, D)] * 5,
                 out_shape=[_sds((s, D), _MXU), _sds((s, D), _STORE), _sds((s, D), _STORE), _sds((s, D), _MXU),
                            _sds((s, D), F32)],
                 args=(x, ry, proj, proj, proj, ay, gn_g, w_ro, w_ao, w_o), big=True)


def _merge_bwd_call(dx1, a, b, ry, proj, gn_g, w_ro, w_ao, w_o, tm):
    s = dx1.shape[0]

    def body(dx1_ref, a_ref, b_ref, ry_ref, rg_ref, ga_ref, gb_ref, gn_ref, wro_ref, wao_ref, wo_ref,
             da_ref, dbr_ref, dp_ref, day_ref, dry_ref, dbias_ref, dgn_ref):
        @pl.when(pl.program_id(0) == 0)
        def _():
            dbias_ref[...] = jnp.zeros_like(dbias_ref)
            dgn_ref[...] = jnp.zeros_like(dgn_ref)

        d_merged = _dot_nt(dx1_ref[...], wo_ref[...])
        sa, sb = _sigmoid(ga_ref[...]), _sigmoid(gb_ref[...])
        d_a = d_merged * sa
        d_b = d_merged * sb
        da_ref[...] = d_a.astype(da_ref.dtype)
        dbr_ref[...] = d_b.astype(dbr_ref.dtype)
        d_ga = d_merged * a_ref[...] * (sa * (1.0 - sa))
        d_gb = d_merged * b_ref[...] * (sb * (1.0 - sb))
        dp_ref[:, 1024:2048] = d_ga.astype(dp_ref.dtype)
        dp_ref[:, 2048:3072] = d_gb.astype(dp_ref.dtype)
        dbias_ref[:, 1024:2048] += jnp.sum(d_ga, axis=0, keepdims=True)
        dbias_ref[:, 2048:3072] += jnp.sum(d_gb, axis=0, keepdims=True)
        day_ref[...] = _dot_nt(d_b, wao_ref[...]).astype(day_ref.dtype)
        d_ain = _dot_nt(d_a, wro_ref[...])
        for h in range(RET_HEADS):
            hv = slice(h * RET_DV, (h + 1) * RET_DV)
            yhat, rs = _group_norm(ry_ref[:, hv])
            rg = rg_ref[:, hv]
            sg = _sigmoid(rg)
            gn = gn_ref[:, hv]
            d_h = d_ain[:, hv]
            d_rg = d_h * (yhat * gn) * (sg * (1.0 + rg * (1.0 - sg)))
            d_ryn = d_h * (rg * sg)
            dgn_ref[:, hv] += jnp.sum(d_ryn * yhat, axis=0, keepdims=True)
            d_yhat = d_ryn * gn
            dry_ref[:, hv] = (rs * (d_yhat - jnp.mean(d_yhat, axis=-1, keepdims=True)
                                    - yhat * jnp.mean(d_yhat * yhat, axis=-1, keepdims=True))).astype(dry_ref.dtype)
            dp_ref[:, hv] = d_rg.astype(dp_ref.dtype)
            dbias_ref[:, hv] += jnp.sum(d_rg, axis=0, keepdims=True)

    return _call(body, name="merge_bwd", grid=(s // tm,),
                 in_specs=[_rows(tm, D), _rows(tm, D), _rows(tm, D), _rows(tm, 1024), _rows(tm, 1024, GATE_COL),
                           _rows(tm, 1024, GATE_COL + 1), _rows(tm, 1024, GATE_COL + 2), _acc(1024),
                           _const((D, D)), _const((D, D)), _const((D, D))],
                 out_specs=[_rows(tm, D), _rows(tm, D), _rows(tm, N_GATE), _rows(tm, D), _rows(tm, D), _acc(N_GATE),
                            _acc(1024)],
                 out_shape=[_sds((s, D), _MXU), _sds((s, D), _MXU), _sds((s, N_GATE), _MXU), _sds((s, D), _MXU),
                            _sds((s, D), _STORE), _sds((1, N_GATE), F32), _sds((1, 1024), F32)],
                 args=(dx1, a, b, ry, proj, proj, proj, gn_g, w_ro, w_ao, w_o), big=True)


def _ffn_fwd_call(x1, target, ln2_g, lnf_g, w_g, w_u, w_d, tm):
    s = x1.shape[0]

    def body(x1_ref, t_ref, g2_ref, gf_ref, wg_ref, wu_ref, wd_ref,
             h2_ref, g_ref, u_ref, f_ref, dx2_ref, loss_ref, dgf_ref):
        @pl.when(pl.program_id(0) == 0)
        def _():
            loss_ref[...] = jnp.zeros_like(loss_ref)
            dgf_ref[...] = jnp.zeros_like(dgf_ref)

        x1v = x1_ref[...]
        r1 = lax.rsqrt(jnp.mean(x1v * x1v, axis=-1, keepdims=True) + EPS)
        h2 = ((x1v * r1) * g2_ref[...]).astype(h2_ref.dtype)
        h2_ref[...] = h2
        g = _dot_nt(h2, wg_ref[...])
        u = _dot_nt(h2, wu_ref[...])
        g_ref[...] = g
        u_ref[...] = u
        f = ((g * _sigmoid(g)) * u).astype(f_ref.dtype)
        f_ref[...] = f
        x2 = x1v + _dot(f, wd_ref[...])
        r2 = lax.rsqrt(jnp.mean(x2 * x2, axis=-1, keepdims=True) + EPS)
        xhat = x2 * r2
        err = xhat * gf_ref[...] - t_ref[...]
        loss_ref[...] += 0.5 * jnp.sum(jnp.mean(err * err, axis=-1, keepdims=True))
        dy = err * (1.0 / D)
        dgf_ref[...] += jnp.sum(dy * xhat, axis=0, keepdims=True)
        dxh = dy * gf_ref[...]
        dx2_ref[...] = r2 * (dxh - xhat * jnp.mean(dxh * xhat, axis=-1, keepdims=True))

    return _call(body, name="ffn_fwd", grid=(s // tm,),
                 in_specs=[_rows(tm, D), _rows(tm, D), _acc(D), _acc(D), _const((D_FF, D)), _const((D_FF, D)),
                           _const((D_FF, D))],
                 out_specs=[_rows(tm, D), _rows(tm, D_FF), _rows(tm, D_FF), _rows(tm, D_FF), _rows(tm, D), _acc(128),
                            _acc(D)],
                 out_shape=[_sds((s, D), _MXU), _sds((s, D_FF), F32), _sds((s, D_FF), F32), _sds((s, D_FF), _MXU),
                            _sds((s, D), F32), _sds((1, 128), F32), _sds((1, D), F32)],
                 args=(x1, target, ln2_g, lnf_g, w_g, w_u, w_d), big=True)


def _ffn_bwd_call(dx2, x1, g, u, ln2_g, w_g, w_u, w_d, tm):
    s = dx2.shape[0]

    def body(dx2_ref, x1_ref, g_ref, u_ref, g2_ref, wg_ref, wu_ref, wd_ref, dx1_ref, dg_ref, du_ref, dg2_ref):
        @pl.when(pl.program_id(0) == 0)
        def _():
            dg2_ref[...] = jnp.zeros_like(dg2_ref)

        dx2v = dx2_ref[...]
        df = _dot_nt(dx2v, wd_ref[...])
        gv, uv = g_ref[...], u_ref[...]
        sg = _sigmoid(gv)
        du = (df * (gv * sg)).astype(du_ref.dtype)
        dg = (df * uv * (sg * (1.0 + gv * (1.0 - sg)))).astype(dg_ref.dtype)
        du_ref[...] = du
        dg_ref[...] = dg
        dh2 = _dot(dg, wg_ref[...]) + _dot(du, wu_ref[...])
        x1v = x1_ref[...]
        r1 = lax.rsqrt(jnp.mean(x1v * x1v, axis=-1, keepdims=True) + EPS)
        xhat = x1v * r1
        dg2_ref[...] += jnp.sum(dh2 * xhat, axis=0, keepdims=True)
        dxh = dh2 * g2_ref[...]
        dx1_ref[...] = dx2v + r1 * (dxh - xhat * jnp.mean(dxh * xhat, axis=-1, keepdims=True))

    return _call(body, name="ffn_bwd", grid=(s // tm,),
                 in_specs=[_rows(tm, D), _rows(tm, D), _rows(tm, D_FF), _rows(tm, D_FF), _acc(D),
                           _const((D_FF, D)), _const((D_FF, D)), _const((D_FF, D))],
                 out_specs=[_rows(tm, D), _rows(tm, D_FF), _rows(tm, D_FF), _acc(D)],
                 out_shape=[_sds((s, D), F32), _sds((s, D_FF), _MXU), _sds((s, D_FF), _MXU), _sds((1, D), F32)],
                 args=(dx2, x1, g, u, ln2_g, w_g, w_u, w_d), big=True)


def _dx_call(x, dx1, dp_ret, dp_gate, dp_attn, ln1_g, w_in, tm, exchange):
    s = x.shape[0]

    def body(x_ref, dx1_ref, dr_ref, dg_ref, da_ref, g1_ref, w_ref, dx_ref, dg1_ref):
        @pl.when(pl.program_id(0) == 0)
        def _():
            dg1_ref[...] = jnp.zeros_like(dg1_ref)

        dh = (_dot(dr_ref[...], w_ref[0:2048, :]) + _dot(dg_ref[:, 0:1024], w_ref[2048:3072, :])
              + _dot(da_ref[...], w_ref[3072:4352, :]) + _dot(dg_ref[:, 1024:3072], w_ref[4352:6400, :]))
        xv = x_ref[...]
        r = lax.rsqrt(jnp.mean(xv * xv, axis=-1, keepdims=True) + EPS)
        xhat = xv * r
        dg1_ref[...] += jnp.sum(dh * xhat, axis=0, keepdims=True)
        dxh = dh * g1_ref[...]
        dx_ref[...] = dx1_ref[...] + r * (dxh - xhat * jnp.mean(dxh * xhat, axis=-1, keepdims=True))

    return _call(body, name="dx", grid=(s // tm,),
                 in_specs=[_rows(tm, D), _rows(tm, D), _rows(tm, N_RET), _rows(tm, N_GATE), _rows(tm, N_ATTN), _acc(D),
                           _const((D_IN, D))],
                 out_specs=[_rows(tm, D), _acc(D)],
                 out_shape=[_sds((s, D), F32), _sds((1, D), F32)],
                 args=(x, dx1, dp_ret, dp_gate, dp_attn, ln1_g, w_in), big=True, exchange=exchange)


def _adamw(g, w, m, v):
    m_new = B1 * m + (1.0 - B1) * g
    v_new = B2 * v + (1.0 - B2) * (g * g)
    m_hat = m_new / (1.0 - B1 ** STEP)
    v_hat = v_new / (1.0 - B2 ** STEP)
    return -LR * (m_hat / (jnp.sqrt(v_hat) + ADAM_EPS) + WD * w), m_new, v_new


def _slot_sum(p_ref):
    g = p_ref[0].astype(F32)
    for k in range(1, N_DEV):
        g = g + p_ref[k].astype(F32)
    return g


def _adamw_call(parts, ws, ms, vs, name, tr):
    n = len(ws)
    rows, cols = ws[0].shape

    def body(*refs):
        p_refs, w_refs, m_refs, v_refs = (refs[k * n:(k + 1) * n] for k in range(4))
        outs = refs[4 * n:]
        for i in range(n):
            g = _slot_sum(p_refs[i])
            outs[4 * i][...] = g
            outs[4 * i + 1][...], outs[4 * i + 2][...], outs[4 * i + 3][...] = _adamw(
                g, w_refs[i][...], m_refs[i][...], v_refs[i][...])

    p_spec = pl.BlockSpec((N_DEV, tr, cols), lambda i: (0, i, 0))
    spec = pl.BlockSpec((tr, cols), lambda i: (i, 0))
    res = _call(body, name=name, grid=(rows // tr,), in_specs=[p_spec] * n + [spec] * (3 * n), out_specs=[spec] * (4 * n),
                out_shape=[_sds((rows, cols), F32)] * (4 * n), args=(*parts, *ws, *ms, *vs), big=True)
    return [res[4 * i:4 * i + 4] for i in range(n)]


SMALL_WIDTHS = [1024, 6400, 1024, 16, 1024, 1024]
SMALL_OFFSETS = [0, 1024, 7424, 8448, 8576, 9600]
LOSS_OFFSET = 10624
SMALL_LEN = 10752


def _pack_small(grads, loss):
    pieces = []
    for gr, width in zip(grads, SMALL_WIDTHS):
        pieces.append(jnp.pad(gr.reshape(1, width), ((0, 0), (0, -width % 128))))
    pieces.append(jnp.pad(loss.reshape(1, 1), ((0, 0), (0, 127))))
    return jnp.concatenate(pieces, axis=1)


def _adamw_small_call(parts, ws, ms, vs):
    n = len(ws)

    def body(*refs):
        p_ref, w_refs, m_refs, v_refs = refs[0], refs[1:1 + n], refs[1 + n:1 + 2 * n], refs[1 + 2 * n:1 + 3 * n]
        outs = refs[1 + 3 * n:]
        g_all = _slot_sum(p_ref)
        for i, (off, width) in enumerate(zip(SMALL_OFFSETS, SMALL_WIDTHS)):
            g = g_all[:, off:off + width]
            outs[i][...] = g
            outs[n + i][...], outs[2 * n + i][...], outs[3 * n + i][...] = _adamw(
                g, w_refs[i][...], m_refs[i][...], v_refs[i][...])
        outs[4 * n][...] = g_all[:, LOSS_OFFSET:LOSS_OFFSET + 128]

    whole = lambda shape: pl.BlockSpec(shape, lambda i: (0,) * len(shape))
    small = [whole((1, w)) for w in SMALL_WIDTHS]
    res = _call(body, name="adamw_small", grid=(1,), in_specs=[whole((N_DEV, 1, SMALL_LEN))] + small * 3,
                out_specs=small * 4 + [whole((1, 128))],
                out_shape=[_sds((1, w), F32) for w in SMALL_WIDTHS] * 4 + [_sds((1, 128), F32)],
                args=(parts, *ws, *ms, *vs))
    return [res[k * n:(k + 1) * n] for k in range(4)], res[4 * n]


def kernel(x, ln1_g, w_in, b_in, ret_norm_g, w_ret_out, attn_sinks, w_attn_out, w_out, ln2_g, w_ffn_gate, w_ffn_up, w_ffn_down, lnf_g, loss_target, m_ln1_g, m_w_in, m_b_in, m_ret_norm_g, m_w_ret_out, m_attn_sinks, m_w_attn_out, m_w_out, m_ln2_g, m_w_ffn_gate, m_w_ffn_up, m_w_ffn_down, m_lnf_g, v_ln1_g, v_w_in, v_b_in, v_ret_norm_g, v_w_ret_out, v_attn_sinks, v_w_attn_out, v_w_out, v_ln2_g, v_w_ffn_gate, v_w_ffn_up, v_w_ffn_down, v_lnf_g):
    cast = lambda a: a.astype(_MXU)
    xs, target = x[0], loss_target[0]
    s = xs.shape[0]
    r_sq = w_ret_out.shape[1]
    r_dn = w_ffn_down.shape[1]
    c_in = w_in.shape[2]
    c_ff = w_ffn_gate.shape[2]
    tm, tm_wide, tk = min(256, s), min(512, s), min(2048, s)
    lnf_row = lnf_g.reshape(1, D)
    cos_t, sin_t = _rope_tables(s)
    decays = _retention_decays()
    tr_shard = lambda a: a[0].T
    per_dev = lambda a, n: a.reshape(N_DEV, n, D)

    (h,), (all_in,) = _ln_call(xs, ln1_g, tm, _AllGather([cast(tr_shard(w_in))]))
    wt_in = all_in.reshape(N_DEV * c_in, D)
    rest = [tr_shard(w_ffn_gate), tr_shard(w_ffn_up), w_ret_out[0], w_attn_out[0], w_out[0], w_ffn_down[0]]
    (proj,), gathered = _proj_call(h, wt_in, b_in, _AllGather([cast(a) for a in rest]))
    wt_g, wt_u, full_ro, full_ao, full_o, full_d = (a.reshape(N_DEV * a.shape[1], D) for a in gathered)
    ry, qr, kr, states = _ret_fwd_call(proj, cos_t, sin_t, decays)
    ay, aqr, akr, avb = _attn_fwd_call(proj, attn_sinks, cos_t, sin_t)
    a_in, br_a, br_b, merged, x1 = _merge_fwd_call(xs, ry, proj, ay, ret_norm_g, full_ro, full_ao, full_o, tm_wide)
    h2, g, u, f, dx2, loss, d_lnf = _ffn_fwd_call(x1, target, ln2_g, lnf_row, wt_g, wt_u, full_d, tm)

    dx1, dg, du, d_ln2 = _ffn_bwd_call(dx2, x1, g, u, ln2_g, wt_g, wt_u, full_d, tm)
    dw_d = _mm_tn(f, dx2, "dw_ffn_down", 1408, 1024, tk)
    dwt_g = _mm_tn(dg, h2, "dw_ffn_gate", 1408, 1024, tk)
    dwt_u = _mm_tn(du, h2, "dw_ffn_up", 1408, 1024, tk)
    d_a, d_b, dp_gate, day, dry, db_gate, d_gn = _merge_bwd_call(
        dx1, br_a, br_b, ry, proj, ret_norm_g, full_ro, full_ao, full_o, tm_wide)
    dw_o = _mm_tn(merged, dx1, "dw_out", 1024, 1024, tk)
    dw_ro = _mm_tn(a_in, d_a, "dw_ret_out", 1024, 1024, tk)
    dw_ao = _mm_tn(ay, d_b, "dw_attn_out", 1024, 1024, tk)
    (dp_attn, d_sinks, db_attn), (got_g, got_u, got_d) = _attn_bwd_call(
        aqr, akr, avb, day, attn_sinks, cos_t, sin_t,
        _AllToAll([per_dev(dwt_g, c_ff), per_dev(dwt_u, c_ff), per_dev(dw_d, r_dn)]))
    (dp_ret, db_ret), (got_ro, got_ao, got_o) = _ret_bwd_call(
        qr, kr, proj, states, dry, cos_t, sin_t, decays,
        _AllToAll([per_dev(dw_ro, r_sq), per_dev(dw_ao, r_sq), per_dev(dw_o, r_sq)]))
    tk_in = min(4096, s)
    dwt_ret = _mm_tn(dp_ret, h, "dw_in_ret", 1024, 1024, tk_in)
    dwt_gate = _mm_tn(dp_gate, h, "dw_in_gate", 1024, 1024, tk_in)
    dwt_attn = _mm_tn(dp_attn, h, "dw_in_attn", 1280, 1024, tk_in)
    in_pieces = [(0, 0, 2048), (1, 0, 1024), (2, 0, 1280), (1, 1024, 3072)]
    (dx, d_ln1), (got_in,) = _dx_call(xs, dx1, dp_ret, dp_gate, dp_attn, ln1_g, wt_in, tm_wide,
                                      _RowScatter([dwt_ret, dwt_gate, dwt_attn], in_pieces, c_in))
    db_in = jnp.concatenate([db_ret, db_gate[:, 0:1024], db_attn, db_gate[:, 1024:3072]], axis=1)
    small = [d_ln1, db_in, d_gn, d_sinks[:, 0:Q_HEADS], d_ln2, d_lnf]
    (got_small,) = _exchange_call(_AllGather([_pack_small(small, loss[0, 0])]), "gather_small")

    transposed = ("w_in", "w_ffn_gate", "w_ffn_up")
    res = {}
    (res["w_in"],) = _adamw_call([got_in], [tr_shard(w_in)], [tr_shard(m_w_in)], [tr_shard(v_w_in)], "adamw_w_in", 160)
    res["w_ffn_gate"], res["w_ffn_up"], res["w_ffn_down"] = _adamw_call(
        [got_g, got_u, got_d], [tr_shard(w_ffn_gate), tr_shard(w_ffn_up), w_ffn_down[0]],
        [tr_shard(m_w_ffn_gate), tr_shard(m_w_ffn_up), m_w_ffn_down[0]],
        [tr_shard(v_w_ffn_gate), tr_shard(v_w_ffn_up), v_w_ffn_down[0]], "adamw_ffn", 176)
    res["w_ret_out"], res["w_attn_out"], res["w_out"] = _adamw_call(
        [got_ro, got_ao, got_o], [w_ret_out[0], w_attn_out[0], w_out[0]],
        [m_w_ret_out[0], m_w_attn_out[0], m_w_out[0]], [v_w_ret_out[0], v_w_attn_out[0], v_w_out[0]], "adamw_square", r_sq)
    small_names = ["ln1_g", "b_in", "ret_norm_g", "attn_sinks", "ln2_g", "lnf_g"]
    small_res, loss_row = _adamw_small_call(
        got_small, [ln1_g, b_in, ret_norm_g, attn_sinks, ln2_g, lnf_row],
        [m_ln1_g, m_b_in, m_ret_norm_g, m_attn_sinks, m_ln2_g, m_lnf_g.reshape(1, D)],
        [v_ln1_g, v_b_in, v_ret_norm_g, v_attn_sinks, v_ln2_g, v_lnf_g.reshape(1, D)])
    for i, nm in enumerate(small_names):
        res[nm] = [small_res[kind][i] for kind in range(4)]

    order = ["ln1_g", "w_in", "b_in", "ret_norm_g", "w_ret_out", "attn_sinks", "w_attn_out", "w_out", "ln2_g",
             "w_ffn_gate", "w_ffn_up", "w_ffn_down", "lnf_g"]
    outs = [loss_row[0, 0], dx[None]]
    for kind in range(4):
        for nm in order:
            val = res[nm][kind]
            if nm in transposed:
                val = val.T
            outs.append(val[None] if nm.startswith("w_") else val.reshape(D) if nm == "lnf_g" else val)
    return tuple(outs)
```

```python
import math

import numpy as np
import jax
import jax.numpy as jnp
from jax import lax
from jax.experimental import pallas as pl
from jax.experimental.pallas import tpu as pltpu

F32 = jnp.float32
_MXU = jnp.bfloat16
_STORE = jnp.bfloat16

N_DEV = 8
D = 1024
RET_HEADS, RET_DK, RET_DV = 4, 128, 256
BLK = 128
Q_HEADS, KV_HEADS, HEAD_DIM = 16, 2, 64
GROUP = Q_HEADS // KV_HEADS
D_FF = 2816
N_RET, N_GATE, N_ATTN = 2048, 3072, 1280
D_IN = N_RET + N_GATE + N_ATTN
ROPE_THETA = 10000.0
EPS = 1e-6
RET_SCALE = RET_DK ** -0.5
ATTN_SCALE = HEAD_DIM ** -0.5
LR, B1, B2, ADAM_EPS, WD, STEP = 0.001, 0.9, 0.999, 1e-08, 0.01, 10
VMEM_LIMIT_MB = 56
MESH = pl.DeviceIdType.MESH


def _dot(a, b):
    return jnp.dot(a.astype(_MXU), b.astype(_MXU), preferred_element_type=F32)


def _dot_nt(a, b):
    return lax.dot_general(a.astype(_MXU), b.astype(_MXU), (((1,), (1,)), ((), ())), preferred_element_type=F32)


def _dot_tn(a, b):
    return lax.dot_general(a.astype(_MXU), b.astype(_MXU), (((0,), (0,)), ((), ())), preferred_element_type=F32)


def _sigmoid(x):
    return 1.0 / (1.0 + jnp.exp(-x))


def _cparams(n_axes, big=False):
    kw = dict(dimension_semantics=("arbitrary",) * n_axes)
    if big:
        kw["vmem_limit_bytes"] = VMEM_LIMIT_MB * 2**20
    return pltpu.CompilerParams(**kw)


def _rows(tm, width, col=0):
    return pl.BlockSpec((tm, width), lambda i: (i, col))


def _const(shape):
    nd = len(shape)
    return pl.BlockSpec(shape, lambda *_: (0,) * nd, pipeline_mode=pl.Buffered(1))


def _acc(width):
    return pl.BlockSpec((1, width), lambda *_: (0, 0))


def _sds(shape, dtype):
    return jax.ShapeDtypeStruct(shape, dtype)


def _swap_halves(x, half):
    w = x.shape[-1]
    if 2 * half == w:
        return pltpu.roll(x, half, 1)
    lane = lax.broadcasted_iota(jnp.int32, x.shape, 1)
    return jnp.where(lane % (2 * half) < half, pltpu.roll(x, w - half, 1), pltpu.roll(x, half, 1))


def _rope_tables(seq):
    lane = jnp.arange(128, dtype=jnp.int32)
    ret_freq = ROPE_THETA ** (-(lane % 64).astype(F32) / 64)
    attn_freq = ROPE_THETA ** (-(lane % 32).astype(F32) / 32)
    ang = jnp.arange(seq, dtype=jnp.int32).astype(F32)[:, None] * jnp.where(lane < 64, ret_freq, attn_freq)[None, :]
    return jnp.cos(ang), jnp.sin(ang)


def _ret_rope(cos, sin):
    low = lax.broadcasted_iota(jnp.int32, cos.shape, 1) < RET_DK // 2
    return jnp.where(low, cos, pltpu.roll(cos, RET_DK // 2, 1)), jnp.where(low, -sin, pltpu.roll(sin, RET_DK // 2, 1))


def _attn_rope(cos, sin):
    lane = lax.broadcasted_iota(jnp.int32, cos.shape, 1)
    half = HEAD_DIM // 2

    def spread(t):
        t = pltpu.roll(t, 64, 1)
        t = jnp.where(lane < half, t, pltpu.roll(t, half, 1))
        return jnp.where(lane < HEAD_DIM, t, pltpu.roll(t, HEAD_DIM, 1))

    return spread(cos), jnp.where(lane % HEAD_DIM < half, -spread(sin), spread(sin))


def _retention_decays():
    log_gamma = np.log1p(-np.exp2(-5.0 - np.arange(RET_HEADS, dtype=np.float32))).astype(np.float32)
    idx = np.arange(BLK, dtype=np.float32)
    rel = idx[:, None] - idx[None, :]
    intra = np.where(rel[None] >= 0, np.exp(log_gamma[:, None, None] * np.maximum(rel, 0.0)[None]), 0.0)
    q_decay = np.exp(log_gamma[:, None] * (idx + 1.0))[:, :, None]
    k_decay = np.exp(log_gamma[:, None] * (BLK - 1.0 - idx))[:, :, None]
    chunk_decay = [float(np.exp(np.float32(lg * BLK))) for lg in log_gamma]
    return (jnp.asarray(intra, F32), jnp.asarray(q_decay, F32), jnp.asarray(k_decay, F32), chunk_decay)


def _position():
    return lax.axis_index("x"), lax.axis_index("y"), lax.axis_index("c")


def _slot(px, py, pc):
    return 4 * px + 2 * py + pc


class _AllGather:
    def __init__(self, blocks):
        self.blocks = list(blocks)
        nb = len(self.blocks)
        self.out_shape = [_sds((N_DEV,) + b.shape, b.dtype) for b in self.blocks]
        self.scratch = [pltpu.SemaphoreType.DMA((nb, 7)), pltpu.SemaphoreType.DMA((nb, 7)),
                        pltpu.SemaphoreType.DMA((nb,))]

    def phases(self, ins, outs, send_sems, recv_sems, local_sems):
        nb = len(ins)
        x, y, c = _position()
        me, sibling = (x, y, c), (x, y, 1 - c)
        chips = [(1 - x, y), (x, 1 - y), (1 - x, 1 - y)]

        def copy(b, k, block, to, src=None):
            dst = outs[b].at[_slot(*block)]
            return pltpu.make_async_remote_copy(
                src_ref=dst if src is None else src, dst_ref=dst, send_sem=send_sems.at[b, k],
                recv_sem=recv_sems.at[b, k], device_id=to, device_id_type=MESH)

        def own(b):
            return pltpu.make_async_copy(ins[b], outs[b].at[_slot(*me)], local_sems.at[b])

        def first(b):
            return [copy(b, 0, me, sibling, src=ins[b])] + [
                copy(b, 1 + j, me, (*chip, c), src=ins[b]) for j, chip in enumerate(chips)]

        def start():
            for b in range(nb):
                own(b).start()
                for cp in first(b):
                    cp.start()

        def forward():
            for b in range(nb):
                for j, chip in enumerate(chips):
                    copy(b, 1 + j, (*chip, c), me).wait_recv()
                    copy(b, 4 + j, (*chip, c), sibling).start()

        def finish():
            for b in range(nb):
                copy(b, 0, sibling, me).wait_recv()
                for j, chip in enumerate(chips):
                    copy(b, 4 + j, (*chip, 1 - c), me).wait_recv()
            for b in range(nb):
                for cp in first(b):
                    cp.wait_send()
                for j, chip in enumerate(chips):
                    copy(b, 4 + j, (*chip, c), sibling).wait_send()
                own(b).wait()

        return start, forward, finish


class _AllToAll:
    def __init__(self, blocks):
        self.blocks = list(blocks)
        nb = len(self.blocks)
        self.out_shape = [_sds(b.shape, b.dtype) for b in self.blocks]
        self.scratch = [pltpu.SemaphoreType.DMA((nb, 7)), pltpu.SemaphoreType.DMA((nb, 7)),
                        pltpu.SemaphoreType.DMA((nb,))]

    def phases(self, ins, outs, send_sems, recv_sems, local_sems):
        nb = len(ins)
        x, y, c = _position()
        flip = lambda v, bit: 1 - v if bit else v
        peers = [(flip(x, k >> 2 & 1), flip(y, k >> 1 & 1), flip(c, k & 1)) for k in range(1, N_DEV)]

        def copy(b, k, peer, landed=False):
            return pltpu.make_async_remote_copy(
                src_ref=ins[b].at[_slot(*peer)], dst_ref=outs[b].at[_slot(*peer) if landed else _slot(x, y, c)],
                send_sem=send_sems.at[b, k], recv_sem=recv_sems.at[b, k], device_id=peer, device_id_type=MESH)

        def own(b):
            return pltpu.make_async_copy(ins[b].at[_slot(x, y, c)], outs[b].at[_slot(x, y, c)], local_sems.at[b])

        def start():
            for b in range(nb):
                own(b).start()
                for k, peer in enumerate(peers):
                    copy(b, k, peer).start()

        def forward():
            pass

        def finish():
            for b in range(nb):
                for k, peer in enumerate(peers):
                    copy(b, k, peer, landed=True).wait_recv()
            for b in range(nb):
                for k, peer in enumerate(peers):
                    copy(b, k, peer).wait_send()
                own(b).wait()

        return start, forward, finish


class _RowScatter:
    def __init__(self, arrays, pieces, n):
        self.blocks = list(arrays)
        self.pieces, self.n = pieces, n
        self.out_shape = [_sds((N_DEV, n, D), arrays[0].dtype)]
        self.scratch = [pltpu.SemaphoreType.DMA((N_DEV,)), pltpu.SemaphoreType.DMA((N_DEV,)), pltpu.SemaphoreType.DMA]

    def _parts(self, k):
        lo, hi, pos, res = k * self.n, (k + 1) * self.n, 0, []
        for arr, first, last in self.pieces:
            a, b = max(lo, pos), min(hi, pos + last - first)
            if a < b:
                res.append((arr, first + a - pos, b - a, a - lo))
            pos += last - first
        return res

    def phases(self, ins, outs, send_sems, recv_sems, local_sem):
        (out,) = outs
        x, y, c = _position()
        me = _slot(x, y, c)

        def start():
            for k in range(N_DEV):
                dist = jnp.bitwise_xor(me, k)

                @pl.when(me != k)
                def _():
                    for arr, first, rows, at in self._parts(k):
                        pltpu.make_async_remote_copy(
                            src_ref=ins[arr].at[pl.ds(first, rows)], dst_ref=out.at[me, pl.ds(at, rows)],
                            send_sem=send_sems.at[dist], recv_sem=recv_sems.at[dist],
                            device_id=(k >> 2 & 1, k >> 1 & 1, k & 1), device_id_type=MESH).start()

                @pl.when(me == k)
                def _():
                    for arr, first, rows, at in self._parts(k):
                        pltpu.make_async_copy(ins[arr].at[pl.ds(first, rows)], out.at[me, pl.ds(at, rows)], local_sem).start()

        def forward():
            pass

        def whole_block(dist):
            return pltpu.make_async_remote_copy(
                src_ref=out.at[me], dst_ref=out.at[jnp.bitwise_xor(me, dist)], send_sem=send_sems.at[dist],
                recv_sem=recv_sems.at[dist], device_id=(x, y, c), device_id_type=MESH)

        def finish():
            for dist in range(1, N_DEV):
                whole_block(dist).wait_recv()
            for dist in range(1, N_DEV):
                whole_block(dist).wait_send()
            pltpu.make_async_copy(out.at[me], out.at[me], local_sem).wait()

        return start, forward, finish


def _call(body, *, name, grid, in_specs, out_specs, out_shape, args, scratch_shapes=(), big=False, exchange=None):
    params = _cparams(len(grid), big)
    if exchange is None:
        return pl.pallas_call(body, name=name, grid=grid, in_specs=in_specs, out_specs=out_specs, out_shape=out_shape,
                              scratch_shapes=list(scratch_shapes), compiler_params=params)(*args)
    n_in, n_out, n_scr = len(in_specs), len(out_specs), len(scratch_shapes)
    nb, nb_out = len(exchange.blocks), len(exchange.out_shape)
    steps = math.prod(grid)

    def carried(*refs):
        pos = 0
        parts = []
        for n in (n_in, nb, n_out, nb_out, n_scr, len(exchange.scratch)):
            parts.append(refs[pos:pos + n])
            pos += n
        ins, x_ins, outs, x_outs, scr, sems = parts
        step = pl.program_id(0)
        for axis in range(1, len(grid)):
            step = step * grid[axis] + pl.program_id(axis)
        start, forward, finish = exchange.phases(x_ins, x_outs, *sems)
        pl.when(step == 0)(start)
        body(*ins, *outs, *scr)

        @pl.when(step == steps - 1)
        def _():
            forward()
            finish()

    any_spec = pl.BlockSpec(memory_space=pl.ANY)
    res = pl.pallas_call(
        carried, name=name, grid=grid, in_specs=list(in_specs) + [any_spec] * nb,
        out_specs=list(out_specs) + [any_spec] * nb_out, out_shape=list(out_shape) + exchange.out_shape,
        scratch_shapes=list(scratch_shapes) + exchange.scratch, compiler_params=params)(*args, *exchange.blocks)
    return res[:n_out], res[n_out:]


def _exchange_call(exchange, name):
    nb = len(exchange.blocks)

    def body(*refs):
        start, forward, finish = exchange.phases(refs[:nb], refs[nb:2 * nb], *refs[2 * nb:])
        start()
        forward()
        finish()

    any_spec = pl.BlockSpec(memory_space=pl.ANY)
    return pl.pallas_call(body, name=name, in_specs=[any_spec] * nb, out_specs=[any_spec] * nb,
                          out_shape=exchange.out_shape, scratch_shapes=exchange.scratch)(*exchange.blocks)


def _ln_call(x, g, tm, exchange):
    s = x.shape[0]

    def body(x_ref, g_ref, h_ref):
        xv = x_ref[...]
        r = lax.rsqrt(jnp.mean(xv * xv, axis=-1, keepdims=True) + EPS)
        h_ref[...] = ((xv * r) * g_ref[...]).astype(h_ref.dtype)

    return _call(body, name="ln1", grid=(s // tm,), in_specs=[_rows(tm, D), _acc(D)], out_specs=[_rows(tm, D)],
                 out_shape=[_sds((s, D), _MXU)], args=(x, g), exchange=exchange)


PROJ_TILE = 256


def _proj_source_tile(j):
    gate_end, attn_end, end = 3072 // PROJ_TILE, 4352 // PROJ_TILE, 6400 // PROJ_TILE
    n_gates = end - attn_end
    return jnp.where(j < gate_end, j, jnp.where(j < gate_end + n_gates, j + (attn_end - gate_end), j - n_gates))


def _proj_call(a, wt, bias, exchange):
    s, k = a.shape
    n = wt.shape[0]
    rows = min(1024, s)

    def body(a_ref, w_ref, b_ref, o_ref):
        for r in range(0, s, rows):
            o_ref[r:r + rows, :] = _dot_nt(a_ref[r:r + rows, :], w_ref[...]) + b_ref[...]

    return _call(body, name="proj", grid=(n // PROJ_TILE,),
                 in_specs=[_const((s, k)), pl.BlockSpec((PROJ_TILE, k), lambda j: (_proj_source_tile(j), 0)),
                           pl.BlockSpec((1, PROJ_TILE), lambda j: (0, _proj_source_tile(j)))],
                 out_specs=[pl.BlockSpec((s, PROJ_TILE), lambda j: (0, j))], out_shape=[_sds((s, n), F32)],
                 args=(a, wt, bias), big=True, exchange=exchange)


def _mm_tn(a, b, name, tm, tn, tk, exchange=None):
    s, m = a.shape
    n = b.shape[1]
    last = s // tk - 1

    def body(a_ref, b_ref, o_ref, acc):
        k = pl.program_id(2)
        part = _dot_tn(a_ref[...], b_ref[...])

        @pl.when(k == 0)
        def _():
            acc[...] = part

        @pl.when(k > 0)
        def _():
            acc[...] += part

        @pl.when(k == last)
        def _():
            o_ref[...] = acc[...].astype(o_ref.dtype)

    res = _call(body, name=name, grid=(m // tm, n // tn, s // tk),
                in_specs=[pl.BlockSpec((tk, tm), lambda i, j, k: (k, i)), pl.BlockSpec((tk, tn), lambda i, j, k: (k, j))],
                out_specs=[pl.BlockSpec((tm, tn), lambda i, j, k: (i, j))], out_shape=[_sds((m, n), _MXU)],
                scratch_shapes=[pltpu.VMEM((tm, tn), F32)], args=(a, b), big=True, exchange=exchange)
    return res[0] if exchange is None else (res[0][0], res[1])


RET_CHUNKS = 4


def _ret_fwd_call(proj, cos, sin, decays):
    s = proj.shape[0]
    nblk = s // BLK
    per = min(RET_CHUNKS, nblk)
    rows = per * BLK
    intra, q_decay, k_decay, chunk_decay = decays

    def body(rq_ref, rk_ref, rv_ref, cos_ref, sin_ref, intra_ref, qd_ref, kd_ref,
             ry_ref, qr_ref, kr_ref, st_ref, state):
        @pl.when(pl.program_id(0) == 0)
        def _():
            state[...] = jnp.zeros_like(state)

        for c in range(per):
            rc = slice(c * BLK, (c + 1) * BLK)
            cos_v, sin_v = _ret_rope(cos_ref[rc, :], sin_ref[rc, :])
            for h in range(RET_HEADS):
                hk = slice(h * RET_DK, (h + 1) * RET_DK)
                hv = slice(h * RET_DV, (h + 1) * RET_DV)
                q, k = rq_ref[rc, hk], rk_ref[rc, hk]
                qr = (q * cos_v + _swap_halves(q, RET_DK // 2) * sin_v) * RET_SCALE
                kr = k * cos_v + _swap_halves(k, RET_DK // 2) * sin_v
                v = rv_ref[rc, hv]
                s_h = state[h]
                st_ref[c, h] = s_h.astype(st_ref.dtype)
                scores = _dot_nt(qr, kr) * intra_ref[h]
                ry_ref[rc, hv] = _dot(scores, v) + _dot(qr, s_h) * qd_ref[h]
                state[h] = s_h * chunk_decay[h] + _dot_tn(kr * kd_ref[h], v)
                qr_ref[rc, hk] = qr.astype(qr_ref.dtype)
                kr_ref[rc, hk] = kr.astype(kr_ref.dtype)

    blk = lambda w, c: pl.BlockSpec((rows, w), lambda n: (n, c))
    return _call(body, name="ret_fwd", grid=(nblk // per,),
                 in_specs=[blk(512, 0), blk(512, 1), blk(1024, 1), blk(128, 0), blk(128, 0),
                           _const(intra.shape), _const(q_decay.shape), _const(k_decay.shape)],
                 out_specs=[blk(1024, 0), blk(512, 0), blk(512, 0),
                            pl.BlockSpec((per, RET_HEADS, RET_DK, RET_DV), lambda n: (n, 0, 0, 0))],
                 out_shape=[_sds((s, 1024), F32), _sds((s, 512), _MXU), _sds((s, 512), _MXU),
                            _sds((nblk, RET_HEADS, RET_DK, RET_DV), _MXU)],
                 scratch_shapes=[pltpu.VMEM((RET_HEADS, RET_DK, RET_DV), F32)],
                 args=(proj, proj, proj, cos, sin, intra, q_decay, k_decay))


def _ret_bwd_call(qr, kr, proj, states, dry, cos, sin, decays, exchange):
    s = qr.shape[0]
    nblk = s // BLK
    per = min(RET_CHUNKS, nblk)
    rows = per * BLK
    steps = nblk // per
    intra, q_decay, k_decay, chunk_decay = decays

    def body(qr_ref, kr_ref, rv_ref, st_ref, dry_ref, cos_ref, sin_ref, intra_ref, qd_ref, kd_ref,
             dp_ref, db_ref, dstate):
        @pl.when(pl.program_id(0) == 0)
        def _():
            dstate[...] = jnp.zeros_like(dstate)
            db_ref[...] = jnp.zeros_like(db_ref)

        for c in reversed(range(per)):
            rc = slice(c * BLK, (c + 1) * BLK)
            cos_v, sin_v = _ret_rope(cos_ref[rc, :], sin_ref[rc, :])
            for h in range(RET_HEADS):
                hk = slice(h * RET_DK, (h + 1) * RET_DK)
                hv = slice(h * RET_DV, (h + 1) * RET_DV)
                q, k, v, d_out = qr_ref[rc, hk], kr_ref[rc, hk], rv_ref[rc, hv], dry_ref[rc, hv]
                d_next = dstate[h]
                scores = _dot_nt(q, k) * intra_ref[h]
                d_scores = _dot_nt(d_out, v) * intra_ref[h]
                d_cross = d_out * qd_ref[h]
                dq = _dot(d_scores, k) + _dot_nt(d_cross, st_ref[c, h])
                dk = _dot_tn(d_scores, q) + _dot_nt(v, d_next) * kd_ref[h]
                dv = _dot_tn(scores, d_out) + _dot(k.astype(F32) * kd_ref[h], d_next)
                dstate[h] = d_next * chunk_decay[h] + _dot_tn(q, d_cross)
                dq = (dq * cos_v - _swap_halves(dq, RET_DK // 2) * sin_v) * RET_SCALE
                dk = dk * cos_v - _swap_halves(dk, RET_DK // 2) * sin_v
                kcols = slice(512 + h * RET_DK, 512 + (h + 1) * RET_DK)
                vcols = slice(1024 + h * RET_DV, 1024 + (h + 1) * RET_DV)
                dp_ref[rc, hk] = dq.astype(dp_ref.dtype)
                dp_ref[rc, kcols] = dk.astype(dp_ref.dtype)
                dp_ref[rc, vcols] = dv.astype(dp_ref.dtype)
                db_ref[:, hk] += jnp.sum(dq, axis=0, keepdims=True)
                db_ref[:, kcols] += jnp.sum(dk, axis=0, keepdims=True)
                db_ref[:, vcols] += jnp.sum(dv, axis=0, keepdims=True)

    rblk = lambda w, c: pl.BlockSpec((rows, w), lambda n: (steps - 1 - n, c))
    return _call(body, name="ret_bwd", grid=(steps,),
                 in_specs=[rblk(512, 0), rblk(512, 0), rblk(1024, 1),
                           pl.BlockSpec((per, RET_HEADS, RET_DK, RET_DV), lambda n: (steps - 1 - n, 0, 0, 0)),
                           rblk(1024, 0), rblk(128, 0), rblk(128, 0),
                           _const(intra.shape), _const(q_decay.shape), _const(k_decay.shape)],
                 out_specs=[rblk(N_RET, 0), _acc(N_RET)],
                 out_shape=[_sds((s, N_RET), _MXU), _sds((1, N_RET), F32)],
                 scratch_shapes=[pltpu.VMEM((RET_HEADS, RET_DK, RET_DV), F32)],
                 args=(qr, kr, proj, states, dry, cos, sin, intra, q_decay, k_decay), exchange=exchange)


def _both_halves(x, g):
    lane = lax.broadcasted_iota(jnp.int32, x.shape, 1)
    keep = lane < HEAD_DIM if g == 0 else lane >= HEAD_DIM
    return jnp.where(keep, x, pltpu.roll(x, HEAD_DIM, 1))


def _stack_heads(ref, g, rows=slice(None)):
    lane = lax.broadcasted_iota(jnp.int32, (BLK, 128), 1)
    pieces = []
    for j in range(g * 4, g * 4 + 4):
        chunk = ref[rows, j * 128:(j + 1) * 128]
        pieces += [jnp.where(lane < HEAD_DIM, chunk, jnp.zeros_like(chunk)),
                   jnp.where(lane >= HEAD_DIM, chunk, jnp.zeros_like(chunk))]
    return jnp.concatenate(pieces, axis=0)


def _window_bias(first_block):
    kj = lax.broadcasted_iota(jnp.int32, (2 * BLK, BLK), 0)
    qi = lax.broadcasted_iota(jnp.int32, (2 * BLK, BLK), 1)
    first_key = jnp.where(first_block, BLK, 0)
    seen = (kj > qi) & (kj <= qi + BLK) & (kj >= first_key)
    return jnp.where(seen, 0.0, -1e30)


def _sink_softmax(scores, sink):
    m = jnp.maximum(jnp.max(scores, axis=0, keepdims=True), sink)
    e = jnp.exp(scores - m)
    e_sink = jnp.exp(sink - m)
    return e, e_sink, 1.0 / (jnp.sum(e, axis=0, keepdims=True) + e_sink)


def _head_pair(stacked_t, jj):
    even = stacked_t[0:HEAD_DIM, 2 * jj * BLK:(2 * jj + 1) * BLK]
    odd = stacked_t[HEAD_DIM:128, (2 * jj + 1) * BLK:(2 * jj + 2) * BLK]
    return jnp.concatenate([even, odd], axis=0).T


ATTN_BLOCKS = 4


def _attn_fwd_call(proj, sinks, cos, sin):
    s = proj.shape[0]
    nblk = s // BLK
    per = min(ATTN_BLOCKS, nblk)
    rows = per * BLK

    def body(sink_ref, q_ref, k_ref, v_ref, cos_ref, sin_ref, ay_ref, qr_ref, kr_ref, vb_ref,
             kwin, vwin, bias, s_scr, p_scr):
        n = pl.program_id(0)

        @pl.when(n == 0)
        def _():
            kwin[...] = jnp.zeros_like(kwin)
            vwin[...] = jnp.zeros_like(vwin)

        for c in range(per):
            rc = slice(c * BLK, (c + 1) * BLK)
            kwin[0:BLK] = kwin[BLK:2 * BLK]
            vwin[0:BLK] = vwin[BLK:2 * BLK]
            cos_v, sin_v = _attn_rope(cos_ref[rc, :], sin_ref[rc, :])
            k = k_ref[rc, :]
            kr = (k * cos_v + _swap_halves(k, HEAD_DIM // 2) * sin_v).astype(kwin.dtype)
            kwin[BLK:2 * BLK] = kr
            vwin[BLK:2 * BLK] = v_ref[rc, :].astype(vwin.dtype)
            kr_ref[rc, :] = kr
            vb_ref[rc, :] = vwin[BLK:2 * BLK]
            for j in range(Q_HEADS // 2):
                cols = slice(j * 128, (j + 1) * 128)
                q = q_ref[rc, cols]
                qr_ref[rc, cols] = ((q * cos_v + _swap_halves(q, HEAD_DIM // 2) * sin_v) * ATTN_SCALE).astype(qr_ref.dtype)
            bias[...] = _window_bias(n == 0) if c == 0 else _window_bias(False)
            for g in range(KV_HEADS):
                kg = _both_halves(kwin[...], g)
                vg_t = _both_halves(vwin[...], g).astype(F32).T
                s_scr[...] = _dot_nt(kg, _stack_heads(qr_ref, g, rc))
                for i in range(GROUP):
                    cols = slice(i * BLK, (i + 1) * BLK)
                    e, _, inv = _sink_softmax(s_scr[:, cols] + bias[...], sink_ref[0, g * GROUP + i])
                    p_scr[:, cols] = (e * inv).astype(p_scr.dtype)
                out_t = _dot(vg_t, p_scr[...])
                for jj in range(4):
                    j = g * 4 + jj
                    ay_ref[rc, j * 128:(j + 1) * 128] = _head_pair(out_t, jj).astype(ay_ref.dtype)

    blk = lambda w, c: pl.BlockSpec((rows, w), lambda n: (n, c))
    off = (N_RET + N_GATE) // 128
    wide = (2 * BLK, GROUP * BLK)
    return _call(body, name="attn_fwd", grid=(nblk // per,),
                 in_specs=[pl.BlockSpec(memory_space=pltpu.SMEM), blk(1024, off // 8), blk(128, off + 8), blk(128, off + 9),
                           blk(128, 0), blk(128, 0)],
                 out_specs=[blk(1024, 0), blk(1024, 0), blk(128, 0), blk(128, 0)],
                 out_shape=[_sds((s, 1024), _MXU), _sds((s, 1024), _MXU), _sds((s, 128), _MXU), _sds((s, 128), _MXU)],
                 scratch_shapes=[pltpu.VMEM((2 * BLK, 128), _MXU), pltpu.VMEM((2 * BLK, 128), _MXU),
                                 pltpu.VMEM((2 * BLK, BLK), F32), pltpu.VMEM(wide, F32), pltpu.VMEM(wide, _MXU)],
                 args=(sinks, proj, proj, proj, cos, sin))


def _attn_bwd_call(qr, kr, vb, day, sinks, cos, sin, exchange):
    s = qr.shape[0]
    nblk = s // BLK

    def body(sink_ref, q_ref, kc_ref, kp_ref, vc_ref, vp_ref, do_ref, cos_ref, sin_ref, cosp_ref, sinp_ref,
             dp_ref, dsink_ref, db_ref, bias, s_scr, dp_scr, p_scr, ds_scr, dq_held, kv_held, kv_prev, kv_new):
        n = pl.program_id(0)
        valid = (n < nblk).astype(F32)

        @pl.when(n == 0)
        def _():
            dsink_ref[...] = jnp.zeros_like(dsink_ref)
            db_ref[...] = jnp.zeros_like(db_ref)

        @pl.when(n >= 1)
        def _():
            dp_ref[:, 0:1024] = dq_held[...]

        cos_v, sin_v = _attn_rope(cos_ref[...], sin_ref[...])
        bias[...] = _window_bias(n == 0)
        lane1 = lax.broadcasted_iota(jnp.int32, (1, 128), 1)
        kwin = jnp.concatenate([kp_ref[...], kc_ref[...]], axis=0)
        vwin = jnp.concatenate([vp_ref[...], vc_ref[...]], axis=0)
        dk_heads, dv_heads = [], []
        dsink = jnp.zeros((1, 128), F32)
        for g in range(KV_HEADS):
            kg = _both_halves(kwin, g)
            vg = _both_halves(vwin, g)
            q_all = _stack_heads(q_ref, g)
            do_all = _stack_heads(do_ref, g)
            s_scr[...] = _dot_nt(kg, q_all)
            dp_scr[...] = _dot_nt(vg, do_all)
            for i in range(GROUP):
                head = g * GROUP + i
                cols = slice(i * BLK, (i + 1) * BLK)
                e, e_sink, inv = _sink_softmax(s_scr[:, cols] + bias[...], sink_ref[0, head])
                p = e * inv
                dp = dp_scr[:, cols]
                delta = jnp.sum(p * dp, axis=0, keepdims=True)
                p_scr[:, cols] = p.astype(p_scr.dtype)
                ds_scr[:, cols] = (p * (dp - delta)).astype(ds_scr.dtype)
                dsink = dsink + jnp.where(lane1 == head, -jnp.sum(e_sink * inv * delta, axis=1, keepdims=True), 0.0)
            dv_both = _dot(p_scr[...], do_all)
            dk_both = _dot(ds_scr[...], q_all)
            dv_heads.append(dv_both + pltpu.roll(dv_both, HEAD_DIM, 1))
            dk_heads.append(dk_both + pltpu.roll(dk_both, HEAD_DIM, 1))
            dq_t = _dot(kg.astype(F32).T, ds_scr[...])
            for jj in range(4):
                cols = slice((g * 4 + jj) * 128, (g * 4 + jj + 1) * 128)
                dq = _head_pair(dq_t, jj)
                dq = (dq * cos_v - _swap_halves(dq, HEAD_DIM // 2) * sin_v) * ATTN_SCALE
                dq_held[:, cols] = dq.astype(dq_held.dtype)
                db_ref[:, cols] += jnp.sum(dq, axis=0, keepdims=True) * valid
        dsink_ref[...] += dsink * valid
        lane2 = lax.broadcasted_iota(jnp.int32, (2 * BLK, 128), 1)
        dk_all = jnp.where(lane2 < HEAD_DIM, dk_heads[0], dk_heads[1])
        dv_all = jnp.where(lane2 < HEAD_DIM, dv_heads[0], dv_heads[1])
        kv_prev[:, 0:128] = dk_all[0:BLK] * valid
        kv_prev[:, 128:256] = dv_all[0:BLK] * valid
        kv_new[:, 0:128] = dk_all[BLK:2 * BLK]
        kv_new[:, 128:256] = dv_all[BLK:2 * BLK]

        @pl.when(n >= 1)
        def _():
            dkv = kv_held[...] + kv_prev[...]
            dk = dkv[:, 0:128]
            cos_p, sin_p = _attn_rope(cosp_ref[...], sinp_ref[...])
            dk = dk * cos_p - _swap_halves(dk, HEAD_DIM // 2) * sin_p
            dv = dkv[:, 128:256]
            dp_ref[:, 1024:1152] = dk.astype(dp_ref.dtype)
            dp_ref[:, 1152:1280] = dv.astype(dp_ref.dtype)
            db_ref[:, 1024:1152] += jnp.sum(dk, axis=0, keepdims=True)
            db_ref[:, 1152:1280] += jnp.sum(dv, axis=0, keepdims=True)

        kv_held[...] = kv_new[...]

    blk = lambda w: pl.BlockSpec((BLK, w), lambda n: (jnp.minimum(n, nblk - 1), 0))
    pblk = lambda w: pl.BlockSpec((BLK, w), lambda n: (jnp.maximum(n - 1, 0), 0))
    wide = (2 * BLK, GROUP * BLK)
    return _call(body, name="attn_bwd", grid=(nblk + 1,),
                 in_specs=[pl.BlockSpec(memory_space=pltpu.SMEM), blk(1024), blk(128), pblk(128), blk(128), pblk(128),
                           blk(1024), blk(128), blk(128), pblk(128), pblk(128)],
                 out_specs=[pblk(N_ATTN), _acc(128), _acc(N_ATTN)],
                 out_shape=[_sds((s, N_ATTN), _MXU), _sds((1, 128), F32), _sds((1, N_ATTN), F32)],
                 scratch_shapes=[pltpu.VMEM((2 * BLK, BLK), F32), pltpu.VMEM(wide, F32), pltpu.VMEM(wide, F32),
                                 pltpu.VMEM(wide, _MXU), pltpu.VMEM(wide, _MXU), pltpu.VMEM((BLK, 1024), _MXU),
                                 pltpu.VMEM((BLK, 256), F32), pltpu.VMEM((BLK, 256), F32), pltpu.VMEM((BLK, 256), F32)],
                 args=(sinks, qr, kr, kr, vb, vb, day, cos, sin, cos, sin), exchange=exchange)


def _group_norm(y):
    mu = jnp.mean(y, axis=-1, keepdims=True)
    yc = y - mu
    rs = lax.rsqrt(jnp.mean(yc * yc, axis=-1, keepdims=True) + EPS)
    return yc * rs, rs


GATE_COL = N_RET // 1024


def _merge_fwd_call(x, ry, proj, ay, gn_g, w_ro, w_ao, w_o, tm):
    s = x.shape[0]

    def body(x_ref, ry_ref, rg_ref, ga_ref, gb_ref, ay_ref, gn_ref, wro_ref, wao_ref, wo_ref,
             ain_ref, a_ref, b_ref, mg_ref, x1_ref):
        for h in range(RET_HEADS):
            hv = slice(h * RET_DV, (h + 1) * RET_DV)
            yhat, _ = _group_norm(ry_ref[:, hv])
            rg = rg_ref[:, hv]
            ain_ref[:, hv] = ((rg * _sigmoid(rg)) * (yhat * gn_ref[:, hv])).astype(ain_ref.dtype)
        a = _dot(ain_ref[...], wro_ref[...])
        b = _dot(ay_ref[...], wao_ref[...])
        a_ref[...] = a.astype(a_ref.dtype)
        b_ref[...] = b.astype(b_ref.dtype)
        merged = (_sigmoid(ga_ref[...]) * a + _sigmoid(gb_ref[...]) * b).astype(mg_ref.dtype)
        mg_ref[...] = merged
        x1_ref[...] = x_ref[...] + _dot(merged, wo_ref[...])

    return _call(body, name="merge_fwd", grid=(s // tm,),
                 in_specs=[_rows(tm, D), _rows(tm, 1024), _rows(tm, 1024, GATE_COL), _rows(tm, 1024, GATE_COL + 1),
                           _rows(tm, 1024, GATE_COL + 2), _rows(tm, 1024), _acc(1024),
                           _const((D, D)), _const((D, D)), _const((D, D))],
                 out_specs=[_rows(tm, D)] * 5,
                 out_shape=[_sds((s, D), _MXU), _sds((s, D), _STORE), _sds((s, D), _STORE), _sds((s, D), _MXU),
                            _sds((s, D), F32)],
                 args=(x, ry, proj, proj, proj, ay, gn_g, w_ro, w_ao, w_o), big=True)


def _merge_bwd_call(dx1, a, b, ry, proj, gn_g, w_ro, w_ao, w_o, tm):
    s = dx1.shape[0]

    def body(dx1_ref, a_ref, b_ref, ry_ref, rg_ref, ga_ref, gb_ref, gn_ref, wro_ref, wao_ref, wo_ref,
             da_ref, dbr_ref, dp_ref, day_ref, dry_ref, dbias_ref, dgn_ref):
        @pl.when(pl.program_id(0) == 0)
        def _():
            dbias_ref[...] = jnp.zeros_like(dbias_ref)
            dgn_ref[...] = jnp.zeros_like(dgn_ref)

        d_merged = _dot_nt(dx1_ref[...], wo_ref[...])
        sa, sb = _sigmoid(ga_ref[...]), _sigmoid(gb_ref[...])
        d_a = d_merged * sa
        d_b = d_merged * sb
        da_ref[...] = d_a.astype(da_ref.dtype)
        dbr_ref[...] = d_b.astype(dbr_ref.dtype)
        d_ga = d_merged * a_ref[...] * (sa * (1.0 - sa))
        d_gb = d_merged * b_ref[...] * (sb * (1.0 - sb))
        dp_ref[:, 1024:2048] = d_ga.astype(dp_ref.dtype)
        dp_ref[:, 2048:3072] = d_gb.astype(dp_ref.dtype)
        dbias_ref[:, 1024:2048] += jnp.sum(d_ga, axis=0, keepdims=True)
        dbias_ref[:, 2048:3072] += jnp.sum(d_gb, axis=0, keepdims=True)
        day_ref[...] = _dot_nt(d_b, wao_ref[...]).astype(day_ref.dtype)
        d_ain = _dot_nt(d_a, wro_ref[...])
        for h in range(RET_HEADS):
            hv = slice(h * RET_DV, (h + 1) * RET_DV)
            yhat, rs = _group_norm(ry_ref[:, hv])
            rg = rg_ref[:, hv]
            sg = _sigmoid(rg)
            gn = gn_ref[:, hv]
            d_h = d_ain[:, hv]
            d_rg = d_h * (yhat * gn) * (sg * (1.0 + rg * (1.0 - sg)))
            d_ryn = d_h * (rg * sg)
            dgn_ref[:, hv] += jnp.sum(d_ryn * yhat, axis=0, keepdims=True)
            d_yhat = d_ryn * gn
            dry_ref[:, hv] = (rs * (d_yhat - jnp.mean(d_yhat, axis=-1, keepdims=True)
                                    - yhat * jnp.mean(d_yhat * yhat, axis=-1, keepdims=True))).astype(dry_ref.dtype)
            dp_ref[:, hv] = d_rg.astype(dp_ref.dtype)
            dbias_ref[:, hv] += jnp.sum(d_rg, axis=0, keepdims=True)

    return _call(body, name="merge_bwd", grid=(s // tm,),
                 in_specs=[_rows(tm, D), _rows(tm, D), _rows(tm, D), _rows(tm, 1024), _rows(tm, 1024, GATE_COL),
                           _rows(tm, 1024, GATE_COL + 1), _rows(tm, 1024, GATE_COL + 2), _acc(1024),
                           _const((D, D)), _const((D, D)), _const((D, D))],
                 out_specs=[_rows(tm, D), _rows(tm, D), _rows(tm, N_GATE), _rows(tm, D), _rows(tm, D), _acc(N_GATE),
                            _acc(1024)],
                 out_shape=[_sds((s, D), _MXU), _sds((s, D), _MXU), _sds((s, N_GATE), _MXU), _sds((s, D), _MXU),
                            _sds((s, D), _STORE), _sds((1, N_GATE), F32), _sds((1, 1024), F32)],
                 args=(dx1, a, b, ry, proj, proj, proj, gn_g, w_ro, w_ao, w_o), big=True)


def _ffn_fwd_call(x1, target, ln2_g, lnf_g, w_g, w_u, w_d, tm):
    s = x1.shape[0]

    def body(x1_ref, t_ref, g2_ref, gf_ref, wg_ref, wu_ref, wd_ref,
             h2_ref, g_ref, u_ref, f_ref, dx2_ref, loss_ref, dgf_ref):
        @pl.when(pl.program_id(0) == 0)
        def _():
            loss_ref[...] = jnp.zeros_like(loss_ref)
            dgf_ref[...] = jnp.zeros_like(dgf_ref)

        x1v = x1_ref[...]
        r1 = lax.rsqrt(jnp.mean(x1v * x1v, axis=-1, keepdims=True) + EPS)
        h2 = ((x1v * r1) * g2_ref[...]).astype(h2_ref.dtype)
        h2_ref[...] = h2
        g = _dot_nt(h2, wg_ref[...])
        u = _dot_nt(h2, wu_ref[...])
        g_ref[...] = g
        u_ref[...] = u
        f = ((g * _sigmoid(g)) * u).astype(f_ref.dtype)
        f_ref[...] = f
        x2 = x1v + _dot(f, wd_ref[...])
        r2 = lax.rsqrt(jnp.mean(x2 * x2, axis=-1, keepdims=True) + EPS)
        xhat = x2 * r2
        err = xhat * gf_ref[...] - t_ref[...]
        loss_ref[...] += 0.5 * jnp.sum(jnp.mean(err * err, axis=-1, keepdims=True))
        dy = err * (1.0 / D)
        dgf_ref[...] += jnp.sum(dy * xhat, axis=0, keepdims=True)
        dxh = dy * gf_ref[...]
        dx2_ref[...] = r2 * (dxh - xhat * jnp.mean(dxh * xhat, axis=-1, keepdims=True))

    return _call(body, name="ffn_fwd", grid=(s // tm,),
                 in_specs=[_rows(tm, D), _rows(tm, D), _acc(D), _acc(D), _const((D_FF, D)), _const((D_FF, D)),
                           _const((D_FF, D))],
                 out_specs=[_rows(tm, D), _rows(tm, D_FF), _rows(tm, D_FF), _rows(tm, D_FF), _rows(tm, D), _acc(128),
                            _acc(D)],
                 out_shape=[_sds((s, D), _MXU), _sds((s, D_FF), F32), _sds((s, D_FF), F32), _sds((s, D_FF), _MXU),
                            _sds((s, D), F32), _sds((1, 128), F32), _sds((1, D), F32)],
                 args=(x1, target, ln2_g, lnf_g, w_g, w_u, w_d), big=True)


def _ffn_bwd_call(dx2, x1, g, u, ln2_g, w_g, w_u, w_d, tm):
    s = dx2.shape[0]

    def body(dx2_ref, x1_ref, g_ref, u_ref, g2_ref, wg_ref, wu_ref, wd_ref, dx1_ref, dg_ref, du_ref, dg2_ref):
        @pl.when(pl.program_id(0) == 0)
        def _():
            dg2_ref[...] = jnp.zeros_like(dg2_ref)

        dx2v = dx2_ref[...]
        df = _dot_nt(dx2v, wd_ref[...])
        gv, uv = g_ref[...], u_ref[...]
        sg = _sigmoid(gv)
        du = (df * (gv * sg)).astype(du_ref.dtype)
        dg = (df * uv * (sg * (1.0 + gv * (1.0 - sg)))).astype(dg_ref.dtype)
        du_ref[...] = du
        dg_ref[...] = dg
        dh2 = _dot(dg, wg_ref[...]) + _dot(du, wu_ref[...])
        x1v = x1_ref[...]
        r1 = lax.rsqrt(jnp.mean(x1v * x1v, axis=-1, keepdims=True) + EPS)
        xhat = x1v * r1
        dg2_ref[...] += jnp.sum(dh2 * xhat, axis=0, keepdims=True)
        dxh = dh2 * g2_ref[...]
        dx1_ref[...] = dx2v + r1 * (dxh - xhat * jnp.mean(dxh * xhat, axis=-1, keepdims=True))

    return _call(body, name="ffn_bwd", grid=(s // tm,),
                 in_specs=[_rows(tm, D), _rows(tm, D), _rows(tm, D_FF), _rows(tm, D_FF), _acc(D),
                           _const((D_FF, D)), _const((D_FF, D)), _const((D_FF, D))],
                 out_specs=[_rows(tm, D), _rows(tm, D_FF), _rows(tm, D_FF), _acc(D)],
                 out_shape=[_sds((s, D), F32), _sds((s, D_FF), _MXU), _sds((s, D_FF), _MXU), _sds((1, D), F32)],
                 args=(dx2, x1, g, u, ln2_g, w_g, w_u, w_d), big=True)


def _dx_call(x, dx1, dp_ret, dp_gate, dp_attn, ln1_g, w_in, tm, exchange):
    s = x.shape[0]

    def body(x_ref, dx1_ref, dr_ref, dg_ref, da_ref, g1_ref, w_ref, dx_ref, dg1_ref):
        @pl.when(pl.program_id(0) == 0)
        def _():
            dg1_ref[...] = jnp.zeros_like(dg1_ref)

        dh = (_dot(dr_ref[...], w_ref[0:2048, :]) + _dot(dg_ref[:, 0:1024], w_ref[2048:3072, :])
              + _dot(da_ref[...], w_ref[3072:4352, :]) + _dot(dg_ref[:, 1024:3072], w_ref[4352:6400, :]))
        xv = x_ref[...]
        r = lax.rsqrt(jnp.mean(xv * xv, axis=-1, keepdims=True) + EPS)
        xhat = xv * r
        dg1_ref[...] += jnp.sum(dh * xhat, axis=0, keepdims=True)
        dxh = dh * g1_ref[...]
        dx_ref[...] = dx1_ref[...] + r * (dxh - xhat * jnp.mean(dxh * xhat, axis=-1, keepdims=True))

    return _call(body, name="dx", grid=(s // tm,),
                 in_specs=[_rows(tm, D), _rows(tm, D), _rows(tm, N_RET), _rows(tm, N_GATE), _rows(tm, N_ATTN), _acc(D),
                           _const((D_IN, D))],
                 out_specs=[_rows(tm, D), _acc(D)],
                 out_shape=[_sds((s, D), F32), _sds((1, D), F32)],
                 args=(x, dx1, dp_ret, dp_gate, dp_attn, ln1_g, w_in), big=True, exchange=exchange)


def _adamw(g, w, m, v):
    m_new = B1 * m + (1.0 - B1) * g
    v_new = B2 * v + (1.0 - B2) * (g * g)
    m_hat = m_new / (1.0 - B1 ** STEP)
    v_hat = v_new / (1.0 - B2 ** STEP)
    return -LR * (m_hat / (jnp.sqrt(v_hat) + ADAM_EPS) + WD * w), m_new, v_new


def _slot_sum(p_ref):
    g = p_ref[0].astype(F32)
    for k in range(1, N_DEV):
        g = g + p_ref[k].astype(F32)
    return g


def _adamw_call(parts, ws, ms, vs, name, tr):
    n = len(ws)
    rows, cols = ws[0].shape

    def body(*refs):
        p_refs, w_refs, m_refs, v_refs = (refs[k * n:(k + 1) * n] for k in range(4))
        outs = refs[4 * n:]
        for i in range(n):
            g = _slot_sum(p_refs[i])
            outs[4 * i][...] = g
            outs[4 * i + 1][...], outs[4 * i + 2][...], outs[4 * i + 3][...] = _adamw(
                g, w_refs[i][...], m_refs[i][...], v_refs[i][...])

    p_spec = pl.BlockSpec((N_DEV, tr, cols), lambda i: (0, i, 0))
    spec = pl.BlockSpec((tr, cols), lambda i: (i, 0))
    res = _call(body, name=name, grid=(rows // tr,), in_specs=[p_spec] * n + [spec] * (3 * n), out_specs=[spec] * (4 * n),
                out_shape=[_sds((rows, cols), F32)] * (4 * n), args=(*parts, *ws, *ms, *vs), big=True)
    return [res[4 * i:4 * i + 4] for i in range(n)]


SMALL_WIDTHS = [1024, 6400, 1024, 16, 1024, 1024]
SMALL_OFFSETS = [0, 1024, 7424, 8448, 8576, 9600]
LOSS_OFFSET = 10624
SMALL_LEN = 10752


def _pack_small(grads, loss):
    pieces = []
    for gr, width in zip(grads, SMALL_WIDTHS):
        pieces.append(jnp.pad(gr.reshape(1, width), ((0, 0), (0, -width % 128))))
    pieces.append(jnp.pad(loss.reshape(1, 1), ((0, 0), (0, 127))))
    return jnp.concatenate(pieces, axis=1)


def _adamw_small_call(parts, ws, ms, vs):
    n = len(ws)

    def body(*refs):
        p_ref, w_refs, m_refs, v_refs = refs[0], refs[1:1 + n], refs[1 + n:1 + 2 * n], refs[1 + 2 * n:1 + 3 * n]
        outs = refs[1 + 3 * n:]
        g_all = _slot_sum(p_ref)
        for i, (off, width) in enumerate(zip(SMALL_OFFSETS, SMALL_WIDTHS)):
            g = g_all[:, off:off + width]
            outs[i][...] = g
            outs[n + i][...], outs[2 * n + i][...], outs[3 * n + i][...] = _adamw(
                g, w_refs[i][...], m_refs[i][...], v_refs[i][...])
        outs[4 * n][...] = g_all[:, LOSS_OFFSET:LOSS_OFFSET + 128]

    whole = lambda shape: pl.BlockSpec(shape, lambda i: (0,) * len(shape))
    small = [whole((1, w)) for w in SMALL_WIDTHS]
    res = _call(body, name="adamw_small", grid=(1,), in_specs=[whole((N_DEV, 1, SMALL_LEN))] + small * 3,
                out_specs=small * 4 + [whole((1, 128))],
                out_shape=[_sds((1, w), F32) for w in SMALL_WIDTHS] * 4 + [_sds((1, 128), F32)],
                args=(parts, *ws, *ms, *vs))
    return [res[k * n:(k + 1) * n] for k in range(4)], res[4 * n]


def kernel(x, ln1_g, w_in, b_in, ret_norm_g, w_ret_out, attn_sinks, w_attn_out, w_out, ln2_g, w_ffn_gate, w_ffn_up, w_ffn_down, lnf_g, loss_target, m_ln1_g, m_w_in, m_b_in, m_ret_norm_g, m_w_ret_out, m_attn_sinks, m_w_attn_out, m_w_out, m_ln2_g, m_w_ffn_gate, m_w_ffn_up, m_w_ffn_down, m_lnf_g, v_ln1_g, v_w_in, v_b_in, v_ret_norm_g, v_w_ret_out, v_attn_sinks, v_w_attn_out, v_w_out, v_ln2_g, v_w_ffn_gate, v_w_ffn_up, v_w_ffn_down, v_lnf_g):
    cast = lambda a: a.astype(_MXU)
    xs, target = x[0], loss_target[0]
    s = xs.shape[0]
    r_sq = w_ret_out.shape[1]
    r_dn = w_ffn_down.shape[1]
    c_in = w_in.shape[2]
    c_ff = w_ffn_gate.shape[2]
    tm, tm_wide, tk = min(256, s), min(512, s), min(2048, s)
    lnf_row = lnf_g.reshape(1, D)
    cos_t, sin_t = _rope_tables(s)
    decays = _retention_decays()
    tr_shard = lambda a: a[0].T
    per_dev = lambda a, n: a.reshape(N_DEV, n, D)

    (h,), (all_in,) = _ln_call(xs, ln1_g, tm, _AllGather([cast(tr_shard(w_in))]))
    wt_in = all_in.reshape(N_DEV * c_in, D)
    rest = [tr_shard(w_ffn_gate), tr_shard(w_ffn_up), w_ret_out[0], w_attn_out[0], w_out[0], w_ffn_down[0]]
    (proj,), gathered = _proj_call(h, wt_in, b_in, _AllGather([cast(a) for a in rest]))
    wt_g, wt_u, full_ro, full_ao, full_o, full_d = (a.reshape(N_DEV * a.shape[1], D) for a in gathered)
    ry, qr, kr, states = _ret_fwd_call(proj, cos_t, sin_t, decays)
    ay, aqr, akr, avb = _attn_fwd_call(proj, attn_sinks, cos_t, sin_t)
    a_in, br_a, br_b, merged, x1 = _merge_fwd_call(xs, ry, proj, ay, ret_norm_g, full_ro, full_ao, full_o, tm_wide)
    h2, g, u, f, dx2, loss, d_lnf = _ffn_fwd_call(x1, target, ln2_g, lnf_row, wt_g, wt_u, full_d, tm)

    dx1, dg, du, d_ln2 = _ffn_bwd_call(dx2, x1, g, u, ln2_g, wt_g, wt_u, full_d, tm)
    dw_d = _mm_tn(f, dx2, "dw_ffn_down", 1408, 1024, tk)
    dwt_g = _mm_tn(dg, h2, "dw_ffn_gate", 1408, 1024, tk)
    dwt_u = _mm_tn(du, h2, "dw_ffn_up", 1408, 1024, tk)
    d_a, d_b, dp_gate, day, dry, db_gate, d_gn = _merge_bwd_call(
        dx1, br_a, br_b, ry, proj, ret_norm_g, full_ro, full_ao, full_o, tm_wide)
    dw_o = _mm_tn(merged, dx1, "dw_out", 1024, 1024, tk)
    dw_ro = _mm_tn(a_in, d_a, "dw_ret_out", 1024, 1024, tk)
    dw_ao = _mm_tn(ay, d_b, "dw_attn_out", 1024, 1024, tk)
    (dp_attn, d_sinks, db_attn), (got_g, got_u, got_d) = _attn_bwd_call(
        aqr, akr, avb, day, attn_sinks, cos_t, sin_t,
        _AllToAll([per_dev(dwt_g, c_ff), per_dev(dwt_u, c_ff), per_dev(dw_d, r_dn)]))
    (dp_ret, db_ret), (got_ro, got_ao, got_o) = _ret_bwd_call(
        qr, kr, proj, states, dry, cos_t, sin_t, decays,
        _AllToAll([per_dev(dw_ro, r_sq), per_dev(dw_ao, r_sq), per_dev(dw_o, r_sq)]))
    tk_in = min(4096, s)
    dwt_ret = _mm_tn(dp_ret, h, "dw_in_ret", 1024, 1024, tk_in)
    dwt_gate = _mm_tn(dp_gate, h, "dw_in_gate", 1024, 1024, tk_in)
    dwt_attn = _mm_tn(dp_attn, h, "dw_in_attn", 1280, 1024, tk_in)
    in_pieces = [(0, 0, 2048), (1, 0, 1024), (2, 0, 1280), (1, 1024, 3072)]
    (dx, d_ln1), (got_in,) = _dx_call(xs, dx1, dp_ret, dp_gate, dp_attn, ln1_g, wt_in, tm_wide,
                                      _RowScatter([dwt_ret, dwt_gate, dwt_attn], in_pieces, c_in))
    db_in = jnp.concatenate([db_ret, db_gate[:, 0:1024], db_attn, db_gate[:, 1024:3072]], axis=1)
    small = [d_ln1, db_in, d_gn, d_sinks[:, 0:Q_HEADS], d_ln2, d_lnf]
    (got_small,) = _exchange_call(_AllGather([_pack_small(small, loss[0, 0])]), "gather_small")

    transposed = ("w_in", "w_ffn_gate", "w_ffn_up")
    res = {}
    (res["w_in"],) = _adamw_call([got_in], [tr_shard(w_in)], [tr_shard(m_w_in)], [tr_shard(v_w_in)], "adamw_w_in", 160)
    res["w_ffn_gate"], res["w_ffn_up"], res["w_ffn_down"] = _adamw_call(
        [got_g, got_u, got_d], [tr_shard(w_ffn_gate), tr_shard(w_ffn_up), w_ffn_down[0]],
        [tr_shard(m_w_ffn_gate), tr_shard(m_w_ffn_up), m_w_ffn_down[0]],
        [tr_shard(v_w_ffn_gate), tr_shard(v_w_ffn_up), v_w_ffn_down[0]], "adamw_ffn", 176)
    res["w_ret_out"], res["w_attn_out"], res["w_out"] = _adamw_call(
        [got_ro, got_ao, got_o], [w_ret_out[0], w_attn_out[0], w_out[0]],
        [m_w_ret_out[0], m_w_attn_out[0], m_w_out[0]], [v_w_ret_out[0], v_w_attn_out[0], v_w_out[0]], "adamw_square", r_sq)
    small_names = ["ln1_g", "b_in", "ret_norm_g", "attn_sinks", "ln2_g", "lnf_g"]
    small_res, loss_row = _adamw_small_call(
        got_small, [ln1_g, b_in, ret_norm_g, attn_sinks, ln2_g, lnf_row],
        [m_ln1_g, m_b_in, m_ret_norm_g, m_attn_sinks, m_ln2_g, m_lnf_g.reshape(1, D)],
        [v_ln1_g, v_b_in, v_ret_norm_g, v_attn_sinks, v_ln2_g, v_lnf_g.reshape(1, D)])
    for i, nm in enumerate(small_names):
        res[nm] = [small_res[kind][i] for kind in range(4)]

    order = ["ln1_g", "w_in", "b_in", "ret_norm_g", "w_ret_out", "attn_sinks", "w_attn_out", "w_out", "ln2_g",
             "w_ffn_gate", "w_ffn_up", "w_ffn_down", "lnf_g"]
    outs = [loss_row[0, 0], dx[None]]
    for kind in range(4):
        for nm in order:
            val = res[nm][kind]
            if nm in transposed:
                val = val.T
            outs.append(val[None] if nm.startswith("w_") else val.reshape(D) if nm == "lnf_g" else val)
    return tuple(outs)
```

```python
import math

import numpy as np
import jax
import jax.numpy as jnp
from jax import lax
from jax.experimental import pallas as pl
from jax.experimental.pallas import tpu as pltpu

F32 = jnp.float32
_MXU = jnp.bfloat16
_STORE = jnp.bfloat16

N_DEV = 8
D = 1024
RET_HEADS, RET_DK, RET_DV = 4, 128, 256
BLK = 128
Q_HEADS, KV_HEADS, HEAD_DIM = 16, 2, 64
GROUP = Q_HEADS // KV_HEADS
D_FF = 2816
N_RET, N_GATE, N_ATTN = 2048, 3072, 1280
D_IN = N_RET + N_GATE + N_ATTN
ROPE_THETA = 10000.0
EPS = 1e-6
RET_SCALE = RET_DK ** -0.5
ATTN_SCALE = HEAD_DIM ** -0.5
LR, B1, B2, ADAM_EPS, WD, STEP = 0.001, 0.9, 0.999, 1e-08, 0.01, 10
VMEM_LIMIT_MB = 56
MESH = pl.DeviceIdType.MESH


def _dot(a, b):
    return jnp.dot(a.astype(_MXU), b.astype(_MXU), preferred_element_type=F32)


def _dot_nt(a, b):
    return lax.dot_general(a.astype(_MXU), b.astype(_MXU), (((1,), (1,)), ((), ())), preferred_element_type=F32)


def _dot_tn(a, b):
    return lax.dot_general(a.astype(_MXU), b.astype(_MXU), (((0,), (0,)), ((), ())), preferred_element_type=F32)


def _sigmoid(x):
    return 1.0 / (1.0 + jnp.exp(-x))


def _cparams(n_axes, big=False):
    kw = dict(dimension_semantics=("arbitrary",) * n_axes)
    if big:
        kw["vmem_limit_bytes"] = VMEM_LIMIT_MB * 2**20
    return pltpu.CompilerParams(**kw)


def _rows(tm, width, col=0):
    return pl.BlockSpec((tm, width), lambda i: (i, col))


def _const(shape):
    nd = len(shape)
    return pl.BlockSpec(shape, lambda *_: (0,) * nd, pipeline_mode=pl.Buffered(1))


def _acc(width):
    return pl.BlockSpec((1, width), lambda *_: (0, 0))


def _sds(shape, dtype):
    return jax.ShapeDtypeStruct(shape, dtype)


def _swap_halves(x, half):
    w = x.shape[-1]
    if 2 * half == w:
        return pltpu.roll(x, half, 1)
    lane = lax.broadcasted_iota(jnp.int32, x.shape, 1)
    return jnp.where(lane % (2 * half) < half, pltpu.roll(x, w - half, 1), pltpu.roll(x, half, 1))


def _rope_tables(seq):
    lane = jnp.arange(128, dtype=jnp.int32)
    ret_freq = ROPE_THETA ** (-(lane % 64).astype(F32) / 64)
    attn_freq = ROPE_THETA ** (-(lane % 32).astype(F32) / 32)
    ang = jnp.arange(seq, dtype=jnp.int32).astype(F32)[:, None] * jnp.where(lane < 64, ret_freq, attn_freq)[None, :]
    return jnp.cos(ang), jnp.sin(ang)


def _ret_rope(cos, sin):
    low = lax.broadcasted_iota(jnp.int32, cos.shape, 1) < RET_DK // 2
    return jnp.where(low, cos, pltpu.roll(cos, RET_DK // 2, 1)), jnp.where(low, -sin, pltpu.roll(sin, RET_DK // 2, 1))


def _attn_rope(cos, sin):
    lane = lax.broadcasted_iota(jnp.int32, cos.shape, 1)
    half = HEAD_DIM // 2

    def spread(t):
        t = pltpu.roll(t, 64, 1)
        t = jnp.where(lane < half, t, pltpu.roll(t, half, 1))
        return jnp.where(lane < HEAD_DIM, t, pltpu.roll(t, HEAD_DIM, 1))

    return spread(cos), jnp.where(lane % HEAD_DIM < half, -spread(sin), spread(sin))


def _retention_decays():
    log_gamma = np.log1p(-np.exp2(-5.0 - np.arange(RET_HEADS, dtype=np.float32))).astype(np.float32)
    idx = np.arange(BLK, dtype=np.float32)
    rel = idx[:, None] - idx[None, :]
    intra = np.where(rel[None] >= 0, np.exp(log_gamma[:, None, None] * np.maximum(rel, 0.0)[None]), 0.0)
    q_decay = np.exp(log_gamma[:, None] * (idx + 1.0))[:, :, None]
    k_decay = np.exp(log_gamma[:, None] * (BLK - 1.0 - idx))[:, :, None]
    chunk_decay = [float(np.exp(np.float32(lg * BLK))) for lg in log_gamma]
    return (jnp.asarray(intra, F32), jnp.asarray(q_decay, F32), jnp.asarray(k_decay, F32), chunk_decay)


def _position():
    return lax.axis_index("x"), lax.axis_index("y"), lax.axis_index("c")


def _slot(px, py, pc):
    return 4 * px + 2 * py + pc


class _AllGather:
    def __init__(self, blocks):
        self.blocks = list(blocks)
        nb = len(self.blocks)
        self.out_shape = [_sds((N_DEV,) + b.shape, b.dtype) for b in self.blocks]
        self.scratch = [pltpu.SemaphoreType.DMA((nb, 7)), pltpu.SemaphoreType.DMA((nb, 7)),
                        pltpu.SemaphoreType.DMA((nb,))]

    def phases(self, ins, outs, send_sems, recv_sems, local_sems):
        nb = len(ins)
        x, y, c = _position()
        me, sibling = (x, y, c), (x, y, 1 - c)
        chips = [(1 - x, y), (x, 1 - y), (1 - x, 1 - y)]

        def copy(b, k, block, to, src=None):
            dst = outs[b].at[_slot(*block)]
            return pltpu.make_async_remote_copy(
                src_ref=dst if src is None else src, dst_ref=dst, send_sem=send_sems.at[b, k],
                recv_sem=recv_sems.at[b, k], device_id=to, device_id_type=MESH)

        def own(b):
            return pltpu.make_async_copy(ins[b], outs[b].at[_slot(*me)], local_sems.at[b])

        def first(b):
            return [copy(b, 0, me, sibling, src=ins[b])] + [
                copy(b, 1 + j, me, (*chip, c), src=ins[b]) for j, chip in enumerate(chips)]

        def start():
            for b in range(nb):
                own(b).start()
                for cp in first(b):
                    cp.start()

        def forward():
            for b in range(nb):
                for j, chip in enumerate(chips):
                    copy(b, 1 + j, (*chip, c), me).wait_recv()
                    copy(b, 4 + j, (*chip, c), sibling).start()

        def finish():
            for b in range(nb):
                copy(b, 0, sibling, me).wait_recv()
                for j, chip in enumerate(chips):
                    copy(b, 4 + j, (*chip, 1 - c), me).wait_recv()
            for b in range(nb):
                for cp in first(b):
                    cp.wait_send()
                for j, chip in enumerate(chips):
                    copy(b, 4 + j, (*chip, c), sibling).wait_send()
                own(b).wait()

        return start, forward, finish


class _AllToAll:
    def __init__(self, blocks):
        self.blocks = list(blocks)
        nb = len(self.blocks)
        self.out_shape = [_sds(b.shape, b.dtype) for b in self.blocks]
        self.scratch = [pltpu.SemaphoreType.DMA((nb, 7)), pltpu.SemaphoreType.DMA((nb, 7)),
                        pltpu.SemaphoreType.DMA((nb,))]

    def phases(self, ins, outs, send_sems, recv_sems, local_sems):
        nb = len(ins)
        x, y, c = _position()
        flip = lambda v, bit: 1 - v if bit else v
        peers = [(flip(x, k >> 2 & 1), flip(y, k >> 1 & 1), flip(c, k & 1)) for k in range(1, N_DEV)]

        def copy(b, k, peer, landed=False):
            return pltpu.make_async_remote_copy(
                src_ref=ins[b].at[_slot(*peer)], dst_ref=outs[b].at[_slot(*peer) if landed else _slot(x, y, c)],
                send_sem=send_sems.at[b, k], recv_sem=recv_sems.at[b, k], device_id=peer, device_id_type=MESH)

        def own(b):
            return pltpu.make_async_copy(ins[b].at[_slot(x, y, c)], outs[b].at[_slot(x, y, c)], local_sems.at[b])

        def start():
            for b in range(nb):
                own(b).start()
                for k, peer in enumerate(peers):
                    copy(b, k, peer).start()

        def forward():
            pass

        def finish():
            for b in range(nb):
                for k, peer in enumerate(peers):
                    copy(b, k, peer, landed=True).wait_recv()
            for b in range(nb):
                for k, peer in enumerate(peers):
                    copy(b, k, peer).wait_send()
                own(b).wait()

        return start, forward, finish


class _RowScatter:
    def __init__(self, arrays, pieces, n):
        self.blocks = list(arrays)
        self.pieces, self.n = pieces, n
        self.out_shape = [_sds((N_DEV, n, D), arrays[0].dtype)]
        self.scratch = [pltpu.SemaphoreType.DMA((N_DEV,)), pltpu.SemaphoreType.DMA((N_DEV,)), pltpu.SemaphoreType.DMA]

    def _parts(self, k):
        lo, hi, pos, res = k * self.n, (k + 1) * self.n, 0, []
        for arr, first, last in self.pieces:
            a, b = max(lo, pos), min(hi, pos + last - first)
            if a < b:
                res.append((arr, first + a - pos, b - a, a - lo))
            pos += last - first
        return res

    def phases(self, ins, outs, send_sems, recv_sems, local_sem):
        (out,) = outs
        x, y, c = _position()
        me = _slot(x, y, c)

        def start():
            for k in range(N_DEV):
                dist = jnp.bitwise_xor(me, k)

                @pl.when(me != k)
                def _():
                    for arr, first, rows, at in self._parts(k):
                        pltpu.make_async_remote_copy(
                            src_ref=ins[arr].at[pl.ds(first, rows)], dst_ref=out.at[me, pl.ds(at, rows)],
                            send_sem=send_sems.at[dist], recv_sem=recv_sems.at[dist],
                            device_id=(k >> 2 & 1, k >> 1 & 1, k & 1), device_id_type=MESH).start()

                @pl.when(me == k)
                def _():
                    for arr, first, rows, at in self._parts(k):
                        pltpu.make_async_copy(ins[arr].at[pl.ds(first, rows)], out.at[me, pl.ds(at, rows)], local_sem).start()

        def forward():
            pass

        def whole_block(dist):
            return pltpu.make_async_remote_copy(
                src_ref=out.at[me], dst_ref=out.at[jnp.bitwise_xor(me, dist)], send_sem=send_sems.at[dist],
                recv_sem=recv_sems.at[dist], device_id=(x, y, c), device_id_type=MESH)

        def finish():
            for dist in range(1, N_DEV):
                whole_block(dist).wait_recv()
            for dist in range(1, N_DEV):
                whole_block(dist).wait_send()
            pltpu.make_async_copy(out.at[me], out.at[me], local_sem).wait()

        return start, forward, finish


def _call(body, *, name, grid, in_specs, out_specs, out_shape, args, scratch_shapes=(), big=False, exchange=None):
    params = _cparams(len(grid), big)
    if exchange is None:
        return pl.pallas_call(body, name=name, grid=grid, in_specs=in_specs, out_specs=out_specs, out_shape=out_shape,
                              scratch_shapes=list(scratch_shapes), compiler_params=params)(*args)
    n_in, n_out, n_scr = len(in_specs), len(out_specs), len(scratch_shapes)
    nb, nb_out = len(exchange.blocks), len(exchange.out_shape)
    steps = math.prod(grid)

    def carried(*refs):
        pos = 0
        parts = []
        for n in (n_in, nb, n_out, nb_out, n_scr, len(exchange.scratch)):
            parts.append(refs[pos:pos + n])
            pos += n
        ins, x_ins, outs, x_outs, scr, sems = parts
        step = pl.program_id(0)
        for axis in range(1, len(grid)):
            step = step * grid[axis] + pl.program_id(axis)
        start, forward, finish = exchange.phases(x_ins, x_outs, *sems)
        pl.when(step == 0)(start)
        body(*ins, *outs, *scr)

        @pl.when(step == steps - 1)
        def _():
            forward()
            finish()

    any_spec = pl.BlockSpec(memory_space=pl.ANY)
    res = pl.pallas_call(
        carried, name=name, grid=grid, in_specs=list(in_specs) + [any_spec] * nb,
        out_specs=list(out_specs) + [any_spec] * nb_out, out_shape=list(out_shape) + exchange.out_shape,
        scratch_shapes=list(scratch_shapes) + exchange.scratch, compiler_params=params)(*args, *exchange.blocks)
    return res[:n_out], res[n_out:]


def _exchange_call(exchange, name):
    nb = len(exchange.blocks)

    def body(*refs):
        start, forward, finish = exchange.phases(refs[:nb], refs[nb:2 * nb], *refs[2 * nb:])
        start()
        forward()
        finish()

    any_spec = pl.BlockSpec(memory_space=pl.ANY)
    return pl.pallas_call(body, name=name, in_specs=[any_spec] * nb, out_specs=[any_spec] * nb,
                          out_shape=exchange.out_shape, scratch_shapes=exchange.scratch)(*exchange.blocks)


def _ln_call(x, g, tm, exchange):
    s = x.shape[0]

    def body(x_ref, g_ref, h_ref):
        xv = x_ref[...]
        r = lax.rsqrt(jnp.mean(xv * xv, axis=-1, keepdims=True) + EPS)
        h_ref[...] = ((xv * r) * g_ref[...]).astype(h_ref.dtype)

    return _call(body, name="ln1", grid=(s // tm,), in_specs=[_rows(tm, D), _acc(D)], out_specs=[_rows(tm, D)],
                 out_shape=[_sds((s, D), _MXU)], args=(x, g), exchange=exchange)


PROJ_TILE = 256


def _proj_source_tile(j):
    gate_end, attn_end, end = 3072 // PROJ_TILE, 4352 // PROJ_TILE, 6400 // PROJ_TILE
    n_gates = end - attn_end
    return jnp.where(j < gate_end, j, jnp.where(j < gate_end + n_gates, j + (attn_end - gate_end), j - n_gates))


def _proj_call(a, wt, bias, exchange):
    s, k = a.shape
    n = wt.shape[0]
    rows = min(1024, s)

    def body(a_ref, w_ref, b_ref, o_ref):
        for r in range(0, s, rows):
            o_ref[r:r + rows, :] = _dot_nt(a_ref[r:r + rows, :], w_ref[...]) + b_ref[...]

    return _call(body, name="proj", grid=(n // PROJ_TILE,),
                 in_specs=[_const((s, k)), pl.BlockSpec((PROJ_TILE, k), lambda j: (_proj_source_tile(j), 0)),
                           pl.BlockSpec((1, PROJ_TILE), lambda j: (0, _proj_source_tile(j)))],
                 out_specs=[pl.BlockSpec((s, PROJ_TILE), lambda j: (0, j))], out_shape=[_sds((s, n), F32)],
                 args=(a, wt, bias), big=True, exchange=exchange)


def _mm_tn(a, b, name, tm, tn, tk, exchange=None):
    s, m = a.shape
    n = b.shape[1]
    last = s // tk - 1

    def body(a_ref, b_ref, o_ref, acc):
        k = pl.program_id(2)
        part = _dot_tn(a_ref[...], b_ref[...])

        @pl.when(k == 0)
        def _():
            acc[...] = part

        @pl.when(k > 0)
        def _():
            acc[...] += part

        @pl.when(k == last)
        def _():
            o_ref[...] = acc[...].astype(o_ref.dtype)

    res = _call(body, name=name, grid=(m // tm, n // tn, s // tk),
                in_specs=[pl.BlockSpec((tk, tm), lambda i, j, k: (k, i)), pl.BlockSpec((tk, tn), lambda i, j, k: (k, j))],
                out_specs=[pl.BlockSpec((tm, tn), lambda i, j, k: (i, j))], out_shape=[_sds((m, n), _MXU)],
                scratch_shapes=[pltpu.VMEM((tm, tn), F32)], args=(a, b), big=True, exchange=exchange)
    return res[0] if exchange is None else (res[0][0], res[1])


RET_CHUNKS = 8


def _ret_fwd_call(proj, cos, sin, decays):
    s = proj.shape[0]
    nblk = s // BLK
    per = min(RET_CHUNKS, nblk)
    rows = per * BLK
    intra, q_decay, k_decay, chunk_decay = decays

    def body(rq_ref, rk_ref, rv_ref, cos_ref, sin_ref, intra_ref, qd_ref, kd_ref,
             ry_ref, qr_ref, kr_ref, st_ref, state):
        @pl.when(pl.program_id(0) == 0)
        def _():
            state[...] = jnp.zeros_like(state)

        for c in range(per):
            rc = slice(c * BLK, (c + 1) * BLK)
            cos_v, sin_v = _ret_rope(cos_ref[rc, :], sin_ref[rc, :])
            for h in range(RET_HEADS):
                hk = slice(h * RET_DK, (h + 1) * RET_DK)
                hv = slice(h * RET_DV, (h + 1) * RET_DV)
                q, k = rq_ref[rc, hk], rk_ref[rc, hk]
                qr = (q * cos_v + _swap_halves(q, RET_DK // 2) * sin_v) * RET_SCALE
                kr = k * cos_v + _swap_halves(k, RET_DK // 2) * sin_v
                v = rv_ref[rc, hv]
                s_h = state[h]
                st_ref[c, h] = s_h.astype(st_ref.dtype)
                scores = _dot_nt(qr, kr) * intra_ref[h]
                ry_ref[rc, hv] = _dot(scores, v) + _dot(qr, s_h) * qd_ref[h]
                state[h] = s_h * chunk_decay[h] + _dot_tn(kr * kd_ref[h], v)
                qr_ref[rc, hk] = qr.astype(qr_ref.dtype)
                kr_ref[rc, hk] = kr.astype(kr_ref.dtype)

    blk = lambda w, c: pl.BlockSpec((rows, w), lambda n: (n, c))
    return _call(body, name="ret_fwd", grid=(nblk // per,),
                 in_specs=[blk(512, 0), blk(512, 1), blk(1024, 1), blk(128, 0), blk(128, 0),
                           _const(intra.shape), _const(q_decay.shape), _const(k_decay.shape)],
                 out_specs=[blk(1024, 0), blk(512, 0), blk(512, 0),
                            pl.BlockSpec((per, RET_HEADS, RET_DK, RET_DV), lambda n: (n, 0, 0, 0))],
                 out_shape=[_sds((s, 1024), F32), _sds((s, 512), _MXU), _sds((s, 512), _MXU),
                            _sds((nblk, RET_HEADS, RET_DK, RET_DV), _MXU)],
                 scratch_shapes=[pltpu.VMEM((RET_HEADS, RET_DK, RET_DV), F32)],
                 args=(proj, proj, proj, cos, sin, intra, q_decay, k_decay))


def _ret_bwd_call(qr, kr, proj, states, dry, cos, sin, decays, exchange):
    s = qr.shape[0]
    nblk = s // BLK
    per = min(RET_CHUNKS, nblk)
    rows = per * BLK
    steps = nblk // per
    intra, q_decay, k_decay, chunk_decay = decays

    def body(qr_ref, kr_ref, rv_ref, st_ref, dry_ref, cos_ref, sin_ref, intra_ref, qd_ref, kd_ref,
             dp_ref, db_ref, dstate):
        @pl.when(pl.program_id(0) == 0)
        def _():
            dstate[...] = jnp.zeros_like(dstate)
            db_ref[...] = jnp.zeros_like(db_ref)

        for c in reversed(range(per)):
            rc = slice(c * BLK, (c + 1) * BLK)
            cos_v, sin_v = _ret_rope(cos_ref[rc, :], sin_ref[rc, :])
            for h in range(RET_HEADS):
                hk = slice(h * RET_DK, (h + 1) * RET_DK)
                hv = slice(h * RET_DV, (h + 1) * RET_DV)
                q, k, v, d_out = qr_ref[rc, hk], kr_ref[rc, hk], rv_ref[rc, hv], dry_ref[rc, hv]
                d_next = dstate[h]
                scores = _dot_nt(q, k) * intra_ref[h]
                d_scores = _dot_nt(d_out, v) * intra_ref[h]
                d_cross = d_out * qd_ref[h]
                dq = _dot(d_scores, k) + _dot_nt(d_cross, st_ref[c, h])
                dk = _dot_tn(d_scores, q) + _dot_nt(v, d_next) * kd_ref[h]
                dv = _dot_tn(scores, d_out) + _dot(k.astype(F32) * kd_ref[h], d_next)
                dstate[h] = d_next * chunk_decay[h] + _dot_tn(q, d_cross)
                dq = (dq * cos_v - _swap_halves(dq, RET_DK // 2) * sin_v) * RET_SCALE
                dk = dk * cos_v - _swap_halves(dk, RET_DK // 2) * sin_v
                kcols = slice(512 + h * RET_DK, 512 + (h + 1) * RET_DK)
                vcols = slice(1024 + h * RET_DV, 1024 + (h + 1) * RET_DV)
                dp_ref[rc, hk] = dq.astype(dp_ref.dtype)
                dp_ref[rc, kcols] = dk.astype(dp_ref.dtype)
                dp_ref[rc, vcols] = dv.astype(dp_ref.dtype)
                db_ref[:, hk] += jnp.sum(dq, axis=0, keepdims=True)
                db_ref[:, kcols] += jnp.sum(dk, axis=0, keepdims=True)
                db_ref[:, vcols] += jnp.sum(dv, axis=0, keepdims=True)

    rblk = lambda w, c: pl.BlockSpec((rows, w), lambda n: (steps - 1 - n, c))
    return _call(body, name="ret_bwd", grid=(steps,),
                 in_specs=[rblk(512, 0), rblk(512, 0), rblk(1024, 1),
                           pl.BlockSpec((per, RET_HEADS, RET_DK, RET_DV), lambda n: (steps - 1 - n, 0, 0, 0)),
                           rblk(1024, 0), rblk(128, 0), rblk(128, 0),
                           _const(intra.shape), _const(q_decay.shape), _const(k_decay.shape)],
                 out_specs=[rblk(N_RET, 0), _acc(N_RET)],
                 out_shape=[_sds((s, N_RET), _MXU), _sds((1, N_RET), F32)],
                 scratch_shapes=[pltpu.VMEM((RET_HEADS, RET_DK, RET_DV), F32)],
                 args=(qr, kr, proj, states, dry, cos, sin, intra, q_decay, k_decay), exchange=exchange)


def _both_halves(x, g):
    lane = lax.broadcasted_iota(jnp.int32, x.shape, 1)
    keep = lane < HEAD_DIM if g == 0 else lane >= HEAD_DIM
    return jnp.where(keep, x, pltpu.roll(x, HEAD_DIM, 1))


def _stack_heads(ref, g, rows=slice(None)):
    lane = lax.broadcasted_iota(jnp.int32, (BLK, 128), 1)
    pieces = []
    for j in range(g * 4, g * 4 + 4):
        chunk = ref[rows, j * 128:(j + 1) * 128]
        pieces += [jnp.where(lane < HEAD_DIM, chunk, jnp.zeros_like(chunk)),
                   jnp.where(lane >= HEAD_DIM, chunk, jnp.zeros_like(chunk))]
    return jnp.concatenate(pieces, axis=0)


def _window_bias(first_block):
    kj = lax.broadcasted_iota(jnp.int32, (2 * BLK, BLK), 0)
    qi = lax.broadcasted_iota(jnp.int32, (2 * BLK, BLK), 1)
    first_key = jnp.where(first_block, BLK, 0)
    seen = (kj > qi) & (kj <= qi + BLK) & (kj >= first_key)
    return jnp.where(seen, 0.0, -1e30)


def _sink_softmax(scores, sink):
    m = jnp.maximum(jnp.max(scores, axis=0, keepdims=True), sink)
    e = jnp.exp(scores - m)
    e_sink = jnp.exp(sink - m)
    return e, e_sink, 1.0 / (jnp.sum(e, axis=0, keepdims=True) + e_sink)


def _head_pair(stacked_t, jj):
    even = stacked_t[0:HEAD_DIM, 2 * jj * BLK:(2 * jj + 1) * BLK]
    odd = stacked_t[HEAD_DIM:128, (2 * jj + 1) * BLK:(2 * jj + 2) * BLK]
    return jnp.concatenate([even, odd], axis=0).T


ATTN_BLOCKS = 8


def _attn_fwd_call(proj, sinks, cos, sin):
    s = proj.shape[0]
    nblk = s // BLK
    per = min(ATTN_BLOCKS, nblk)
    rows = per * BLK

    def body(sink_ref, q_ref, k_ref, v_ref, cos_ref, sin_ref, ay_ref, qr_ref, kr_ref, vb_ref,
             kwin, vwin, bias, s_scr, p_scr):
        n = pl.program_id(0)

        @pl.when(n == 0)
        def _():
            kwin[...] = jnp.zeros_like(kwin)
            vwin[...] = jnp.zeros_like(vwin)

        for c in range(per):
            rc = slice(c * BLK, (c + 1) * BLK)
            kwin[0:BLK] = kwin[BLK:2 * BLK]
            vwin[0:BLK] = vwin[BLK:2 * BLK]
            cos_v, sin_v = _attn_rope(cos_ref[rc, :], sin_ref[rc, :])
            k = k_ref[rc, :]
            kr = (k * cos_v + _swap_halves(k, HEAD_DIM // 2) * sin_v).astype(kwin.dtype)
            kwin[BLK:2 * BLK] = kr
            vwin[BLK:2 * BLK] = v_ref[rc, :].astype(vwin.dtype)
            kr_ref[rc, :] = kr
            vb_ref[rc, :] = vwin[BLK:2 * BLK]
            for j in range(Q_HEADS // 2):
                cols = slice(j * 128, (j + 1) * 128)
                q = q_ref[rc, cols]
                qr_ref[rc, cols] = ((q * cos_v + _swap_halves(q, HEAD_DIM // 2) * sin_v) * ATTN_SCALE).astype(qr_ref.dtype)
            bias[...] = _window_bias(n == 0) if c == 0 else _window_bias(False)
            for g in range(KV_HEADS):
                kg = _both_halves(kwin[...], g)
                vg_t = _both_halves(vwin[...], g).astype(F32).T
                s_scr[...] = _dot_nt(kg, _stack_heads(qr_ref, g, rc))
                for i in range(GROUP):
                    cols = slice(i * BLK, (i + 1) * BLK)
                    e, _, inv = _sink_softmax(s_scr[:, cols] + bias[...], sink_ref[0, g * GROUP + i])
                    p_scr[:, cols] = (e * inv).astype(p_scr.dtype)
                out_t = _dot(vg_t, p_scr[...])
                for jj in range(4):
                    j = g * 4 + jj
                    ay_ref[rc, j * 128:(j + 1) * 128] = _head_pair(out_t, jj).astype(ay_ref.dtype)

    blk = lambda w, c: pl.BlockSpec((rows, w), lambda n: (n, c))
    off = (N_RET + N_GATE) // 128
    wide = (2 * BLK, GROUP * BLK)
    return _call(body, name="attn_fwd", grid=(nblk // per,),
                 in_specs=[pl.BlockSpec(memory_space=pltpu.SMEM), blk(1024, off // 8), blk(128, off + 8), blk(128, off + 9),
                           blk(128, 0), blk(128, 0)],
                 out_specs=[blk(1024, 0), blk(1024, 0), blk(128, 0), blk(128, 0)],
                 out_shape=[_sds((s, 1024), _MXU), _sds((s, 1024), _MXU), _sds((s, 128), _MXU), _sds((s, 128), _MXU)],
                 scratch_shapes=[pltpu.VMEM((2 * BLK, 128), _MXU), pltpu.VMEM((2 * BLK, 128), _MXU),
                                 pltpu.VMEM((2 * BLK, BLK), F32), pltpu.VMEM(wide, F32), pltpu.VMEM(wide, _MXU)],
                 args=(sinks, proj, proj, proj, cos, sin))


def _attn_bwd_call(qr, kr, vb, day, sinks, cos, sin, exchange):
    s = qr.shape[0]
    nblk = s // BLK

    def body(sink_ref, q_ref, kc_ref, kp_ref, vc_ref, vp_ref, do_ref, cos_ref, sin_ref, cosp_ref, sinp_ref,
             dp_ref, dsink_ref, db_ref, bias, s_scr, dp_scr, p_scr, ds_scr, dq_held, kv_held, kv_prev, kv_new):
        n = pl.program_id(0)
        valid = (n < nblk).astype(F32)

        @pl.when(n == 0)
        def _():
            dsink_ref[...] = jnp.zeros_like(dsink_ref)
            db_ref[...] = jnp.zeros_like(db_ref)

        @pl.when(n >= 1)
        def _():
            dp_ref[:, 0:1024] = dq_held[...]

        cos_v, sin_v = _attn_rope(cos_ref[...], sin_ref[...])
        bias[...] = _window_bias(n == 0)
        lane1 = lax.broadcasted_iota(jnp.int32, (1, 128), 1)
        kwin = jnp.concatenate([kp_ref[...], kc_ref[...]], axis=0)
        vwin = jnp.concatenate([vp_ref[...], vc_ref[...]], axis=0)
        dk_heads, dv_heads = [], []
        dsink = jnp.zeros((1, 128), F32)
        for g in range(KV_HEADS):
            kg = _both_halves(kwin, g)
            vg = _both_halves(vwin, g)
            q_all = _stack_heads(q_ref, g)
            do_all = _stack_heads(do_ref, g)
            s_scr[...] = _dot_nt(kg, q_all)
            dp_scr[...] = _dot_nt(vg, do_all)
            for i in range(GROUP):
                head = g * GROUP + i
                cols = slice(i * BLK, (i + 1) * BLK)
                e, e_sink, inv = _sink_softmax(s_scr[:, cols] + bias[...], sink_ref[0, head])
                p = e * inv
                dp = dp_scr[:, cols]
                delta = jnp.sum(p * dp, axis=0, keepdims=True)
                p_scr[:, cols] = p.astype(p_scr.dtype)
                ds_scr[:, cols] = (p * (dp - delta)).astype(ds_scr.dtype)
                dsink = dsink + jnp.where(lane1 == head, -jnp.sum(e_sink * inv * delta, axis=1, keepdims=True), 0.0)
            dv_both = _dot(p_scr[...], do_all)
            dk_both = _dot(ds_scr[...], q_all)
            dv_heads.append(dv_both + pltpu.roll(dv_both, HEAD_DIM, 1))
            dk_heads.append(dk_both + pltpu.roll(dk_both, HEAD_DIM, 1))
            dq_t = _dot(kg.astype(F32).T, ds_scr[...])
            for jj in range(4):
                cols = slice((g * 4 + jj) * 128, (g * 4 + jj + 1) * 128)
                dq = _head_pair(dq_t, jj)
                dq = (dq * cos_v - _swap_halves(dq, HEAD_DIM // 2) * sin_v) * ATTN_SCALE
                dq_held[:, cols] = dq.astype(dq_held.dtype)
                db_ref[:, cols] += jnp.sum(dq, axis=0, keepdims=True) * valid
        dsink_ref[...] += dsink * valid
        lane2 = lax.broadcasted_iota(jnp.int32, (2 * BLK, 128), 1)
        dk_all = jnp.where(lane2 < HEAD_DIM, dk_heads[0], dk_heads[1])
        dv_all = jnp.where(lane2 < HEAD_DIM, dv_heads[0], dv_heads[1])
        kv_prev[:, 0:128] = dk_all[0:BLK] * valid
        kv_prev[:, 128:256] = dv_all[0:BLK] * valid
        kv_new[:, 0:128] = dk_all[BLK:2 * BLK]
        kv_new[:, 128:256] = dv_all[BLK:2 * BLK]

        @pl.when(n >= 1)
        def _():
            dkv = kv_held[...] + kv_prev[...]
            dk = dkv[:, 0:128]
            cos_p, sin_p = _attn_rope(cosp_ref[...], sinp_ref[...])
            dk = dk * cos_p - _swap_halves(dk, HEAD_DIM // 2) * sin_p
            dv = dkv[:, 128:256]
            dp_ref[:, 1024:1152] = dk.astype(dp_ref.dtype)
            dp_ref[:, 1152:1280] = dv.astype(dp_ref.dtype)
            db_ref[:, 1024:1152] += jnp.sum(dk, axis=0, keepdims=True)
            db_ref[:, 1152:1280] += jnp.sum(dv, axis=0, keepdims=True)

        kv_held[...] = kv_new[...]

    blk = lambda w: pl.BlockSpec((BLK, w), lambda n: (jnp.minimum(n, nblk - 1), 0))
    pblk = lambda w: pl.BlockSpec((BLK, w), lambda n: (jnp.maximum(n - 1, 0), 0))
    wide = (2 * BLK, GROUP * BLK)
    return _call(body, name="attn_bwd", grid=(nblk + 1,),
                 in_specs=[pl.BlockSpec(memory_space=pltpu.SMEM), blk(1024), blk(128), pblk(128), blk(128), pblk(128),
                           blk(1024), blk(128), blk(128), pblk(128), pblk(128)],
                 out_specs=[pblk(N_ATTN), _acc(128), _acc(N_ATTN)],
                 out_shape=[_sds((s, N_ATTN), _MXU), _sds((1, 128), F32), _sds((1, N_ATTN), F32)],
                 scratch_shapes=[pltpu.VMEM((2 * BLK, BLK), F32), pltpu.VMEM(wide, F32), pltpu.VMEM(wide, F32),
                                 pltpu.VMEM(wide, _MXU), pltpu.VMEM(wide, _MXU), pltpu.VMEM((BLK, 1024), _MXU),
                                 pltpu.VMEM((BLK, 256), F32), pltpu.VMEM((BLK, 256), F32), pltpu.VMEM((BLK, 256), F32)],
                 args=(sinks, qr, kr, kr, vb, vb, day, cos, sin, cos, sin), exchange=exchange)


def _group_norm(y):
    mu = jnp.mean(y, axis=-1, keepdims=True)
    yc = y - mu
    rs = lax.rsqrt(jnp.mean(yc * yc, axis=-1, keepdims=True) + EPS)
    return yc * rs, rs


GATE_COL = N_RET // 1024


def _merge_fwd_call(x, ry, proj, ay, gn_g, w_ro, w_ao, w_o, tm):
    s = x.shape[0]

    def body(x_ref, ry_ref, rg_ref, ga_ref, gb_ref, ay_ref, gn_ref, wro_ref, wao_ref, wo_ref,
             ain_ref, a_ref, b_ref, mg_ref, x1_ref):
        for h in range(RET_HEADS):
            hv = slice(h * RET_DV, (h + 1) * RET_DV)
            yhat, _ = _group_norm(ry_ref[:, hv])
            rg = rg_ref[:, hv]
            ain_ref[:, hv] = ((rg * _sigmoid(rg)) * (yhat * gn_ref[:, hv])).astype(ain_ref.dtype)
        a = _dot(ain_ref[...], wro_ref[...])
        b = _dot(ay_ref[...], wao_ref[...])
        a_ref[...] = a.astype(a_ref.dtype)
        b_ref[...] = b.astype(b_ref.dtype)
        merged = (_sigmoid(ga_ref[...]) * a + _sigmoid(gb_ref[...]) * b).astype(mg_ref.dtype)
        mg_ref[...] = merged
        x1_ref[...] = x_ref[...] + _dot(merged, wo_ref[...])

    return _call(body, name="merge_fwd", grid=(s // tm,),
                 in_specs=[_rows(tm, D), _rows(tm, 1024), _rows(tm, 1024, GATE_COL), _rows(tm, 1024, GATE_COL + 1),
                           _rows(tm, 1024, GATE_COL + 2), _rows(tm, 1024), _acc(1024),
                           _const((D, D)), _const((D, D)), _const((D, D))],
                 out_specs=[_rows(tm, D)] * 5,
                 out_shape=[_sds((s, D), _MXU), _sds((s, D), _STORE), _sds((s, D), _STORE), _sds((s, D), _MXU),
                            _sds((s, D), F32)],
                 args=(x, ry, proj, proj, proj, ay, gn_g, w_ro, w_ao, w_o), big=True)


def _merge_bwd_call(dx1, a, b, ry, proj, gn_g, w_ro, w_ao, w_o, tm):
    s = dx1.shape[0]

    def body(dx1_ref, a_ref, b_ref, ry_ref, rg_ref, ga_ref, gb_ref, gn_ref, wro_ref, wao_ref, wo_ref,
             da_ref, dbr_ref, dp_ref, day_ref, dry_ref, dbias_ref, dgn_ref):
        @pl.when(pl.program_id(0) == 0)
        def _():
            dbias_ref[...] = jnp.zeros_like(dbias_ref)
            dgn_ref[...] = jnp.zeros_like(dgn_ref)

        d_merged = _dot_nt(dx1_ref[...], wo_ref[...])
        sa, sb = _sigmoid(ga_ref[...]), _sigmoid(gb_ref[...])
        d_a = d_merged * sa
        d_b = d_merged * sb
        da_ref[...] = d_a.astype(da_ref.dtype)
        dbr_ref[...] = d_b.astype(dbr_ref.dtype)
        d_ga = d_merged * a_ref[...] * (sa * (1.0 - sa))
        d_gb = d_merged * b_ref[...] * (sb * (1.0 - sb))
        dp_ref[:, 1024:2048] = d_ga.astype(dp_ref.dtype)
        dp_ref[:, 2048:3072] = d_gb.astype(dp_ref.dtype)
        dbias_ref[:, 1024:2048] += jnp.sum(d_ga, axis=0, keepdims=True)
        dbias_ref[:, 2048:3072] += jnp.sum(d_gb, axis=0, keepdims=True)
        day_ref[...] = _dot_nt(d_b, wao_ref[...]).astype(day_ref.dtype)
        d_ain = _dot_nt(d_a, wro_ref[...])
        for h in range(RET_HEADS):
            hv = slice(h * RET_DV, (h + 1) * RET_DV)
            yhat, rs = _group_norm(ry_ref[:, hv])
            rg = rg_ref[:, hv]
            sg = _sigmoid(rg)
            gn = gn_ref[:, hv]
            d_h = d_ain[:, hv]
            d_rg = d_h * (yhat * gn) * (sg * (1.0 + rg * (1.0 - sg)))
            d_ryn = d_h * (rg * sg)
            dgn_ref[:, hv] += jnp.sum(d_ryn * yhat, axis=0, keepdims=True)
            d_yhat = d_ryn * gn
            dry_ref[:, hv] = (rs * (d_yhat - jnp.mean(d_yhat, axis=-1, keepdims=True)
                                    - yhat * jnp.mean(d_yhat * yhat, axis=-1, keepdims=True))).astype(dry_ref.dtype)
            dp_ref[:, hv] = d_rg.astype(dp_ref.dtype)
            dbias_ref[:, hv] += jnp.sum(d_rg, axis=0, keepdims=True)

    return _call(body, name="merge_bwd", grid=(s // tm,),
                 in_specs=[_rows(tm, D), _rows(tm, D), _rows(tm, D), _rows(tm, 1024), _rows(tm, 1024, GATE_COL),
                           _rows(tm, 1024, GATE_COL + 1), _rows(tm, 1024, GATE_COL + 2), _acc(1024),
                           _const((D, D)), _const((D, D)), _const((D, D))],
                 out_specs=[_rows(tm, D), _rows(tm, D), _rows(tm, N_GATE), _rows(tm, D), _rows(tm, D), _acc(N_GATE),
                            _acc(1024)],
                 out_shape=[_sds((s, D), _MXU), _sds((s, D), _MXU), _sds((s, N_GATE), _MXU), _sds((s, D), _MXU),
                            _sds((s, D), _STORE), _sds((1, N_GATE), F32), _sds((1, 1024), F32)],
                 args=(dx1, a, b, ry, proj, proj, proj, gn_g, w_ro, w_ao, w_o), big=True)


def _ffn_fwd_call(x1, target, ln2_g, lnf_g, w_g, w_u, w_d, tm):
    s = x1.shape[0]

    def body(x1_ref, t_ref, g2_ref, gf_ref, wg_ref, wu_ref, wd_ref,
             h2_ref, g_ref, u_ref, f_ref, dx2_ref, loss_ref, dgf_ref):
        @pl.when(pl.program_id(0) == 0)
        def _():
            loss_ref[...] = jnp.zeros_like(loss_ref)
            dgf_ref[...] = jnp.zeros_like(dgf_ref)

        x1v = x1_ref[...]
        r1 = lax.rsqrt(jnp.mean(x1v * x1v, axis=-1, keepdims=True) + EPS)
        h2 = ((x1v * r1) * g2_ref[...]).astype(h2_ref.dtype)
        h2_ref[...] = h2
        g = _dot_nt(h2, wg_ref[...])
        u = _dot_nt(h2, wu_ref[...])
        g_ref[...] = g
        u_ref[...] = u
        f = ((g * _sigmoid(g)) * u).astype(f_ref.dtype)
        f_ref[...] = f
        x2 = x1v + _dot(f, wd_ref[...])
        r2 = lax.rsqrt(jnp.mean(x2 * x2, axis=-1, keepdims=True) + EPS)
        xhat = x2 * r2
        err = xhat * gf_ref[...] - t_ref[...]
        loss_ref[...] += 0.5 * jnp.sum(jnp.mean(err * err, axis=-1, keepdims=True))
        dy = err * (1.0 / D)
        dgf_ref[...] += jnp.sum(dy * xhat, axis=0, keepdims=True)
        dxh = dy * gf_ref[...]
        dx2_ref[...] = r2 * (dxh - xhat * jnp.mean(dxh * xhat, axis=-1, keepdims=True))

    return _call(body, name="ffn_fwd", grid=(s // tm,),
                 in_specs=[_rows(tm, D), _rows(tm, D), _acc(D), _acc(D), _const((D_FF, D)), _const((D_FF, D)),
                           _const((D_FF, D))],
                 out_specs=[_rows(tm, D), _rows(tm, D_FF), _rows(tm, D_FF), _rows(tm, D_FF), _rows(tm, D), _acc(128),
                            _acc(D)],
                 out_shape=[_sds((s, D), _MXU), _sds((s, D_FF), F32), _sds((s, D_FF), F32), _sds((s, D_FF), _MXU),
                            _sds((s, D), F32), _sds((1, 128), F32), _sds((1, D), F32)],
                 args=(x1, target, ln2_g, lnf_g, w_g, w_u, w_d), big=True)


def _ffn_bwd_call(dx2, x1, g, u, ln2_g, w_g, w_u, w_d, tm):
    s = dx2.shape[0]

    def body(dx2_ref, x1_ref, g_ref, u_ref, g2_ref, wg_ref, wu_ref, wd_ref, dx1_ref, dg_ref, du_ref, dg2_ref):
        @pl.when(pl.program_id(0) == 0)
        def _():
            dg2_ref[...] = jnp.zeros_like(dg2_ref)

        dx2v = dx2_ref[...]
        df = _dot_nt(dx2v, wd_ref[...])
        gv, uv = g_ref[...], u_ref[...]
        sg = _sigmoid(gv)
        du = (df * (gv * sg)).astype(du_ref.dtype)
        dg = (df * uv * (sg * (1.0 + gv * (1.0 - sg)))).astype(dg_ref.dtype)
        du_ref[...] = du
        dg_ref[...] = dg
        dh2 = _dot(dg, wg_ref[...]) + _dot(du, wu_ref[...])
        x1v = x1_ref[...]
        r1 = lax.rsqrt(jnp.mean(x1v * x1v, axis=-1, keepdims=True) + EPS)
        xhat = x1v * r1
        dg2_ref[...] += jnp.sum(dh2 * xhat, axis=0, keepdims=True)
        dxh = dh2 * g2_ref[...]
        dx1_ref[...] = dx2v + r1 * (dxh - xhat * jnp.mean(dxh * xhat, axis=-1, keepdims=True))

    return _call(body, name="ffn_bwd", grid=(s // tm,),
                 in_specs=[_rows(tm, D), _rows(tm, D), _rows(tm, D_FF), _rows(tm, D_FF), _acc(D),
                           _const((D_FF, D)), _const((D_FF, D)), _const((D_FF, D))],
                 out_specs=[_rows(tm, D), _rows(tm, D_FF), _rows(tm, D_FF), _acc(D)],
                 out_shape=[_sds((s, D), F32), _sds((s, D_FF), _MXU), _sds((s, D_FF), _MXU), _sds((1, D), F32)],
                 args=(dx2, x1, g, u, ln2_g, w_g, w_u, w_d), big=True)


def _dx_call(x, dx1, dp_ret, dp_gate, dp_attn, ln1_g, w_in, tm, exchange):
    s = x.shape[0]

    def body(x_ref, dx1_ref, dr_ref, dg_ref, da_ref, g1_ref, w_ref, dx_ref, dg1_ref):
        @pl.when(pl.program_id(0) == 0)
        def _():
            dg1_ref[...] = jnp.zeros_like(dg1_ref)

        dh = (_dot(dr_ref[...], w_ref[0:2048, :]) + _dot(dg_ref[:, 0:1024], w_ref[2048:3072, :])
              + _dot(da_ref[...], w_ref[3072:4352, :]) + _dot(dg_ref[:, 1024:3072], w_ref[4352:6400, :]))
        xv = x_ref[...]
        r = lax.rsqrt(jnp.mean(xv * xv, axis=-1, keepdims=True) + EPS)
        xhat = xv * r
        dg1_ref[...] += jnp.sum(dh * xhat, axis=0, keepdims=True)
        dxh = dh * g1_ref[...]
        dx_ref[...] = dx1_ref[...] + r * (dxh - xhat * jnp.mean(dxh * xhat, axis=-1, keepdims=True))

    return _call(body, name="dx", grid=(s // tm,),
                 in_specs=[_rows(tm, D), _rows(tm, D), _rows(tm, N_RET), _rows(tm, N_GATE), _rows(tm, N_ATTN), _acc(D),
                           _const((D_IN, D))],
                 out_specs=[_rows(tm, D), _acc(D)],
                 out_shape=[_sds((s, D), F32), _sds((1, D), F32)],
                 args=(x, dx1, dp_ret, dp_gate, dp_attn, ln1_g, w_in), big=True, exchange=exchange)


def _adamw(g, w, m, v):
    m_new = B1 * m + (1.0 - B1) * g
    v_new = B2 * v + (1.0 - B2) * (g * g)
    m_hat = m_new / (1.0 - B1 ** STEP)
    v_hat = v_new / (1.0 - B2 ** STEP)
    return -LR * (m_hat / (jnp.sqrt(v_hat) + ADAM_EPS) + WD * w), m_new, v_new


def _slot_sum(p_ref):
    g = p_ref[0].astype(F32)
    for k in range(1, N_DEV):
        g = g + p_ref[k].astype(F32)
    return g


def _adamw_call(parts, ws, ms, vs, name, tr):
    n = len(ws)
    rows, cols = ws[0].shape

    def body(*refs):
        p_refs, w_refs, m_refs, v_refs = (refs[k * n:(k + 1) * n] for k in range(4))
        outs = refs[4 * n:]
        for i in range(n):
            g = _slot_sum(p_refs[i])
            outs[4 * i][...] = g
            outs[4 * i + 1][...], outs[4 * i + 2][...], outs[4 * i + 3][...] = _adamw(
                g, w_refs[i][...], m_refs[i][...], v_refs[i][...])

    p_spec = pl.BlockSpec((N_DEV, tr, cols), lambda i: (0, i, 0))
    spec = pl.BlockSpec((tr, cols), lambda i: (i, 0))
    res = _call(body, name=name, grid=(rows // tr,), in_specs=[p_spec] * n + [spec] * (3 * n), out_specs=[spec] * (4 * n),
                out_shape=[_sds((rows, cols), F32)] * (4 * n), args=(*parts, *ws, *ms, *vs), big=True)
    return [res[4 * i:4 * i + 4] for i in range(n)]


SMALL_WIDTHS = [1024, 6400, 1024, 16, 1024, 1024]
SMALL_OFFSETS = [0, 1024, 7424, 8448, 8576, 9600]
LOSS_OFFSET = 10624
SMALL_LEN = 10752


def _pack_small(grads, loss):
    pieces = []
    for gr, width in zip(grads, SMALL_WIDTHS):
        pieces.append(jnp.pad(gr.reshape(1, width), ((0, 0), (0, -width % 128))))
    pieces.append(jnp.pad(loss.reshape(1, 1), ((0, 0), (0, 127))))
    return jnp.concatenate(pieces, axis=1)


def _adamw_small_call(parts, ws, ms, vs):
    n = len(ws)

    def body(*refs):
        p_ref, w_refs, m_refs, v_refs = refs[0], refs[1:1 + n], refs[1 + n:1 + 2 * n], refs[1 + 2 * n:1 + 3 * n]
        outs = refs[1 + 3 * n:]
        g_all = _slot_sum(p_ref)
        for i, (off, width) in enumerate(zip(SMALL_OFFSETS, SMALL_WIDTHS)):
            g = g_all[:, off:off + width]
            outs[i][...] = g
            outs[n + i][...], outs[2 * n + i][...], outs[3 * n + i][...] = _adamw(
                g, w_refs[i][...], m_refs[i][...], v_refs[i][...])
        outs[4 * n][...] = g_all[:, LOSS_OFFSET:LOSS_OFFSET + 128]

    whole = lambda shape: pl.BlockSpec(shape, lambda i: (0,) * len(shape))
    small = [whole((1, w)) for w in SMALL_WIDTHS]
    res = _call(body, name="adamw_small", grid=(1,), in_specs=[whole((N_DEV, 1, SMALL_LEN))] + small * 3,
                out_specs=small * 4 + [whole((1, 128))],
                out_shape=[_sds((1, w), F32) for w in SMALL_WIDTHS] * 4 + [_sds((1, 128), F32)],
                args=(parts, *ws, *ms, *vs))
    return [res[k * n:(k + 1) * n] for k in range(4)], res[4 * n]


def kernel(x, ln1_g, w_in, b_in, ret_norm_g, w_ret_out, attn_sinks, w_attn_out, w_out, ln2_g, w_ffn_gate, w_ffn_up, w_ffn_down, lnf_g, loss_target, m_ln1_g, m_w_in, m_b_in, m_ret_norm_g, m_w_ret_out, m_attn_sinks, m_w_attn_out, m_w_out, m_ln2_g, m_w_ffn_gate, m_w_ffn_up, m_w_ffn_down, m_lnf_g, v_ln1_g, v_w_in, v_b_in, v_ret_norm_g, v_w_ret_out, v_attn_sinks, v_w_attn_out, v_w_out, v_ln2_g, v_w_ffn_gate, v_w_ffn_up, v_w_ffn_down, v_lnf_g):
    cast = lambda a: a.astype(_MXU)
    xs, target = x[0], loss_target[0]
    s = xs.shape[0]
    r_sq = w_ret_out.shape[1]
    r_dn = w_ffn_down.shape[1]
    c_in = w_in.shape[2]
    c_ff = w_ffn_gate.shape[2]
    tm, tm_wide, tk = min(256, s), min(512, s), min(2048, s)
    lnf_row = lnf_g.reshape(1, D)
    cos_t, sin_t = _rope_tables(s)
    decays = _retention_decays()
    tr_shard = lambda a: a[0].T
    per_dev = lambda a, n: a.reshape(N_DEV, n, D)

    (h,), (all_in,) = _ln_call(xs, ln1_g, tm, _AllGather([cast(tr_shard(w_in))]))
    wt_in = all_in.reshape(N_DEV * c_in, D)
    rest = [tr_shard(w_ffn_gate), tr_shard(w_ffn_up), w_ret_out[0], w_attn_out[0], w_out[0], w_ffn_down[0]]
    (proj,), gathered = _proj_call(h, wt_in, b_in, _AllGather([cast(a) for a in rest]))
    wt_g, wt_u, full_ro, full_ao, full_o, full_d = (a.reshape(N_DEV * a.shape[1], D) for a in gathered)
    ry, qr, kr, states = _ret_fwd_call(proj, cos_t, sin_t, decays)
    ay, aqr, akr, avb = _attn_fwd_call(proj, attn_sinks, cos_t, sin_t)
    a_in, br_a, br_b, merged, x1 = _merge_fwd_call(xs, ry, proj, ay, ret_norm_g, full_ro, full_ao, full_o, tm_wide)
    h2, g, u, f, dx2, loss, d_lnf = _ffn_fwd_call(x1, target, ln2_g, lnf_row, wt_g, wt_u, full_d, tm)

    dx1, dg, du, d_ln2 = _ffn_bwd_call(dx2, x1, g, u, ln2_g, wt_g, wt_u, full_d, tm)
    dw_d = _mm_tn(f, dx2, "dw_ffn_down", 1408, 1024, tk)
    dwt_g = _mm_tn(dg, h2, "dw_ffn_gate", 1408, 1024, tk)
    dwt_u = _mm_tn(du, h2, "dw_ffn_up", 1408, 1024, tk)
    d_a, d_b, dp_gate, day, dry, db_gate, d_gn = _merge_bwd_call(
        dx1, br_a, br_b, ry, proj, ret_norm_g, full_ro, full_ao, full_o, tm_wide)
    dw_o = _mm_tn(merged, dx1, "dw_out", 1024, 1024, tk)
    dw_ro = _mm_tn(a_in, d_a, "dw_ret_out", 1024, 1024, tk)
    dw_ao = _mm_tn(ay, d_b, "dw_attn_out", 1024, 1024, tk)
    (dp_attn, d_sinks, db_attn), (got_g, got_u, got_d) = _attn_bwd_call(
        aqr, akr, avb, day, attn_sinks, cos_t, sin_t,
        _AllToAll([per_dev(dwt_g, c_ff), per_dev(dwt_u, c_ff), per_dev(dw_d, r_dn)]))
    (dp_ret, db_ret), (got_ro, got_ao, got_o) = _ret_bwd_call(
        qr, kr, proj, states, dry, cos_t, sin_t, decays,
        _AllToAll([per_dev(dw_ro, r_sq), per_dev(dw_ao, r_sq), per_dev(dw_o, r_sq)]))
    tk_in = min(4096, s)
    dwt_ret = _mm_tn(dp_ret, h, "dw_in_ret", 1024, 1024, tk_in)
    dwt_gate = _mm_tn(dp_gate, h, "dw_in_gate", 1024, 1024, tk_in)
    dwt_attn = _mm_tn(dp_attn, h, "dw_in_attn", 1280, 1024, tk_in)
    in_pieces = [(0, 0, 2048), (1, 0, 1024), (2, 0, 1280), (1, 1024, 3072)]
    (dx, d_ln1), (got_in,) = _dx_call(xs, dx1, dp_ret, dp_gate, dp_attn, ln1_g, wt_in, tm_wide,
                                      _RowScatter([dwt_ret, dwt_gate, dwt_attn], in_pieces, c_in))
    db_in = jnp.concatenate([db_ret, db_gate[:, 0:1024], db_attn, db_gate[:, 1024:3072]], axis=1)
    small = [d_ln1, db_in, d_gn, d_sinks[:, 0:Q_HEADS], d_ln2, d_lnf]
    (got_small,) = _exchange_call(_AllGather([_pack_small(small, loss[0, 0])]), "gather_small")

    transposed = ("w_in", "w_ffn_gate", "w_ffn_up")
    res = {}
    (res["w_in"],) = _adamw_call([got_in], [tr_shard(w_in)], [tr_shard(m_w_in)], [tr_shard(v_w_in)], "adamw_w_in", 160)
    res["w_ffn_gate"], res["w_ffn_up"], res["w_ffn_down"] = _adamw_call(
        [got_g, got_u, got_d], [tr_shard(w_ffn_gate), tr_shard(w_ffn_up), w_ffn_down[0]],
        [tr_shard(m_w_ffn_gate), tr_shard(m_w_ffn_up), m_w_ffn_down[0]],
        [tr_shard(v_w_ffn_gate), tr_shard(v_w_ffn_up), v_w_ffn_down[0]], "adamw_ffn", 176)
    res["w_ret_out"], res["w_attn_out"], res["w_out"] = _adamw_call(
        [got_ro, got_ao, got_o], [w_ret_out[0], w_attn_out[0], w_out[0]],
        [m_w_ret_out[0], m_w_attn_out[0], m_w_out[0]], [v_w_ret_out[0], v_w_attn_out[0], v_w_out[0]], "adamw_square", r_sq)
    small_names = ["ln1_g", "b_in", "ret_norm_g", "attn_sinks", "ln2_g", "lnf_g"]
    small_res, loss_row = _adamw_small_call(
        got_small, [ln1_g, b_in, ret_norm_g, attn_sinks, ln2_g, lnf_row],
        [m_ln1_g, m_b_in, m_ret_norm_g, m_attn_sinks, m_ln2_g, m_lnf_g.reshape(1, D)],
        [v_ln1_g, v_b_in, v_ret_norm_g, v_attn_sinks, v_ln2_g, v_lnf_g.reshape(1, D)])
    for i, nm in enumerate(small_names):
        res[nm] = [small_res[kind][i] for kind in range(4)]

    order = ["ln1_g", "w_in", "b_in", "ret_norm_g", "w_ret_out", "attn_sinks", "w_attn_out", "w_out", "ln2_g",
             "w_ffn_gate", "w_ffn_up", "w_ffn_down", "lnf_g"]
    outs = [loss_row[0, 0], dx[None]]
    for kind in range(4):
        for nm in order:
            val = res[nm][kind]
            if nm in transposed:
                val = val.T
            outs.append(val[None] if nm.startswith("w_") else val.reshape(D) if nm == "lnf_g" else val)
    return tuple(outs)
```

```python
import math

import numpy as np
import jax
import jax.numpy as jnp
from jax import lax
from jax.experimental import pallas as pl
from jax.experimental.pallas import tpu as pltpu

F32 = jnp.float32
_MXU = jnp.bfloat16
_STORE = jnp.bfloat16

N_DEV = 8
D = 1024
RET_HEADS, RET_DK, RET_DV = 4, 128, 256
BLK = 128
Q_HEADS, KV_HEADS, HEAD_DIM = 16, 2, 64
GROUP = Q_HEADS // KV_HEADS
D_FF = 2816
N_RET, N_GATE, N_ATTN = 2048, 3072, 1280
D_IN = N_RET + N_GATE + N_ATTN
ROPE_THETA = 10000.0
EPS = 1e-6
RET_SCALE = RET_DK ** -0.5
ATTN_SCALE = HEAD_DIM ** -0.5
LR, B1, B2, ADAM_EPS, WD, STEP = 0.001, 0.9, 0.999, 1e-08, 0.01, 10
VMEM_LIMIT_MB = 56
MESH = pl.DeviceIdType.MESH


def _dot(a, b):
    return jnp.dot(a.astype(_MXU), b.astype(_MXU), preferred_element_type=F32)


def _dot_nt(a, b):
    return lax.dot_general(a.astype(_MXU), b.astype(_MXU), (((1,), (1,)), ((), ())), preferred_element_type=F32)


def _dot_tn(a, b):
    return lax.dot_general(a.astype(_MXU), b.astype(_MXU), (((0,), (0,)), ((), ())), preferred_element_type=F32)


def _sigmoid(x):
    return 1.0 / (1.0 + jnp.exp(-x))


def _cparams(n_axes, big=False):
    kw = dict(dimension_semantics=("arbitrary",) * n_axes)
    if big:
        kw["vmem_limit_bytes"] = VMEM_LIMIT_MB * 2**20
    return pltpu.CompilerParams(**kw)


def _rows(tm, width, col=0):
    return pl.BlockSpec((tm, width), lambda i: (i, col))


def _const(shape):
    nd = len(shape)
    return pl.BlockSpec(shape, lambda *_: (0,) * nd, pipeline_mode=pl.Buffered(1))


def _acc(width):
    return pl.BlockSpec((1, width), lambda *_: (0, 0))


def _sds(shape, dtype):
    return jax.ShapeDtypeStruct(shape, dtype)


def _swap_halves(x, half):
    w = x.shape[-1]
    if 2 * half == w:
        return pltpu.roll(x, half, 1)
    lane = lax.broadcasted_iota(jnp.int32, x.shape, 1)
    return jnp.where(lane % (2 * half) < half, pltpu.roll(x, w - half, 1), pltpu.roll(x, half, 1))


def _rope_tables(seq):
    lane = jnp.arange(128, dtype=jnp.int32)
    ret_freq = ROPE_THETA ** (-(lane % 64).astype(F32) / 64)
    attn_freq = ROPE_THETA ** (-(lane % 32).astype(F32) / 32)
    ang = jnp.arange(seq, dtype=jnp.int32).astype(F32)[:, None] * jnp.where(lane < 64, ret_freq, attn_freq)[None, :]
    return jnp.cos(ang), jnp.sin(ang)


def _ret_rope(cos, sin):
    low = lax.broadcasted_iota(jnp.int32, cos.shape, 1) < RET_DK // 2
    return jnp.where(low, cos, pltpu.roll(cos, RET_DK // 2, 1)), jnp.where(low, -sin, pltpu.roll(sin, RET_DK // 2, 1))


def _attn_rope(cos, sin):
    lane = lax.broadcasted_iota(jnp.int32, cos.shape, 1)
    half = HEAD_DIM // 2

    def spread(t):
        t = pltpu.roll(t, 64, 1)
        t = jnp.where(lane < half, t, pltpu.roll(t, half, 1))
        return jnp.where(lane < HEAD_DIM, t, pltpu.roll(t, HEAD_DIM, 1))

    return spread(cos), jnp.where(lane % HEAD_DIM < half, -spread(sin), spread(sin))


def _retention_decays():
    log_gamma = np.log1p(-np.exp2(-5.0 - np.arange(RET_HEADS, dtype=np.float32))).astype(np.float32)
    idx = np.arange(BLK, dtype=np.float32)
    rel = idx[:, None] - idx[None, :]
    intra = np.where(rel[None] >= 0, np.exp(log_gamma[:, None, None] * np.maximum(rel, 0.0)[None]), 0.0)
    q_decay = np.exp(log_gamma[:, None] * (idx + 1.0))[:, :, None]
    k_decay = np.exp(log_gamma[:, None] * (BLK - 1.0 - idx))[:, :, None]
    chunk_decay = [float(np.exp(np.float32(lg * BLK))) for lg in log_gamma]
    return (jnp.asarray(intra, F32), jnp.asarray(q_decay, F32), jnp.asarray(k_decay, F32), chunk_decay)


def _position():
    return lax.axis_index("x"), lax.axis_index("y"), lax.axis_index("c")


def _slot(px, py, pc):
    return 4 * px + 2 * py + pc


class _AllGather:
    def __init__(self, blocks):
        self.blocks = list(blocks)
        nb = len(self.blocks)
        self.out_shape = [_sds((N_DEV,) + b.shape, b.dtype) for b in self.blocks]
        self.scratch = [pltpu.SemaphoreType.DMA((nb, 7)), pltpu.SemaphoreType.DMA((nb, 7)),
                        pltpu.SemaphoreType.DMA((nb,))]

    def phases(self, ins, outs, send_sems, recv_sems, local_sems):
        nb = len(ins)
        x, y, c = _position()
        me, sibling = (x, y, c), (x, y, 1 - c)
        chips = [(1 - x, y), (x, 1 - y), (1 - x, 1 - y)]

        def copy(b, k, block, to, src=None):
            dst = outs[b].at[_slot(*block)]
            return pltpu.make_async_remote_copy(
                src_ref=dst if src is None else src, dst_ref=dst, send_sem=send_sems.at[b, k],
                recv_sem=recv_sems.at[b, k], device_id=to, device_id_type=MESH)

        def own(b):
            return pltpu.make_async_copy(ins[b], outs[b].at[_slot(*me)], local_sems.at[b])

        def first(b):
            return [copy(b, 0, me, sibling, src=ins[b])] + [
                copy(b, 1 + j, me, (*chip, c), src=ins[b]) for j, chip in enumerate(chips)]

        def start():
            for b in range(nb):
                own(b).start()
                for cp in first(b):
                    cp.start()

        def forward():
            for b in range(nb):
                for j, chip in enumerate(chips):
                    copy(b, 1 + j, (*chip, c), me).wait_recv()
                    copy(b, 4 + j, (*chip, c), sibling).start()

        def finish():
            for b in range(nb):
                copy(b, 0, sibling, me).wait_recv()
                for j, chip in enumerate(chips):
                    copy(b, 4 + j, (*chip, 1 - c), me).wait_recv()
            for b in range(nb):
                for cp in first(b):
                    cp.wait_send()
                for j, chip in enumerate(chips):
                    copy(b, 4 + j, (*chip, c), sibling).wait_send()
                own(b).wait()

        return start, forward, finish


class _AllToAll:
    def __init__(self, blocks):
        self.blocks = list(blocks)
        nb = len(self.blocks)
        self.out_shape = [_sds(b.shape, b.dtype) for b in self.blocks]
        self.scratch = [pltpu.SemaphoreType.DMA((nb, 7)), pltpu.SemaphoreType.DMA((nb, 7)),
                        pltpu.SemaphoreType.DMA((nb,))]

    def phases(self, ins, outs, send_sems, recv_sems, local_sems):
        nb = len(ins)
        x, y, c = _position()
        flip = lambda v, bit: 1 - v if bit else v
        peers = [(flip(x, k >> 2 & 1), flip(y, k >> 1 & 1), flip(c, k & 1)) for k in range(1, N_DEV)]

        def copy(b, k, peer, landed=False):
            return pltpu.make_async_remote_copy(
                src_ref=ins[b].at[_slot(*peer)], dst_ref=outs[b].at[_slot(*peer) if landed else _slot(x, y, c)],
                send_sem=send_sems.at[b, k], recv_sem=recv_sems.at[b, k], device_id=peer, device_id_type=MESH)

        def own(b):
            return pltpu.make_async_copy(ins[b].at[_slot(x, y, c)], outs[b].at[_slot(x, y, c)], local_sems.at[b])

        def start():
            for b in range(nb):
                own(b).start()
                for k, peer in enumerate(peers):
                    copy(b, k, peer).start()

        def forward():
            pass

        def finish():
            for b in range(nb):
                for k, peer in enumerate(peers):
                    copy(b, k, peer, landed=True).wait_recv()
            for b in range(nb):
                for k, peer in enumerate(peers):
                    copy(b, k, peer).wait_send()
                own(b).wait()

        return start, forward, finish


class _RowScatter:
    def __init__(self, arrays, pieces, n):
        self.blocks = list(arrays)
        self.pieces, self.n = pieces, n
        self.out_shape = [_sds((N_DEV, n, D), arrays[0].dtype)]
        self.scratch = [pltpu.SemaphoreType.DMA((N_DEV,)), pltpu.SemaphoreType.DMA((N_DEV,)), pltpu.SemaphoreType.DMA]

    def _parts(self, k):
        lo, hi, pos, res = k * self.n, (k + 1) * self.n, 0, []
        for arr, first, last in self.pieces:
            a, b = max(lo, pos), min(hi, pos + last - first)
            if a < b:
                res.append((arr, first + a - pos, b - a, a - lo))
            pos += last - first
        return res

    def phases(self, ins, outs, send_sems, recv_sems, local_sem):
        (out,) = outs
        x, y, c = _position()
        me = _slot(x, y, c)

        def start():
            for k in range(N_DEV):
                dist = jnp.bitwise_xor(me, k)

                @pl.when(me != k)
                def _():
                    for arr, first, rows, at in self._parts(k):
                        pltpu.make_async_remote_copy(
                            src_ref=ins[arr].at[pl.ds(first, rows)], dst_ref=out.at[me, pl.ds(at, rows)],
                            send_sem=send_sems.at[dist], recv_sem=recv_sems.at[dist],
                            device_id=(k >> 2 & 1, k >> 1 & 1, k & 1), device_id_type=MESH).start()

                @pl.when(me == k)
                def _():
                    for arr, first, rows, at in self._parts(k):
                        pltpu.make_async_copy(ins[arr].at[pl.ds(first, rows)], out.at[me, pl.ds(at, rows)], local_sem).start()

        def forward():
            pass

        def whole_block(dist):
            return pltpu.make_async_remote_copy(
                src_ref=out.at[me], dst_ref=out.at[jnp.bitwise_xor(me, dist)], send_sem=send_sems.at[dist],
                recv_sem=recv_sems.at[dist], device_id=(x, y, c), device_id_type=MESH)

        def finish():
            for dist in range(1, N_DEV):
                whole_block(dist).wait_recv()
            for dist in range(1, N_DEV):
                whole_block(dist).wait_send()
            pltpu.make_async_copy(out.at[me], out.at[me], local_sem).wait()

        return start, forward, finish


def _call(body, *, name, grid, in_specs, out_specs, out_shape, args, scratch_shapes=(), big=False, exchange=None):
    params = _cparams(len(grid), big)
    if exchange is None:
        return pl.pallas_call(body, name=name, grid=grid, in_specs=in_specs, out_specs=out_specs, out_shape=out_shape,
                              scratch_shapes=list(scratch_shapes), compiler_params=params)(*args)
    n_in, n_out, n_scr = len(in_specs), len(out_specs), len(scratch_shapes)
    nb, nb_out = len(exchange.blocks), len(exchange.out_shape)
    steps = math.prod(grid)

    def carried(*refs):
        pos = 0
        parts = []
        for n in (n_in, nb, n_out, nb_out, n_scr, len(exchange.scratch)):
            parts.append(refs[pos:pos + n])
            pos += n
        ins, x_ins, outs, x_outs, scr, sems = parts
        step = pl.program_id(0)
        for axis in range(1, len(grid)):
            step = step * grid[axis] + pl.program_id(axis)
        start, forward, finish = exchange.phases(x_ins, x_outs, *sems)
        pl.when(step == 0)(start)
        body(*ins, *outs, *scr)

        @pl.when(step == steps - 1)
        def _():
            forward()
            finish()

    any_spec = pl.BlockSpec(memory_space=pl.ANY)
    res = pl.pallas_call(
        carried, name=name, grid=grid, in_specs=list(in_specs) + [any_spec] * nb,
        out_specs=list(out_specs) + [any_spec] * nb_out, out_shape=list(out_shape) + exchange.out_shape,
        scratch_shapes=list(scratch_shapes) + exchange.scratch, compiler_params=params)(*args, *exchange.blocks)
    return res[:n_out], res[n_out:]


def _exchange_call(exchange, name):
    nb = len(exchange.blocks)

    def body(*refs):
        start, forward, finish = exchange.phases(refs[:nb], refs[nb:2 * nb], *refs[2 * nb:])
        start()
        forward()
        finish()

    any_spec = pl.BlockSpec(memory_space=pl.ANY)
    return pl.pallas_call(body, name=name, in_specs=[any_spec] * nb, out_specs=[any_spec] * nb,
                          out_shape=exchange.out_shape, scratch_shapes=exchange.scratch)(*exchange.blocks)


def _ln_call(x, g, tm, exchange):
    s = x.shape[0]

    def body(x_ref, g_ref, h_ref):
        xv = x_ref[...]
        r = lax.rsqrt(jnp.mean(xv * xv, axis=-1, keepdims=True) + EPS)
        h_ref[...] = ((xv * r) * g_ref[...]).astype(h_ref.dtype)

    return _call(body, name="ln1", grid=(s // tm,), in_specs=[_rows(tm, D), _acc(D)], out_specs=[_rows(tm, D)],
                 out_shape=[_sds((s, D), _MXU)], args=(x, g), exchange=exchange)


PROJ_TILE = 256


def _proj_source_tile(j):
    gate_end, attn_end, end = 3072 // PROJ_TILE, 4352 // PROJ_TILE, 6400 // PROJ_TILE
    n_gates = end - attn_end
    return jnp.where(j < gate_end, j, jnp.where(j < gate_end + n_gates, j + (attn_end - gate_end), j - n_gates))


def _proj_call(a, wt, bias, exchange):
    s, k = a.shape
    n = wt.shape[0]
    rows = min(1024, s)

    def body(a_ref, w_ref, b_ref, o_ref):
        for r in range(0, s, rows):
            o_ref[r:r + rows, :] = _dot_nt(a_ref[r:r + rows, :], w_ref[...]) + b_ref[...]

    return _call(body, name="proj", grid=(n // PROJ_TILE,),
                 in_specs=[_const((s, k)), pl.BlockSpec((PROJ_TILE, k), lambda j: (_proj_source_tile(j), 0)),
                           pl.BlockSpec((1, PROJ_TILE), lambda j: (0, _proj_source_tile(j)))],
                 out_specs=[pl.BlockSpec((s, PROJ_TILE), lambda j: (0, j))], out_shape=[_sds((s, n), F32)],
                 args=(a, wt, bias), big=True, exchange=exchange)


def _mm_tn(a, b, name, tm, tn, tk, exchange=None):
    s, m = a.shape
    n = b.shape[1]
    last = s // tk - 1

    def body(a_ref, b_ref, o_ref, acc):
        k = pl.program_id(2)
        part = _dot_tn(a_ref[...], b_ref[...])

        @pl.when(k == 0)
        def _():
            acc[...] = part

        @pl.when(k > 0)
        def _():
            acc[...] += part

        @pl.when(k == last)
        def _():
            o_ref[...] = acc[...].astype(o_ref.dtype)

    res = _call(body, name=name, grid=(m // tm, n // tn, s // tk),
                in_specs=[pl.BlockSpec((tk, tm), lambda i, j, k: (k, i)), pl.BlockSpec((tk, tn), lambda i, j, k: (k, j))],
                out_specs=[pl.BlockSpec((tm, tn), lambda i, j, k: (i, j))], out_shape=[_sds((m, n), _MXU)],
                scratch_shapes=[pltpu.VMEM((tm, tn), F32)], args=(a, b), big=True, exchange=exchange)
    return res[0] if exchange is None else (res[0][0], res[1])


RET_CHUNKS = 4


def _ret_fwd_call(proj, cos, sin, decays):
    s = proj.shape[0]
    nblk = s // BLK
    per = min(RET_CHUNKS, nblk)
    rows = per * BLK
    intra, q_decay, k_decay, chunk_decay = decays

    def body(rq_ref, rk_ref, rv_ref, cos_ref, sin_ref, intra_ref, qd_ref, kd_ref,
             ry_ref, qr_ref, kr_ref, st_ref, state):
        @pl.when(pl.program_id(0) == 0)
        def _():
            state[...] = jnp.zeros_like(state)

        for c in range(per):
            rc = slice(c * BLK, (c + 1) * BLK)
            cos_v, sin_v = _ret_rope(cos_ref[rc, :], sin_ref[rc, :])
            for h in range(RET_HEADS):
                hk = slice(h * RET_DK, (h + 1) * RET_DK)
                hv = slice(h * RET_DV, (h + 1) * RET_DV)
                q, k = rq_ref[rc, hk], rk_ref[rc, hk]
                qr = (q * cos_v + _swap_halves(q, RET_DK // 2) * sin_v) * RET_SCALE
                kr = k * cos_v + _swap_halves(k, RET_DK // 2) * sin_v
                v = rv_ref[rc, hv]
                s_h = state[h]
                st_ref[c, h] = s_h.astype(st_ref.dtype)
                scores = _dot_nt(qr, kr) * intra_ref[h]
                ry_ref[rc, hv] = _dot(scores, v) + _dot(qr, s_h) * qd_ref[h]
                state[h] = s_h * chunk_decay[h] + _dot_tn(kr * kd_ref[h], v)
                qr_ref[rc, hk] = qr.astype(qr_ref.dtype)
                kr_ref[rc, hk] = kr.astype(kr_ref.dtype)

    blk = lambda w, c: pl.BlockSpec((rows, w), lambda n: (n, c))
    return _call(body, name="ret_fwd", grid=(nblk // per,),
                 in_specs=[blk(512, 0), blk(512, 1), blk(1024, 1), blk(128, 0), blk(128, 0),
                           _const(intra.shape), _const(q_decay.shape), _const(k_decay.shape)],
                 out_specs=[blk(1024, 0), blk(512, 0), blk(512, 0),
                            pl.BlockSpec((per, RET_HEADS, RET_DK, RET_DV), lambda n: (n, 0, 0, 0))],
                 out_shape=[_sds((s, 1024), F32), _sds((s, 512), _MXU), _sds((s, 512), _MXU),
                            _sds((nblk, RET_HEADS, RET_DK, RET_DV), _MXU)],
                 scratch_shapes=[pltpu.VMEM((RET_HEADS, RET_DK, RET_DV), F32)],
                 args=(proj, proj, proj, cos, sin, intra, q_decay, k_decay))


def _ret_bwd_call(qr, kr, proj, states, dry, cos, sin, decays, exchange):
    s = qr.shape[0]
    nblk = s // BLK
    per = min(RET_CHUNKS, nblk)
    rows = per * BLK
    steps = nblk // per
    intra, q_decay, k_decay, chunk_decay = decays

    def body(qr_ref, kr_ref, rv_ref, st_ref, dry_ref, cos_ref, sin_ref, intra_ref, qd_ref, kd_ref,
             dp_ref, db_ref, dstate):
        @pl.when(pl.program_id(0) == 0)
        def _():
            dstate[...] = jnp.zeros_like(dstate)
            db_ref[...] = jnp.zeros_like(db_ref)

        for c in reversed(range(per)):
            rc = slice(c * BLK, (c + 1) * BLK)
            cos_v, sin_v = _ret_rope(cos_ref[rc, :], sin_ref[rc, :])
            for h in range(RET_HEADS):
                hk = slice(h * RET_DK, (h + 1) * RET_DK)
                hv = slice(h * RET_DV, (h + 1) * RET_DV)
                q, k, v, d_out = qr_ref[rc, hk], kr_ref[rc, hk], rv_ref[rc, hv], dry_ref[rc, hv]
                d_next = dstate[h]
                scores = _dot_nt(q, k) * intra_ref[h]
                d_scores = _dot_nt(d_out, v) * intra_ref[h]
                d_cross = d_out * qd_ref[h]
                dq = _dot(d_scores, k) + _dot_nt(d_cross, st_ref[c, h])
                dk = _dot_tn(d_scores, q) + _dot_nt(v, d_next) * kd_ref[h]
                dv = _dot_tn(scores, d_out) + _dot(k.astype(F32) * kd_ref[h], d_next)
                dstate[h] = d_next * chunk_decay[h] + _dot_tn(q, d_cross)
                dq = (dq * cos_v - _swap_halves(dq, RET_DK // 2) * sin_v) * RET_SCALE
                dk = dk * cos_v - _swap_halves(dk, RET_DK // 2) * sin_v
                kcols = slice(512 + h * RET_DK, 512 + (h + 1) * RET_DK)
                vcols = slice(1024 + h * RET_DV, 1024 + (h + 1) * RET_DV)
                dp_ref[rc, hk] = dq.astype(dp_ref.dtype)
                dp_ref[rc, kcols] = dk.astype(dp_ref.dtype)
                dp_ref[rc, vcols] = dv.astype(dp_ref.dtype)
                db_ref[:, hk] += jnp.sum(dq, axis=0, keepdims=True)
                db_ref[:, kcols] += jnp.sum(dk, axis=0, keepdims=True)
                db_ref[:, vcols] += jnp.sum(dv, axis=0, keepdims=True)

    rblk = lambda w, c: pl.BlockSpec((rows, w), lambda n: (steps - 1 - n, c))
    return _call(body, name="ret_bwd", grid=(steps,),
                 in_specs=[rblk(512, 0), rblk(512, 0), rblk(1024, 1),
                           pl.BlockSpec((per, RET_HEADS, RET_DK, RET_DV), lambda n: (steps - 1 - n, 0, 0, 0)),
                           rblk(1024, 0), rblk(128, 0), rblk(128, 0),
                           _const(intra.shape), _const(q_decay.shape), _const(k_decay.shape)],
                 out_specs=[rblk(N_RET, 0), _acc(N_RET)],
                 out_shape=[_sds((s, N_RET), _MXU), _sds((1, N_RET), F32)],
                 scratch_shapes=[pltpu.VMEM((RET_HEADS, RET_DK, RET_DV), F32)],
                 args=(qr, kr, proj, states, dry, cos, sin, intra, q_decay, k_decay), exchange=exchange)


def _both_halves(x, g):
    lane = lax.broadcasted_iota(jnp.int32, x.shape, 1)
    keep = lane < HEAD_DIM if g == 0 else lane >= HEAD_DIM
    return jnp.where(keep, x, pltpu.roll(x, HEAD_DIM, 1))


def _stack_heads(ref, g, rows=slice(None)):
    lane = lax.broadcasted_iota(jnp.int32, (BLK, 128), 1)
    pieces = []
    for j in range(g * 4, g * 4 + 4):
        chunk = ref[rows, j * 128:(j + 1) * 128]
        pieces += [jnp.where(lane < HEAD_DIM, chunk, jnp.zeros_like(chunk)),
                   jnp.where(lane >= HEAD_DIM, chunk, jnp.zeros_like(chunk))]
    return jnp.concatenate(pieces, axis=0)


def _window_bias(first_block):
    kj = lax.broadcasted_iota(jnp.int32, (2 * BLK, BLK), 0)
    qi = lax.broadcasted_iota(jnp.int32, (2 * BLK, BLK), 1)
    first_key = jnp.where(first_block, BLK, 0)
    seen = (kj > qi) & (kj <= qi + BLK) & (kj >= first_key)
    return jnp.where(seen, 0.0, -1e30)


def _sink_softmax(scores, sink):
    m = jnp.maximum(jnp.max(scores, axis=0, keepdims=True), sink)
    e = jnp.exp(scores - m)
    return e, jnp.sum(e, axis=0, keepdims=True) + jnp.exp(sink - m), m


def _head_pair(stacked_t, jj):
    even = stacked_t[0:HEAD_DIM, 2 * jj * BLK:(2 * jj + 1) * BLK]
    odd = stacked_t[HEAD_DIM:128, (2 * jj + 1) * BLK:(2 * jj + 2) * BLK]
    return jnp.concatenate([even, odd], axis=0).T


ATTN_BLOCKS = 4


def _attn_fwd_call(proj, sinks, cos, sin):
    s = proj.shape[0]
    nblk = s // BLK
    per = min(ATTN_BLOCKS, nblk)
    rows = per * BLK

    def body(sink_ref, q_ref, k_ref, v_ref, cos_ref, sin_ref, ay_ref, qr_ref, kr_ref, vb_ref, lse_ref,
             kwin, vwin, bias, s_scr, p_scr):
        n = pl.program_id(0)

        @pl.when(n == 0)
        def _():
            kwin[...] = jnp.zeros_like(kwin)
            vwin[...] = jnp.zeros_like(vwin)

        for c in range(per):
            rc = slice(c * BLK, (c + 1) * BLK)
            kwin[0:BLK] = kwin[BLK:2 * BLK]
            vwin[0:BLK] = vwin[BLK:2 * BLK]
            cos_v, sin_v = _attn_rope(cos_ref[rc, :], sin_ref[rc, :])
            k = k_ref[rc, :]
            kr = (k * cos_v + _swap_halves(k, HEAD_DIM // 2) * sin_v).astype(kwin.dtype)
            kwin[BLK:2 * BLK] = kr
            vwin[BLK:2 * BLK] = v_ref[rc, :].astype(vwin.dtype)
            kr_ref[rc, :] = kr
            vb_ref[rc, :] = vwin[BLK:2 * BLK]
            for j in range(Q_HEADS // 2):
                cols = slice(j * 128, (j + 1) * 128)
                q = q_ref[rc, cols]
                qr_ref[rc, cols] = ((q * cos_v + _swap_halves(q, HEAD_DIM // 2) * sin_v) * ATTN_SCALE).astype(qr_ref.dtype)
            bias[...] = _window_bias(n == 0) if c == 0 else _window_bias(False)
            for g in range(KV_HEADS):
                kg = _both_halves(kwin[...], g)
                vg_t = _both_halves(vwin[...], g).astype(F32).T
                s_scr[...] = _dot_nt(kg, _stack_heads(qr_ref, g, rc))
                for i in range(GROUP):
                    cols = slice(i * BLK, (i + 1) * BLK)
                    e, den, m = _sink_softmax(s_scr[:, cols] + bias[...], sink_ref[0, g * GROUP + i])
                    p_scr[:, cols] = (e * (1.0 / den)).astype(p_scr.dtype)
                    lse_ref[c, g, :, cols] = m + jnp.log(den)
                out_t = _dot(vg_t, p_scr[...])
                for jj in range(4):
                    j = g * 4 + jj
                    ay_ref[rc, j * 128:(j + 1) * 128] = _head_pair(out_t, jj).astype(ay_ref.dtype)

    blk = lambda w, c: pl.BlockSpec((rows, w), lambda n: (n, c))
    off = (N_RET + N_GATE) // 128
    wide = (2 * BLK, GROUP * BLK)
    return _call(body, name="attn_fwd", grid=(nblk // per,),
                 in_specs=[pl.BlockSpec(memory_space=pltpu.SMEM), blk(1024, off // 8), blk(128, off + 8), blk(128, off + 9),
                           blk(128, 0), blk(128, 0)],
                 out_specs=[blk(1024, 0), blk(1024, 0), blk(128, 0), blk(128, 0),
                            pl.BlockSpec((per, KV_HEADS, 1, GROUP * BLK), lambda n: (n, 0, 0, 0))],
                 out_shape=[_sds((s, 1024), _MXU), _sds((s, 1024), _MXU), _sds((s, 128), _MXU), _sds((s, 128), _MXU),
                            _sds((nblk, KV_HEADS, 1, GROUP * BLK), F32)],
                 scratch_shapes=[pltpu.VMEM((2 * BLK, 128), _MXU), pltpu.VMEM((2 * BLK, 128), _MXU),
                                 pltpu.VMEM((2 * BLK, BLK), F32), pltpu.VMEM(wide, F32), pltpu.VMEM(wide, _MXU)],
                 args=(sinks, proj, proj, proj, cos, sin))


def _attn_bwd_call(qr, kr, vb, day, lse, sinks, cos, sin, exchange):
    s = qr.shape[0]
    nblk = s // BLK

    def body(sink_ref, q_ref, kc_ref, kp_ref, vc_ref, vp_ref, do_ref, cos_ref, sin_ref, cosp_ref, sinp_ref, lse_ref,
             dp_ref, dsink_ref, db_ref, bias, s_scr, dp_scr, p_scr, ds_scr, dq_held, kv_held, kv_prev, kv_new):
        n = pl.program_id(0)
        valid = (n < nblk).astype(F32)

        @pl.when(n == 0)
        def _():
            dsink_ref[...] = jnp.zeros_like(dsink_ref)
            db_ref[...] = jnp.zeros_like(db_ref)

        @pl.when(n >= 1)
        def _():
            dp_ref[:, 0:1024] = dq_held[...]

        cos_v, sin_v = _attn_rope(cos_ref[...], sin_ref[...])
        bias[...] = _window_bias(n == 0)
        lane1 = lax.broadcasted_iota(jnp.int32, (1, 128), 1)
        kwin = jnp.concatenate([kp_ref[...], kc_ref[...]], axis=0)
        vwin = jnp.concatenate([vp_ref[...], vc_ref[...]], axis=0)
        dk_heads, dv_heads = [], []
        dsink = jnp.zeros((1, 128), F32)
        for g in range(KV_HEADS):
            kg = _both_halves(kwin, g)
            vg = _both_halves(vwin, g)
            q_all = _stack_heads(q_ref, g)
            do_all = _stack_heads(do_ref, g)
            s_scr[...] = _dot_nt(kg, q_all)
            dp_scr[...] = _dot_nt(vg, do_all)
            for i in range(GROUP):
                head = g * GROUP + i
                cols = slice(i * BLK, (i + 1) * BLK)
                lse = lse_ref[0, g, :, cols]
                p = jnp.exp(s_scr[:, cols] + bias[...] - lse)
                p_sink = jnp.exp(sink_ref[0, head] - lse)
                dp = dp_scr[:, cols]
                delta = jnp.sum(p * dp, axis=0, keepdims=True)
                p_scr[:, cols] = p.astype(p_scr.dtype)
                ds_scr[:, cols] = (p * (dp - delta)).astype(ds_scr.dtype)
                dsink = dsink + jnp.where(lane1 == head, -jnp.sum(p_sink * delta, axis=1, keepdims=True), 0.0)
            dv_both = _dot(p_scr[...], do_all)
            dk_both = _dot(ds_scr[...], q_all)
            dv_heads.append(dv_both + pltpu.roll(dv_both, HEAD_DIM, 1))
            dk_heads.append(dk_both + pltpu.roll(dk_both, HEAD_DIM, 1))
            dq_t = _dot(kg.astype(F32).T, ds_scr[...])
            for jj in range(4):
                cols = slice((g * 4 + jj) * 128, (g * 4 + jj + 1) * 128)
                dq = _head_pair(dq_t, jj)
                dq = (dq * cos_v - _swap_halves(dq, HEAD_DIM // 2) * sin_v) * ATTN_SCALE
                dq_held[:, cols] = dq.astype(dq_held.dtype)
                db_ref[:, cols] += jnp.sum(dq, axis=0, keepdims=True) * valid
        dsink_ref[...] += dsink * valid
        lane2 = lax.broadcasted_iota(jnp.int32, (2 * BLK, 128), 1)
        dk_all = jnp.where(lane2 < HEAD_DIM, dk_heads[0], dk_heads[1])
        dv_all = jnp.where(lane2 < HEAD_DIM, dv_heads[0], dv_heads[1])
        kv_prev[:, 0:128] = dk_all[0:BLK] * valid
        kv_prev[:, 128:256] = dv_all[0:BLK] * valid
        kv_new[:, 0:128] = dk_all[BLK:2 * BLK]
        kv_new[:, 128:256] = dv_all[BLK:2 * BLK]

        @pl.when(n >= 1)
        def _():
            dkv = kv_held[...] + kv_prev[...]
            dk = dkv[:, 0:128]
            cos_p, sin_p = _attn_rope(cosp_ref[...], sinp_ref[...])
            dk = dk * cos_p - _swap_halves(dk, HEAD_DIM // 2) * sin_p
            dv = dkv[:, 128:256]
            dp_ref[:, 1024:1152] = dk.astype(dp_ref.dtype)
            dp_ref[:, 1152:1280] = dv.astype(dp_ref.dtype)
            db_ref[:, 1024:1152] += jnp.sum(dk, axis=0, keepdims=True)
            db_ref[:, 1152:1280] += jnp.sum(dv, axis=0, keepdims=True)

        kv_held[...] = kv_new[...]

    blk = lambda w: pl.BlockSpec((BLK, w), lambda n: (jnp.minimum(n, nblk - 1), 0))
    pblk = lambda w: pl.BlockSpec((BLK, w), lambda n: (jnp.maximum(n - 1, 0), 0))
    wide = (2 * BLK, GROUP * BLK)
    return _call(body, name="attn_bwd", grid=(nblk + 1,),
                 in_specs=[pl.BlockSpec(memory_space=pltpu.SMEM), blk(1024), blk(128), pblk(128), blk(128), pblk(128),
                           blk(1024), blk(128), blk(128), pblk(128), pblk(128),
                           pl.BlockSpec((1, KV_HEADS, 1, GROUP * BLK), lambda n: (jnp.minimum(n, nblk - 1), 0, 0, 0))],
                 out_specs=[pblk(N_ATTN), _acc(128), _acc(N_ATTN)],
                 out_shape=[_sds((s, N_ATTN), _MXU), _sds((1, 128), F32), _sds((1, N_ATTN), F32)],
                 scratch_shapes=[pltpu.VMEM((2 * BLK, BLK), F32), pltpu.VMEM(wide, F32), pltpu.VMEM(wide, F32),
                                 pltpu.VMEM(wide, _MXU), pltpu.VMEM(wide, _MXU), pltpu.VMEM((BLK, 1024), _MXU),
                                 pltpu.VMEM((BLK, 256), F32), pltpu.VMEM((BLK, 256), F32), pltpu.VMEM((BLK, 256), F32)],
                 args=(sinks, qr, kr, kr, vb, vb, day, cos, sin, cos, sin, lse), exchange=exchange)


def _group_norm(y):
    mu = jnp.mean(y, axis=-1, keepdims=True)
    yc = y - mu
    rs = lax.rsqrt(jnp.mean(yc * yc, axis=-1, keepdims=True) + EPS)
    return yc * rs, rs


GATE_COL = N_RET // 1024


def _merge_fwd_call(x, ry, proj, ay, gn_g, w_ro, w_ao, w_o, tm):
    s = x.shape[0]

    def body(x_ref, ry_ref, rg_ref, ga_ref, gb_ref, ay_ref, gn_ref, wro_ref, wao_ref, wo_ref,
             ain_ref, a_ref, b_ref, mg_ref, x1_ref):
        for h in range(RET_HEADS):
            hv = slice(h * RET_DV, (h + 1) * RET_DV)
            yhat, _ = _group_norm(ry_ref[:, hv])
            rg = rg_ref[:, hv]
            ain_ref[:, hv] = ((rg * _sigmoid(rg)) * (yhat * gn_ref[:, hv])).astype(ain_ref.dtype)
        a = _dot(ain_ref[...], wro_ref[...])
        b = _dot(ay_ref[...], wao_ref[...])
        a_ref[...] = a.astype(a_ref.dtype)
        b_ref[...] = b.astype(b_ref.dtype)
        merged = (_sigmoid(ga_ref[...]) * a + _sigmoid(gb_ref[...]) * b).astype(mg_ref.dtype)
        mg_ref[...] = merged
        x1_ref[...] = x_ref[...] + _dot(merged, wo_ref[...])

    return _call(body, name="merge_fwd", grid=(s // tm,),
                 in_specs=[_rows(tm, D), _rows(tm, 1024), _rows(tm, 1024, GATE_COL), _rows(tm, 1024, GATE_COL + 1),
                           _rows(tm, 1024, GATE_COL + 2), _rows(tm, 1024), _acc(1024),
                           _const((D, D)), _const((D, D)), _const((D, D))],
                 out_specs=[_rows(tm, D)] * 5,
                 out_shape=[_sds((s, D), _MXU), _sds((s, D), _STORE), _sds((s, D), _STORE), _sds((s, D), _MXU),
                            _sds((s, D), F32)],
                 args=(x, ry, proj, proj, proj, ay, gn_g, w_ro, w_ao, w_o), big=True)


def _merge_bwd_call(dx1, a, b, ry, proj, gn_g, w_ro, w_ao, w_o, tm):
    s = dx1.shape[0]

    def body(dx1_ref, a_ref, b_ref, ry_ref, rg_ref, ga_ref, gb_ref, gn_ref, wro_ref, wao_ref, wo_ref,
             da_ref, dbr_ref, dp_ref, day_ref, dry_ref, dbias_ref, dgn_ref):
        @pl.when(pl.program_id(0) == 0)
        def _():
            dbias_ref[...] = jnp.zeros_like(dbias_ref)
            dgn_ref[...] = jnp.zeros_like(dgn_ref)

        d_merged = _dot_nt(dx1_ref[...], wo_ref[...])
        sa, sb = _sigmoid(ga_ref[...]), _sigmoid(gb_ref[...])
        d_a = d_merged * sa
        d_b = d_merged * sb
        da_ref[...] = d_a.astype(da_ref.dtype)
        dbr_ref[...] = d_b.astype(dbr_ref.dtype)
        d_ga = d_merged * a_ref[...] * (sa * (1.0 - sa))
        d_gb = d_merged * b_ref[...] * (sb * (1.0 - sb))
        dp_ref[:, 1024:2048] = d_ga.astype(dp_ref.dtype)
        dp_ref[:, 2048:3072] = d_gb.astype(dp_ref.dtype)
        dbias_ref[:, 1024:2048] += jnp.sum(d_ga, axis=0, keepdims=True)
        dbias_ref[:, 2048:3072] += jnp.sum(d_gb, axis=0, keepdims=True)
        day_ref[...] = _dot_nt(d_b, wao_ref[...]).astype(day_ref.dtype)
        d_ain = _dot_nt(d_a, wro_ref[...])
        for h in range(RET_HEADS):
            hv = slice(h * RET_DV, (h + 1) * RET_DV)
            yhat, rs = _group_norm(ry_ref[:, hv])
            rg = rg_ref[:, hv]
            sg = _sigmoid(rg)
            gn = gn_ref[:, hv]
            d_h = d_ain[:, hv]
            d_rg = d_h * (yhat * gn) * (sg * (1.0 + rg * (1.0 - sg)))
            d_ryn = d_h * (rg * sg)
            dgn_ref[:, hv] += jnp.sum(d_ryn * yhat, axis=0, keepdims=True)
            d_yhat = d_ryn * gn
            dry_ref[:, hv] = (rs * (d_yhat - jnp.mean(d_yhat, axis=-1, keepdims=True)
                                    - yhat * jnp.mean(d_yhat * yhat, axis=-1, keepdims=True))).astype(dry_ref.dtype)
            dp_ref[:, hv] = d_rg.astype(dp_ref.dtype)
            dbias_ref[:, hv] += jnp.sum(d_rg, axis=0, keepdims=True)

    return _call(body, name="merge_bwd", grid=(s // tm,),
                 in_specs=[_rows(tm, D), _rows(tm, D), _rows(tm, D), _rows(tm, 1024), _rows(tm, 1024, GATE_COL),
                           _rows(tm, 1024, GATE_COL + 1), _rows(tm, 1024, GATE_COL + 2), _acc(1024),
                           _const((D, D)), _const((D, D)), _const((D, D))],
                 out_specs=[_rows(tm, D), _rows(tm, D), _rows(tm, N_GATE), _rows(tm, D), _rows(tm, D), _acc(N_GATE),
                            _acc(1024)],
                 out_shape=[_sds((s, D), _MXU), _sds((s, D), _MXU), _sds((s, N_GATE), _MXU), _sds((s, D), _MXU),
                            _sds((s, D), _STORE), _sds((1, N_GATE), F32), _sds((1, 1024), F32)],
                 args=(dx1, a, b, ry, proj, proj, proj, gn_g, w_ro, w_ao, w_o), big=True)


def _ffn_fwd_call(x1, target, ln2_g, lnf_g, w_g, w_u, w_d, tm):
    s = x1.shape[0]

    def body(x1_ref, t_ref, g2_ref, gf_ref, wg_ref, wu_ref, wd_ref,
             h2_ref, g_ref, u_ref, f_ref, dx2_ref, loss_ref, dgf_ref):
        @pl.when(pl.program_id(0) == 0)
        def _():
            loss_ref[...] = jnp.zeros_like(loss_ref)
            dgf_ref[...] = jnp.zeros_like(dgf_ref)

        x1v = x1_ref[...]
        r1 = lax.rsqrt(jnp.mean(x1v * x1v, axis=-1, keepdims=True) + EPS)
        h2 = ((x1v * r1) * g2_ref[...]).astype(h2_ref.dtype)
        h2_ref[...] = h2
        g = _dot_nt(h2, wg_ref[...])
        u = _dot_nt(h2, wu_ref[...])
        g_ref[...] = g
        u_ref[...] = u
        f = ((g * _sigmoid(g)) * u).astype(f_ref.dtype)
        f_ref[...] = f
        x2 = x1v + _dot(f, wd_ref[...])
        r2 = lax.rsqrt(jnp.mean(x2 * x2, axis=-1, keepdims=True) + EPS)
        xhat = x2 * r2
        err = xhat * gf_ref[...] - t_ref[...]
        loss_ref[...] += 0.5 * jnp.sum(jnp.mean(err * err, axis=-1, keepdims=True))
        dy = err * (1.0 / D)
        dgf_ref[...] += jnp.sum(dy * xhat, axis=0, keepdims=True)
        dxh = dy * gf_ref[...]
        dx2_ref[...] = r2 * (dxh - xhat * jnp.mean(dxh * xhat, axis=-1, keepdims=True))

    return _call(body, name="ffn_fwd", grid=(s // tm,),
                 in_specs=[_rows(tm, D), _rows(tm, D), _acc(D), _acc(D), _const((D_FF, D)), _const((D_FF, D)),
                           _const((D_FF, D))],
                 out_specs=[_rows(tm, D), _rows(tm, D_FF), _rows(tm, D_FF), _rows(tm, D_FF), _rows(tm, D), _acc(128),
                            _acc(D)],
                 out_shape=[_sds((s, D), _MXU), _sds((s, D_FF), F32), _sds((s, D_FF), F32), _sds((s, D_FF), _MXU),
                            _sds((s, D), F32), _sds((1, 128), F32), _sds((1, D), F32)],
                 args=(x1, target, ln2_g, lnf_g, w_g, w_u, w_d), big=True)


def _ffn_bwd_call(dx2, x1, g, u, ln2_g, w_g, w_u, w_d, tm):
    s = dx2.shape[0]

    def body(dx2_ref, x1_ref, g_ref, u_ref, g2_ref, wg_ref, wu_ref, wd_ref, dx1_ref, dg_ref, du_ref, dg2_ref):
        @pl.when(pl.program_id(0) == 0)
        def _():
            dg2_ref[...] = jnp.zeros_like(dg2_ref)

        dx2v = dx2_ref[...]
        df = _dot_nt(dx2v, wd_ref[...])
        gv, uv = g_ref[...], u_ref[...]
        sg = _sigmoid(gv)
        du = (df * (gv * sg)).astype(du_ref.dtype)
        dg = (df * uv * (sg * (1.0 + gv * (1.0 - sg)))).astype(dg_ref.dtype)
        du_ref[...] = du
        dg_ref[...] = dg
        dh2 = _dot(dg, wg_ref[...]) + _dot(du, wu_ref[...])
        x1v = x1_ref[...]
        r1 = lax.rsqrt(jnp.mean(x1v * x1v, axis=-1, keepdims=True) + EPS)
        xhat = x1v * r1
        dg2_ref[...] += jnp.sum(dh2 * xhat, axis=0, keepdims=True)
        dxh = dh2 * g2_ref[...]
        dx1_ref[...] = dx2v + r1 * (dxh - xhat * jnp.mean(dxh * xhat, axis=-1, keepdims=True))

    return _call(body, name="ffn_bwd", grid=(s // tm,),
                 in_specs=[_rows(tm, D), _rows(tm, D), _rows(tm, D_FF), _rows(tm, D_FF), _acc(D),
                           _const((D_FF, D)), _const((D_FF, D)), _const((D_FF, D))],
                 out_specs=[_rows(tm, D), _rows(tm, D_FF), _rows(tm, D_FF), _acc(D)],
                 out_shape=[_sds((s, D), F32), _sds((s, D_FF), _MXU), _sds((s, D_FF), _MXU), _sds((1, D), F32)],
                 args=(dx2, x1, g, u, ln2_g, w_g, w_u, w_d), big=True)


def _dx_call(x, dx1, dp_ret, dp_gate, dp_attn, ln1_g, w_in, tm, exchange):
    s = x.shape[0]

    def body(x_ref, dx1_ref, dr_ref, dg_ref, da_ref, g1_ref, w_ref, dx_ref, dg1_ref):
        @pl.when(pl.program_id(0) == 0)
        def _():
            dg1_ref[...] = jnp.zeros_like(dg1_ref)

        dh = (_dot(dr_ref[...], w_ref[0:2048, :]) + _dot(dg_ref[:, 0:1024], w_ref[2048:3072, :])
              + _dot(da_ref[...], w_ref[3072:4352, :]) + _dot(dg_ref[:, 1024:3072], w_ref[4352:6400, :]))
        xv = x_ref[...]
        r = lax.rsqrt(jnp.mean(xv * xv, axis=-1, keepdims=True) + EPS)
        xhat = xv * r
        dg1_ref[...] += jnp.sum(dh * xhat, axis=0, keepdims=True)
        dxh = dh * g1_ref[...]
        dx_ref[...] = dx1_ref[...] + r * (dxh - xhat * jnp.mean(dxh * xhat, axis=-1, keepdims=True))

    return _call(body, name="dx", grid=(s // tm,),
                 in_specs=[_rows(tm, D), _rows(tm, D), _rows(tm, N_RET), _rows(tm, N_GATE), _rows(tm, N_ATTN), _acc(D),
                           _const((D_IN, D))],
                 out_specs=[_rows(tm, D), _acc(D)],
                 out_shape=[_sds((s, D), F32), _sds((1, D), F32)],
                 args=(x, dx1, dp_ret, dp_gate, dp_attn, ln1_g, w_in), big=True, exchange=exchange)


def _adamw(g, w, m, v):
    m_new = B1 * m + (1.0 - B1) * g
    v_new = B2 * v + (1.0 - B2) * (g * g)
    m_hat = m_new / (1.0 - B1 ** STEP)
    v_hat = v_new / (1.0 - B2 ** STEP)
    return -LR * (m_hat / (jnp.sqrt(v_hat) + ADAM_EPS) + WD * w), m_new, v_new


def _slot_sum(p_ref):
    g = p_ref[0].astype(F32)
    for k in range(1, N_DEV):
        g = g + p_ref[k].astype(F32)
    return g


def _adamw_call(parts, ws, ms, vs, name, tr):
    n = len(ws)
    rows, cols = ws[0].shape

    def body(*refs):
        p_refs, w_refs, m_refs, v_refs = (refs[k * n:(k + 1) * n] for k in range(4))
        outs = refs[4 * n:]
        for i in range(n):
            g = _slot_sum(p_refs[i])
            outs[4 * i][...] = g
            outs[4 * i + 1][...], outs[4 * i + 2][...], outs[4 * i + 3][...] = _adamw(
                g, w_refs[i][...], m_refs[i][...], v_refs[i][...])

    p_spec = pl.BlockSpec((N_DEV, tr, cols), lambda i: (0, i, 0))
    spec = pl.BlockSpec((tr, cols), lambda i: (i, 0))
    res = _call(body, name=name, grid=(rows // tr,), in_specs=[p_spec] * n + [spec] * (3 * n), out_specs=[spec] * (4 * n),
                out_shape=[_sds((rows, cols), F32)] * (4 * n), args=(*parts, *ws, *ms, *vs), big=True)
    return [res[4 * i:4 * i + 4] for i in range(n)]


SMALL_WIDTHS = [1024, 6400, 1024, 16, 1024, 1024]
SMALL_OFFSETS = [0, 1024, 7424, 8448, 8576, 9600]
LOSS_OFFSET = 10624
SMALL_LEN = 10752


def _pack_small(grads, loss):
    pieces = []
    for gr, width in zip(grads, SMALL_WIDTHS):
        pieces.append(jnp.pad(gr.reshape(1, width), ((0, 0), (0, -width % 128))))
    pieces.append(jnp.pad(loss.reshape(1, 1), ((0, 0), (0, 127))))
    return jnp.concatenate(pieces, axis=1)


def _adamw_small_call(parts, ws, ms, vs):
    n = len(ws)

    def body(*refs):
        p_ref, w_refs, m_refs, v_refs = refs[0], refs[1:1 + n], refs[1 + n:1 + 2 * n], refs[1 + 2 * n:1 + 3 * n]
        outs = refs[1 + 3 * n:]
        g_all = _slot_sum(p_ref)
        for i, (off, width) in enumerate(zip(SMALL_OFFSETS, SMALL_WIDTHS)):
            g = g_all[:, off:off + width]
            outs[i][...] = g
            outs[n + i][...], outs[2 * n + i][...], outs[3 * n + i][...] = _adamw(
                g, w_refs[i][...], m_refs[i][...], v_refs[i][...])
        outs[4 * n][...] = g_all[:, LOSS_OFFSET:LOSS_OFFSET + 128]

    whole = lambda shape: pl.BlockSpec(shape, lambda i: (0,) * len(shape))
    small = [whole((1, w)) for w in SMALL_WIDTHS]
    res = _call(body, name="adamw_small", grid=(1,), in_specs=[whole((N_DEV, 1, SMALL_LEN))] + small * 3,
                out_specs=small * 4 + [whole((1, 128))],
                out_shape=[_sds((1, w), F32) for w in SMALL_WIDTHS] * 4 + [_sds((1, 128), F32)],
                args=(parts, *ws, *ms, *vs))
    return [res[k * n:(k + 1) * n] for k in range(4)], res[4 * n]


def kernel(x, ln1_g, w_in, b_in, ret_norm_g, w_ret_out, attn_sinks, w_attn_out, w_out, ln2_g, w_ffn_gate, w_ffn_up, w_ffn_down, lnf_g, loss_target, m_ln1_g, m_w_in, m_b_in, m_ret_norm_g, m_w_ret_out, m_attn_sinks, m_w_attn_out, m_w_out, m_ln2_g, m_w_ffn_gate, m_w_ffn_up, m_w_ffn_down, m_lnf_g, v_ln1_g, v_w_in, v_b_in, v_ret_norm_g, v_w_ret_out, v_attn_sinks, v_w_attn_out, v_w_out, v_ln2_g, v_w_ffn_gate, v_w_ffn_up, v_w_ffn_down, v_lnf_g):
    cast = lambda a: a.astype(_MXU)
    xs, target = x[0], loss_target[0]
    s = xs.shape[0]
    r_sq = w_ret_out.shape[1]
    r_dn = w_ffn_down.shape[1]
    c_in = w_in.shape[2]
    c_ff = w_ffn_gate.shape[2]
    tm, tm_wide, tk = min(256, s), min(512, s), min(2048, s)
    lnf_row = lnf_g.reshape(1, D)
    cos_t, sin_t = _rope_tables(s)
    decays = _retention_decays()
    tr_shard = lambda a: a[0].T
    per_dev = lambda a, n: a.reshape(N_DEV, n, D)

    (h,), (all_in,) = _ln_call(xs, ln1_g, tm, _AllGather([cast(tr_shard(w_in))]))
    wt_in = all_in.reshape(N_DEV * c_in, D)
    rest = [tr_shard(w_ffn_gate), tr_shard(w_ffn_up), w_ret_out[0], w_attn_out[0], w_out[0], w_ffn_down[0]]
    (proj,), gathered = _proj_call(h, wt_in, b_in, _AllGather([cast(a) for a in rest]))
    wt_g, wt_u, full_ro, full_ao, full_o, full_d = (a.reshape(N_DEV * a.shape[1], D) for a in gathered)
    ry, qr, kr, states = _ret_fwd_call(proj, cos_t, sin_t, decays)
    ay, aqr, akr, avb, lse = _attn_fwd_call(proj, attn_sinks, cos_t, sin_t)
    a_in, br_a, br_b, merged, x1 = _merge_fwd_call(xs, ry, proj, ay, ret_norm_g, full_ro, full_ao, full_o, tm_wide)
    h2, g, u, f, dx2, loss, d_lnf = _ffn_fwd_call(x1, target, ln2_g, lnf_row, wt_g, wt_u, full_d, tm)

    dx1, dg, du, d_ln2 = _ffn_bwd_call(dx2, x1, g, u, ln2_g, wt_g, wt_u, full_d, tm)
    dw_d = _mm_tn(f, dx2, "dw_ffn_down", 1408, 1024, tk)
    dwt_g = _mm_tn(dg, h2, "dw_ffn_gate", 1408, 1024, tk)
    dwt_u = _mm_tn(du, h2, "dw_ffn_up", 1408, 1024, tk)
    d_a, d_b, dp_gate, day, dry, db_gate, d_gn = _merge_bwd_call(
        dx1, br_a, br_b, ry, proj, ret_norm_g, full_ro, full_ao, full_o, tm_wide)
    dw_o = _mm_tn(merged, dx1, "dw_out", 1024, 1024, tk)
    dw_ro = _mm_tn(a_in, d_a, "dw_ret_out", 1024, 1024, tk)
    dw_ao = _mm_tn(ay, d_b, "dw_attn_out", 1024, 1024, tk)
    (dp_attn, d_sinks, db_attn), (got_g, got_u, got_d) = _attn_bwd_call(
        aqr, akr, avb, day, lse, attn_sinks, cos_t, sin_t,
        _AllToAll([per_dev(dwt_g, c_ff), per_dev(dwt_u, c_ff), per_dev(dw_d, r_dn)]))
    (dp_ret, db_ret), (got_ro, got_ao, got_o) = _ret_bwd_call(
        qr, kr, proj, states, dry, cos_t, sin_t, decays,
        _AllToAll([per_dev(dw_ro, r_sq), per_dev(dw_ao, r_sq), per_dev(dw_o, r_sq)]))
    tk_in = min(4096, s)
    dwt_ret = _mm_tn(dp_ret, h, "dw_in_ret", 1024, 1024, tk_in)
    dwt_gate = _mm_tn(dp_gate, h, "dw_in_gate", 1024, 1024, tk_in)
    dwt_attn = _mm_tn(dp_attn, h, "dw_in_attn", 1280, 1024, tk_in)
    in_pieces = [(0, 0, 2048), (1, 0, 1024), (2, 0, 1280), (1, 1024, 3072)]
    (dx, d_ln1), (got_in,) = _dx_call(xs, dx1, dp_ret, dp_gate, dp_attn, ln1_g, wt_in, tm_wide,
                                      _RowScatter([dwt_ret, dwt_gate, dwt_attn], in_pieces, c_in))
    db_in = jnp.concatenate([db_ret, db_gate[:, 0:1024], db_attn, db_gate[:, 1024:3072]], axis=1)
    small = [d_ln1, db_in, d_gn, d_sinks[:, 0:Q_HEADS], d_ln2, d_lnf]
    (got_small,) = _exchange_call(_AllGather([_pack_small(small, loss[0, 0])]), "gather_small")

    transposed = ("w_in", "w_ffn_gate", "w_ffn_up")
    res = {}
    (res["w_in"],) = _adamw_call([got_in], [tr_shard(w_in)], [tr_shard(m_w_in)], [tr_shard(v_w_in)], "adamw_w_in", 160)
    res["w_ffn_gate"], res["w_ffn_up"], res["w_ffn_down"] = _adamw_call(
        [got_g, got_u, got_d], [tr_shard(w_ffn_gate), tr_shard(w_ffn_up), w_ffn_down[0]],
        [tr_shard(m_w_ffn_gate), tr_shard(m_w_ffn_up), m_w_ffn_down[0]],
        [tr_shard(v_w_ffn_gate), tr_shard(v_w_ffn_up), v_w_ffn_down[0]], "adamw_ffn", 176)
    res["w_ret_out"], res["w_attn_out"], res["w_out"] = _adamw_call(
        [got_ro, got_ao, got_o], [w_ret_out[0], w_attn_out[0], w_out[0]],
        [m_w_ret_out[0], m_w_attn_out[0], m_w_out[0]], [v_w_ret_out[0], v_w_attn_out[0], v_w_out[0]], "adamw_square", r_sq)
    small_names = ["ln1_g", "b_in", "ret_norm_g", "attn_sinks", "ln2_g", "lnf_g"]
    small_res, loss_row = _adamw_small_call(
        got_small, [ln1_g, b_in, ret_norm_g, attn_sinks, ln2_g, lnf_row],
        [m_ln1_g, m_b_in, m_ret_norm_g, m_attn_sinks, m_ln2_g, m_lnf_g.reshape(1, D)],
        [v_ln1_g, v_b_in, v_ret_norm_g, v_attn_sinks, v_ln2_g, v_lnf_g.reshape(1, D)])
    for i, nm in enumerate(small_names):
        res[nm] = [small_res[kind][i] for kind in range(4)]

    order = ["ln1_g", "w_in", "b_in", "ret_norm_g", "w_ret_out", "attn_sinks", "w_attn_out", "w_out", "ln2_g",
             "w_ffn_gate", "w_ffn_up", "w_ffn_down", "lnf_g"]
    outs = [loss_row[0, 0], dx[None]]
    for kind in range(4):
        for nm in order:
            val = res[nm][kind]
            if nm in transposed:
                val = val.T
            outs.append(val[None] if nm.startswith("w_") else val.reshape(D) if nm == "lnf_g" else val)
    return tuple(outs)
```

```python
import math

import numpy as np
import jax
import jax.numpy as jnp
from jax import lax
from jax.experimental import pallas as pl
from jax.experimental.pallas import tpu as pltpu

F32 = jnp.float32
_MXU = jnp.bfloat16
_STORE = jnp.bfloat16

N_DEV = 8
D = 1024
RET_HEADS, RET_DK, RET_DV = 4, 128, 256
BLK = 128
Q_HEADS, KV_HEADS, HEAD_DIM = 16, 2, 64
GROUP = Q_HEADS // KV_HEADS
D_FF = 2816
N_RET, N_GATE, N_ATTN = 2048, 3072, 1280
D_IN = N_RET + N_GATE + N_ATTN
ROPE_THETA = 10000.0
EPS = 1e-6
RET_SCALE = RET_DK ** -0.5
ATTN_SCALE = HEAD_DIM ** -0.5
LR, B1, B2, ADAM_EPS, WD, STEP = 0.001, 0.9, 0.999, 1e-08, 0.01, 10
VMEM_LIMIT_MB = 56
MESH = pl.DeviceIdType.MESH


def _dot(a, b):
    return jnp.dot(a.astype(_MXU), b.astype(_MXU), preferred_element_type=F32)


def _dot_nt(a, b):
    return lax.dot_general(a.astype(_MXU), b.astype(_MXU), (((1,), (1,)), ((), ())), preferred_element_type=F32)


def _dot_tn(a, b):
    return lax.dot_general(a.astype(_MXU), b.astype(_MXU), (((0,), (0,)), ((), ())), preferred_element_type=F32)


def _sigmoid(x):
    return 1.0 / (1.0 + jnp.exp(-x))


def _cparams(n_axes, big=False):
    kw = dict(dimension_semantics=("arbitrary",) * n_axes)
    if big:
        kw["vmem_limit_bytes"] = VMEM_LIMIT_MB * 2**20
    return pltpu.CompilerParams(**kw)


def _rows(tm, width, col=0):
    return pl.BlockSpec((tm, width), lambda i: (i, col))


def _const(shape):
    nd = len(shape)
    return pl.BlockSpec(shape, lambda *_: (0,) * nd, pipeline_mode=pl.Buffered(1))


def _acc(width):
    return pl.BlockSpec((1, width), lambda *_: (0, 0))


def _sds(shape, dtype):
    return jax.ShapeDtypeStruct(shape, dtype)


def _swap_halves(x, half):
    w = x.shape[-1]
    if 2 * half == w:
        return pltpu.roll(x, half, 1)
    lane = lax.broadcasted_iota(jnp.int32, x.shape, 1)
    return jnp.where(lane % (2 * half) < half, pltpu.roll(x, w - half, 1), pltpu.roll(x, half, 1))


def _rope_tables(seq):
    lane = jnp.arange(128, dtype=jnp.int32)
    ret_freq = ROPE_THETA ** (-(lane % 64).astype(F32) / 64)
    attn_freq = ROPE_THETA ** (-(lane % 32).astype(F32) / 32)
    ang = jnp.arange(seq, dtype=jnp.int32).astype(F32)[:, None] * jnp.where(lane < 64, ret_freq, attn_freq)[None, :]
    return jnp.cos(ang), jnp.sin(ang)


def _ret_rope(cos, sin):
    low = lax.broadcasted_iota(jnp.int32, cos.shape, 1) < RET_DK // 2
    return jnp.where(low, cos, pltpu.roll(cos, RET_DK // 2, 1)), jnp.where(low, -sin, pltpu.roll(sin, RET_DK // 2, 1))


def _attn_rope(cos, sin):
    lane = lax.broadcasted_iota(jnp.int32, cos.shape, 1)
    half = HEAD_DIM // 2

    def spread(t):
        t = pltpu.roll(t, 64, 1)
        t = jnp.where(lane < half, t, pltpu.roll(t, half, 1))
        return jnp.where(lane < HEAD_DIM, t, pltpu.roll(t, HEAD_DIM, 1))

    return spread(cos), jnp.where(lane % HEAD_DIM < half, -spread(sin), spread(sin))


def _retention_decays():
    log_gamma = np.log1p(-np.exp2(-5.0 - np.arange(RET_HEADS, dtype=np.float32))).astype(np.float32)
    idx = np.arange(BLK, dtype=np.float32)
    rel = idx[:, None] - idx[None, :]
    intra = np.where(rel[None] >= 0, np.exp(log_gamma[:, None, None] * np.maximum(rel, 0.0)[None]), 0.0)
    q_decay = np.exp(log_gamma[:, None] * (idx + 1.0))[:, :, None]
    k_decay = np.exp(log_gamma[:, None] * (BLK - 1.0 - idx))[:, :, None]
    chunk_decay = [float(np.exp(np.float32(lg * BLK))) for lg in log_gamma]
    return (jnp.asarray(intra, F32), jnp.asarray(q_decay, F32), jnp.asarray(k_decay, F32), chunk_decay)


def _position():
    return lax.axis_index("x"), lax.axis_index("y"), lax.axis_index("c")


def _slot(px, py, pc):
    return 4 * px + 2 * py + pc


class _AllGather:
    def __init__(self, blocks):
        self.blocks = list(blocks)
        nb = len(self.blocks)
        self.out_shape = [_sds((N_DEV,) + b.shape, b.dtype) for b in self.blocks]
        self.scratch = [pltpu.SemaphoreType.DMA((nb, 7)), pltpu.SemaphoreType.DMA((nb, 7)),
                        pltpu.SemaphoreType.DMA((nb,))]

    def phases(self, ins, outs, send_sems, recv_sems, local_sems):
        nb = len(ins)
        x, y, c = _position()
        me, sibling = (x, y, c), (x, y, 1 - c)
        chips = [(1 - x, y), (x, 1 - y), (1 - x, 1 - y)]

        def copy(b, k, block, to, src=None):
            dst = outs[b].at[_slot(*block)]
            return pltpu.make_async_remote_copy(
                src_ref=dst if src is None else src, dst_ref=dst, send_sem=send_sems.at[b, k],
                recv_sem=recv_sems.at[b, k], device_id=to, device_id_type=MESH)

        def own(b):
            return pltpu.make_async_copy(ins[b], outs[b].at[_slot(*me)], local_sems.at[b])

        def first(b):
            return [copy(b, 0, me, sibling, src=ins[b])] + [
                copy(b, 1 + j, me, (*chip, c), src=ins[b]) for j, chip in enumerate(chips)]

        def start():
            for b in range(nb):
                own(b).start()
                for cp in first(b):
                    cp.start()

        def forward():
            for b in range(nb):
                for j, chip in enumerate(chips):
                    copy(b, 1 + j, (*chip, c), me).wait_recv()
                    copy(b, 4 + j, (*chip, c), sibling).start()

        def finish():
            for b in range(nb):
                copy(b, 0, sibling, me).wait_recv()
                for j, chip in enumerate(chips):
                    copy(b, 4 + j, (*chip, 1 - c), me).wait_recv()
            for b in range(nb):
                for cp in first(b):
                    cp.wait_send()
                for j, chip in enumerate(chips):
                    copy(b, 4 + j, (*chip, c), sibling).wait_send()
                own(b).wait()

        return start, forward, finish


class _AllToAll:
    def __init__(self, blocks):
        self.blocks = list(blocks)
        nb = len(self.blocks)
        self.out_shape = [_sds(b.shape, b.dtype) for b in self.blocks]
        self.scratch = [pltpu.SemaphoreType.DMA((nb, 7)), pltpu.SemaphoreType.DMA((nb, 7)),
                        pltpu.SemaphoreType.DMA((nb,))]

    def phases(self, ins, outs, send_sems, recv_sems, local_sems):
        nb = len(ins)
        x, y, c = _position()
        flip = lambda v, bit: 1 - v if bit else v
        peers = [(flip(x, k >> 2 & 1), flip(y, k >> 1 & 1), flip(c, k & 1)) for k in range(1, N_DEV)]

        def copy(b, k, peer, landed=False):
            return pltpu.make_async_remote_copy(
                src_ref=ins[b].at[_slot(*peer)], dst_ref=outs[b].at[_slot(*peer) if landed else _slot(x, y, c)],
                send_sem=send_sems.at[b, k], recv_sem=recv_sems.at[b, k], device_id=peer, device_id_type=MESH)

        def own(b):
            return pltpu.make_async_copy(ins[b].at[_slot(x, y, c)], outs[b].at[_slot(x, y, c)], local_sems.at[b])

        def start():
            for b in range(nb):
                own(b).start()
                for k, peer in enumerate(peers):
                    copy(b, k, peer).start()

        def forward():
            pass

        def finish():
            for b in range(nb):
                for k, peer in enumerate(peers):
                    copy(b, k, peer, landed=True).wait_recv()
            for b in range(nb):
                for k, peer in enumerate(peers):
                    copy(b, k, peer).wait_send()
                own(b).wait()

        return start, forward, finish


class _RowScatter:
    def __init__(self, arrays, pieces, n):
        self.blocks = list(arrays)
        self.pieces, self.n = pieces, n
        self.out_shape = [_sds((N_DEV, n, D), arrays[0].dtype)]
        self.scratch = [pltpu.SemaphoreType.DMA((N_DEV,)), pltpu.SemaphoreType.DMA((N_DEV,)), pltpu.SemaphoreType.DMA]

    def _parts(self, k):
        lo, hi, pos, res = k * self.n, (k + 1) * self.n, 0, []
        for arr, first, last in self.pieces:
            a, b = max(lo, pos), min(hi, pos + last - first)
            if a < b:
                res.append((arr, first + a - pos, b - a, a - lo))
            pos += last - first
        return res

    def phases(self, ins, outs, send_sems, recv_sems, local_sem):
        (out,) = outs
        x, y, c = _position()
        me = _slot(x, y, c)

        def start():
            for k in range(N_DEV):
                dist = jnp.bitwise_xor(me, k)

                @pl.when(me != k)
                def _():
                    for arr, first, rows, at in self._parts(k):
                        pltpu.make_async_remote_copy(
                            src_ref=ins[arr].at[pl.ds(first, rows)], dst_ref=out.at[me, pl.ds(at, rows)],
                            send_sem=send_sems.at[dist], recv_sem=recv_sems.at[dist],
                            device_id=(k >> 2 & 1, k >> 1 & 1, k & 1), device_id_type=MESH).start()

                @pl.when(me == k)
                def _():
                    for arr, first, rows, at in self._parts(k):
                        pltpu.make_async_copy(ins[arr].at[pl.ds(first, rows)], out.at[me, pl.ds(at, rows)], local_sem).start()

        def forward():
            pass

        def whole_block(dist):
            return pltpu.make_async_remote_copy(
                src_ref=out.at[me], dst_ref=out.at[jnp.bitwise_xor(me, dist)], send_sem=send_sems.at[dist],
                recv_sem=recv_sems.at[dist], device_id=(x, y, c), device_id_type=MESH)

        def finish():
            for dist in range(1, N_DEV):
                whole_block(dist).wait_recv()
            for dist in range(1, N_DEV):
                whole_block(dist).wait_send()
            pltpu.make_async_copy(out.at[me], out.at[me], local_sem).wait()

        return start, forward, finish


def _call(body, *, name, grid, in_specs, out_specs, out_shape, args, scratch_shapes=(), big=False, exchange=None):
    params = _cparams(len(grid), big)
    if exchange is None:
        return pl.pallas_call(body, name=name, grid=grid, in_specs=in_specs, out_specs=out_specs, out_shape=out_shape,
                              scratch_shapes=list(scratch_shapes), compiler_params=params)(*args)
    n_in, n_out, n_scr = len(in_specs), len(out_specs), len(scratch_shapes)
    nb, nb_out = len(exchange.blocks), len(exchange.out_shape)
    steps = math.prod(grid)

    def carried(*refs):
        pos = 0
        parts = []
        for n in (n_in, nb, n_out, nb_out, n_scr, len(exchange.scratch)):
            parts.append(refs[pos:pos + n])
            pos += n
        ins, x_ins, outs, x_outs, scr, sems = parts
        step = pl.program_id(0)
        for axis in range(1, len(grid)):
            step = step * grid[axis] + pl.program_id(axis)
        start, forward, finish = exchange.phases(x_ins, x_outs, *sems)
        pl.when(step == 0)(start)
        body(*ins, *outs, *scr)

        @pl.when(step == steps - 1)
        def _():
            forward()
            finish()

    any_spec = pl.BlockSpec(memory_space=pl.ANY)
    res = pl.pallas_call(
        carried, name=name, grid=grid, in_specs=list(in_specs) + [any_spec] * nb,
        out_specs=list(out_specs) + [any_spec] * nb_out, out_shape=list(out_shape) + exchange.out_shape,
        scratch_shapes=list(scratch_shapes) + exchange.scratch, compiler_params=params)(*args, *exchange.blocks)
    return res[:n_out], res[n_out:]


def _exchange_call(exchange, name):
    nb = len(exchange.blocks)

    def body(*refs):
        start, forward, finish = exchange.phases(refs[:nb], refs[nb:2 * nb], *refs[2 * nb:])
        start()
        forward()
        finish()

    any_spec = pl.BlockSpec(memory_space=pl.ANY)
    return pl.pallas_call(body, name=name, in_specs=[any_spec] * nb, out_specs=[any_spec] * nb,
                          out_shape=exchange.out_shape, scratch_shapes=exchange.scratch)(*exchange.blocks)


def _ln_call(x, g, tm, exchange):
    s = x.shape[0]

    def body(x_ref, g_ref, h_ref):
        xv = x_ref[...]
        r = lax.rsqrt(jnp.mean(xv * xv, axis=-1, keepdims=True) + EPS)
        h_ref[...] = ((xv * r) * g_ref[...]).astype(h_ref.dtype)

    return _call(body, name="ln1", grid=(s // tm,), in_specs=[_rows(tm, D), _acc(D)], out_specs=[_rows(tm, D)],
                 out_shape=[_sds((s, D), _MXU)], args=(x, g), exchange=exchange)


PROJ_TILE = 256


def _proj_source_tile(j):
    gate_end, attn_end, end = 3072 // PROJ_TILE, 4352 // PROJ_TILE, 6400 // PROJ_TILE
    n_gates = end - attn_end
    return jnp.where(j < gate_end, j, jnp.where(j < gate_end + n_gates, j + (attn_end - gate_end), j - n_gates))


def _proj_call(a, wt, bias, exchange):
    s, k = a.shape
    n = wt.shape[0]
    rows = min(1024, s)

    def body(a_ref, w_ref, b_ref, o_ref):
        for r in range(0, s, rows):
            o_ref[r:r + rows, :] = _dot_nt(a_ref[r:r + rows, :], w_ref[...]) + b_ref[...]

    return _call(body, name="proj", grid=(n // PROJ_TILE,),
                 in_specs=[_const((s, k)), pl.BlockSpec((PROJ_TILE, k), lambda j: (_proj_source_tile(j), 0)),
                           pl.BlockSpec((1, PROJ_TILE), lambda j: (0, _proj_source_tile(j)))],
                 out_specs=[pl.BlockSpec((s, PROJ_TILE), lambda j: (0, j))], out_shape=[_sds((s, n), F32)],
                 args=(a, wt, bias), big=True, exchange=exchange)


def _mm_tn(a, b, name, tm, tn, tk, exchange=None):
    s, m = a.shape
    n = b.shape[1]
    last = s // tk - 1

    def body(a_ref, b_ref, o_ref, acc):
        k = pl.program_id(2)
        part = _dot_tn(a_ref[...], b_ref[...])

        @pl.when(k == 0)
        def _():
            acc[...] = part

        @pl.when(k > 0)
        def _():
            acc[...] += part

        @pl.when(k == last)
        def _():
            o_ref[...] = acc[...].astype(o_ref.dtype)

    res = _call(body, name=name, grid=(m // tm, n // tn, s // tk),
                in_specs=[pl.BlockSpec((tk, tm), lambda i, j, k: (k, i)), pl.BlockSpec((tk, tn), lambda i, j, k: (k, j))],
                out_specs=[pl.BlockSpec((tm, tn), lambda i, j, k: (i, j))], out_shape=[_sds((m, n), _MXU)],
                scratch_shapes=[pltpu.VMEM((tm, tn), F32)], args=(a, b), big=True, exchange=exchange)
    return res[0] if exchange is None else (res[0][0], res[1])


RET_CHUNKS = 4


def _ret_fwd_call(proj, cos, sin, decays):
    s = proj.shape[0]
    nblk = s // BLK
    per = min(RET_CHUNKS, nblk)
    rows = per * BLK
    intra, q_decay, k_decay, chunk_decay = decays

    def body(rq_ref, rk_ref, rv_ref, cos_ref, sin_ref, intra_ref, qd_ref, kd_ref,
             ry_ref, qr_ref, kr_ref, st_ref, state):
        @pl.when(pl.program_id(0) == 0)
        def _():
            state[...] = jnp.zeros_like(state)

        for c in range(per):
            rc = slice(c * BLK, (c + 1) * BLK)
            cos_v, sin_v = _ret_rope(cos_ref[rc, :], sin_ref[rc, :])
            for h in range(RET_HEADS):
                hk = slice(h * RET_DK, (h + 1) * RET_DK)
                hv = slice(h * RET_DV, (h + 1) * RET_DV)
                q, k = rq_ref[rc, hk], rk_ref[rc, hk]
                qr = (q * cos_v + _swap_halves(q, RET_DK // 2) * sin_v) * RET_SCALE
                kr = k * cos_v + _swap_halves(k, RET_DK // 2) * sin_v
                v = rv_ref[rc, hv]
                s_h = state[h]
                st_ref[c, h] = s_h.astype(st_ref.dtype)
                scores = _dot_nt(qr, kr) * intra_ref[h]
                ry_ref[rc, hv] = _dot(scores, v) + _dot(qr, s_h) * qd_ref[h]
                state[h] = s_h * chunk_decay[h] + _dot_tn(kr * kd_ref[h], v)
                qr_ref[rc, hk] = qr.astype(qr_ref.dtype)
                kr_ref[rc, hk] = kr.astype(kr_ref.dtype)

    blk = lambda w, c: pl.BlockSpec((rows, w), lambda n: (n, c))
    return _call(body, name="ret_fwd", grid=(nblk // per,),
                 in_specs=[blk(512, 0), blk(512, 1), blk(1024, 1), blk(128, 0), blk(128, 0),
                           _const(intra.shape), _const(q_decay.shape), _const(k_decay.shape)],
                 out_specs=[blk(1024, 0), blk(512, 0), blk(512, 0),
                            pl.BlockSpec((per, RET_HEADS, RET_DK, RET_DV), lambda n: (n, 0, 0, 0))],
                 out_shape=[_sds((s, 1024), F32), _sds((s, 512), _MXU), _sds((s, 512), _MXU),
                            _sds((nblk, RET_HEADS, RET_DK, RET_DV), _MXU)],
                 scratch_shapes=[pltpu.VMEM((RET_HEADS, RET_DK, RET_DV), F32)],
                 args=(proj, proj, proj, cos, sin, intra, q_decay, k_decay))


def _ret_bwd_call(qr, kr, proj, states, dry, cos, sin, decays, exchange):
    s = qr.shape[0]
    nblk = s // BLK
    per = min(RET_CHUNKS, nblk)
    rows = per * BLK
    steps = nblk // per
    intra, q_decay, k_decay, chunk_decay = decays

    def body(qr_ref, kr_ref, rv_ref, st_ref, dry_ref, cos_ref, sin_ref, intra_ref, qd_ref, kd_ref,
             dp_ref, db_ref, dstate):
        @pl.when(pl.program_id(0) == 0)
        def _():
            dstate[...] = jnp.zeros_like(dstate)
            db_ref[...] = jnp.zeros_like(db_ref)

        for c in reversed(range(per)):
            rc = slice(c * BLK, (c + 1) * BLK)
            cos_v, sin_v = _ret_rope(cos_ref[rc, :], sin_ref[rc, :])
            for h in range(RET_HEADS):
                hk = slice(h * RET_DK, (h + 1) * RET_DK)
                hv = slice(h * RET_DV, (h + 1) * RET_DV)
                q, k, v, d_out = qr_ref[rc, hk], kr_ref[rc, hk], rv_ref[rc, hv], dry_ref[rc, hv]
                d_next = dstate[h]
                scores = _dot_nt(q, k) * intra_ref[h]
                d_scores = _dot_nt(d_out, v) * intra_ref[h]
                d_cross = d_out * qd_ref[h]
                dq = _dot(d_scores, k) + _dot_nt(d_cross, st_ref[c, h])
                dk = _dot_tn(d_scores, q) + _dot_nt(v, d_next) * kd_ref[h]
                dv = _dot_tn(scores, d_out) + _dot(k.astype(F32) * kd_ref[h], d_next)
                dstate[h] = d_next * chunk_decay[h] + _dot_tn(q, d_cross)
                dq = (dq * cos_v - _swap_halves(dq, RET_DK // 2) * sin_v) * RET_SCALE
                dk = dk * cos_v - _swap_halves(dk, RET_DK // 2) * sin_v
                kcols = slice(512 + h * RET_DK, 512 + (h + 1) * RET_DK)
                vcols = slice(1024 + h * RET_DV, 1024 + (h + 1) * RET_DV)
                dp_ref[rc, hk] = dq.astype(dp_ref.dtype)
                dp_ref[rc, kcols] = dk.astype(dp_ref.dtype)
                dp_ref[rc, vcols] = dv.astype(dp_ref.dtype)
                db_ref[:, hk] += jnp.sum(dq, axis=0, keepdims=True)
                db_ref[:, kcols] += jnp.sum(dk, axis=0, keepdims=True)
                db_ref[:, vcols] += jnp.sum(dv, axis=0, keepdims=True)

    rblk = lambda w, c: pl.BlockSpec((rows, w), lambda n: (steps - 1 - n, c))
    return _call(body, name="ret_bwd", grid=(steps,),
                 in_specs=[rblk(512, 0), rblk(512, 0), rblk(1024, 1),
                           pl.BlockSpec((per, RET_HEADS, RET_DK, RET_DV), lambda n: (steps - 1 - n, 0, 0, 0)),
                           rblk(1024, 0), rblk(128, 0), rblk(128, 0),
                           _const(intra.shape), _const(q_decay.shape), _const(k_decay.shape)],
                 out_specs=[rblk(N_RET, 0), _acc(N_RET)],
                 out_shape=[_sds((s, N_RET), _MXU), _sds((1, N_RET), F32)],
                 scratch_shapes=[pltpu.VMEM((RET_HEADS, RET_DK, RET_DV), F32)],
                 args=(qr, kr, proj, states, dry, cos, sin, intra, q_decay, k_decay), exchange=exchange)


def _both_halves(x, g):
    lane = lax.broadcasted_iota(jnp.int32, x.shape, 1)
    keep = lane < HEAD_DIM if g == 0 else lane >= HEAD_DIM
    return jnp.where(keep, x, pltpu.roll(x, HEAD_DIM, 1))


def _stack_heads(ref, g, rows=slice(None)):
    lane = lax.broadcasted_iota(jnp.int32, (BLK, 128), 1)
    pieces = []
    for j in range(g * 4, g * 4 + 4):
        chunk = ref[rows, j * 128:(j + 1) * 128]
        pieces += [jnp.where(lane < HEAD_DIM, chunk, jnp.zeros_like(chunk)),
                   jnp.where(lane >= HEAD_DIM, chunk, jnp.zeros_like(chunk))]
    return jnp.concatenate(pieces, axis=0)


def _window_bias(first_block):
    kj = lax.broadcasted_iota(jnp.int32, (2 * BLK, BLK), 0)
    qi = lax.broadcasted_iota(jnp.int32, (2 * BLK, BLK), 1)
    first_key = jnp.where(first_block, BLK, 0)
    seen = (kj > qi) & (kj <= qi + BLK) & (kj >= first_key)
    return jnp.where(seen, 0.0, -1e30)


def _sink_softmax(scores, sink):
    m = jnp.maximum(jnp.max(scores, axis=0, keepdims=True), sink)
    e = jnp.exp(scores - m)
    return e, jnp.sum(e, axis=0, keepdims=True) + jnp.exp(sink - m), m


def _head_pair(stacked_t, jj):
    even = stacked_t[0:HEAD_DIM, 2 * jj * BLK:(2 * jj + 1) * BLK]
    odd = stacked_t[HEAD_DIM:128, (2 * jj + 1) * BLK:(2 * jj + 2) * BLK]
    return jnp.concatenate([even, odd], axis=0).T


ATTN_BLOCKS = 4


def _attn_fwd_call(proj, sinks, cos, sin):
    s = proj.shape[0]
    nblk = s // BLK
    per = min(ATTN_BLOCKS, nblk)
    rows = per * BLK

    def body(sink_ref, q_ref, k_ref, v_ref, cos_ref, sin_ref, ay_ref, qr_ref, kr_ref, vb_ref, lse_ref, pr_ref,
             kwin, vwin, bias, s_scr, p_scr):
        n = pl.program_id(0)

        @pl.when(n == 0)
        def _():
            kwin[...] = jnp.zeros_like(kwin)
            vwin[...] = jnp.zeros_like(vwin)

        for c in range(per):
            rc = slice(c * BLK, (c + 1) * BLK)
            kwin[0:BLK] = kwin[BLK:2 * BLK]
            vwin[0:BLK] = vwin[BLK:2 * BLK]
            cos_v, sin_v = _attn_rope(cos_ref[rc, :], sin_ref[rc, :])
            k = k_ref[rc, :]
            kr = (k * cos_v + _swap_halves(k, HEAD_DIM // 2) * sin_v).astype(kwin.dtype)
            kwin[BLK:2 * BLK] = kr
            vwin[BLK:2 * BLK] = v_ref[rc, :].astype(vwin.dtype)
            kr_ref[rc, :] = kr
            vb_ref[rc, :] = vwin[BLK:2 * BLK]
            for j in range(Q_HEADS // 2):
                cols = slice(j * 128, (j + 1) * 128)
                q = q_ref[rc, cols]
                qr_ref[rc, cols] = ((q * cos_v + _swap_halves(q, HEAD_DIM // 2) * sin_v) * ATTN_SCALE).astype(qr_ref.dtype)
            bias[...] = _window_bias(n == 0) if c == 0 else _window_bias(False)
            for g in range(KV_HEADS):
                kg = _both_halves(kwin[...], g)
                vg_t = _both_halves(vwin[...], g).astype(F32).T
                s_scr[...] = _dot_nt(kg, _stack_heads(qr_ref, g, rc))
                for i in range(GROUP):
                    cols = slice(i * BLK, (i + 1) * BLK)
                    e, den, m = _sink_softmax(s_scr[:, cols] + bias[...], sink_ref[0, g * GROUP + i])
                    p_scr[:, cols] = (e * (1.0 / den)).astype(p_scr.dtype)
                    lse_ref[c, g, :, cols] = m + jnp.log(den)
                out_t = _dot(vg_t, p_scr[...])
                pr_ref[c, g] = p_scr[...]
                for jj in range(4):
                    j = g * 4 + jj
                    ay_ref[rc, j * 128:(j + 1) * 128] = _head_pair(out_t, jj).astype(ay_ref.dtype)

    blk = lambda w, c: pl.BlockSpec((rows, w), lambda n: (n, c))
    off = (N_RET + N_GATE) // 128
    wide = (2 * BLK, GROUP * BLK)
    return _call(body, name="attn_fwd", grid=(nblk // per,),
                 in_specs=[pl.BlockSpec(memory_space=pltpu.SMEM), blk(1024, off // 8), blk(128, off + 8), blk(128, off + 9),
                           blk(128, 0), blk(128, 0)],
                 out_specs=[blk(1024, 0), blk(1024, 0), blk(128, 0), blk(128, 0),
                            pl.BlockSpec((per, KV_HEADS, 1, GROUP * BLK), lambda n: (n, 0, 0, 0)),
                            pl.BlockSpec((per, KV_HEADS, 2 * BLK, GROUP * BLK), lambda n: (n, 0, 0, 0))],
                 out_shape=[_sds((s, 1024), _MXU), _sds((s, 1024), _MXU), _sds((s, 128), _MXU), _sds((s, 128), _MXU),
                            _sds((nblk, KV_HEADS, 1, GROUP * BLK), F32), _sds((nblk, KV_HEADS, 2 * BLK, GROUP * BLK), _MXU)],
                 scratch_shapes=[pltpu.VMEM((2 * BLK, 128), _MXU), pltpu.VMEM((2 * BLK, 128), _MXU),
                                 pltpu.VMEM((2 * BLK, BLK), F32), pltpu.VMEM(wide, F32), pltpu.VMEM(wide, _MXU)],
                 args=(sinks, proj, proj, proj, cos, sin))


def _attn_bwd_call(qr, kr, vb, day, lse, probs, sinks, cos, sin, exchange):
    s = qr.shape[0]
    nblk = s // BLK

    def body(sink_ref, q_ref, kc_ref, kp_ref, vc_ref, vp_ref, do_ref, cos_ref, sin_ref, cosp_ref, sinp_ref, lse_ref, pr_ref,
             dp_ref, dsink_ref, db_ref, bias, s_scr, dp_scr, p_scr, ds_scr, dq_held, kv_held, kv_prev, kv_new):
        n = pl.program_id(0)
        valid = (n < nblk).astype(F32)

        @pl.when(n == 0)
        def _():
            dsink_ref[...] = jnp.zeros_like(dsink_ref)
            db_ref[...] = jnp.zeros_like(db_ref)

        @pl.when(n >= 1)
        def _():
            dp_ref[:, 0:1024] = dq_held[...]

        cos_v, sin_v = _attn_rope(cos_ref[...], sin_ref[...])
        bias[...] = _window_bias(n == 0)
        lane1 = lax.broadcasted_iota(jnp.int32, (1, 128), 1)
        kwin = jnp.concatenate([kp_ref[...], kc_ref[...]], axis=0)
        vwin = jnp.concatenate([vp_ref[...], vc_ref[...]], axis=0)
        dk_heads, dv_heads = [], []
        dsink = jnp.zeros((1, 128), F32)
        for g in range(KV_HEADS):
            kg = _both_halves(kwin, g)
            vg = _both_halves(vwin, g)
            q_all = _stack_heads(q_ref, g)
            do_all = _stack_heads(do_ref, g)
            dp_scr[...] = _dot_nt(vg, do_all)
            for i in range(GROUP):
                head = g * GROUP + i
                cols = slice(i * BLK, (i + 1) * BLK)
                lse = lse_ref[0, g, :, cols]
                p = pr_ref[0, g, :, cols].astype(F32)
                p_sink = jnp.exp(sink_ref[0, head] - lse)
                dp = dp_scr[:, cols]
                delta = jnp.sum(p * dp, axis=0, keepdims=True)
                p_scr[:, cols] = p.astype(p_scr.dtype)
                ds_scr[:, cols] = (p * (dp - delta)).astype(ds_scr.dtype)
                dsink = dsink + jnp.where(lane1 == head, -jnp.sum(p_sink * delta, axis=1, keepdims=True), 0.0)
            dv_both = _dot(p_scr[...], do_all)
            dk_both = _dot(ds_scr[...], q_all)
            dv_heads.append(dv_both + pltpu.roll(dv_both, HEAD_DIM, 1))
            dk_heads.append(dk_both + pltpu.roll(dk_both, HEAD_DIM, 1))
            dq_t = _dot(kg.astype(F32).T, ds_scr[...])
            for jj in range(4):
                cols = slice((g * 4 + jj) * 128, (g * 4 + jj + 1) * 128)
                dq = _head_pair(dq_t, jj)
                dq = (dq * cos_v - _swap_halves(dq, HEAD_DIM // 2) * sin_v) * ATTN_SCALE
                dq_held[:, cols] = dq.astype(dq_held.dtype)
                db_ref[:, cols] += jnp.sum(dq, axis=0, keepdims=True) * valid
        dsink_ref[...] += dsink * valid
        lane2 = lax.broadcasted_iota(jnp.int32, (2 * BLK, 128), 1)
        dk_all = jnp.where(lane2 < HEAD_DIM, dk_heads[0], dk_heads[1])
        dv_all = jnp.where(lane2 < HEAD_DIM, dv_heads[0], dv_heads[1])
        kv_prev[:, 0:128] = dk_all[0:BLK] * valid
        kv_prev[:, 128:256] = dv_all[0:BLK] * valid
        kv_new[:, 0:128] = dk_all[BLK:2 * BLK]
        kv_new[:, 128:256] = dv_all[BLK:2 * BLK]

        @pl.when(n >= 1)
        def _():
            dkv = kv_held[...] + kv_prev[...]
            dk = dkv[:, 0:128]
            cos_p, sin_p = _attn_rope(cosp_ref[...], sinp_ref[...])
            dk = dk * cos_p - _swap_halves(dk, HEAD_DIM // 2) * sin_p
            dv = dkv[:, 128:256]
            dp_ref[:, 1024:1152] = dk.astype(dp_ref.dtype)
            dp_ref[:, 1152:1280] = dv.astype(dp_ref.dtype)
            db_ref[:, 1024:1152] += jnp.sum(dk, axis=0, keepdims=True)
            db_ref[:, 1152:1280] += jnp.sum(dv, axis=0, keepdims=True)

        kv_held[...] = kv_new[...]

    blk = lambda w: pl.BlockSpec((BLK, w), lambda n: (jnp.minimum(n, nblk - 1), 0))
    pblk = lambda w: pl.BlockSpec((BLK, w), lambda n: (jnp.maximum(n - 1, 0), 0))
    wide = (2 * BLK, GROUP * BLK)
    return _call(body, name="attn_bwd", grid=(nblk + 1,),
                 in_specs=[pl.BlockSpec(memory_space=pltpu.SMEM), blk(1024), blk(128), pblk(128), blk(128), pblk(128),
                           blk(1024), blk(128), blk(128), pblk(128), pblk(128),
                           pl.BlockSpec((1, KV_HEADS, 1, GROUP * BLK), lambda n: (jnp.minimum(n, nblk - 1), 0, 0, 0)),
                           pl.BlockSpec((1, KV_HEADS, 2 * BLK, GROUP * BLK), lambda n: (jnp.minimum(n, nblk - 1), 0, 0, 0))],
                 out_specs=[pblk(N_ATTN), _acc(128), _acc(N_ATTN)],
                 out_shape=[_sds((s, N_ATTN), _MXU), _sds((1, 128), F32), _sds((1, N_ATTN), F32)],
                 scratch_shapes=[pltpu.VMEM((2 * BLK, BLK), F32), pltpu.VMEM(wide, F32), pltpu.VMEM(wide, F32),
                                 pltpu.VMEM(wide, _MXU), pltpu.VMEM(wide, _MXU), pltpu.VMEM((BLK, 1024), _MXU),
                                 pltpu.VMEM((BLK, 256), F32), pltpu.VMEM((BLK, 256), F32), pltpu.VMEM((BLK, 256), F32)],
                 args=(sinks, qr, kr, kr, vb, vb, day, cos, sin, cos, sin, lse, probs), exchange=exchange)


def _group_norm(y):
    mu = jnp.mean(y, axis=-1, keepdims=True)
    yc = y - mu
    rs = lax.rsqrt(jnp.mean(yc * yc, axis=-1, keepdims=True) + EPS)
    return yc * rs, rs


GATE_COL = N_RET // 1024


def _merge_fwd_call(x, ry, proj, ay, gn_g, w_ro, w_ao, w_o, tm):
    s = x.shape[0]

    def body(x_ref, ry_ref, rg_ref, ga_ref, gb_ref, ay_ref, gn_ref, wro_ref, wao_ref, wo_ref,
             ain_ref, a_ref, b_ref, mg_ref, x1_ref):
        for h in range(RET_HEADS):
            hv = slice(h * RET_DV, (h + 1) * RET_DV)
            yhat, _ = _group_norm(ry_ref[:, hv])
            rg = rg_ref[:, hv]
            ain_ref[:, hv] = ((rg * _sigmoid(rg)) * (yhat * gn_ref[:, hv])).astype(ain_ref.dtype)
        a = _dot(ain_ref[...], wro_ref[...])
        b = _dot(ay_ref[...], wao_ref[...])
        a_ref[...] = a.astype(a_ref.dtype)
        b_ref[...] = b.astype(b_ref.dtype)
        merged = (_sigmoid(ga_ref[...]) * a + _sigmoid(gb_ref[...]) * b).astype(mg_ref.dtype)
        mg_ref[...] = merged
        x1_ref[...] = x_ref[...] + _dot(merged, wo_ref[...])

    return _call(body, name="merge_fwd", grid=(s // tm,),
                 in_specs=[_rows(tm, D), _rows(tm, 1024), _rows(tm, 1024, GATE_COL), _rows(tm, 1024, GATE_COL + 1),
                           _rows(tm, 1024, GATE_COL + 2), _rows(tm, 1024), _acc(1024),
                           _const((D, D)), _const((D, D)), _const((D, D))],
                 out_specs=[_rows(tm, D)] * 5,
                 out_shape=[_sds((s, D), _MXU), _sds((s, D), _STORE), _sds((s, D), _STORE), _sds((s, D), _MXU),
                            _sds((s, D), F32)],
                 args=(x, ry, proj, proj, proj, ay, gn_g, w_ro, w_ao, w_o), big=True)


def _merge_bwd_call(dx1, a, b, ry, proj, gn_g, w_ro, w_ao, w_o, tm):
    s = dx1.shape[0]

    def body(dx1_ref, a_ref, b_ref, ry_ref, rg_ref, ga_ref, gb_ref, gn_ref, wro_ref, wao_ref, wo_ref,
             da_ref, dbr_ref, dp_ref, day_ref, dry_ref, dbias_ref, dgn_ref):
        @pl.when(pl.program_id(0) == 0)
        def _():
            dbias_ref[...] = jnp.zeros_like(dbias_ref)
            dgn_ref[...] = jnp.zeros_like(dgn_ref)

        d_merged = _dot_nt(dx1_ref[...], wo_ref[...])
        sa, sb = _sigmoid(ga_ref[...]), _sigmoid(gb_ref[...])
        d_a = d_merged * sa
        d_b = d_merged * sb
        da_ref[...] = d_a.astype(da_ref.dtype)
        dbr_ref[...] = d_b.astype(dbr_ref.dtype)
        d_ga = d_merged * a_ref[...] * (sa * (1.0 - sa))
        d_gb = d_merged * b_ref[...] * (sb * (1.0 - sb))
        dp_ref[:, 1024:2048] = d_ga.astype(dp_ref.dtype)
        dp_ref[:, 2048:3072] = d_gb.astype(dp_ref.dtype)
        dbias_ref[:, 1024:2048] += jnp.sum(d_ga, axis=0, keepdims=True)
        dbias_ref[:, 2048:3072] += jnp.sum(d_gb, axis=0, keepdims=True)
        day_ref[...] = _dot_nt(d_b, wao_ref[...]).astype(day_ref.dtype)
        d_ain = _dot_nt(d_a, wro_ref[...])
        for h in range(RET_HEADS):
            hv = slice(h * RET_DV, (h + 1) * RET_DV)
            yhat, rs = _group_norm(ry_ref[:, hv])
            rg = rg_ref[:, hv]
            sg = _sigmoid(rg)
            gn = gn_ref[:, hv]
            d_h = d_ain[:, hv]
            d_rg = d_h * (yhat * gn) * (sg * (1.0 + rg * (1.0 - sg)))
            d_ryn = d_h * (rg * sg)
            dgn_ref[:, hv] += jnp.sum(d_ryn * yhat, axis=0, keepdims=True)
            d_yhat = d_ryn * gn
            dry_ref[:, hv] = (rs * (d_yhat - jnp.mean(d_yhat, axis=-1, keepdims=True)
                                    - yhat * jnp.mean(d_yhat * yhat, axis=-1, keepdims=True))).astype(dry_ref.dtype)
            dp_ref[:, hv] = d_rg.astype(dp_ref.dtype)
            dbias_ref[:, hv] += jnp.sum(d_rg, axis=0, keepdims=True)

    return _call(body, name="merge_bwd", grid=(s // tm,),
                 in_specs=[_rows(tm, D), _rows(tm, D), _rows(tm, D), _rows(tm, 1024), _rows(tm, 1024, GATE_COL),
                           _rows(tm, 1024, GATE_COL + 1), _rows(tm, 1024, GATE_COL + 2), _acc(1024),
                           _const((D, D)), _const((D, D)), _const((D, D))],
                 out_specs=[_rows(tm, D), _rows(tm, D), _rows(tm, N_GATE), _rows(tm, D), _rows(tm, D), _acc(N_GATE),
                            _acc(1024)],
                 out_shape=[_sds((s, D), _MXU), _sds((s, D), _MXU), _sds((s, N_GATE), _MXU), _sds((s, D), _MXU),
                            _sds((s, D), _STORE), _sds((1, N_GATE), F32), _sds((1, 1024), F32)],
                 args=(dx1, a, b, ry, proj, proj, proj, gn_g, w_ro, w_ao, w_o), big=True)


def _ffn_fwd_call(x1, target, ln2_g, lnf_g, w_g, w_u, w_d, tm):
    s = x1.shape[0]

    def body(x1_ref, t_ref, g2_ref, gf_ref, wg_ref, wu_ref, wd_ref,
             h2_ref, g_ref, u_ref, f_ref, dx2_ref, loss_ref, dgf_ref):
        @pl.when(pl.program_id(0) == 0)
        def _():
            loss_ref[...] = jnp.zeros_like(loss_ref)
            dgf_ref[...] = jnp.zeros_like(dgf_ref)

        x1v = x1_ref[...]
        r1 = lax.rsqrt(jnp.mean(x1v * x1v, axis=-1, keepdims=True) + EPS)
        h2 = ((x1v * r1) * g2_ref[...]).astype(h2_ref.dtype)
        h2_ref[...] = h2
        g = _dot_nt(h2, wg_ref[...])
        u = _dot_nt(h2, wu_ref[...])
        g_ref[...] = g
        u_ref[...] = u
        f = ((g * _sigmoid(g)) * u).astype(f_ref.dtype)
        f_ref[...] = f
        x2 = x1v + _dot(f, wd_ref[...])
        r2 = lax.rsqrt(jnp.mean(x2 * x2, axis=-1, keepdims=True) + EPS)
        xhat = x2 * r2
        err = xhat * gf_ref[...] - t_ref[...]
        loss_ref[...] += 0.5 * jnp.sum(jnp.mean(err * err, axis=-1, keepdims=True))
        dy = err * (1.0 / D)
        dgf_ref[...] += jnp.sum(dy * xhat, axis=0, keepdims=True)
        dxh = dy * gf_ref[...]
        dx2_ref[...] = r2 * (dxh - xhat * jnp.mean(dxh * xhat, axis=-1, keepdims=True))

    return _call(body, name="ffn_fwd", grid=(s // tm,),
                 in_specs=[_rows(tm, D), _rows(tm, D), _acc(D), _acc(D), _const((D_FF, D)), _const((D_FF, D)),
                           _const((D_FF, D))],
                 out_specs=[_rows(tm, D), _rows(tm, D_FF), _rows(tm, D_FF), _rows(tm, D_FF), _rows(tm, D), _acc(128),
                            _acc(D)],
                 out_shape=[_sds((s, D), _MXU), _sds((s, D_FF), F32), _sds((s, D_FF), F32), _sds((s, D_FF), _MXU),
                            _sds((s, D), F32), _sds((1, 128), F32), _sds((1, D), F32)],
                 args=(x1, target, ln2_g, lnf_g, w_g, w_u, w_d), big=True)


def _ffn_bwd_call(dx2, x1, g, u, ln2_g, w_g, w_u, w_d, tm):
    s = dx2.shape[0]

    def body(dx2_ref, x1_ref, g_ref, u_ref, g2_ref, wg_ref, wu_ref, wd_ref, dx1_ref, dg_ref, du_ref, dg2_ref):
        @pl.when(pl.program_id(0) == 0)
        def _():
            dg2_ref[...] = jnp.zeros_like(dg2_ref)

        dx2v = dx2_ref[...]
        df = _dot_nt(dx2v, wd_ref[...])
        gv, uv = g_ref[...], u_ref[...]
        sg = _sigmoid(gv)
        du = (df * (gv * sg)).astype(du_ref.dtype)
        dg = (df * uv * (sg * (1.0 + gv * (1.0 - sg)))).astype(dg_ref.dtype)
        du_ref[...] = du
        dg_ref[...] = dg
        dh2 = _dot(dg, wg_ref[...]) + _dot(du, wu_ref[...])
        x1v = x1_ref[...]
        r1 = lax.rsqrt(jnp.mean(x1v * x1v, axis=-1, keepdims=True) + EPS)
        xhat = x1v * r1
        dg2_ref[...] += jnp.sum(dh2 * xhat, axis=0, keepdims=True)
        dxh = dh2 * g2_ref[...]
        dx1_ref[...] = dx2v + r1 * (dxh - xhat * jnp.mean(dxh * xhat, axis=-1, keepdims=True))

    return _call(body, name="ffn_bwd", grid=(s // tm,),
                 in_specs=[_rows(tm, D), _rows(tm, D), _rows(tm, D_FF), _rows(tm, D_FF), _acc(D),
                           _const((D_FF, D)), _const((D_FF, D)), _const((D_FF, D))],
                 out_specs=[_rows(tm, D), _rows(tm, D_FF), _rows(tm, D_FF), _acc(D)],
                 out_shape=[_sds((s, D), F32), _sds((s, D_FF), _MXU), _sds((s, D_FF), _MXU), _sds((1, D), F32)],
                 args=(dx2, x1, g, u, ln2_g, w_g, w_u, w_d), big=True)


def _dx_call(x, dx1, dp_ret, dp_gate, dp_attn, ln1_g, w_in, tm, exchange):
    s = x.shape[0]

    def body(x_ref, dx1_ref, dr_ref, dg_ref, da_ref, g1_ref, w_ref, dx_ref, dg1_ref):
        @pl.when(pl.program_id(0) == 0)
        def _():
            dg1_ref[...] = jnp.zeros_like(dg1_ref)

        dh = (_dot(dr_ref[...], w_ref[0:2048, :]) + _dot(dg_ref[:, 0:1024], w_ref[2048:3072, :])
              + _dot(da_ref[...], w_ref[3072:4352, :]) + _dot(dg_ref[:, 1024:3072], w_ref[4352:6400, :]))
        xv = x_ref[...]
        r = lax.rsqrt(jnp.mean(xv * xv, axis=-1, keepdims=True) + EPS)
        xhat = xv * r
        dg1_ref[...] += jnp.sum(dh * xhat, axis=0, keepdims=True)
        dxh = dh * g1_ref[...]
        dx_ref[...] = dx1_ref[...] + r * (dxh - xhat * jnp.mean(dxh * xhat, axis=-1, keepdims=True))

    return _call(body, name="dx", grid=(s // tm,),
                 in_specs=[_rows(tm, D), _rows(tm, D), _rows(tm, N_RET), _rows(tm, N_GATE), _rows(tm, N_ATTN), _acc(D),
                           _const((D_IN, D))],
                 out_specs=[_rows(tm, D), _acc(D)],
                 out_shape=[_sds((s, D), F32), _sds((1, D), F32)],
                 args=(x, dx1, dp_ret, dp_gate, dp_attn, ln1_g, w_in), big=True, exchange=exchange)


def _adamw(g, w, m, v):
    m_new = B1 * m + (1.0 - B1) * g
    v_new = B2 * v + (1.0 - B2) * (g * g)
    m_hat = m_new / (1.0 - B1 ** STEP)
    v_hat = v_new / (1.0 - B2 ** STEP)
    return -LR * (m_hat / (jnp.sqrt(v_hat) + ADAM_EPS) + WD * w), m_new, v_new


def _slot_sum(p_ref):
    g = p_ref[0].astype(F32)
    for k in range(1, N_DEV):
        g = g + p_ref[k].astype(F32)
    return g


def _adamw_call(parts, ws, ms, vs, name, tr):
    n = len(ws)
    rows, cols = ws[0].shape

    def body(*refs):
        p_refs, w_refs, m_refs, v_refs = (refs[k * n:(k + 1) * n] for k in range(4))
        outs = refs[4 * n:]
        for i in range(n):
            g = _slot_sum(p_refs[i])
            outs[4 * i][...] = g
            outs[4 * i + 1][...], outs[4 * i + 2][...], outs[4 * i + 3][...] = _adamw(
                g, w_refs[i][...], m_refs[i][...], v_refs[i][...])

    p_spec = pl.BlockSpec((N_DEV, tr, cols), lambda i: (0, i, 0))
    spec = pl.BlockSpec((tr, cols), lambda i: (i, 0))
    res = _call(body, name=name, grid=(rows // tr,), in_specs=[p_spec] * n + [spec] * (3 * n), out_specs=[spec] * (4 * n),
                out_shape=[_sds((rows, cols), F32)] * (4 * n), args=(*parts, *ws, *ms, *vs), big=True)
    return [res[4 * i:4 * i + 4] for i in range(n)]


SMALL_WIDTHS = [1024, 6400, 1024, 16, 1024, 1024]
SMALL_OFFSETS = [0, 1024, 7424, 8448, 8576, 9600]
LOSS_OFFSET = 10624
SMALL_LEN = 10752


def _pack_small(grads, loss):
    pieces = []
    for gr, width in zip(grads, SMALL_WIDTHS):
        pieces.append(jnp.pad(gr.reshape(1, width), ((0, 0), (0, -width % 128))))
    pieces.append(jnp.pad(loss.reshape(1, 1), ((0, 0), (0, 127))))
    return jnp.concatenate(pieces, axis=1)


def _adamw_small_call(parts, ws, ms, vs):
    n = len(ws)

    def body(*refs):
        p_ref, w_refs, m_refs, v_refs = refs[0], refs[1:1 + n], refs[1 + n:1 + 2 * n], refs[1 + 2 * n:1 + 3 * n]
        outs = refs[1 + 3 * n:]
        g_all = _slot_sum(p_ref)
        for i, (off, width) in enumerate(zip(SMALL_OFFSETS, SMALL_WIDTHS)):
            g = g_all[:, off:off + width]
            outs[i][...] = g
            outs[n + i][...], outs[2 * n + i][...], outs[3 * n + i][...] = _adamw(
                g, w_refs[i][...], m_refs[i][...], v_refs[i][...])
        outs[4 * n][...] = g_all[:, LOSS_OFFSET:LOSS_OFFSET + 128]

    whole = lambda shape: pl.BlockSpec(shape, lambda i: (0,) * len(shape))
    small = [whole((1, w)) for w in SMALL_WIDTHS]
    res = _call(body, name="adamw_small", grid=(1,), in_specs=[whole((N_DEV, 1, SMALL_LEN))] + small * 3,
                out_specs=small * 4 + [whole((1, 128))],
                out_shape=[_sds((1, w), F32) for w in SMALL_WIDTHS] * 4 + [_sds((1, 128), F32)],
                args=(parts, *ws, *ms, *vs))
    return [res[k * n:(k + 1) * n] for k in range(4)], res[4 * n]


def kernel(x, ln1_g, w_in, b_in, ret_norm_g, w_ret_out, attn_sinks, w_attn_out, w_out, ln2_g, w_ffn_gate, w_ffn_up, w_ffn_down, lnf_g, loss_target, m_ln1_g, m_w_in, m_b_in, m_ret_norm_g, m_w_ret_out, m_attn_sinks, m_w_attn_out, m_w_out, m_ln2_g, m_w_ffn_gate, m_w_ffn_up, m_w_ffn_down, m_lnf_g, v_ln1_g, v_w_in, v_b_in, v_ret_norm_g, v_w_ret_out, v_attn_sinks, v_w_attn_out, v_w_out, v_ln2_g, v_w_ffn_gate, v_w_ffn_up, v_w_ffn_down, v_lnf_g):
    cast = lambda a: a.astype(_MXU)
    xs, target = x[0], loss_target[0]
    s = xs.shape[0]
    r_sq = w_ret_out.shape[1]
    r_dn = w_ffn_down.shape[1]
    c_in = w_in.shape[2]
    c_ff = w_ffn_gate.shape[2]
    tm, tm_wide, tk = min(256, s), min(512, s), min(2048, s)
    lnf_row = lnf_g.reshape(1, D)
    cos_t, sin_t = _rope_tables(s)
    decays = _retention_decays()
    tr_shard = lambda a: a[0].T
    per_dev = lambda a, n: a.reshape(N_DEV, n, D)

    (h,), (all_in,) = _ln_call(xs, ln1_g, tm, _AllGather([cast(tr_shard(w_in))]))
    wt_in = all_in.reshape(N_DEV * c_in, D)
    rest = [tr_shard(w_ffn_gate), tr_shard(w_ffn_up), w_ret_out[0], w_attn_out[0], w_out[0], w_ffn_down[0]]
    (proj,), gathered = _proj_call(h, wt_in, b_in, _AllGather([cast(a) for a in rest]))
    wt_g, wt_u, full_ro, full_ao, full_o, full_d = (a.reshape(N_DEV * a.shape[1], D) for a in gathered)
    ry, qr, kr, states = _ret_fwd_call(proj, cos_t, sin_t, decays)
    ay, aqr, akr, avb, lse, probs = _attn_fwd_call(proj, attn_sinks, cos_t, sin_t)
    a_in, br_a, br_b, merged, x1 = _merge_fwd_call(xs, ry, proj, ay, ret_norm_g, full_ro, full_ao, full_o, tm_wide)
    h2, g, u, f, dx2, loss, d_lnf = _ffn_fwd_call(x1, target, ln2_g, lnf_row, wt_g, wt_u, full_d, tm)

    dx1, dg, du, d_ln2 = _ffn_bwd_call(dx2, x1, g, u, ln2_g, wt_g, wt_u, full_d, tm)
    dw_d = _mm_tn(f, dx2, "dw_ffn_down", 1408, 1024, tk)
    dwt_g = _mm_tn(dg, h2, "dw_ffn_gate", 1408, 1024, tk)
    dwt_u = _mm_tn(du, h2, "dw_ffn_up", 1408, 1024, tk)
    d_a, d_b, dp_gate, day, dry, db_gate, d_gn = _merge_bwd_call(
        dx1, br_a, br_b, ry, proj, ret_norm_g, full_ro, full_ao, full_o, tm_wide)
    dw_o = _mm_tn(merged, dx1, "dw_out", 1024, 1024, tk)
    dw_ro = _mm_tn(a_in, d_a, "dw_ret_out", 1024, 1024, tk)
    dw_ao = _mm_tn(ay, d_b, "dw_attn_out", 1024, 1024, tk)
    (dp_attn, d_sinks, db_attn), (got_g, got_u, got_d) = _attn_bwd_call(
        aqr, akr, avb, day, lse, probs, attn_sinks, cos_t, sin_t,
        _AllToAll([per_dev(dwt_g, c_ff), per_dev(dwt_u, c_ff), per_dev(dw_d, r_dn)]))
    (dp_ret, db_ret), (got_ro, got_ao, got_o) = _ret_bwd_call(
        qr, kr, proj, states, dry, cos_t, sin_t, decays,
        _AllToAll([per_dev(dw_ro, r_sq), per_dev(dw_ao, r_sq), per_dev(dw_o, r_sq)]))
    tk_in = min(4096, s)
    dwt_ret = _mm_tn(dp_ret, h, "dw_in_ret", 1024, 1024, tk_in)
    dwt_gate = _mm_tn(dp_gate, h, "dw_in_gate", 1024, 1024, tk_in)
    dwt_attn = _mm_tn(dp_attn, h, "dw_in_attn", 1280, 1024, tk_in)
    in_pieces = [(0, 0, 2048), (1, 0, 1024), (2, 0, 1280), (1, 1024, 3072)]
    (dx, d_ln1), (got_in,) = _dx_call(xs, dx1, dp_ret, dp_gate, dp_attn, ln1_g, wt_in, tm_wide,
                                      _RowScatter([dwt_ret, dwt_gate, dwt_attn], in_pieces, c_in))
    db_in = jnp.concatenate([db_ret, db_gate[:, 0:1024], db_attn, db_gate[:, 1024:3072]], axis=1)
    small = [d_ln1, db_in, d_gn, d_sinks[:, 0:Q_HEADS], d_ln2, d_lnf]
    (got_small,) = _exchange_call(_AllGather([_pack_small(small, loss[0, 0])]), "gather_small")

    transposed = ("w_in", "w_ffn_gate", "w_ffn_up")
    res = {}
    (res["w_in"],) = _adamw_call([got_in], [tr_shard(w_in)], [tr_shard(m_w_in)], [tr_shard(v_w_in)], "adamw_w_in", 160)
    res["w_ffn_gate"], res["w_ffn_up"], res["w_ffn_down"] = _adamw_call(
        [got_g, got_u, got_d], [tr_shard(w_ffn_gate), tr_shard(w_ffn_up), w_ffn_down[0]],
        [tr_shard(m_w_ffn_gate), tr_shard(m_w_ffn_up), m_w_ffn_down[0]],
        [tr_shard(v_w_ffn_gate), tr_shard(v_w_ffn_up), v_w_ffn_down[0]], "adamw_ffn", 176)
    res["w_ret_out"], res["w_attn_out"], res["w_out"] = _adamw_call(
        [got_ro, got_ao, got_o], [w_ret_out[0], w_attn_out[0], w_out[0]],
        [m_w_ret_out[0], m_w_attn_out[0], m_w_out[0]], [v_w_ret_out[0], v_w_attn_out[0], v_w_out[0]], "adamw_square", r_sq)
    small_names = ["ln1_g", "b_in", "ret_norm_g", "attn_sinks", "ln2_g", "lnf_g"]
    small_res, loss_row = _adamw_small_call(
        got_small, [ln1_g, b_in, ret_norm_g, attn_sinks, ln2_g, lnf_row],
        [m_ln1_g, m_b_in, m_ret_norm_g, m_attn_sinks, m_ln2_g, m_lnf_g.reshape(1, D)],
        [v_ln1_g, v_b_in, v_ret_norm_g, v_attn_sinks, v_ln2_g, v_lnf_g.reshape(1, D)])
    for i, nm in enumerate(small_names):
        res[nm] = [small_res[kind][i] for kind in range(4)]

    order = ["ln1_g", "w_in", "b_in", "ret_norm_g", "w_ret_out", "attn_sinks", "w_attn_out", "w_out", "ln2_g",
             "w_ffn_gate", "w_ffn_up", "w_ffn_down", "lnf_g"]
    outs = [loss_row[0, 0], dx[None]]
    for kind in range(4):
        for nm in order:
            val = res[nm][kind]
            if nm in transposed:
                val = val.T
            outs.append(val[None] if nm.startswith("w_") else val.reshape(D) if nm == "lnf_g" else val)
    return tuple(outs)
```
